```python
import math
import jax, jax.numpy as jnp
from jax import lax
import numpy as np

D_MODEL = 2048
BATCH = 8
SEQ = 8192
DEPTH = 4

HEAD_DIM = 128
A_HEADS = 8
A_WIDTH = A_HEADS * HEAD_DIM
DILATED_BRANCHES = ((128, 1), (512, 4), (2048, 16))
A_QBLOCK = 128
B_WIDTH = D_MODEL - A_WIDTH
B_GROUP = 16
B_GROUPS = B_WIDTH // B_GROUP
B_STATE = 64
DT_MIN = 1e-3
DT_MAX = 1e-1
C_HEADS = D_MODEL // HEAD_DIM
C_WIDTH = C_HEADS * HEAD_DIM
GRID_W = 64
NA_ROWS_MAX = 8
NA_COLS = 16
T5_BUCKETS = 32
T5_MAX_DISTANCE = 1024
D_FF = 4 * D_MODEL
RMS_EPS = 1e-6
NEG_INF = -1e30
N_EVEN = (DEPTH + 1) // 2
N_ODD = DEPTH // 2

kernel_name = "hybrid_dilated_s5_neighbourhood_encoder"


def rmsnorm(x, g):
    x32 = x.astype(jnp.float32)
    y = x32 * lax.rsqrt(jnp.mean(x32 * x32, axis=-1, keepdims=True) + RMS_EPS)
    return (y * g.astype(jnp.float32)).astype(x.dtype)


def t5_bucket(rel):
    half = T5_BUCKETS // 2
    max_exact = half // 2
    n = jnp.abs(rel)
    nf = jnp.maximum(n, 1).astype(jnp.float32)
    large = max_exact + (jnp.log(nf / max_exact) / math.log(T5_MAX_DISTANCE / max_exact)
                         * (half - max_exact)).astype(jnp.int32)
    large = jnp.minimum(large, half - 1)
    return jnp.where(rel > 0, half, 0) + jnp.where(n < max_exact, n, large)


def dilated_branch(q, k, v, t5_table, window, dilation):
    b, h, s, hd = q.shape
    half = window // (2 * dilation)
    L = s // dilation
    nb = -(-L // A_QBLOCK)
    lp = nb * A_QBLOCK
    kb_len = A_QBLOCK + 2 * half
    scale = 1.0 / math.sqrt(hd)

    def strided(t):
        return t.reshape(b, h, L, dilation, hd).transpose(0, 1, 3, 2, 4)

    qs = jnp.pad(strided(q), ((0, 0), (0, 0), (0, 0), (0, lp - L), (0, 0)))
    qs = qs.reshape(b, h, dilation, nb, A_QBLOCK, hd)
    pad_k = ((0, 0), (0, 0), (0, 0), (half, lp - L + half), (0, 0))
    key_idx = jnp.arange(nb)[:, None] * A_QBLOCK + jnp.arange(kb_len)[None, :]
    ks = jnp.pad(strided(k), pad_k)[:, :, :, key_idx]
    vs = jnp.pad(strided(v), pad_k)[:, :, :, key_idx]

    off = jnp.arange(kb_len)[None, :] - half - jnp.arange(A_QBLOCK)[:, None]
    bias = jnp.transpose(t5_table[t5_bucket(off * dilation)], (2, 0, 1)).astype(jnp.float32)
    key_pos = key_idx - half
    valid = (jnp.abs(off) <= half)[None] & ((key_pos >= 0) & (key_pos < L))[:, None, :]

    sc = jnp.einsum('bhrnqc,bhrnkc->bhrnqk', qs, ks) * scale + bias[None, :, None, None]
    sc = jnp.where(valid, sc, NEG_INF)
    m = sc.max(axis=-1)
    p = jnp.exp(sc - m[..., None])
    l = p.sum(axis=-1)
    num = jnp.einsum('bhrnqk,bhrnkc->bhrnqc', p, vs)

    def unstride(t):
        t = t.reshape(b, h, dilation, lp, *t.shape[5:])[:, :, :, :L]
        t = jnp.moveaxis(t, 2, 3)
        return t.reshape(b, h, s, *t.shape[4:])

    return unstride(m), unstride(l), unstride(num)


def dilated_attention(q, k, v, t5_table):
    outs = [dilated_branch(q, k, v, t5_table, w, d) for (w, d) in DILATED_BRANCHES]
    m = jnp.stack([o[0] for o in outs])
    l = jnp.stack([o[1] for o in outs])
    num = jnp.stack([o[2] for o in outs])
    wgt = jnp.exp(m - m.max(axis=0, keepdims=True))
    return (wgt[..., None] * num).sum(0) / (wgt * l).sum(0)[..., None]


def s5_direction(u, lam_re, lam_im, log_step, b_re, b_im, c_re, c_im, reverse):
    f32 = jnp.float32
    step = jnp.exp(log_step.astype(f32))[:, None]
    lr = jnp.minimum(lam_re.astype(f32), -1e-4)
    li = lam_im.astype(f32)
    mag = jnp.exp(lr * step)
    ab_re = mag * jnp.cos(li * step)
    ab_im = mag * jnp.sin(li * step)
    den = lr * lr + li * li
    zr = ((ab_re - 1.0) * lr + ab_im * li) / den
    zi = (ab_im * lr - (ab_re - 1.0) * li) / den
    br = b_re.astype(f32)
    bi = b_im.astype(f32)
    bb_re = zr[..., None] * br - zi[..., None] * bi
    bb_im = zr[..., None] * bi + zi[..., None] * br
    bu_re = jnp.einsum('bsgc,gpc->bsgp', u, bb_re)
    bu_im = jnp.einsum('bsgc,gpc->bsgp', u, bb_im)
    a_re = jnp.broadcast_to(ab_re, bu_re.shape)
    a_im = jnp.broadcast_to(ab_im, bu_im.shape)

    def combine(e1, e2):
        a1r, a1i, b1r, b1i = e1
        a2r, a2i, b2r, b2i = e2
        return (a2r * a1r - a2i * a1i,
                a2r * a1i + a2i * a1r,
                a2r * b1r - a2i * b1i + b2r,
                a2r * b1i + a2i * b1r + b2i)

    _, _, xr, xi = lax.associative_scan(combine, (a_re, a_im, bu_re, bu_im), reverse=reverse, axis=1)
    return (jnp.einsum('bsgp,gcp->bsgc', xr, c_re.astype(f32))
            - jnp.einsum('bsgp,gcp->bsgc', xi, c_im.astype(f32)))


def neighbourhood_attention(q, k, v, rpb):
    b, h, s, hd = q.shape
    rows = s // GRID_W
    kr = min(NA_ROWS_MAX, rows)
    scale = 1.0 / math.sqrt(hd)
    r = jnp.arange(rows)
    c = jnp.arange(GRID_W)
    row_start = jnp.clip(r - kr // 2, 0, rows - kr)
    row_idx = row_start[:, None] + jnp.arange(kr)[None, :]
    col_start = jnp.clip(c - NA_COLS // 2, 0, GRID_W - NA_COLS)
    col_ok = (c[None, :] >= col_start[:, None]) & (c[None, :] < col_start[:, None] + NA_COLS)
    col_off = c[None, :] - c[:, None]
    row_off = row_idx - r[:, None]

    qg = q.reshape(b, h, rows, GRID_W, hd)
    kg = k.reshape(b, h, rows, GRID_W, hd)[:, :, row_idx]
    vg = v.reshape(b, h, rows, GRID_W, hd)[:, :, row_idx]
    bias = rpb[:, (row_off + NA_ROWS_MAX - 1)[:, None, :, None],
               jnp.clip(col_off + NA_COLS - 1, 0, 2 * NA_COLS - 2)[None, :, None, :]]
    sc = jnp.einsum('bhrqc,bhrjkc->bhrqjk', qg, kg) * scale + bias[None].astype(jnp.float32)
    sc = jnp.where(col_ok[:, None, :], sc, NEG_INF)
    p = jax.nn.softmax(sc.reshape(b, h, rows, GRID_W, kr * GRID_W), axis=-1).reshape(sc.shape)
    o = jnp.einsum('bhrqjk,bhrjkc->bhrqc', p, vg)
    return o.reshape(b, h, s, hd)


def mixer_ab(xn, w_in, w_out, t5_table, lam_re, lam_im, log_step, b_re, b_im, c_re, c_im, d_skip, w_glu):
    b, s, _ = xn.shape
    f32 = jnp.float32
    proj = xn @ w_in
    q, k, v, u = jnp.split(proj, [A_WIDTH, 2 * A_WIDTH, 3 * A_WIDTH], axis=-1)

    def heads(t):
        return t.astype(f32).reshape(b, s, A_HEADS, HEAD_DIM).transpose(0, 2, 1, 3)

    o_a = dilated_attention(heads(q), heads(k), heads(v), t5_table)
    o_a = o_a.transpose(0, 2, 1, 3).reshape(b, s, A_WIDTH)

    ug = u.astype(f32).reshape(b, s, B_GROUPS, B_GROUP)
    y = (s5_direction(ug, lam_re[0], lam_im[0], log_step[0], b_re, b_im, c_re[0], c_im[0], False)
         + s5_direction(ug, lam_re[1], lam_im[1], log_step[1], b_re, b_im, c_re[1], c_im[1], True)
         + d_skip.astype(f32).reshape(B_GROUPS, B_GROUP) * ug)
    y = jax.nn.gelu(y.reshape(b, s, B_WIDTH))
    o_b = y * jax.nn.sigmoid((y.astype(xn.dtype) @ w_glu).astype(f32))
    merged = jnp.concatenate([o_a, o_b], axis=-1).astype(xn.dtype)
    return merged @ w_out


def mixer_c(xn, w_qkv, w_out, rpb):
    b, s, _ = xn.shape
    q, k, v = jnp.split(xn @ w_qkv, 3, axis=-1)

    def heads(t):
        return t.astype(jnp.float32).reshape(b, s, C_HEADS, HEAD_DIM).transpose(0, 2, 1, 3)

    o = neighbourhood_attention(heads(q), heads(k), heads(v), rpb)
    o = o.transpose(0, 2, 1, 3).reshape(b, s, C_WIDTH).astype(xn.dtype)
    return o @ w_out


def squared_relu_mlp(xn, w1, w2):
    hdn = jnp.square(jax.nn.relu(xn @ w1))
    return hdn @ w2


def _fwd_setup_inputs(seed: int = 0) -> dict:
    key = jax.random.key(seed)
    ks = jax.random.split(key, 22)
    f32 = jnp.float32
    nrm = lambda k, shape, sc: (jax.random.normal(k, shape, f32) * sc)
    lam_im_init = jnp.pi * jnp.arange(B_STATE, dtype=f32)
    return {
        "x": nrm(ks[0], (BATCH, SEQ, D_MODEL), 1.0),
        "t5_bias": nrm(ks[1], (T5_BUCKETS, A_HEADS), 0.5),
        "ab_w_in": nrm(ks[2], (N_EVEN, D_MODEL, 3 * A_WIDTH + B_WIDTH), D_MODEL ** -0.5),
        "ab_w_out": nrm(ks[3], (N_EVEN, A_WIDTH + B_WIDTH, D_MODEL), (A_WIDTH + B_WIDTH) ** -0.5),
        "s5_lam_re": -0.5 + nrm(ks[4], (N_EVEN, 2, B_GROUPS, B_STATE), 0.01),
        "s5_lam_im": lam_im_init + nrm(ks[5], (N_EVEN, 2, B_GROUPS, B_STATE), 0.01),
        "s5_log_step": jax.random.uniform(ks[6], (N_EVEN, 2, B_GROUPS), f32,
                                          minval=math.log(DT_MIN), maxval=math.log(DT_MAX)),
        "s5_b_re": nrm(ks[7], (N_EVEN, B_GROUPS, B_STATE, B_GROUP), (2 * B_GROUP) ** -0.5),
        "s5_b_im": nrm(ks[8], (N_EVEN, B_GROUPS, B_STATE, B_GROUP), (2 * B_GROUP) ** -0.5),
        "s5_c_re": nrm(ks[9], (N_EVEN, 2, B_GROUPS, B_GROUP, B_STATE), (2 * B_STATE) ** -0.5),
        "s5_c_im": nrm(ks[10], (N_EVEN, 2, B_GROUPS, B_GROUP, B_STATE), (2 * B_STATE) ** -0.5),
        "s5_d": nrm(ks[11], (N_EVEN, B_WIDTH), 1.0),
        "s5_w_glu": nrm(ks[12], (N_EVEN, B_WIDTH, B_WIDTH), B_WIDTH ** -0.5),
        "c_w_qkv": nrm(ks[13], (N_ODD, D_MODEL, 3 * C_WIDTH), D_MODEL ** -0.5),
        "c_w_out": nrm(ks[14], (N_ODD, C_WIDTH, D_MODEL), C_WIDTH ** -0.5),
        "c_rpb": nrm(ks[15], (N_ODD, C_HEADS, 2 * NA_ROWS_MAX - 1, 2 * NA_COLS - 1), 0.5),
        "norm_mix": 1.0 + nrm(ks[16], (DEPTH, D_MODEL), 0.02),
        "norm_mlp": 1.0 + nrm(ks[17], (DEPTH, D_MODEL), 0.02),
        "mlp_w1": nrm(ks[18], (DEPTH, D_MODEL, D_FF), D_MODEL ** -0.5),
        "mlp_w2": nrm(ks[19], (DEPTH, D_FF, D_MODEL), D_FF ** -0.5),
        "norm_final": 1.0 + nrm(ks[20], (D_MODEL,), 0.02),
    }


def _fwd_reference(x, t5_bias, ab_w_in, ab_w_out, s5_lam_re, s5_lam_im, s5_log_step, s5_b_re, s5_b_im,
              s5_c_re, s5_c_im, s5_d, s5_w_glu, c_w_qkv, c_w_out, c_rpb, norm_mix, norm_mlp,
              mlp_w1, mlp_w2, norm_final):
    for i in range(DEPTH):
        j = i // 2
        hn = rmsnorm(x, norm_mix[i])
        if i % 2 == 0:
            mix = mixer_ab(hn, ab_w_in[j], ab_w_out[j], t5_bias, s5_lam_re[j], s5_lam_im[j],
                           s5_log_step[j], s5_b_re[j], s5_b_im[j], s5_c_re[j], s5_c_im[j],
                           s5_d[j], s5_w_glu[j])
        else:
            mix = mixer_c(hn, c_w_qkv[j], c_w_out[j], c_rpb[j])
        x = x + mix.astype(x.dtype)
        hn = rmsnorm(x, norm_mlp[i])
        x = x + squared_relu_mlp(hn, mlp_w1[i], mlp_w2[i]).astype(x.dtype)
    return rmsnorm(x, norm_final)


import jax as _jax
import jax.numpy as _jnp

TWIN_FORMAT = 'train_step'
FWD_PARAMS = ['x', 't5_bias', 'ab_w_in', 'ab_w_out', 's5_lam_re', 's5_lam_im', 's5_log_step', 's5_b_re', 's5_b_im', 's5_c_re', 's5_c_im', 's5_d', 's5_w_glu', 'c_w_qkv', 'c_w_out', 'c_rpb', 'norm_mix', 'norm_mlp', 'mlp_w1', 'mlp_w2', 'norm_final']
TWIN_WEIGHTS = ['t5_bias', 'ab_w_in', 'ab_w_out', 's5_lam_re', 's5_lam_im', 's5_log_step', 's5_b_re', 's5_b_im', 's5_c_re', 's5_c_im', 's5_d', 's5_w_glu', 'c_w_qkv', 'c_w_out', 'c_rpb', 'norm_mix', 'norm_mlp', 'mlp_w1', 'mlp_w2', 'norm_final']
TWIN_DIFF_INPUT = 'x'
TWIN_INPUTS = ['x', 't5_bias', 'ab_w_in', 'ab_w_out', 's5_lam_re', 's5_lam_im', 's5_log_step', 's5_b_re', 's5_b_im', 's5_c_re', 's5_c_im', 's5_d', 's5_w_glu', 'c_w_qkv', 'c_w_out', 'c_rpb', 'norm_mix', 'norm_mlp', 'mlp_w1', 'mlp_w2', 'norm_final', 'loss_target', 'm_t5_bias', 'm_ab_w_in', 'm_ab_w_out', 'm_s5_lam_re', 'm_s5_lam_im', 'm_s5_log_step', 'm_s5_b_re', 'm_s5_b_im', 'm_s5_c_re', 'm_s5_c_im', 'm_s5_d', 'm_s5_w_glu', 'm_c_w_qkv', 'm_c_w_out', 'm_c_rpb', 'm_norm_mix', 'm_norm_mlp', 'm_mlp_w1', 'm_mlp_w2', 'm_norm_final', 'v_t5_bias', 'v_ab_w_in', 'v_ab_w_out', 'v_s5_lam_re', 'v_s5_lam_im', 'v_s5_log_step', 'v_s5_b_re', 'v_s5_b_im', 'v_s5_c_re', 'v_s5_c_im', 'v_s5_d', 'v_s5_w_glu', 'v_c_w_qkv', 'v_c_w_out', 'v_c_rpb', 'v_norm_mix', 'v_norm_mlp', 'v_mlp_w1', 'v_mlp_w2', 'v_norm_final']
TWIN_OUTPUTS = ['loss', 'grad_x', 'grad_t5_bias', 'grad_ab_w_in', 'grad_ab_w_out', 'grad_s5_lam_re', 'grad_s5_lam_im', 'grad_s5_log_step', 'grad_s5_b_re', 'grad_s5_b_im', 'grad_s5_c_re', 'grad_s5_c_im', 'grad_s5_d', 'grad_s5_w_glu', 'grad_c_w_qkv', 'grad_c_w_out', 'grad_c_rpb', 'grad_norm_mix', 'grad_norm_mlp', 'grad_mlp_w1', 'grad_mlp_w2', 'grad_norm_final', 'delta_t5_bias', 'delta_ab_w_in', 'delta_ab_w_out', 'delta_s5_lam_re', 'delta_s5_lam_im', 'delta_s5_log_step', 'delta_s5_b_re', 'delta_s5_b_im', 'delta_s5_c_re', 'delta_s5_c_im', 'delta_s5_d', 'delta_s5_w_glu', 'delta_c_w_qkv', 'delta_c_w_out', 'delta_c_rpb', 'delta_norm_mix', 'delta_norm_mlp', 'delta_mlp_w1', 'delta_mlp_w2', 'delta_norm_final', 'new_m_t5_bias', 'new_m_ab_w_in', 'new_m_ab_w_out', 'new_m_s5_lam_re', 'new_m_s5_lam_im', 'new_m_s5_log_step', 'new_m_s5_b_re', 'new_m_s5_b_im', 'new_m_s5_c_re', 'new_m_s5_c_im', 'new_m_s5_d', 'new_m_s5_w_glu', 'new_m_c_w_qkv', 'new_m_c_w_out', 'new_m_c_rpb', 'new_m_norm_mix', 'new_m_norm_mlp', 'new_m_mlp_w1', 'new_m_mlp_w2', 'new_m_norm_final', 'new_v_t5_bias', 'new_v_ab_w_in', 'new_v_ab_w_out', 'new_v_s5_lam_re', 'new_v_s5_lam_im', 'new_v_s5_log_step', 'new_v_s5_b_re', 'new_v_s5_b_im', 'new_v_s5_c_re', 'new_v_s5_c_im', 'new_v_s5_d', 'new_v_s5_w_glu', 'new_v_c_w_qkv', 'new_v_c_w_out', 'new_v_c_rpb', 'new_v_norm_mix', 'new_v_norm_mlp', 'new_v_mlp_w1', 'new_v_mlp_w2', 'new_v_norm_final']
TWIN_LEAF_KINDS = {'loss': 'loss', 'grad_x': 'grad_x', 'grad_t5_bias': 'grad_w', 'grad_ab_w_in': 'grad_w', 'grad_ab_w_out': 'grad_w', 'grad_s5_lam_re': 'grad_w', 'grad_s5_lam_im': 'grad_w', 'grad_s5_log_step': 'grad_w', 'grad_s5_b_re': 'grad_w', 'grad_s5_b_im': 'grad_w', 'grad_s5_c_re': 'grad_w', 'grad_s5_c_im': 'grad_w', 'grad_s5_d': 'grad_w', 'grad_s5_w_glu': 'grad_w', 'grad_c_w_qkv': 'grad_w', 'grad_c_w_out': 'grad_w', 'grad_c_rpb': 'grad_w', 'grad_norm_mix': 'grad_w', 'grad_norm_mlp': 'grad_w', 'grad_mlp_w1': 'grad_w', 'grad_mlp_w2': 'grad_w', 'grad_norm_final': 'grad_w', 'delta_t5_bias': 'delta_w', 'delta_ab_w_in': 'delta_w', 'delta_ab_w_out': 'delta_w', 'delta_s5_lam_re': 'delta_w', 'delta_s5_lam_im': 'delta_w', 'delta_s5_log_step': 'delta_w', 'delta_s5_b_re': 'delta_w', 'delta_s5_b_im': 'delta_w', 'delta_s5_c_re': 'delta_w', 'delta_s5_c_im': 'delta_w', 'delta_s5_d': 'delta_w', 'delta_s5_w_glu': 'delta_w', 'delta_c_w_qkv': 'delta_w', 'delta_c_w_out': 'delta_w', 'delta_c_rpb': 'delta_w', 'delta_norm_mix': 'delta_w', 'delta_norm_mlp': 'delta_w', 'delta_mlp_w1': 'delta_w', 'delta_mlp_w2': 'delta_w', 'delta_norm_final': 'delta_w', 'new_m_t5_bias': 'new_m', 'new_m_ab_w_in': 'new_m', 'new_m_ab_w_out': 'new_m', 'new_m_s5_lam_re': 'new_m', 'new_m_s5_lam_im': 'new_m', 'new_m_s5_log_step': 'new_m', 'new_m_s5_b_re': 'new_m', 'new_m_s5_b_im': 'new_m', 'new_m_s5_c_re': 'new_m', 'new_m_s5_c_im': 'new_m', 'new_m_s5_d': 'new_m', 'new_m_s5_w_glu': 'new_m', 'new_m_c_w_qkv': 'new_m', 'new_m_c_w_out': 'new_m', 'new_m_c_rpb': 'new_m', 'new_m_norm_mix': 'new_m', 'new_m_norm_mlp': 'new_m', 'new_m_mlp_w1': 'new_m', 'new_m_mlp_w2': 'new_m', 'new_m_norm_final': 'new_m', 'new_v_t5_bias': 'new_v', 'new_v_ab_w_in': 'new_v', 'new_v_ab_w_out': 'new_v', 'new_v_s5_lam_re': 'new_v', 'new_v_s5_lam_im': 'new_v', 'new_v_s5_log_step': 'new_v', 'new_v_s5_b_re': 'new_v', 'new_v_s5_b_im': 'new_v', 'new_v_s5_c_re': 'new_v', 'new_v_s5_c_im': 'new_v', 'new_v_s5_d': 'new_v', 'new_v_s5_w_glu': 'new_v', 'new_v_c_w_qkv': 'new_v', 'new_v_c_w_out': 'new_v', 'new_v_c_rpb': 'new_v', 'new_v_norm_mix': 'new_v', 'new_v_norm_mlp': 'new_v', 'new_v_mlp_w1': 'new_v', 'new_v_mlp_w2': 'new_v', 'new_v_norm_final': 'new_v'}


def _forward(args):
    return _fwd_reference(*[args[k] for k in FWD_PARAMS])


def _output_shape():
    def fwd():
        inp = _fwd_setup_inputs(0)
        return _fwd_reference(*[inp[k] for k in FWD_PARAMS])
    out = _jax.eval_shape(fwd)
    return out.shape, out.dtype

N_MICROBATCH = 1
ADAM_LR = 0.001
ADAM_B1 = 0.9
ADAM_B2 = 0.999
ADAM_EPS = 1e-08
ADAM_WD = 0.01
ADAM_STEP = 10
PER_EXAMPLE_BATCH_AXIS = {'x': 0, 'loss_target': 0}
SHARED_INPUTS = []
_WEIGHT_DTYPES = {'t5_bias': _jnp.float32, 'ab_w_in': _jnp.float32, 'ab_w_out': _jnp.float32, 's5_lam_re': _jnp.float32, 's5_lam_im': _jnp.float32, 's5_log_step': _jnp.float32, 's5_b_re': _jnp.float32, 's5_b_im': _jnp.float32, 's5_c_re': _jnp.float32, 's5_c_im': _jnp.float32, 's5_d': _jnp.float32, 's5_w_glu': _jnp.float32, 'c_w_qkv': _jnp.float32, 'c_w_out': _jnp.float32, 'c_rpb': _jnp.float32, 'norm_mix': _jnp.float32, 'norm_mlp': _jnp.float32, 'mlp_w1': _jnp.float32, 'mlp_w2': _jnp.float32, 'norm_final': _jnp.float32}
MOMENT_SCALE = {'t5_bias': 3.905193e-02, 'ab_w_in': 3.376934e-02, 'ab_w_out': 5.732782e-02, 's5_lam_re': 4.198375e-03, 's5_lam_im': 4.367836e-03, 's5_log_step': 1.326892e+00, 's5_b_re': 2.809222e-03, 's5_b_im': 2.825315e-03, 's5_c_re': 3.697496e-03, 's5_c_im': 3.966291e-03, 's5_d': 6.261367e-02, 's5_w_glu': 1.369335e-02, 'c_w_qkv': 4.573063e-02, 'c_w_out': 8.298801e-02, 'c_rpb': 6.662483e-03, 'norm_mix': 6.147582e-02, 'norm_mlp': 1.031430e-01, 'mlp_w1': 5.185886e-02, 'mlp_w2': 1.371136e-01, 'norm_final': 3.417663e+01}


def _to_microbatches(a, axis):
    t = _jnp.moveaxis(a, axis, 0)
    t = t.reshape((N_MICROBATCH, t.shape[0] // N_MICROBATCH) + t.shape[1:])
    return _jnp.moveaxis(t, 1, axis + 1)


def setup_inputs(seed: int = 0) -> dict:
    inp = _fwd_setup_inputs(seed)
    key = _jax.random.fold_in(_jax.random.key(seed), 7919)
    shape, _ = _output_shape()
    out = dict(inp)
    out["loss_target"] = _jax.random.normal(_jax.random.fold_in(key, 0), shape, _jnp.float32)
    for i, name in enumerate(TWIN_WEIGHTS):
        w = inp[name].astype(_jnp.float32)
        if MOMENT_SCALE is None:
            s = _jnp.sqrt(_jnp.mean(_jnp.square(w)) + 1e-30)
        else:
            s = MOMENT_SCALE[name]
        km, kv = _jax.random.split(_jax.random.fold_in(key, i + 1))
        out[name] = w
        out["m_" + name] = s * _jax.random.normal(km, w.shape, _jnp.float32)
        out["v_" + name] = (s * s) * _jax.random.uniform(kv, w.shape, _jnp.float32, 0.5, 1.5)
    if N_MICROBATCH > 1:
        for name, axis in PER_EXAMPLE_BATCH_AXIS.items():
            out[name] = _to_microbatches(out[name], axis)
    return {'x': out['x'], 't5_bias': out['t5_bias'], 'ab_w_in': out['ab_w_in'], 'ab_w_out': out['ab_w_out'], 's5_lam_re': out['s5_lam_re'], 's5_lam_im': out['s5_lam_im'], 's5_log_step': out['s5_log_step'], 's5_b_re': out['s5_b_re'], 's5_b_im': out['s5_b_im'], 's5_c_re': out['s5_c_re'], 's5_c_im': out['s5_c_im'], 's5_d': out['s5_d'], 's5_w_glu': out['s5_w_glu'], 'c_w_qkv': out['c_w_qkv'], 'c_w_out': out['c_w_out'], 'c_rpb': out['c_rpb'], 'norm_mix': out['norm_mix'], 'norm_mlp': out['norm_mlp'], 'mlp_w1': out['mlp_w1'], 'mlp_w2': out['mlp_w2'], 'norm_final': out['norm_final'], 'loss_target': out['loss_target'], 'm_t5_bias': out['m_t5_bias'], 'm_ab_w_in': out['m_ab_w_in'], 'm_ab_w_out': out['m_ab_w_out'], 'm_s5_lam_re': out['m_s5_lam_re'], 'm_s5_lam_im': out['m_s5_lam_im'], 'm_s5_log_step': out['m_s5_log_step'], 'm_s5_b_re': out['m_s5_b_re'], 'm_s5_b_im': out['m_s5_b_im'], 'm_s5_c_re': out['m_s5_c_re'], 'm_s5_c_im': out['m_s5_c_im'], 'm_s5_d': out['m_s5_d'], 'm_s5_w_glu': out['m_s5_w_glu'], 'm_c_w_qkv': out['m_c_w_qkv'], 'm_c_w_out': out['m_c_w_out'], 'm_c_rpb': out['m_c_rpb'], 'm_norm_mix': out['m_norm_mix'], 'm_norm_mlp': out['m_norm_mlp'], 'm_mlp_w1': out['m_mlp_w1'], 'm_mlp_w2': out['m_mlp_w2'], 'm_norm_final': out['m_norm_final'], 'v_t5_bias': out['v_t5_bias'], 'v_ab_w_in': out['v_ab_w_in'], 'v_ab_w_out': out['v_ab_w_out'], 'v_s5_lam_re': out['v_s5_lam_re'], 'v_s5_lam_im': out['v_s5_lam_im'], 'v_s5_log_step': out['v_s5_log_step'], 'v_s5_b_re': out['v_s5_b_re'], 'v_s5_b_im': out['v_s5_b_im'], 'v_s5_c_re': out['v_s5_c_re'], 'v_s5_c_im': out['v_s5_c_im'], 'v_s5_d': out['v_s5_d'], 'v_s5_w_glu': out['v_s5_w_glu'], 'v_c_w_qkv': out['v_c_w_qkv'], 'v_c_w_out': out['v_c_w_out'], 'v_c_rpb': out['v_c_rpb'], 'v_norm_mix': out['v_norm_mix'], 'v_norm_mlp': out['v_norm_mlp'], 'v_mlp_w1': out['v_mlp_w1'], 'v_mlp_w2': out['v_mlp_w2'], 'v_norm_final': out['v_norm_final']}


def _loss(weights, diff, rest, loss_target):
    with _jax.named_scope("forward"):
        args = {**rest, TWIN_DIFF_INPUT: diff, **{k: w.astype(_WEIGHT_DTYPES[k]) for k, w in weights.items()}}
        y = _forward(args)
    with _jax.named_scope("loss_head"):
        err = _jnp.square(y.astype(_jnp.float32) - loss_target)
        return 0.5 * _jnp.sum(_jnp.mean(err, axis=-1)) if err.ndim else 0.5 * err


def _adamw(w, g, m, v):
    m = ADAM_B1 * m + (1.0 - ADAM_B1) * g
    v = ADAM_B2 * v + (1.0 - ADAM_B2) * _jnp.square(g)
    m_hat = m / (1.0 - ADAM_B1 ** ADAM_STEP)
    v_hat = v / (1.0 - ADAM_B2 ** ADAM_STEP)
    delta = -ADAM_LR * (m_hat / (_jnp.sqrt(v_hat) + ADAM_EPS) + ADAM_WD * w)
    return delta, m, v


def reference(x, t5_bias, ab_w_in, ab_w_out, s5_lam_re, s5_lam_im, s5_log_step, s5_b_re, s5_b_im, s5_c_re, s5_c_im, s5_d, s5_w_glu, c_w_qkv, c_w_out, c_rpb, norm_mix, norm_mlp, mlp_w1, mlp_w2, norm_final, loss_target, m_t5_bias, m_ab_w_in, m_ab_w_out, m_s5_lam_re, m_s5_lam_im, m_s5_log_step, m_s5_b_re, m_s5_b_im, m_s5_c_re, m_s5_c_im, m_s5_d, m_s5_w_glu, m_c_w_qkv, m_c_w_out, m_c_rpb, m_norm_mix, m_norm_mlp, m_mlp_w1, m_mlp_w2, m_norm_final, v_t5_bias, v_ab_w_in, v_ab_w_out, v_s5_lam_re, v_s5_lam_im, v_s5_log_step, v_s5_b_re, v_s5_b_im, v_s5_c_re, v_s5_c_im, v_s5_d, v_s5_w_glu, v_c_w_qkv, v_c_w_out, v_c_rpb, v_norm_mix, v_norm_mlp, v_mlp_w1, v_mlp_w2, v_norm_final):
    given = dict(x=x, t5_bias=t5_bias, ab_w_in=ab_w_in, ab_w_out=ab_w_out, s5_lam_re=s5_lam_re, s5_lam_im=s5_lam_im, s5_log_step=s5_log_step, s5_b_re=s5_b_re, s5_b_im=s5_b_im, s5_c_re=s5_c_re, s5_c_im=s5_c_im, s5_d=s5_d, s5_w_glu=s5_w_glu, c_w_qkv=c_w_qkv, c_w_out=c_w_out, c_rpb=c_rpb, norm_mix=norm_mix, norm_mlp=norm_mlp, mlp_w1=mlp_w1, mlp_w2=mlp_w2, norm_final=norm_final, loss_target=loss_target, m_t5_bias=m_t5_bias, m_ab_w_in=m_ab_w_in, m_ab_w_out=m_ab_w_out, m_s5_lam_re=m_s5_lam_re, m_s5_lam_im=m_s5_lam_im, m_s5_log_step=m_s5_log_step, m_s5_b_re=m_s5_b_re, m_s5_b_im=m_s5_b_im, m_s5_c_re=m_s5_c_re, m_s5_c_im=m_s5_c_im, m_s5_d=m_s5_d, m_s5_w_glu=m_s5_w_glu, m_c_w_qkv=m_c_w_qkv, m_c_w_out=m_c_w_out, m_c_rpb=m_c_rpb, m_norm_mix=m_norm_mix, m_norm_mlp=m_norm_mlp, m_mlp_w1=m_mlp_w1, m_mlp_w2=m_mlp_w2, m_norm_final=m_norm_final, v_t5_bias=v_t5_bias, v_ab_w_in=v_ab_w_in, v_ab_w_out=v_ab_w_out, v_s5_lam_re=v_s5_lam_re, v_s5_lam_im=v_s5_lam_im, v_s5_log_step=v_s5_log_step, v_s5_b_re=v_s5_b_re, v_s5_b_im=v_s5_b_im, v_s5_c_re=v_s5_c_re, v_s5_c_im=v_s5_c_im, v_s5_d=v_s5_d, v_s5_w_glu=v_s5_w_glu, v_c_w_qkv=v_c_w_qkv, v_c_w_out=v_c_w_out, v_c_rpb=v_c_rpb, v_norm_mix=v_norm_mix, v_norm_mlp=v_norm_mlp, v_mlp_w1=v_mlp_w1, v_mlp_w2=v_mlp_w2, v_norm_final=v_norm_final)
    weights = {n: given[n] for n in TWIN_WEIGHTS}
    shared = {n: given[n] for n in SHARED_INPUTS}
    per_example = {n: given[n] for n in ['x']}
    grad_fn = _jax.value_and_grad(_loss, argnums=(0, 1))

    def one_microbatch(ex, loss_target):
        ex = dict(ex)
        diff = ex.pop(TWIN_DIFF_INPUT)
        return grad_fn(weights, diff, {**shared, **ex}, loss_target)

    if N_MICROBATCH == 1:
        loss, (grad_w, grad_x) = one_microbatch(per_example, given["loss_target"])
    else:
        def body(carry, xs):
            loss_sum, grad_sum = carry
            l_k, (gw_k, gx_k) = one_microbatch(xs[0], xs[1])
            with _jax.named_scope("update"):
                return (loss_sum + l_k, _jax.tree.map(_jnp.add, grad_sum, gw_k)), gx_k

        init = (_jnp.zeros((), _jnp.float32), _jax.tree.map(_jnp.zeros_like, weights))
        (loss, grad_w), grad_x = _jax.lax.scan(body, init, (per_example, given["loss_target"]))
    with _jax.named_scope("update"):
        delta_w, new_m, new_v = {}, {}, {}
        for n in TWIN_WEIGHTS:
            delta_w[n], new_m[n], new_v[n] = _adamw(weights[n], grad_w[n], given["m_" + n], given["v_" + n])
    return (loss, grad_x, *[grad_w[n] for n in TWIN_WEIGHTS], *[delta_w[n] for n in TWIN_WEIGHTS],
            *[new_m[n] for n in TWIN_WEIGHTS], *[new_v[n] for n in TWIN_WEIGHTS])
```

```python
import math

import jax
import jax.numpy as jnp
import numpy as np
from jax import lax
from jax.experimental import pallas as pl
from jax.experimental.pallas import tpu as pltpu

F32 = jnp.float32
BF16 = jnp.bfloat16

N_DEV = 8
HEAD_DIM = 128
LANES = 128
QBLOCK = 128
DIL_HALF = 64
DILATED_BRANCHES = ((128, 1), (512, 4), (2048, 16))
S5_GROUP = 16
S5_GROUPS_PER_BLOCK = LANES // S5_GROUP
S5_CHUNK = 512
SUBLANES = 8
GRID_W = 64
NA_ROWS = 8
NA_COLS = 16
NA_ROWBLOCK = 8
T5_BUCKETS = 32
T5_MAX_DISTANCE = 1024
RMS_EPS = 1e-6
NEG_INF = -1e30
ADAM_LR = 0.001
ADAM_B1 = 0.9
ADAM_B2 = 0.999
ADAM_EPS = 1e-08
ADAM_WD = 0.01
ADAM_STEP = 10
VMEM_LIMIT_BYTES = 56 * 1024 * 1024
MESH = pl.DeviceIdType.MESH


def _cparams(sem=None):
    return pltpu.CompilerParams(dimension_semantics=sem, vmem_limit_bytes=VMEM_LIMIT_BYTES)


def _tile(dim, pref):
    t = min(dim, pref)
    assert dim % t == 0, (dim, pref)
    return t


def _dot(a, b, ca, cb):
    return lax.dot_general(a, b, (((ca,), (cb,)), ((), ())), preferred_element_type=F32)


def _nn(a, b):
    return _dot(a, b, 1, 0)


def _nt(a, b):
    return _dot(a, b, 1, 1)


def _tn(a, b):
    return _dot(a, b, 0, 0)


def _mm_call(name, a, b, a_spec, b_spec, grid, nk, out_shapes, out_specs, acc_shape,
             ta=False, tb=False, a_fn=None, epi=None, extras=(), extra_specs=()):
    ne, no = len(extras), len(out_shapes)

    def body(a_ref, b_ref, *rest):
        ex, outs, acc = rest[:ne], rest[ne:ne + no], rest[ne + no]
        k = pl.program_id(2)

        @pl.when(k == 0)
        def _():
            acc[...] = jnp.zeros_like(acc)

        av = a_ref[...]
        if a_fn is not None:
            av = a_fn(av)
        acc[...] += _dot(av.astype(BF16), b_ref[...].astype(BF16), 0 if ta else 1, 1 if tb else 0)

        @pl.when(k == nk - 1)
        def _():
            r = acc[...]
            res = epi(r, *[e[...] for e in ex]) if epi is not None else (r,)
            for o, v in zip(outs, res):
                o[...] = v.astype(o.dtype)

    return pl.pallas_call(
        body, grid=grid, in_specs=[a_spec, b_spec, *extra_specs], out_specs=list(out_specs),
        out_shape=list(out_shapes), scratch_shapes=[pltpu.VMEM(acc_shape, F32)],
        compiler_params=_cparams(("parallel", "parallel", "arbitrary")), name=name,
    )(a, b, *extras)


def _mm(name, a, b, *, ta=False, tb=False, b_layer=None, a_fn=None, epi=None, extras=(), extra_cols=None,
        out_dtypes=(F32,), tm=1024, tn=1024, tk=2048):
    m, kdim = (a.shape[1], a.shape[0]) if ta else a.shape
    bs = b.shape[-2:]
    n = bs[0] if tb else bs[1]
    assert (bs[1] if tb else bs[0]) == kdim, (a.shape, b.shape)
    tm, tn, tk = _tile(m, tm), _tile(n, tn), _tile(kdim, tk)
    a_spec = pl.BlockSpec((tk, tm), lambda i, j, k: (k, i)) if ta else pl.BlockSpec((tm, tk), lambda i, j, k: (i, k))
    if b_layer is None:
        b_spec = pl.BlockSpec((tn, tk), lambda i, j, k: (j, k)) if tb else pl.BlockSpec((tk, tn), lambda i, j, k: (k, j))
    else:
        lyr = b_layer
        b_spec = (pl.BlockSpec((None, tn, tk), lambda i, j, k: (lyr, j, k)) if tb
                  else pl.BlockSpec((None, tk, tn), lambda i, j, k: (lyr, k, j)))
    o_spec = pl.BlockSpec((tm, tn), lambda i, j, k: (i, j))
    extra_cols = extra_cols or (0,) * len(extras)
    especs = []
    for c0 in extra_cols:
        assert c0 % tn == 0
        cb = c0 // tn
        especs.append(pl.BlockSpec((tm, tn), lambda i, j, k, cb=cb: (i, cb + j)))
    return _mm_call(name, a, b, a_spec, b_spec, (m // tm, n // tn, kdim // tk), kdim // tk,
                    [jax.ShapeDtypeStruct((m, n), d) for d in out_dtypes], [o_spec] * len(out_dtypes), (tm, tn),
                    ta=ta, tb=tb, a_fn=a_fn, epi=epi, extras=extras, extra_specs=especs)


def _mm_cols(name, a, b4, layer, *, out_dtype=BF16, tm=1024, tk=2048):
    m, kdim = a.shape
    n = b4.shape[-1]
    tm, tk = _tile(m, tm), _tile(kdim, tk)
    return _mm_call(name, a, b4, pl.BlockSpec((tm, tk), lambda i, j, k: (i, k)),
                    pl.BlockSpec((None, None, tk, n), lambda i, j, k: (layer, j, k, 0)),
                    (m // tm, N_DEV, kdim // tk), kdim // tk,
                    [jax.ShapeDtypeStruct((m, N_DEV * n), out_dtype)], [pl.BlockSpec((tm, n), lambda i, j, k: (i, j))],
                    (tm, n))[0]


def _mm_cols_t(name, a, b4, layer, *, out_dtype=BF16, tm=1024, tn=1024):
    m = a.shape[0]
    kout, n = b4.shape[-2:]
    tm, tn = _tile(m, tm), _tile(kout, tn)
    return _mm_call(name, a, b4, pl.BlockSpec((tm, n), lambda i, j, k: (i, k)),
                    pl.BlockSpec((None, None, tn, n), lambda i, j, k: (layer, k, j, 0)),
                    (m // tm, kout // tn, N_DEV), N_DEV,
                    [jax.ShapeDtypeStruct((m, kout), out_dtype)], [pl.BlockSpec((tm, tn), lambda i, j, k: (i, j))],
                    (tm, tn), tb=True)[0]


def _mm_cols_grad(name, a, dy, *, tm=1024, tk=1024):
    s, kout = a.shape
    n = dy.shape[1] // N_DEV
    tm, tk = _tile(kout, tm), _tile(s, tk)
    return _mm_call(name, a, dy, pl.BlockSpec((tk, tm), lambda i, j, k: (k, i)),
                    pl.BlockSpec((tk, n), lambda i, j, k: (k, j)),
                    (kout // tm, N_DEV, s // tk), s // tk,
                    [jax.ShapeDtypeStruct((N_DEV, kout, n), F32)], [pl.BlockSpec((None, tm, n), lambda i, j, k: (j, i, 0))],
                    (tm, n), ta=True)[0]


_GELU_C = math.sqrt(2.0 / math.pi)


def _gelu(x):
    return 0.5 * x * (1.0 + jnp.tanh(_GELU_C * (x + 0.044715 * x * x * x)))


def _gelu_grad(x):
    t = jnp.tanh(_GELU_C * (x + 0.044715 * x * x * x))
    return 0.5 * (1.0 + t) + 0.5 * x * (1.0 - t * t) * _GELU_C * (1.0 + 3.0 * 0.044715 * x * x)


def _relu_sq(x):
    r = jnp.maximum(x.astype(F32), 0.0)
    return r * r


def _rms_fwd(x, g, name):
    s, d = x.shape
    tr = _tile(s, 512)

    def body(x_ref, g_ref, o_ref):
        xv = x_ref[...]
        r = lax.rsqrt(jnp.mean(xv * xv, axis=-1, keepdims=True) + RMS_EPS)
        o_ref[...] = (xv * r * g_ref[...]).astype(BF16)

    return pl.pallas_call(
        body, grid=(s // tr,),
        in_specs=[pl.BlockSpec((tr, d), lambda i: (i, 0)), pl.BlockSpec((1, d), lambda i: (0, 0))],
        out_specs=pl.BlockSpec((tr, d), lambda i: (i, 0)), out_shape=jax.ShapeDtypeStruct((s, d), BF16),
        compiler_params=_cparams(("parallel",)), name=name)(x, g.reshape(1, d))


def _rms_bwd(x, g, dy, dres, name):
    s, d = x.shape
    tr = _tile(s, 512)

    def body(x_ref, g_ref, dy_ref, dres_ref, dx_ref, dg_ref):
        @pl.when(pl.program_id(0) == 0)
        def _():
            dg_ref[...] = jnp.zeros_like(dg_ref)

        xv = x_ref[...]
        dyv = dy_ref[...].astype(F32)
        r = lax.rsqrt(jnp.mean(xv * xv, axis=-1, keepdims=True) + RMS_EPS)
        xh = xv * r
        gdy = dyv * g_ref[...]
        dx_ref[...] = dres_ref[...] + r * (gdy - xh * jnp.mean(gdy * xh, axis=-1, keepdims=True))
        dg_ref[...] += jnp.sum(dyv * xh, axis=0, keepdims=True)

    return pl.pallas_call(
        body, grid=(s // tr,),
        in_specs=[pl.BlockSpec((tr, d), lambda i: (i, 0)), pl.BlockSpec((1, d), lambda i: (0, 0)),
                  pl.BlockSpec((tr, d), lambda i: (i, 0)), pl.BlockSpec((tr, d), lambda i: (i, 0))],
        out_specs=[pl.BlockSpec((tr, d), lambda i: (i, 0)), pl.BlockSpec((1, d), lambda i: (0, 0))],
        out_shape=[jax.ShapeDtypeStruct((s, d), F32), jax.ShapeDtypeStruct((1, d), F32)],
        compiler_params=_cparams(("arbitrary",)), name=name)(x, g.reshape(1, d), dy, dres)


def _final_loss(x, g, target):
    s, d = x.shape
    tr = _tile(s, 512)

    def body(x_ref, g_ref, t_ref, loss_ref, dx_ref, dg_ref):
        @pl.when(pl.program_id(0) == 0)
        def _():
            dg_ref[...] = jnp.zeros_like(dg_ref)
            loss_ref[...] = jnp.zeros_like(loss_ref)

        xv = x_ref[...]
        gv = g_ref[...]
        r = lax.rsqrt(jnp.mean(xv * xv, axis=-1, keepdims=True) + RMS_EPS)
        xh = xv * r
        err = xh * gv - t_ref[...]
        loss_ref[...] += jnp.sum(jnp.sum(err * err, axis=-1, keepdims=True), axis=0, keepdims=True)
        dyv = err * (1.0 / d)
        gdy = dyv * gv
        dx_ref[...] = r * (gdy - xh * jnp.mean(gdy * xh, axis=-1, keepdims=True))
        dg_ref[...] += jnp.sum(dyv * xh, axis=0, keepdims=True)

    return pl.pallas_call(
        body, grid=(s // tr,),
        in_specs=[pl.BlockSpec((tr, d), lambda i: (i, 0)), pl.BlockSpec((1, d), lambda i: (0, 0)),
                  pl.BlockSpec((tr, d), lambda i: (i, 0))],
        out_specs=[pl.BlockSpec((1, 1), lambda i: (0, 0)), pl.BlockSpec((tr, d), lambda i: (i, 0)),
                   pl.BlockSpec((1, d), lambda i: (0, 0))],
        out_shape=[jax.ShapeDtypeStruct((1, 1), F32), jax.ShapeDtypeStruct((s, d), F32),
                   jax.ShapeDtypeStruct((1, d), F32)],
        compiler_params=_cparams(("arbitrary",)), name="final_norm_loss")(x, g.reshape(1, d), target)


def _t5_bucket(rel):
    half = T5_BUCKETS // 2
    max_exact = half // 2
    n = jnp.abs(rel)
    nf = jnp.maximum(n, 1).astype(F32)
    large = max_exact + (jnp.log(nf / max_exact) / math.log(T5_MAX_DISTANCE / max_exact)
                         * (half - max_exact)).astype(jnp.int32)
    large = jnp.minimum(large, half - 1)
    return jnp.where(rel > 0, half, 0) + jnp.where(n < max_exact, n, large)


def _dil_offsets():
    i = jnp.arange(QBLOCK)[:, None]
    kk = jnp.arange(2 * QBLOCK)[None, :]
    return kk - DIL_HALF - i, (jnp.arange(QBLOCK)[None, :] + DIL_HALF) - jnp.arange(2 * QBLOCK)[:, None]


def _dil_bias(t5, dil):
    off1, off2 = _dil_offsets()
    b1 = jnp.transpose(t5[_t5_bucket(off1 * dil)], (2, 0, 1)).astype(F32)
    b2 = jnp.transpose(t5[_t5_bucket(off2 * dil)], (2, 0, 1)).astype(F32)
    return b1, b2


def _window(p, c, n):
    return jnp.concatenate([p[pl.ds(DIL_HALF, DIL_HALF), :], c[...], n[pl.ds(0, DIL_HALF), :]], axis=0)


def _dil_specs(ncol_blocks, first_col_block, nb):
    def spec(dn):
        return pl.BlockSpec((QBLOCK, HEAD_DIM),
                            lambda h, r, n: (jnp.clip(n + dn, 0, nb - 1), r * ncol_blocks + first_col_block + h))
    return [spec(-1), spec(0), spec(1)]


def _dil_fwd(proj, bias1, dil, ah, name):
    s, wtot = proj.shape
    ln = s // dil
    nb = ln // QBLOCK
    assert nb * QBLOCK * dil == s
    wb = wtot // LANES
    aw = ah * HEAD_DIM
    scale = 1.0 / math.sqrt(HEAD_DIM)
    pv = proj.reshape(ln, dil * wtot)

    def body(q_ref, kp, kc, kn, vp, vc, vn, b_ref, o_ref, lse_ref):
        n = pl.program_id(2)
        kw, vw = _window(kp, kc, kn), _window(vp, vc, vn)
        sc = _nt(q_ref[...], kw) * scale + b_ref[...]
        ii = lax.broadcasted_iota(jnp.int32, sc.shape, 0)
        jj = lax.broadcasted_iota(jnp.int32, sc.shape, 1)
        kpos = n * QBLOCK + jj - DIL_HALF
        valid = (jnp.abs(jj - DIL_HALF - ii) <= DIL_HALF) & (kpos >= 0) & (kpos < ln)
        sc = jnp.where(valid, sc, NEG_INF)
        m = jnp.max(sc, axis=-1, keepdims=True)
        p = jnp.exp(sc - m)
        l = jnp.sum(p, axis=-1, keepdims=True)
        o_ref[...] = (_nn(p.astype(BF16), vw) / l).astype(BF16)
        lse_ref[...] = jnp.broadcast_to(m + jnp.log(l), lse_ref.shape)

    qspec = _dil_specs(wb, 0, nb)[1]
    ospec = pl.BlockSpec((QBLOCK, HEAD_DIM), lambda h, r, n: (n, r * ah + h))
    o, lse = pl.pallas_call(
        body, grid=(ah, dil, nb),
        in_specs=[qspec, *_dil_specs(wb, ah, nb), *_dil_specs(wb, 2 * ah, nb),
                  pl.BlockSpec((None, QBLOCK, 2 * QBLOCK), lambda h, r, n: (h, 0, 0))],
        out_specs=[ospec, ospec],
        out_shape=[jax.ShapeDtypeStruct((ln, dil * aw), BF16), jax.ShapeDtypeStruct((ln, dil * aw), F32)],
        compiler_params=_cparams(("parallel", "parallel", "parallel")), name=name,
    )(pv, pv, pv, pv, pv, pv, pv, bias1)
    return o.reshape(s, aw), lse.reshape(s, aw)


def _dil_merge(outs, lses):
    s, aw = outs[0].shape
    tr = _tile(s, 512)

    def body(o1, l1, o2, l2, o3, l3, o_ref, lse_ref):
        a, b, c = l1[...], l2[...], l3[...]
        m = jnp.maximum(jnp.maximum(a, b), c)
        w1, w2, w3 = jnp.exp(a - m), jnp.exp(b - m), jnp.exp(c - m)
        tot = w1 + w2 + w3
        o_ref[...] = ((w1 * o1[...].astype(F32) + w2 * o2[...].astype(F32) + w3 * o3[...].astype(F32)) / tot).astype(BF16)
        lse_ref[...] = m + jnp.log(tot)

    spec = pl.BlockSpec((tr, aw), lambda i: (i, 0))
    return pl.pallas_call(
        body, grid=(s // tr,), in_specs=[spec] * 6, out_specs=[spec, spec],
        out_shape=[jax.ShapeDtypeStruct((s, aw), BF16), jax.ShapeDtypeStruct((s, aw), F32)],
        compiler_params=_cparams(("parallel",)), name="dil_merge",
    )(outs[0], lses[0], outs[1], lses[1], outs[2], lses[2])


def _head_delta(do, do_col0, o, name):
    s, w = o.shape
    tr = _tile(s, 512)
    cb = do_col0 // HEAD_DIM

    def body(do_ref, o_ref, d_ref):
        d = jnp.sum(do_ref[...].astype(F32) * o_ref[...].astype(F32), axis=-1, keepdims=True)
        d_ref[...] = jnp.broadcast_to(d, d_ref.shape)

    return pl.pallas_call(
        body, grid=(s // tr, w // HEAD_DIM),
        in_specs=[pl.BlockSpec((tr, HEAD_DIM), lambda i, h: (i, cb + h)), pl.BlockSpec((tr, HEAD_DIM), lambda i, h: (i, h))],
        out_specs=pl.BlockSpec((tr, HEAD_DIM), lambda i, h: (i, h)), out_shape=jax.ShapeDtypeStruct((s, w), F32),
        compiler_params=_cparams(("parallel", "parallel")), name=name)(do, o)


def _dil_bwd(proj, dmerged, lse, delta, bias1, bias2, dil, ah, name):
    s, wtot = proj.shape
    ln = s // dil
    nb = ln // QBLOCK
    wb = wtot // LANES
    wd = dmerged.shape[1] // LANES
    aw = ah * HEAD_DIM
    scale = 1.0 / math.sqrt(HEAD_DIM)
    pv = proj.reshape(ln, dil * wtot)
    dov = dmerged.reshape(ln, dil * dmerged.shape[1])
    lv = lse.reshape(ln, dil * aw)
    dlv = delta.reshape(ln, dil * aw)

    def body(qp, qc, qn, kp, kc, kn, vp, vc, vn, dop, doc, don, lp, lc, lnx, dp, dc, dn, b1_ref, b2_ref,
             dq_ref, dk_ref, dv_ref, db_ref):
        n = pl.program_id(2)

        @pl.when((pl.program_id(1) == 0) & (n == 0))
        def _():
            db_ref[...] = jnp.zeros_like(db_ref)

        kw, vw = _window(kp, kc, kn), _window(vp, vc, vn)
        do = doc[...]
        sc = _nt(qc[...], kw) * scale + b1_ref[...]
        ii = lax.broadcasted_iota(jnp.int32, sc.shape, 0)
        jj = lax.broadcasted_iota(jnp.int32, sc.shape, 1)
        kpos = n * QBLOCK + jj - DIL_HALF
        valid = (jnp.abs(jj - DIL_HALF - ii) <= DIL_HALF) & (kpos >= 0) & (kpos < ln)
        lse2 = jnp.concatenate([lc[...], lc[...]], axis=1)
        p = jnp.where(valid, jnp.exp(jnp.where(valid, sc - lse2, 0.0)), 0.0)
        ds = p * (_nt(do, vw) - jnp.concatenate([dc[...], dc[...]], axis=1))
        dq_ref[...] = _nn(ds.astype(BF16), kw) * scale
        db_ref[...] += ds

        qw, dow = _window(qp, qc, qn), _window(dop, doc, don)
        sc2 = _nt(qw, kc[...]) * scale + b2_ref[...]
        ww = lax.broadcasted_iota(jnp.int32, sc2.shape, 0)
        cc = lax.broadcasted_iota(jnp.int32, sc2.shape, 1)
        qpos = n * QBLOCK - DIL_HALF + ww
        valid2 = (jnp.abs(cc + DIL_HALF - ww) <= DIL_HALF) & (qpos >= 0) & (qpos < ln)
        p2 = jnp.where(valid2, jnp.exp(jnp.where(valid2, sc2 - _window(lp, lc, lnx), 0.0)), 0.0)
        dv_ref[...] = _tn(p2.astype(BF16), dow)
        ds2 = p2 * (_nt(dow, vc[...]) - _window(dp, dc, dn))
        dk_ref[...] = _tn(ds2.astype(BF16), qw) * scale

    ospec = pl.BlockSpec((QBLOCK, HEAD_DIM), lambda h, r, n: (n, r * ah + h))
    dq, dk, dv, db = pl.pallas_call(
        body, grid=(ah, dil, nb),
        in_specs=[*_dil_specs(wb, 0, nb), *_dil_specs(wb, ah, nb), *_dil_specs(wb, 2 * ah, nb),
                  *_dil_specs(wd, 0, nb), *_dil_specs(ah, 0, nb), *_dil_specs(ah, 0, nb),
                  pl.BlockSpec((None, QBLOCK, 2 * QBLOCK), lambda h, r, n: (h, 0, 0)),
                  pl.BlockSpec((None, 2 * QBLOCK, QBLOCK), lambda h, r, n: (h, 0, 0))],
        out_specs=[ospec, ospec, ospec, pl.BlockSpec((None, QBLOCK, 2 * QBLOCK), lambda h, r, n: (h, 0, 0))],
        out_shape=[jax.ShapeDtypeStruct((ln, dil * aw), F32)] * 3 + [jax.ShapeDtypeStruct((ah, QBLOCK, 2 * QBLOCK), F32)],
        compiler_params=_cparams(("parallel", "arbitrary", "arbitrary")), name=name,
    )(pv, pv, pv, pv, pv, pv, pv, pv, pv, dov, dov, dov, lv, lv, lv, dlv, dlv, dlv, bias1, bias2)
    return dq.reshape(s, aw), dk.reshape(s, aw), dv.reshape(s, aw), db


def _bucket_sum(vals, onehot, name):
    r, n = vals.shape
    b = onehot.shape[1]

    def body(v_ref, oh_ref, o_ref):
        o_ref[...] = lax.dot_general(v_ref[...], oh_ref[...], (((1,), (0,)), ((), ())),
                                     precision=lax.Precision.HIGHEST, preferred_element_type=F32)

    return pl.pallas_call(body, out_shape=jax.ShapeDtypeStruct((r, b), F32), compiler_params=_cparams(), name=name)(vals, onehot)


def _t5_grad(dbias, dil):
    ah = dbias.shape[0]
    off1, _ = _dil_offsets()
    bucket = _t5_bucket(off1 * dil).reshape(-1)
    inside = (jnp.abs(off1) <= DIL_HALF).reshape(-1)
    onehot = ((bucket[:, None] == jnp.arange(LANES)[None, :]) & inside[:, None]).astype(F32)
    rows = max(ah, SUBLANES)
    vals = jnp.zeros((rows, onehot.shape[0]), F32).at[:ah].set(dbias.reshape(ah, -1))
    return _bucket_sum(vals, onehot, f"t5_grad_d{dil}")[:ah, :T5_BUCKETS].T


def _na_bias(rpb):
    c = jnp.arange(GRID_W)
    col_start = jnp.clip(c - NA_COLS // 2, 0, GRID_W - NA_COLS)
    col_ok = (c[None, :] >= col_start[:, None]) & (c[None, :] < col_start[:, None] + NA_COLS)
    col_idx = jnp.clip(c[None, :] - c[:, None] + NA_COLS - 1, 0, 2 * NA_COLS - 2)
    var = jnp.arange(NA_ROWS)
    row_idx = jnp.arange(NA_ROWS)[None, :] - var[:, None] + NA_ROWS - 1
    b = rpb[:, row_idx[:, None, :, None], col_idx[None, :, None, :]]
    b = jnp.where(col_ok[None, None, :, None, :], b.astype(F32), NEG_INF)
    return b.reshape(rpb.shape[0], NA_ROWS, GRID_W, NA_ROWS * GRID_W)


def _na_row(r, rows):
    rs = jnp.clip(r - NA_ROWS // 2, 0, rows - NA_ROWS)
    return pl.multiple_of(rs * GRID_W, GRID_W), r - rs


def _na_fwd(qkv, bias, ch, name):
    s = qkv.shape[0]
    rows = s // GRID_W
    assert rows >= NA_ROWS and rows % NA_ROWBLOCK == 0
    cw = ch * HEAD_DIM
    tq = NA_ROWBLOCK * GRID_W
    win = NA_ROWS * GRID_W
    scale = 1.0 / math.sqrt(HEAD_DIM)

    def body(q_ref, k_ref, v_ref, b_ref, o_ref, lse_ref):
        rb = pl.program_id(1)
        for i in range(NA_ROWBLOCK):
            st, var = _na_row(rb * NA_ROWBLOCK + i, rows)
            kw, vw = k_ref[pl.ds(st, win), :], v_ref[pl.ds(st, win), :]
            qs = pl.ds(i * GRID_W, GRID_W)
            sc = _nt(q_ref[qs, :], kw) * scale + b_ref[var]
            m = jnp.max(sc, axis=-1, keepdims=True)
            p = jnp.exp(sc - m)
            l = jnp.sum(p, axis=-1, keepdims=True)
            o_ref[qs, :] = (_nn(p.astype(BF16), vw) / l).astype(BF16)
            lse_ref[qs, :] = jnp.broadcast_to(m + jnp.log(l), (GRID_W, HEAD_DIM))

    ospec = pl.BlockSpec((tq, HEAD_DIM), lambda h, rb: (rb, h))
    return pl.pallas_call(
        body, grid=(ch, rows // NA_ROWBLOCK),
        in_specs=[pl.BlockSpec((tq, HEAD_DIM), lambda h, rb: (rb, h)),
                  pl.BlockSpec((s, HEAD_DIM), lambda h, rb: (0, ch + h)),
                  pl.BlockSpec((s, HEAD_DIM), lambda h, rb: (0, 2 * ch + h)),
                  pl.BlockSpec((None, NA_ROWS, GRID_W, win), lambda h, rb: (h, 0, 0, 0))],
        out_specs=[ospec, ospec],
        out_shape=[jax.ShapeDtypeStruct((s, cw), BF16), jax.ShapeDtypeStruct((s, cw), F32)],
        compiler_params=_cparams(("parallel", "parallel")), name=name)(qkv, qkv, qkv, bias)


def _na_bwd(qkv, o, do, lse, bias, ch, name):
    s = qkv.shape[0]
    rows = s // GRID_W
    cw = ch * HEAD_DIM
    tq = NA_ROWBLOCK * GRID_W
    win = NA_ROWS * GRID_W
    scale = 1.0 / math.sqrt(HEAD_DIM)

    def body(q_ref, k_ref, v_ref, o_ref, do_ref, lse_ref, b_ref, dq_ref, dk_ref, dv_ref, db_ref):
        rb = pl.program_id(1)

        @pl.when(rb == 0)
        def _():
            dk_ref[...] = jnp.zeros_like(dk_ref)
            dv_ref[...] = jnp.zeros_like(dv_ref)
            db_ref[...] = jnp.zeros_like(db_ref)

        for i in range(NA_ROWBLOCK):
            st, var = _na_row(rb * NA_ROWBLOCK + i, rows)
            ws = pl.ds(st, win)
            kw, vw = k_ref[ws, :], v_ref[ws, :]
            qs = pl.ds(i * GRID_W, GRID_W)
            q, dov = q_ref[qs, :], do_ref[qs, :]
            sc = _nt(q, kw) * scale + b_ref[var]
            p = jnp.exp(sc - lse_ref[qs, :][:, :1])
            delta = jnp.sum(dov.astype(F32) * o_ref[qs, :].astype(F32), axis=-1, keepdims=True)
            ds = p * (_nt(dov, vw) - delta)
            dsb = ds.astype(BF16)
            dq_ref[qs, :] = (_nn(dsb, kw) * scale).astype(BF16)
            dk_ref[ws, :] += _tn(dsb, q) * scale
            dv_ref[ws, :] += _tn(p.astype(BF16), dov)
            db_ref[var] += ds

    qspec = pl.BlockSpec((tq, HEAD_DIM), lambda h, rb: (rb, h))
    kvspec = pl.BlockSpec((s, HEAD_DIM), lambda h, rb: (0, h))
    return pl.pallas_call(
        body, grid=(ch, rows // NA_ROWBLOCK),
        in_specs=[qspec, pl.BlockSpec((s, HEAD_DIM), lambda h, rb: (0, ch + h)),
                  pl.BlockSpec((s, HEAD_DIM), lambda h, rb: (0, 2 * ch + h)), qspec, qspec, qspec,
                  pl.BlockSpec((None, NA_ROWS, GRID_W, win), lambda h, rb: (h, 0, 0, 0))],
        out_specs=[qspec, kvspec, kvspec, pl.BlockSpec((None, NA_ROWS, GRID_W, win), lambda h, rb: (h, 0, 0, 0))],
        out_shape=[jax.ShapeDtypeStruct((s, cw), BF16), jax.ShapeDtypeStruct((s, cw), F32),
                   jax.ShapeDtypeStruct((s, cw), F32), jax.ShapeDtypeStruct((ch, NA_ROWS, GRID_W, win), F32)],
        compiler_params=_cparams(("parallel", "arbitrary")), name=name)(qkv, qkv, qkv, o, do, lse, bias)


def _rpb_grad(dbias):
    ch = dbias.shape[0]
    c = np.arange(GRID_W)
    col_idx = (c[None, :] - c[:, None] + NA_COLS - 1).reshape(-1)
    oh_col = (col_idx[:, None] == np.arange(LANES)[None, :]).astype(np.float32)
    d5 = dbias.reshape(ch, NA_ROWS, GRID_W, NA_ROWS, GRID_W)
    vals = jnp.transpose(d5, (0, 1, 3, 2, 4)).reshape(ch * NA_ROWS * NA_ROWS, GRID_W * GRID_W)
    by_col = _bucket_sum(vals, jnp.asarray(oh_col), "rpb_grad_cols")
    row_idx = (np.arange(NA_ROWS)[None, :] - np.arange(NA_ROWS)[:, None] + NA_ROWS - 1).reshape(-1)
    oh_row = (row_idx[:, None] == np.arange(LANES)[None, :]).astype(np.float32)
    vals2 = jnp.transpose(by_col.reshape(ch, NA_ROWS * NA_ROWS, LANES), (0, 2, 1)).reshape(ch * LANES, NA_ROWS * NA_ROWS)
    by_row = _bucket_sum(vals2, jnp.asarray(oh_row), "rpb_grad_rows")
    return jnp.transpose(by_row.reshape(ch, LANES, LANES), (0, 2, 1))[:, :2 * NA_ROWS - 1, :2 * NA_COLS - 1]


def _s5_discretize(lam_re, lam_im, log_step, b_re, b_im):
    step = jnp.exp(log_step.astype(F32))[:, None]
    lr = jnp.minimum(lam_re.astype(F32), -1e-4)
    li = lam_im.astype(F32)
    mag = jnp.exp(lr * step)
    ab_re = mag * jnp.cos(li * step)
    ab_im = mag * jnp.sin(li * step)
    den = lr * lr + li * li
    zr = ((ab_re - 1.0) * lr + ab_im * li) / den
    zi = (ab_im * lr - (ab_re - 1.0) * li) / den
    br = b_re.astype(F32)
    bi = b_im.astype(F32)
    return ab_re, ab_im, zr[..., None] * br - zi[..., None] * bi, zr[..., None] * bi + zi[..., None] * br


def _scan_tables(a_re, a_im, rev):
    ar, ai = a_re.reshape(-1), a_im.reshape(-1)
    pows = [(ar, ai)]
    for _ in range(SUBLANES - 1):
        pr, pi = pows[-1]
        pows.append((pr * ar - pi * ai, pr * ai + pi * ar))
    row = jnp.arange(SUBLANES)[:, None]
    tabs = []
    for k in (1, 2, 4):
        keep = (row < SUBLANES - k) if rev else (row >= k)
        tabs += [jnp.where(keep, pows[k - 1][0][None, :], 0.0), jnp.where(keep, pows[k - 1][1][None, :], 0.0)]
    order = list(range(SUBLANES - 1, -1, -1)) if rev else list(range(SUBLANES))
    tabs += [jnp.stack([pows[i][0] for i in order]), jnp.stack([pows[i][1] for i in order])]
    t = jnp.stack(tabs)
    nblk = t.shape[-1] // (4 * LANES)
    return jnp.transpose(t.reshape(8, SUBLANES, nblk, 4 * LANES), (2, 0, 1, 3))


def _block_diag(w):
    g, a, b = w.shape
    nblk = g // S5_GROUPS_PER_BLOCK
    eye = jnp.eye(S5_GROUPS_PER_BLOCK, dtype=w.dtype)
    w4 = w.reshape(nblk, S5_GROUPS_PER_BLOCK, a, b)
    return (w4[:, :, :, None, :] * eye[None, :, None, :, None]).reshape(nblk, S5_GROUPS_PER_BLOCK * a, S5_GROUPS_PER_BLOCK * b)


def _block_diag_take(w, a, b):
    nblk = w.shape[0]
    w5 = w.reshape(nblk, S5_GROUPS_PER_BLOCK, a, S5_GROUPS_PER_BLOCK, b)
    d = jnp.diagonal(w5, axis1=1, axis2=3)
    return jnp.transpose(d, (0, 3, 1, 2)).reshape(nblk * S5_GROUPS_PER_BLOCK, a, b)


def _scan_tile(r, i, tab_ref, carry, rev):
    for lvl, k in enumerate((1, 2, 4)):
        mr, mi = tab_ref[2 * lvl], tab_ref[2 * lvl + 1]
        sh = SUBLANES - k if rev else k
        rr, ri = pltpu.roll(r, sh, 0), pltpu.roll(i, sh, 0)
        r, i = r + (mr * rr - mi * ri), i + (mr * ri + mi * rr)
    pr, pi = tab_ref[6], tab_ref[7]
    cr, ci = carry
    return r + (pr * cr - pi * ci), i + (pr * ci + pi * cr)


def _s5_fwd(proj, ucol0, tabs, bre, bim, cre, cim, rev, final, name):
    s = proj.shape[0]
    nblk = tabs.shape[0]
    bw = nblk * LANES
    w = 4 * LANES
    t = _tile(s, S5_CHUNK)
    nc, nt = s // t, t // SUBLANES
    ub = ucol0 // LANES
    cm = (lambda c: nc - 1 - c) if rev else (lambda c: c)
    last = 0 if rev else SUBLANES - 1

    def body(u_ref, tab_ref, bre_ref, bim_ref, cre_ref, cim_ref, *rest):
        if final is not None:
            yo_ref, d_ref, y_ref, xr_ref, xi_ref, xr_s, xi_s, car_r, car_i = rest
        else:
            y_ref, xr_ref, xi_ref, xr_s, xi_s, car_r, car_i = rest

        @pl.when(pl.program_id(1) == 0)
        def _():
            car_r[...] = jnp.zeros_like(car_r)
            car_i[...] = jnp.zeros_like(car_i)

        u = u_ref[...]
        xr_s[...] = _nn(u, bre_ref[...])
        xi_s[...] = _nn(u, bim_ref[...])

        def tile(tt, carry):
            k = nt - 1 - tt if rev else tt
            rows = pl.ds(pl.multiple_of(k * SUBLANES, SUBLANES), SUBLANES)
            r, i = _scan_tile(xr_s[rows, :], xi_s[rows, :], tab_ref, carry, rev)
            xr_s[rows, :] = r
            xi_s[rows, :] = i
            return (jnp.broadcast_to(r[last:last + 1, :], r.shape), jnp.broadcast_to(i[last:last + 1, :], i.shape))

        carry = lax.fori_loop(0, nt, tile, (car_r[...], car_i[...]))
        car_r[...], car_i[...] = carry
        xr, xi = xr_s[...].astype(BF16), xi_s[...].astype(BF16)
        y = _nn(xr, cre_ref[...]) - _nn(xi, cim_ref[...])
        if final is not None:
            y = y + yo_ref[...] + d_ref[...] * u.astype(F32)
        y_ref[...] = y
        xr_ref[...] = xr
        xi_ref[...] = xi

    yspec = pl.BlockSpec((t, LANES), lambda j, c: (cm(c), j))
    xspec = pl.BlockSpec((t, w), lambda j, c: (cm(c), j))
    in_specs = [pl.BlockSpec((t, LANES), lambda j, c: (cm(c), ub + j)),
                pl.BlockSpec((None, 8, SUBLANES, w), lambda j, c: (j, 0, 0, 0)),
                pl.BlockSpec((None, LANES, w), lambda j, c: (j, 0, 0)), pl.BlockSpec((None, LANES, w), lambda j, c: (j, 0, 0)),
                pl.BlockSpec((None, w, LANES), lambda j, c: (j, 0, 0)), pl.BlockSpec((None, w, LANES), lambda j, c: (j, 0, 0))]
    args = [proj, tabs, bre, bim, cre, cim]
    if final is not None:
        in_specs += [yspec, pl.BlockSpec((1, LANES), lambda j, c: (0, j))]
        args += [final[0], final[1].reshape(1, bw)]
    return pl.pallas_call(
        body, grid=(nblk, nc), in_specs=in_specs, out_specs=[yspec, xspec, xspec],
        out_shape=[jax.ShapeDtypeStruct((s, bw), F32), jax.ShapeDtypeStruct((s, nblk * w), BF16),
                   jax.ShapeDtypeStruct((s, nblk * w), BF16)],
        scratch_shapes=[pltpu.VMEM((t, w), F32), pltpu.VMEM((t, w), F32), pltpu.VMEM((SUBLANES, w), F32),
                        pltpu.VMEM((SUBLANES, w), F32)],
        compiler_params=_cparams(("parallel", "arbitrary")), name=name)(*args)


def _s5_bwd(proj, ucol0, dy, xr, xi, gtabs, bre, bim, cre, cim, rev, final, name):
    s = proj.shape[0]
    nblk = gtabs.shape[0]
    bw = nblk * LANES
    w = 4 * LANES
    t = _tile(s, S5_CHUNK)
    nc, nt = s // t, t // SUBLANES
    ub = ucol0 // LANES
    grev = not rev
    cm = (lambda c: nc - 1 - c) if grev else (lambda c: c)
    last = 0 if grev else SUBLANES - 1
    nfin = 2 if final is not None else 0

    def body(dy_ref, u_ref, xr_ref, xi_ref, tab_ref, bre_ref, bim_ref, cre_ref, cim_ref, *rest):
        fin, rest = rest[:nfin], rest[nfin:]
        du_ref, dar_ref, dai_ref, dbr_ref, dbi_ref, dcr_ref, dci_ref = rest[:7]
        rest = rest[7:]
        if final is not None:
            dd_ref, rest = rest[0], rest[1:]
        gr_s, gi_s, xr_s, xi_s, car_r, car_i = rest

        @pl.when(pl.program_id(1) == 0)
        def _():
            for ref in (car_r, car_i, dar_ref, dai_ref, dbr_ref, dbi_ref, dcr_ref, dci_ref):
                ref[...] = jnp.zeros_like(ref)
            if final is not None:
                dd_ref[...] = jnp.zeros_like(dd_ref)

        dyv = dy_ref[...]
        dyb = dyv.astype(BF16)
        u = u_ref[...]
        xrb, xib = xr_ref[...], xi_ref[...]
        gr_s[...] = _nt(dyb, cre_ref[...])
        gi_s[...] = -_nt(dyb, cim_ref[...])
        xr_s[...] = xrb.astype(F32)
        xi_s[...] = xib.astype(F32)
        rowid = lax.broadcasted_iota(jnp.int32, (SUBLANES, w), 0)

        def tile(tt, carry):
            k = nt - 1 - tt if grev else tt
            rows = pl.ds(pl.multiple_of(k * SUBLANES, SUBLANES), SUBLANES)
            r, i = _scan_tile(gr_s[rows, :], gi_s[rows, :], tab_ref, carry, grev)
            gr_s[rows, :] = r
            gi_s[rows, :] = i
            if grev:
                er = jnp.where(rowid == SUBLANES - 1, carry[0], pltpu.roll(r, SUBLANES - 1, 0))
                ei = jnp.where(rowid == SUBLANES - 1, carry[1], pltpu.roll(i, SUBLANES - 1, 0))
            else:
                er = jnp.where(rowid == 0, carry[0], pltpu.roll(r, 1, 0))
                ei = jnp.where(rowid == 0, carry[1], pltpu.roll(i, 1, 0))
            sr, si = xr_s[rows, :], xi_s[rows, :]
            dar_ref[...] += er * sr + ei * si
            dai_ref[...] += ei * sr - er * si
            return (jnp.broadcast_to(r[last:last + 1, :], r.shape), jnp.broadcast_to(i[last:last + 1, :], i.shape))

        carry = lax.fori_loop(0, nt, tile, (car_r[...], car_i[...]))
        car_r[...], car_i[...] = carry
        gr, gi = gr_s[...].astype(BF16), gi_s[...].astype(BF16)
        du = _nt(gr, bre_ref[...]) + _nt(gi, bim_ref[...])
        if final is not None:
            du = du + fin[0][...] + fin[1][...] * dyv.astype(F32)
            dd_ref[...] += jnp.sum(dyv.astype(F32) * u.astype(F32), axis=0, keepdims=True)
        du_ref[...] = du
        dbr_ref[...] += _tn(u, gr)
        dbi_ref[...] += _tn(u, gi)
        dcr_ref[...] += _tn(xrb, dyb)
        dci_ref[...] -= _tn(xib, dyb)

    yspec = pl.BlockSpec((t, LANES), lambda j, c: (cm(c), j))
    xspec = pl.BlockSpec((t, w), lambda j, c: (cm(c), j))
    bspec = pl.BlockSpec((None, LANES, w), lambda j, c: (j, 0, 0))
    cspec = pl.BlockSpec((None, w, LANES), lambda j, c: (j, 0, 0))
    aspec = pl.BlockSpec((None, SUBLANES, w), lambda j, c: (j, 0, 0))
    dspec = pl.BlockSpec((1, LANES), lambda j, c: (0, j))
    in_specs = [yspec, pl.BlockSpec((t, LANES), lambda j, c: (cm(c), ub + j)), xspec, xspec,
                pl.BlockSpec((None, 8, SUBLANES, w), lambda j, c: (j, 0, 0, 0)), bspec, bspec, cspec, cspec]
    args = [dy, proj, xr, xi, gtabs, bre, bim, cre, cim]
    out_specs = [yspec, aspec, aspec, bspec, bspec, cspec, cspec]
    out_shape = [jax.ShapeDtypeStruct((s, bw), F32)] + [jax.ShapeDtypeStruct((nblk, SUBLANES, w), F32)] * 2 \
        + [jax.ShapeDtypeStruct((nblk, LANES, w), F32)] * 2 + [jax.ShapeDtypeStruct((nblk, w, LANES), F32)] * 2
    if final is not None:
        in_specs += [yspec, dspec]
        args += [final[0], final[1].reshape(1, bw)]
        out_specs.append(dspec)
        out_shape.append(jax.ShapeDtypeStruct((1, bw), F32))
    return pl.pallas_call(
        body, grid=(nblk, nc), in_specs=in_specs, out_specs=out_specs, out_shape=out_shape,
        scratch_shapes=[pltpu.VMEM((t, w), F32)] * 4 + [pltpu.VMEM((SUBLANES, w), F32)] * 2,
        compiler_params=_cparams(("parallel", "arbitrary")), name=name)(*args)


HBM_SPEC = pl.BlockSpec(memory_space=pltpu.HBM)


def _peers():
    mx, my, mc = lax.axis_index("x"), lax.axis_index("y"), lax.axis_index("c")
    me = 4 * mx + 2 * my + mc
    peers = []
    for k in range(1, N_DEV):
        px = 1 - mx if k & 4 else mx
        py = 1 - my if k & 2 else my
        pc = 1 - mc if k & 1 else mc
        peers.append(((px, py, pc), 4 * px + 2 * py + pc))
    return me, peers


def _all_gather(x, name):
    nl, r, c = x.shape

    def body(x_ref, out_ref, send_sems, recv_sems, local_sem):
        me, peers = _peers()
        own = pltpu.make_async_copy(x_ref, out_ref.at[:, me], local_sem)
        own.start()
        sends = []
        for k, (dev, _) in enumerate(peers):
            cp = pltpu.make_async_remote_copy(src_ref=x_ref, dst_ref=out_ref.at[:, me], send_sem=send_sems.at[k],
                                              recv_sem=recv_sems.at[k], device_id=dev, device_id_type=MESH)
            cp.start()
            sends.append(cp)
        for k, (dev, idx) in enumerate(peers):
            pltpu.make_async_remote_copy(src_ref=x_ref, dst_ref=out_ref.at[:, idx], send_sem=send_sems.at[k],
                                         recv_sem=recv_sems.at[k], device_id=dev, device_id_type=MESH).wait_recv()
        for cp in sends:
            cp.wait_send()
        own.wait()

    return pl.pallas_call(
        body, in_specs=[HBM_SPEC], out_specs=HBM_SPEC, out_shape=jax.ShapeDtypeStruct((nl, N_DEV, r, c), x.dtype),
        scratch_shapes=[pltpu.SemaphoreType.DMA((N_DEV - 1,)), pltpu.SemaphoreType.DMA((N_DEV - 1,)),
                        pltpu.SemaphoreType.DMA],
        compiler_params=pltpu.CompilerParams(has_side_effects=True), name=name)(x)


def _all_to_all(gs, name):
    nl = len(gs)
    _, r, c = gs[0].shape

    def body(*refs):
        g_refs, out_ref, send_sems, recv_sems, local_sems = refs[:nl], refs[nl], refs[nl + 1], refs[nl + 2], refs[nl + 3]
        me, peers = _peers()
        owns = [pltpu.make_async_copy(g_refs[l].at[me], out_ref.at[me, l], local_sems.at[l]) for l in range(nl)]
        for cp in owns:
            cp.start()
        sends = []
        for k, (dev, idx) in enumerate(peers):
            for l in range(nl):
                cp = pltpu.make_async_remote_copy(src_ref=g_refs[l].at[idx], dst_ref=out_ref.at[me, l],
                                                  send_sem=send_sems.at[k, l], recv_sem=recv_sems.at[k, l],
                                                  device_id=dev, device_id_type=MESH)
                cp.start()
                sends.append(cp)
        for k, (dev, idx) in enumerate(peers):
            for l in range(nl):
                pltpu.make_async_remote_copy(src_ref=g_refs[l].at[idx], dst_ref=out_ref.at[idx, l],
                                             send_sem=send_sems.at[k, l], recv_sem=recv_sems.at[k, l],
                                             device_id=dev, device_id_type=MESH).wait_recv()
        for cp in sends:
            cp.wait_send()
        for cp in owns:
            cp.wait()

    return pl.pallas_call(
        body, in_specs=[HBM_SPEC] * nl, out_specs=HBM_SPEC, out_shape=jax.ShapeDtypeStruct((N_DEV, nl, r, c), F32),
        scratch_shapes=[pltpu.SemaphoreType.DMA((N_DEV - 1, nl)), pltpu.SemaphoreType.DMA((N_DEV - 1, nl)),
                        pltpu.SemaphoreType.DMA((nl,))],
        compiler_params=pltpu.CompilerParams(has_side_effects=True), name=name)(*gs)


def _adamw(w, gstack, m, v, name):
    r, c = w.shape
    tr = r
    while tr * c > 128 * 1024 and tr % 16 == 0:
        tr //= 2
    c1 = 1.0 - ADAM_B1 ** ADAM_STEP
    c2 = 1.0 - ADAM_B2 ** ADAM_STEP

    def body(w_ref, g_ref, m_ref, v_ref, go_ref, d_ref, mo_ref, vo_ref):
        g = g_ref[0]
        for p in range(1, N_DEV):
            g = g + g_ref[p]
        mn = ADAM_B1 * m_ref[...] + (1.0 - ADAM_B1) * g
        vn = ADAM_B2 * v_ref[...] + (1.0 - ADAM_B2) * (g * g)
        go_ref[...] = g
        mo_ref[...] = mn
        vo_ref[...] = vn
        d_ref[...] = -ADAM_LR * ((mn / c1) / (jnp.sqrt(vn / c2) + ADAM_EPS) + ADAM_WD * w_ref[...])

    spec = pl.BlockSpec((tr, c), lambda i: (i, 0))
    return pl.pallas_call(
        body, grid=(r // tr,), in_specs=[spec, pl.BlockSpec((N_DEV, tr, c), lambda i: (0, i, 0)), spec, spec],
        out_specs=[spec] * 4, out_shape=[jax.ShapeDtypeStruct((r, c), F32)] * 4,
        compiler_params=_cparams(("parallel",)), name=name)(w, gstack, m, v)


def _s5_tables(lam_re, lam_im, log_step, b_re, b_im, c_re, c_im):
    out = []
    for d in range(2):
        ab_re, ab_im, bb_re, bb_im = _s5_discretize(lam_re[d], lam_im[d], log_step[d], b_re, b_im)
        rev = d == 1
        out.append(dict(
            tabs=_scan_tables(ab_re, ab_im, rev), gtabs=_scan_tables(ab_re, -ab_im, not rev),
            bre=_block_diag(jnp.transpose(bb_re, (0, 2, 1))).astype(BF16), bim=_block_diag(jnp.transpose(bb_im, (0, 2, 1))).astype(BF16),
            cre=_block_diag(jnp.transpose(c_re[d], (0, 2, 1))).astype(BF16), cim=_block_diag(jnp.transpose(c_im[d], (0, 2, 1))).astype(BF16)))
    return out


def _forward_backward(x, target, p, wg):
    s, d = x.shape
    depth = p["norm_mix"].shape[0]
    ah = p["t5_bias"].shape[1]
    aw = ah * HEAD_DIM
    ch = p["c_rpb"].shape[1]
    groups, pstate = p["s5_lam_re"].shape[2:]
    bw = groups * S5_GROUP
    assert aw + bw == d and ch * HEAD_DIM == d

    dil_bias = [_dil_bias(p["t5_bias"], dil) for _, dil in DILATED_BRANCHES]
    saved = []
    for i in range(depth):
        j = i // 2
        sv = dict(x=x)
        hn = _rms_fwd(x, p["norm_mix"][i], f"norm_mix_fwd{i}")
        sv["hn"] = hn
        if i % 2 == 0:
            proj = _mm_cols(f"ab_in_fwd{i}", hn, wg["ab_w_in"], j)
            outs = [_dil_fwd(proj, dil_bias[b][0], dil, ah, f"dil_fwd_d{dil}_{i}") for b, (_, dil) in enumerate(DILATED_BRANCHES)]
            o_a, lse = _dil_merge([o for o, _ in outs], [l for _, l in outs])
            tb = _s5_tables(p["s5_lam_re"][j], p["s5_lam_im"][j], p["s5_log_step"][j], p["s5_b_re"][j], p["s5_b_im"][j],
                            p["s5_c_re"][j], p["s5_c_im"][j])
            y0, x0r, x0i = _s5_fwd(proj, 3 * aw, tb[0]["tabs"], tb[0]["bre"], tb[0]["bim"], tb[0]["cre"], tb[0]["cim"],
                                   False, None, f"s5_fwd_a{i}")
            y_pre, x1r, x1i = _s5_fwd(proj, 3 * aw, tb[1]["tabs"], tb[1]["bre"], tb[1]["bim"], tb[1]["cre"], tb[1]["cim"],
                                      True, (y0, p["s5_d"][j]), f"s5_fwd_b{i}")
            o_b = _mm(f"glu_fwd{i}", y_pre, wg["s5_w_glu"], b_layer=j, a_fn=_gelu, extras=(y_pre,), out_dtypes=(BF16,),
                      epi=lambda acc, yp: (_gelu(yp) * jax.nn.sigmoid(acc),))[0]
            merged = jnp.concatenate([o_a, o_b], axis=1)
            x = _mm(f"ab_out_fwd{i}", merged, wg["ab_w_out"], b_layer=j, extras=(x,), epi=lambda acc, xr: (acc + xr,))[0]
            sv.update(proj=proj, o_a=o_a, lse=lse, tb=tb, states=((x0r, x0i), (x1r, x1i)), y_pre=y_pre, merged=merged)
        else:
            qkv = _mm_cols(f"c_qkv_fwd{i}", hn, wg["c_w_qkv"], j)
            nbias = _na_bias(p["c_rpb"][j])
            o, lse = _na_fwd(qkv, nbias, ch, f"na_fwd{i}")
            x = _mm(f"c_out_fwd{i}", o, wg["c_w_out"], b_layer=j, extras=(x,), epi=lambda acc, xr: (acc + xr,))[0]
            sv.update(qkv=qkv, o=o, lse=lse, nbias=nbias)
        sv["x_mid"] = x
        hn2 = _rms_fwd(x, p["norm_mlp"][i], f"norm_mlp_fwd{i}")
        h_pre = _mm_cols(f"mlp_up_fwd{i}", hn2, wg["mlp_w1"], i)
        x = _mm(f"mlp_down_fwd{i}", h_pre, wg["mlp_w2"], b_layer=i, a_fn=_relu_sq, extras=(x,), epi=lambda acc, xr: (acc + xr,))[0]
        sv.update(hn2=hn2, h_pre=h_pre)
        saved.append(sv)

    loss_sum, dx, g_final = _final_loss(x, p["norm_final"], target)

    g = {k: [None] * wg[k].shape[0] for k in BIG}
    g.update({k: [None] * p[k].shape[0] for k in ("norm_mix", "norm_mlp", "s5_lam_re", "s5_lam_im", "s5_log_step", "s5_b_re",
                                                  "s5_b_im", "s5_c_re", "s5_c_im", "s5_d", "c_rpb")})
    g_t5 = jnp.zeros_like(p["t5_bias"], dtype=F32)
    for i in reversed(range(depth)):
        j = i // 2
        sv = saved[i]
        dh = _mm(f"mlp_down_bwd{i}", dx, wg["mlp_w2"], b_layer=i, tb=True, extras=(sv["h_pre"],), out_dtypes=(BF16,),
                 epi=lambda acc, hp: (acc * (2.0 * jnp.maximum(hp.astype(F32), 0.0)),))[0]
        g["mlp_w2"][i] = _mm(f"mlp_w2_grad{i}", sv["h_pre"], dx, ta=True, a_fn=_relu_sq, tk=1024)[0]
        g["mlp_w1"][i] = _mm_cols_grad(f"mlp_w1_grad{i}", sv["hn2"], dh)
        dhn2 = _mm_cols_t(f"mlp_up_bwd{i}", dh, wg["mlp_w1"], i)
        dx, gn = _rms_bwd(sv["x_mid"], p["norm_mlp"][i], dhn2, dx, f"norm_mlp_bwd{i}")
        g["norm_mlp"][i] = gn[0]
        if i % 2 == 0:
            tb = sv["tb"]
            dmerged = _mm(f"ab_out_bwd{i}", dx, wg["ab_w_out"], b_layer=j, tb=True, out_dtypes=(BF16,))[0]
            g["ab_w_out"][j] = _mm(f"ab_w_out_grad{i}", sv["merged"], dx, ta=True, tk=1024)[0]
            def glu_epi(acc, yp, dob):
                sg = jax.nn.sigmoid(acc)
                dob = dob.astype(F32)
                return dob * _gelu(yp) * sg * (1.0 - sg), dob * sg
            dz, t1 = _mm(f"glu_bwd_z{i}", sv["y_pre"], wg["s5_w_glu"], b_layer=j, a_fn=_gelu, extras=(sv["y_pre"], dmerged),
                         extra_cols=(0, aw), epi=glu_epi, out_dtypes=(BF16, F32))
            dy_pre = _mm(f"glu_bwd_y{i}", dz, wg["s5_w_glu"], b_layer=j, tb=True, extras=(t1, sv["y_pre"]),
                         epi=lambda acc, t, yp: ((acc + t) * _gelu_grad(yp),), out_dtypes=(BF16,))[0]
            g["s5_w_glu"][j] = _mm(f"glu_w_grad{i}", sv["y_pre"], dz, ta=True, a_fn=_gelu, tk=1024)[0]
            r0 = _s5_bwd(sv["proj"], 3 * aw, dy_pre, *sv["states"][0], tb[0]["gtabs"], tb[0]["bre"], tb[0]["bim"],
                         tb[0]["cre"], tb[0]["cim"], False, None, f"s5_bwd_a{i}")
            r1 = _s5_bwd(sv["proj"], 3 * aw, dy_pre, *sv["states"][1], tb[1]["gtabs"], tb[1]["bre"], tb[1]["bim"],
                         tb[1]["cre"], tb[1]["cim"], True, (r0[0], p["s5_d"][j]), f"s5_bwd_b{i}")
            du = r1[0]
            g["s5_d"][j] = r1[7][0]
            gl_re, gl_im, gls, gb_re, gb_im, gc_re, gc_im = [], [], [], 0.0, 0.0, [], []
            for dnum, rr in enumerate((r0, r1)):
                da_re = jnp.sum(rr[1], axis=1).reshape(groups, pstate)
                da_im = jnp.sum(rr[2], axis=1).reshape(groups, pstate)
                dbb_re = jnp.transpose(_block_diag_take(rr[3], S5_GROUP, pstate), (0, 2, 1))
                dbb_im = jnp.transpose(_block_diag_take(rr[4], S5_GROUP, pstate), (0, 2, 1))
                _, vjp = jax.vjp(_s5_discretize, p["s5_lam_re"][j][dnum], p["s5_lam_im"][j][dnum], p["s5_log_step"][j][dnum],
                                 p["s5_b_re"][j], p["s5_b_im"][j])
                a, b, c, e, f = vjp((da_re, da_im, dbb_re, dbb_im))
                gl_re.append(a)
                gl_im.append(b)
                gls.append(c)
                gb_re, gb_im = gb_re + e, gb_im + f
                gc_re.append(jnp.transpose(_block_diag_take(rr[5], pstate, S5_GROUP), (0, 2, 1)))
                gc_im.append(jnp.transpose(_block_diag_take(rr[6], pstate, S5_GROUP), (0, 2, 1)))
            g["s5_lam_re"][j], g["s5_lam_im"][j], g["s5_log_step"][j] = jnp.stack(gl_re), jnp.stack(gl_im), jnp.stack(gls)
            g["s5_b_re"][j], g["s5_b_im"][j] = gb_re, gb_im
            g["s5_c_re"][j], g["s5_c_im"][j] = jnp.stack(gc_re), jnp.stack(gc_im)
            delta = _head_delta(dmerged, 0, sv["o_a"], f"dil_delta{i}")
            dq = dk = dv = 0.0
            for b, (_, dil) in enumerate(DILATED_BRANCHES):
                q1, k1, v1, db = _dil_bwd(sv["proj"], dmerged, sv["lse"], delta, dil_bias[b][0], dil_bias[b][1], dil, ah,
                                          f"dil_bwd_d{dil}_{i}")
                dq, dk, dv = dq + q1, dk + k1, dv + v1
                g_t5 = g_t5 + _t5_grad(db, dil)
            dproj = jnp.concatenate([dq, dk, dv, du], axis=1).astype(BF16)
            g["ab_w_in"][j] = _mm_cols_grad(f"ab_w_in_grad{i}", sv["hn"], dproj)
            dhn = _mm_cols_t(f"ab_in_bwd{i}", dproj, wg["ab_w_in"], j)
        else:
            do = _mm(f"c_out_bwd{i}", dx, wg["c_w_out"], b_layer=j, tb=True, out_dtypes=(BF16,))[0]
            g["c_w_out"][j] = _mm(f"c_w_out_grad{i}", sv["o"], dx, ta=True, tk=1024)[0]
            dq, dk, dv, db = _na_bwd(sv["qkv"], sv["o"], do, sv["lse"], sv["nbias"], ch, f"na_bwd{i}")
            g["c_rpb"][j] = _rpb_grad(db)
            dqkv = jnp.concatenate([dq, dk.astype(BF16), dv.astype(BF16)], axis=1)
            g["c_w_qkv"][j] = _mm_cols_grad(f"c_w_qkv_grad{i}", sv["hn"], dqkv)
            dhn = _mm_cols_t(f"c_qkv_bwd{i}", dqkv, wg["c_w_qkv"], j)
        dx, gn = _rms_bwd(sv["x"], p["norm_mix"][i], dhn, dx, f"norm_mix_bwd{i}")
        g["norm_mix"][i] = gn[0]
    g["t5_bias"] = g_t5
    g["norm_final"] = g_final[0]
    return loss_sum[0, 0], dx, g


BIG = ("ab_w_in", "ab_w_out", "s5_w_glu", "c_w_qkv", "c_w_out", "mlp_w1", "mlp_w2")
ROW_SHARDED = ("ab_w_out", "s5_w_glu", "c_w_out", "mlp_w2")
WEIGHTS = ("t5_bias", "ab_w_in", "ab_w_out", "s5_lam_re", "s5_lam_im", "s5_log_step", "s5_b_re", "s5_b_im", "s5_c_re",
           "s5_c_im", "s5_d", "s5_w_glu", "c_w_qkv", "c_w_out", "c_rpb", "norm_mix", "norm_mlp", "mlp_w1", "mlp_w2",
           "norm_final")


def _step(x, target, w, m, v):
    d = x.shape[-1]
    wg = {}
    for k in BIG:
        full = _all_gather(w[k].astype(BF16), f"gather_{k}")
        if k in ROW_SHARDED:
            full = full.reshape(full.shape[0], N_DEV * full.shape[2], full.shape[3])
        wg[k] = full
    small = {k: w[k] for k in WEIGHTS if k not in BIG}
    loss_sum, dx, g = _forward_backward(x[0], target[0], small, wg)
    loss = lax.psum(0.5 * loss_sum / d, ("x", "y", "c"))

    out = {}
    for k in BIG:
        nl, r, c = w[k].shape
        gs = [gl.reshape(N_DEV, r, c) for gl in g[k]]
        recv = _all_to_all(gs, f"exchange_{k}").reshape(N_DEV, nl * r, c)
        res = _adamw(w[k].reshape(nl * r, c), recv, m[k].reshape(nl * r, c), v[k].reshape(nl * r, c), f"adamw_{k}")
        out[k] = [a.reshape(nl, r, c) for a in res]
    names = [k for k in WEIGHTS if k not in BIG]
    def flat(tree):
        return jnp.concatenate([jnp.asarray(jnp.stack(tree[k]) if isinstance(tree[k], list) else tree[k], F32).reshape(-1)
                                for k in names])
    total = sum(int(np.prod(w[k].shape)) for k in names)
    rows = -(-total // LANES)
    rows = -(-rows // SUBLANES) * SUBLANES
    pad = rows * LANES - total
    def pack(tree):
        return jnp.pad(flat(tree), (0, pad)).reshape(rows, LANES)
    gall = _all_gather(pack(g)[None], "gather_small_grads")[0]
    res = _adamw(pack(w), gall, pack(m), pack(v), "adamw_small")
    off = 0
    for k in names:
        n = int(np.prod(w[k].shape))
        out[k] = [a.reshape(-1)[off:off + n].reshape(w[k].shape) for a in res]
        off += n
    return (loss, dx[None], *[out[k][0] for k in WEIGHTS], *[out[k][1] for k in WEIGHTS],
            *[out[k][2] for k in WEIGHTS], *[out[k][3] for k in WEIGHTS])


def kernel(x, t5_bias, ab_w_in, ab_w_out, s5_lam_re, s5_lam_im, s5_log_step, s5_b_re, s5_b_im, s5_c_re, s5_c_im, s5_d, s5_w_glu, c_w_qkv, c_w_out, c_rpb, norm_mix, norm_mlp, mlp_w1, mlp_w2, norm_final, loss_target, m_t5_bias, m_ab_w_in, m_ab_w_out, m_s5_lam_re, m_s5_lam_im, m_s5_log_step, m_s5_b_re, m_s5_b_im, m_s5_c_re, m_s5_c_im, m_s5_d, m_s5_w_glu, m_c_w_qkv, m_c_w_out, m_c_rpb, m_norm_mix, m_norm_mlp, m_mlp_w1, m_mlp_w2, m_norm_final, v_t5_bias, v_ab_w_in, v_ab_w_out, v_s5_lam_re, v_s5_lam_im, v_s5_log_step, v_s5_b_re, v_s5_b_im, v_s5_c_re, v_s5_c_im, v_s5_d, v_s5_w_glu, v_c_w_qkv, v_c_w_out, v_c_rpb, v_norm_mix, v_norm_mlp, v_mlp_w1, v_mlp_w2, v_norm_final):
    w = dict(t5_bias=t5_bias, ab_w_in=ab_w_in, ab_w_out=ab_w_out, s5_lam_re=s5_lam_re, s5_lam_im=s5_lam_im,
             s5_log_step=s5_log_step, s5_b_re=s5_b_re, s5_b_im=s5_b_im, s5_c_re=s5_c_re, s5_c_im=s5_c_im, s5_d=s5_d,
             s5_w_glu=s5_w_glu, c_w_qkv=c_w_qkv, c_w_out=c_w_out, c_rpb=c_rpb, norm_mix=norm_mix, norm_mlp=norm_mlp,
             mlp_w1=mlp_w1, mlp_w2=mlp_w2, norm_final=norm_final)
    m = dict(t5_bias=m_t5_bias, ab_w_in=m_ab_w_in, ab_w_out=m_ab_w_out, s5_lam_re=m_s5_lam_re, s5_lam_im=m_s5_lam_im,
             s5_log_step=m_s5_log_step, s5_b_re=m_s5_b_re, s5_b_im=m_s5_b_im, s5_c_re=m_s5_c_re, s5_c_im=m_s5_c_im,
             s5_d=m_s5_d, s5_w_glu=m_s5_w_glu, c_w_qkv=m_c_w_qkv, c_w_out=m_c_w_out, c_rpb=m_c_rpb, norm_mix=m_norm_mix,
             norm_mlp=m_norm_mlp, mlp_w1=m_mlp_w1, mlp_w2=m_mlp_w2, norm_final=m_norm_final)
    v = dict(t5_bias=v_t5_bias, ab_w_in=v_ab_w_in, ab_w_out=v_ab_w_out, s5_lam_re=v_s5_lam_re, s5_lam_im=v_s5_lam_im,
             s5_log_step=v_s5_log_step, s5_b_re=v_s5_b_re, s5_b_im=v_s5_b_im, s5_c_re=v_s5_c_re, s5_c_im=v_s5_c_im,
             s5_d=v_s5_d, s5_w_glu=v_s5_w_glu, c_w_qkv=v_c_w_qkv, c_w_out=v_c_w_out, c_rpb=v_c_rpb, norm_mix=v_norm_mix,
             norm_mlp=v_norm_mlp, mlp_w1=v_mlp_w1, mlp_w2=v_mlp_w2, norm_final=v_norm_final)
    return _step(x, loss_target, w, m, v)
```

```python
import math

import jax
import jax.numpy as jnp
import numpy as np
from jax import lax
from jax.experimental import pallas as pl
from jax.experimental.pallas import tpu as pltpu

F32 = jnp.float32
BF16 = jnp.bfloat16

N_DEV = 8
HEAD_DIM = 128
LANES = 128
QBLOCK = 128
DIL_HALF = 64
DILATED_BRANCHES = ((128, 1), (512, 4), (2048, 16))
S5_GROUP = 16
S5_GROUPS_PER_BLOCK = LANES // S5_GROUP
S5_CHUNK = 512
SUBLANES = 8
GRID_W = 64
NA_ROWS = 8
NA_COLS = 16
NA_ROWBLOCK = 8
T5_BUCKETS = 32
T5_MAX_DISTANCE = 1024
RMS_EPS = 1e-6
NEG_INF = -1e30
ADAM_LR = 0.001
ADAM_B1 = 0.9
ADAM_B2 = 0.999
ADAM_EPS = 1e-08
ADAM_WD = 0.01
ADAM_STEP = 10
VMEM_LIMIT_BYTES = 56 * 1024 * 1024
MESH = pl.DeviceIdType.MESH


def _cparams(sem=None):
    return pltpu.CompilerParams(dimension_semantics=sem, vmem_limit_bytes=VMEM_LIMIT_BYTES)


def _tile(dim, pref):
    t = min(dim, pref)
    assert dim % t == 0, (dim, pref)
    return t


def _dot(a, b, ca, cb):
    return lax.dot_general(a, b, (((ca,), (cb,)), ((), ())), preferred_element_type=F32)


def _nn(a, b):
    return _dot(a, b, 1, 0)


def _nt(a, b):
    return _dot(a, b, 1, 1)


def _tn(a, b):
    return _dot(a, b, 0, 0)


def _mm_call(name, a, b, a_spec, b_spec, grid, nk, out_shapes, out_specs, acc_shape,
             ta=False, tb=False, a_fn=None, epi=None, extras=(), extra_specs=()):
    ne, no = len(extras), len(out_shapes)

    def body(a_ref, b_ref, *rest):
        ex, outs, acc = rest[:ne], rest[ne:ne + no], rest[ne + no]
        k = pl.program_id(2)

        @pl.when(k == 0)
        def _():
            acc[...] = jnp.zeros_like(acc)

        av = a_ref[...]
        if a_fn is not None:
            av = a_fn(av)
        acc[...] += _dot(av.astype(BF16), b_ref[...].astype(BF16), 0 if ta else 1, 1 if tb else 0)

        @pl.when(k == nk - 1)
        def _():
            r = acc[...]
            res = epi(r, *[e[...] for e in ex]) if epi is not None else (r,)
            for o, v in zip(outs, res):
                o[...] = v.astype(o.dtype)

    return pl.pallas_call(
        body, grid=grid, in_specs=[a_spec, b_spec, *extra_specs], out_specs=list(out_specs),
        out_shape=list(out_shapes), scratch_shapes=[pltpu.VMEM(acc_shape, F32)],
        compiler_params=_cparams(("parallel", "parallel", "arbitrary")), name=name,
    )(a, b, *extras)


def _mm(name, a, b, *, ta=False, tb=False, b_layer=None, a_fn=None, epi=None, extras=(), extra_cols=None,
        out_dtypes=(F32,), tm=1024, tn=1024, tk=2048):
    m, kdim = (a.shape[1], a.shape[0]) if ta else a.shape
    bs = b.shape[-2:]
    n = bs[0] if tb else bs[1]
    assert (bs[1] if tb else bs[0]) == kdim, (a.shape, b.shape)
    tm, tn, tk = _tile(m, tm), _tile(n, tn), _tile(kdim, tk)
    a_spec = pl.BlockSpec((tk, tm), lambda i, j, k: (k, i)) if ta else pl.BlockSpec((tm, tk), lambda i, j, k: (i, k))
    if b_layer is None:
        b_spec = pl.BlockSpec((tn, tk), lambda i, j, k: (j, k)) if tb else pl.BlockSpec((tk, tn), lambda i, j, k: (k, j))
    else:
        lyr = b_layer
        b_spec = (pl.BlockSpec((None, tn, tk), lambda i, j, k: (lyr, j, k)) if tb
                  else pl.BlockSpec((None, tk, tn), lambda i, j, k: (lyr, k, j)))
    o_spec = pl.BlockSpec((tm, tn), lambda i, j, k: (i, j))
    extra_cols = extra_cols or (0,) * len(extras)
    especs = []
    for c0 in extra_cols:
        assert c0 % tn == 0
        cb = c0 // tn
        especs.append(pl.BlockSpec((tm, tn), lambda i, j, k, cb=cb: (i, cb + j)))
    return _mm_call(name, a, b, a_spec, b_spec, (m // tm, n // tn, kdim // tk), kdim // tk,
                    [jax.ShapeDtypeStruct((m, n), d) for d in out_dtypes], [o_spec] * len(out_dtypes), (tm, tn),
                    ta=ta, tb=tb, a_fn=a_fn, epi=epi, extras=extras, extra_specs=especs)


def _mm_cols(name, a, b4, layer, *, out_dtype=BF16, tm=1024, tk=2048):
    m, kdim = a.shape
    n = b4.shape[-1]
    tm, tk = _tile(m, tm), _tile(kdim, tk)
    return _mm_call(name, a, b4, pl.BlockSpec((tm, tk), lambda i, j, k: (i, k)),
                    pl.BlockSpec((None, None, tk, n), lambda i, j, k: (layer, j, k, 0)),
                    (m // tm, N_DEV, kdim // tk), kdim // tk,
                    [jax.ShapeDtypeStruct((m, N_DEV * n), out_dtype)], [pl.BlockSpec((tm, n), lambda i, j, k: (i, j))],
                    (tm, n))[0]


def _mm_cols_t(name, a, b4, layer, *, out_dtype=BF16, tm=1024, tn=1024):
    m = a.shape[0]
    kout, n = b4.shape[-2:]
    tm, tn = _tile(m, tm), _tile(kout, tn)
    return _mm_call(name, a, b4, pl.BlockSpec((tm, n), lambda i, j, k: (i, k)),
                    pl.BlockSpec((None, None, tn, n), lambda i, j, k: (layer, k, j, 0)),
                    (m // tm, kout // tn, N_DEV), N_DEV,
                    [jax.ShapeDtypeStruct((m, kout), out_dtype)], [pl.BlockSpec((tm, tn), lambda i, j, k: (i, j))],
                    (tm, tn), tb=True)[0]


def _mm_cols_grad(name, a, dy, *, tm=1024, tk=1024):
    s, kout = a.shape
    n = dy.shape[1] // N_DEV
    tm, tk = _tile(kout, tm), _tile(s, tk)
    return _mm_call(name, a, dy, pl.BlockSpec((tk, tm), lambda i, j, k: (k, i)),
                    pl.BlockSpec((tk, n), lambda i, j, k: (k, j)),
                    (kout // tm, N_DEV, s // tk), s // tk,
                    [jax.ShapeDtypeStruct((N_DEV, kout, n), F32)], [pl.BlockSpec((None, tm, n), lambda i, j, k: (j, i, 0))],
                    (tm, n), ta=True)[0]


_GELU_C = math.sqrt(2.0 / math.pi)


def _gelu(x):
    return 0.5 * x * (1.0 + jnp.tanh(_GELU_C * (x + 0.044715 * x * x * x)))


def _gelu_grad(x):
    t = jnp.tanh(_GELU_C * (x + 0.044715 * x * x * x))
    return 0.5 * (1.0 + t) + 0.5 * x * (1.0 - t * t) * _GELU_C * (1.0 + 3.0 * 0.044715 * x * x)


def _relu_sq(x):
    r = jnp.maximum(x.astype(F32), 0.0)
    return r * r


def _rms_fwd(x, g, name):
    s, d = x.shape
    tr = _tile(s, 512)

    def body(x_ref, g_ref, o_ref):
        xv = x_ref[...]
        r = lax.rsqrt(jnp.mean(xv * xv, axis=-1, keepdims=True) + RMS_EPS)
        o_ref[...] = (xv * r * g_ref[...]).astype(BF16)

    return pl.pallas_call(
        body, grid=(s // tr,),
        in_specs=[pl.BlockSpec((tr, d), lambda i: (i, 0)), pl.BlockSpec((1, d), lambda i: (0, 0))],
        out_specs=pl.BlockSpec((tr, d), lambda i: (i, 0)), out_shape=jax.ShapeDtypeStruct((s, d), BF16),
        compiler_params=_cparams(("parallel",)), name=name)(x, g.reshape(1, d))


def _rms_bwd(x, g, dy, dres, name):
    s, d = x.shape
    tr = _tile(s, 512)

    def body(x_ref, g_ref, dy_ref, dres_ref, dx_ref, dg_ref):
        @pl.when(pl.program_id(0) == 0)
        def _():
            dg_ref[...] = jnp.zeros_like(dg_ref)

        xv = x_ref[...]
        dyv = dy_ref[...].astype(F32)
        r = lax.rsqrt(jnp.mean(xv * xv, axis=-1, keepdims=True) + RMS_EPS)
        xh = xv * r
        gdy = dyv * g_ref[...]
        dx_ref[...] = dres_ref[...] + r * (gdy - xh * jnp.mean(gdy * xh, axis=-1, keepdims=True))
        dg_ref[...] += jnp.sum(dyv * xh, axis=0, keepdims=True)

    return pl.pallas_call(
        body, grid=(s // tr,),
        in_specs=[pl.BlockSpec((tr, d), lambda i: (i, 0)), pl.BlockSpec((1, d), lambda i: (0, 0)),
                  pl.BlockSpec((tr, d), lambda i: (i, 0)), pl.BlockSpec((tr, d), lambda i: (i, 0))],
        out_specs=[pl.BlockSpec((tr, d), lambda i: (i, 0)), pl.BlockSpec((1, d), lambda i: (0, 0))],
        out_shape=[jax.ShapeDtypeStruct((s, d), F32), jax.ShapeDtypeStruct((1, d), F32)],
        compiler_params=_cparams(("arbitrary",)), name=name)(x, g.reshape(1, d), dy, dres)


def _final_loss(x, g, target):
    s, d = x.shape
    tr = _tile(s, 512)

    def body(x_ref, g_ref, t_ref, loss_ref, dx_ref, dg_ref):
        @pl.when(pl.program_id(0) == 0)
        def _():
            dg_ref[...] = jnp.zeros_like(dg_ref)
            loss_ref[...] = jnp.zeros_like(loss_ref)

        xv = x_ref[...]
        gv = g_ref[...]
        r = lax.rsqrt(jnp.mean(xv * xv, axis=-1, keepdims=True) + RMS_EPS)
        xh = xv * r
        err = xh * gv - t_ref[...]
        loss_ref[...] += jnp.sum(jnp.sum(err * err, axis=-1, keepdims=True), axis=0, keepdims=True)
        dyv = err * (1.0 / d)
        gdy = dyv * gv
        dx_ref[...] = r * (gdy - xh * jnp.mean(gdy * xh, axis=-1, keepdims=True))
        dg_ref[...] += jnp.sum(dyv * xh, axis=0, keepdims=True)

    return pl.pallas_call(
        body, grid=(s // tr,),
        in_specs=[pl.BlockSpec((tr, d), lambda i: (i, 0)), pl.BlockSpec((1, d), lambda i: (0, 0)),
                  pl.BlockSpec((tr, d), lambda i: (i, 0))],
        out_specs=[pl.BlockSpec((1, 1), lambda i: (0, 0)), pl.BlockSpec((tr, d), lambda i: (i, 0)),
                   pl.BlockSpec((1, d), lambda i: (0, 0))],
        out_shape=[jax.ShapeDtypeStruct((1, 1), F32), jax.ShapeDtypeStruct((s, d), F32),
                   jax.ShapeDtypeStruct((1, d), F32)],
        compiler_params=_cparams(("arbitrary",)), name="final_norm_loss")(x, g.reshape(1, d), target)


def _t5_bucket(rel):
    half = T5_BUCKETS // 2
    max_exact = half // 2
    n = jnp.abs(rel)
    nf = jnp.maximum(n, 1).astype(F32)
    large = max_exact + (jnp.log(nf / max_exact) / math.log(T5_MAX_DISTANCE / max_exact)
                         * (half - max_exact)).astype(jnp.int32)
    large = jnp.minimum(large, half - 1)
    return jnp.where(rel > 0, half, 0) + jnp.where(n < max_exact, n, large)


def _dil_offsets():
    i = jnp.arange(QBLOCK)[:, None]
    kk = jnp.arange(2 * QBLOCK)[None, :]
    return kk - DIL_HALF - i, (jnp.arange(QBLOCK)[None, :] + DIL_HALF) - jnp.arange(2 * QBLOCK)[:, None]


def _expand(table, onehot, name, tn=8192):
    r, n = table.shape[0], onehot.shape[1]
    tn = _tile(n, tn)

    def body(t_ref, oh_ref, o_ref):
        o_ref[...] = lax.dot_general(t_ref[...], oh_ref[...], (((1,), (0,)), ((), ())),
                                     precision=lax.Precision.HIGHEST, preferred_element_type=F32)

    return pl.pallas_call(
        body, grid=(n // tn,),
        in_specs=[pl.BlockSpec((r, LANES), lambda i: (0, 0)), pl.BlockSpec((LANES, tn), lambda i: (0, i))],
        out_specs=pl.BlockSpec((r, tn), lambda i: (0, i)), out_shape=jax.ShapeDtypeStruct((r, n), F32),
        compiler_params=_cparams(("parallel",)), name=name)(table, onehot)


def _pad_rows_lanes(t):
    r, c = t.shape
    return jnp.pad(t.astype(F32), ((0, -r % SUBLANES), (0, LANES - c)))


def _dil_bias(t5, dil):
    ah = t5.shape[1]
    off1, off2 = _dil_offsets()
    bucket = jnp.concatenate([_t5_bucket(off1 * dil).reshape(-1), _t5_bucket(off2 * dil).reshape(-1)])
    onehot = (jnp.arange(LANES)[:, None] == bucket[None, :]).astype(F32)
    b = _expand(_pad_rows_lanes(t5.T), onehot, f"t5_bias_d{dil}")[:ah]
    n1 = QBLOCK * 2 * QBLOCK
    return b[:, :n1].reshape(ah, QBLOCK, 2 * QBLOCK), b[:, n1:].reshape(ah, 2 * QBLOCK, QBLOCK)


def _window(p, c, n, cols=slice(None)):
    return jnp.concatenate([p[pl.ds(DIL_HALF, DIL_HALF), cols], c[:, cols], n[pl.ds(0, DIL_HALF), cols]], axis=0)


def _dil_specs(width, ncol_blocks, col_block, nb):
    def spec(dn):
        return pl.BlockSpec((QBLOCK, width), lambda r, n: (jnp.clip(n + dn, 0, nb - 1), r * ncol_blocks + col_block))
    return [spec(-1), spec(0), spec(1)]


def _dil_fwd(proj, bias1, dil, ah, name):
    s, wtot = proj.shape
    ln = s // dil
    nb = ln // QBLOCK
    assert nb * QBLOCK * dil == s
    aw = ah * HEAD_DIM
    wb = wtot // aw
    scale = 1.0 / math.sqrt(HEAD_DIM)
    pv = proj.reshape(ln, dil * wtot)

    def body(q_ref, kp, kc, kn, vp, vc, vn, b_ref, o_ref, lse_ref):
        n = pl.program_id(1)
        ii = lax.broadcasted_iota(jnp.int32, (QBLOCK, 2 * QBLOCK), 0)
        jj = lax.broadcasted_iota(jnp.int32, (QBLOCK, 2 * QBLOCK), 1)
        kpos = n * QBLOCK + jj - DIL_HALF
        valid = (jnp.abs(jj - DIL_HALF - ii) <= DIL_HALF) & (kpos >= 0) & (kpos < ln)
        for h in range(ah):
            cs = pl.ds(h * HEAD_DIM, HEAD_DIM)
            kw, vw = _window(kp, kc, kn, cs), _window(vp, vc, vn, cs)
            sc = jnp.where(valid, _nt(q_ref[:, cs], kw) * scale + b_ref[h], NEG_INF)
            m = jnp.max(sc, axis=-1, keepdims=True)
            p = jnp.exp(sc - m)
            l = jnp.sum(p, axis=-1, keepdims=True)
            o_ref[:, cs] = (_nn(p.astype(BF16), vw) / l).astype(BF16)
            lse_ref[:, cs] = jnp.broadcast_to(m + jnp.log(l), (QBLOCK, HEAD_DIM))

    ospec = pl.BlockSpec((QBLOCK, aw), lambda r, n: (n, r))
    o, lse = pl.pallas_call(
        body, grid=(dil, nb),
        in_specs=[_dil_specs(aw, wb, 0, nb)[1], *_dil_specs(aw, wb, 1, nb), *_dil_specs(aw, wb, 2, nb),
                  pl.BlockSpec((ah, QBLOCK, 2 * QBLOCK), lambda r, n: (0, 0, 0))],
        out_specs=[ospec, ospec],
        out_shape=[jax.ShapeDtypeStruct((ln, dil * aw), BF16), jax.ShapeDtypeStruct((ln, dil * aw), F32)],
        compiler_params=_cparams(("parallel", "parallel")), name=name,
    )(pv, pv, pv, pv, pv, pv, pv, bias1)
    return o.reshape(s, aw), lse.reshape(s, aw)


def _dil_merge(outs, lses):
    s, aw = outs[0].shape
    tr = _tile(s, 512)

    def body(o1, l1, o2, l2, o3, l3, o_ref, lse_ref):
        a, b, c = l1[...], l2[...], l3[...]
        m = jnp.maximum(jnp.maximum(a, b), c)
        w1, w2, w3 = jnp.exp(a - m), jnp.exp(b - m), jnp.exp(c - m)
        tot = w1 + w2 + w3
        o_ref[...] = ((w1 * o1[...].astype(F32) + w2 * o2[...].astype(F32) + w3 * o3[...].astype(F32)) / tot).astype(BF16)
        lse_ref[...] = m + jnp.log(tot)

    spec = pl.BlockSpec((tr, aw), lambda i: (i, 0))
    return pl.pallas_call(
        body, grid=(s // tr,), in_specs=[spec] * 6, out_specs=[spec, spec],
        out_shape=[jax.ShapeDtypeStruct((s, aw), BF16), jax.ShapeDtypeStruct((s, aw), F32)],
        compiler_params=_cparams(("parallel",)), name="dil_merge",
    )(outs[0], lses[0], outs[1], lses[1], outs[2], lses[2])


def _head_delta(do, do_col0, o, name):
    s, w = o.shape
    tr = _tile(s, 512)
    cb = do_col0 // HEAD_DIM

    def body(do_ref, o_ref, d_ref):
        d = jnp.sum(do_ref[...].astype(F32) * o_ref[...].astype(F32), axis=-1, keepdims=True)
        d_ref[...] = jnp.broadcast_to(d, d_ref.shape)

    return pl.pallas_call(
        body, grid=(s // tr, w // HEAD_DIM),
        in_specs=[pl.BlockSpec((tr, HEAD_DIM), lambda i, h: (i, cb + h)), pl.BlockSpec((tr, HEAD_DIM), lambda i, h: (i, h))],
        out_specs=pl.BlockSpec((tr, HEAD_DIM), lambda i, h: (i, h)), out_shape=jax.ShapeDtypeStruct((s, w), F32),
        compiler_params=_cparams(("parallel", "parallel")), name=name)(do, o)


def _dil_bwd(proj, dmerged, lse, delta, bias1, bias2, dil, ah, name):
    s, wtot = proj.shape
    ln = s // dil
    nb = ln // QBLOCK
    aw = ah * HEAD_DIM
    wb = wtot // aw
    wd = dmerged.shape[1] // aw
    scale = 1.0 / math.sqrt(HEAD_DIM)
    pv = proj.reshape(ln, dil * wtot)
    dov = dmerged.reshape(ln, dil * dmerged.shape[1])
    lv = lse.reshape(ln, dil * aw)
    dlv = delta.reshape(ln, dil * aw)

    def body(qp, qc, qn, kp, kc, kn, vp, vc, vn, dop, doc, don, lp, lc, lnx, dp, dc, dn, b1_ref, b2_ref,
             dq_ref, dk_ref, dv_ref, db_ref):
        n = pl.program_id(1)

        @pl.when((pl.program_id(0) == 0) & (n == 0))
        def _():
            db_ref[...] = jnp.zeros_like(db_ref)

        ii = lax.broadcasted_iota(jnp.int32, (QBLOCK, 2 * QBLOCK), 0)
        jj = lax.broadcasted_iota(jnp.int32, (QBLOCK, 2 * QBLOCK), 1)
        kpos = n * QBLOCK + jj - DIL_HALF
        valid = (jnp.abs(jj - DIL_HALF - ii) <= DIL_HALF) & (kpos >= 0) & (kpos < ln)
        ww = lax.broadcasted_iota(jnp.int32, (2 * QBLOCK, QBLOCK), 0)
        cc = lax.broadcasted_iota(jnp.int32, (2 * QBLOCK, QBLOCK), 1)
        qpos = n * QBLOCK - DIL_HALF + ww
        valid2 = (jnp.abs(cc + DIL_HALF - ww) <= DIL_HALF) & (qpos >= 0) & (qpos < ln)
        for h in range(ah):
            cs = pl.ds(h * HEAD_DIM, HEAD_DIM)
            kw, vw = _window(kp, kc, kn, cs), _window(vp, vc, vn, cs)
            sc = _nt(qc[:, cs], kw) * scale + b1_ref[h]
            lse2 = jnp.concatenate([lc[:, cs], lc[:, cs]], axis=1)
            p = jnp.where(valid, jnp.exp(jnp.where(valid, sc - lse2, 0.0)), 0.0)
            ds = p * (_nt(doc[:, cs], vw) - jnp.concatenate([dc[:, cs], dc[:, cs]], axis=1))
            dq_ref[:, cs] = _nn(ds.astype(BF16), kw) * scale
            db_ref[h] += ds
            qw, dow = _window(qp, qc, qn, cs), _window(dop, doc, don, cs)
            sc2 = _nt(qw, kc[:, cs]) * scale + b2_ref[h]
            p2 = jnp.where(valid2, jnp.exp(jnp.where(valid2, sc2 - _window(lp, lc, lnx, cs), 0.0)), 0.0)
            dv_ref[:, cs] = _tn(p2.astype(BF16), dow)
            ds2 = p2 * (_nt(dow, vc[:, cs]) - _window(dp, dc, dn, cs))
            dk_ref[:, cs] = _tn(ds2.astype(BF16), qw) * scale

    ospec = pl.BlockSpec((QBLOCK, aw), lambda r, n: (n, r))
    dq, dk, dv, db = pl.pallas_call(
        body, grid=(dil, nb),
        in_specs=[*_dil_specs(aw, wb, 0, nb), *_dil_specs(aw, wb, 1, nb), *_dil_specs(aw, wb, 2, nb),
                  *_dil_specs(aw, wd, 0, nb), *_dil_specs(aw, 1, 0, nb), *_dil_specs(aw, 1, 0, nb),
                  pl.BlockSpec((ah, QBLOCK, 2 * QBLOCK), lambda r, n: (0, 0, 0)),
                  pl.BlockSpec((ah, 2 * QBLOCK, QBLOCK), lambda r, n: (0, 0, 0))],
        out_specs=[ospec, ospec, ospec, pl.BlockSpec((ah, QBLOCK, 2 * QBLOCK), lambda r, n: (0, 0, 0))],
        out_shape=[jax.ShapeDtypeStruct((ln, dil * aw), F32)] * 3 + [jax.ShapeDtypeStruct((ah, QBLOCK, 2 * QBLOCK), F32)],
        compiler_params=_cparams(("arbitrary", "arbitrary")), name=name,
    )(pv, pv, pv, pv, pv, pv, pv, pv, pv, dov, dov, dov, lv, lv, lv, dlv, dlv, dlv, bias1, bias2)
    return dq.reshape(s, aw), dk.reshape(s, aw), dv.reshape(s, aw), db


def _bucket_sum(vals, onehot, name):
    r, n = vals.shape
    b = onehot.shape[1]

    def body(v_ref, oh_ref, o_ref):
        o_ref[...] = lax.dot_general(v_ref[...], oh_ref[...], (((1,), (0,)), ((), ())),
                                     precision=lax.Precision.HIGHEST, preferred_element_type=F32)

    return pl.pallas_call(body, out_shape=jax.ShapeDtypeStruct((r, b), F32), compiler_params=_cparams(), name=name)(vals, onehot)


def _t5_grad(dbias, dil):
    ah = dbias.shape[0]
    off1, _ = _dil_offsets()
    bucket = _t5_bucket(off1 * dil).reshape(-1)
    inside = (jnp.abs(off1) <= DIL_HALF).reshape(-1)
    onehot = ((bucket[:, None] == jnp.arange(LANES)[None, :]) & inside[:, None]).astype(F32)
    vals = jnp.pad(dbias.reshape(ah, -1), ((0, -ah % SUBLANES), (0, 0)))
    return _bucket_sum(vals, onehot, f"t5_grad_d{dil}")[:ah, :T5_BUCKETS].T


def _na_bias(rpb):
    ch, nro, nco = rpb.shape
    c = np.arange(GRID_W)
    col_start = np.clip(c - NA_COLS // 2, 0, GRID_W - NA_COLS)
    col_ok = (c[None, :] >= col_start[:, None]) & (c[None, :] < col_start[:, None] + NA_COLS)
    col_idx = np.clip(c[None, :] - c[:, None] + NA_COLS - 1, 0, 2 * NA_COLS - 2).reshape(-1)
    onehot = (np.arange(LANES)[:, None] == col_idx[None, :]).astype(np.float32)
    table = jnp.pad(rpb.astype(F32).reshape(ch * nro, nco), ((0, -(ch * nro) % SUBLANES), (0, LANES - nco)))
    by_row = _expand(table, jnp.asarray(onehot), "rpb_bias", tn=GRID_W * GRID_W)[:ch * nro]
    by_row = jnp.where(jnp.asarray(col_ok.reshape(-1))[None, :], by_row, NEG_INF).reshape(ch, nro, GRID_W, GRID_W)
    per_variant = [jnp.transpose(by_row[:, NA_ROWS - 1 - v:2 * NA_ROWS - 1 - v], (0, 2, 1, 3)) for v in range(NA_ROWS)]
    return jnp.stack(per_variant, axis=1).reshape(ch, NA_ROWS, GRID_W, NA_ROWS * GRID_W)


def _na_row(r, rows):
    rs = jnp.clip(r - NA_ROWS // 2, 0, rows - NA_ROWS)
    return pl.multiple_of(rs * GRID_W, GRID_W), r - rs


def _na_fwd(qkv, bias, ch, name):
    s = qkv.shape[0]
    rows = s // GRID_W
    assert rows >= NA_ROWS and rows % NA_ROWBLOCK == 0
    cw = ch * HEAD_DIM
    tq = NA_ROWBLOCK * GRID_W
    win = NA_ROWS * GRID_W
    scale = 1.0 / math.sqrt(HEAD_DIM)

    def body(q_ref, k_ref, v_ref, b_ref, o_ref, lse_ref):
        rb = pl.program_id(1)
        for i in range(NA_ROWBLOCK):
            st, var = _na_row(rb * NA_ROWBLOCK + i, rows)
            kw, vw = k_ref[pl.ds(st, win), :], v_ref[pl.ds(st, win), :]
            qs = pl.ds(i * GRID_W, GRID_W)
            sc = _nt(q_ref[qs, :], kw) * scale + b_ref[var]
            m = jnp.max(sc, axis=-1, keepdims=True)
            p = jnp.exp(sc - m)
            l = jnp.sum(p, axis=-1, keepdims=True)
            o_ref[qs, :] = (_nn(p.astype(BF16), vw) / l).astype(BF16)
            lse_ref[qs, :] = jnp.broadcast_to(m + jnp.log(l), (GRID_W, HEAD_DIM))

    ospec = pl.BlockSpec((tq, HEAD_DIM), lambda h, rb: (rb, h))
    return pl.pallas_call(
        body, grid=(ch, rows // NA_ROWBLOCK),
        in_specs=[pl.BlockSpec((tq, HEAD_DIM), lambda h, rb: (rb, h)),
                  pl.BlockSpec((s, HEAD_DIM), lambda h, rb: (0, ch + h)),
                  pl.BlockSpec((s, HEAD_DIM), lambda h, rb: (0, 2 * ch + h)),
                  pl.BlockSpec((None, NA_ROWS, GRID_W, win), lambda h, rb: (h, 0, 0, 0))],
        out_specs=[ospec, ospec],
        out_shape=[jax.ShapeDtypeStruct((s, cw), BF16), jax.ShapeDtypeStruct((s, cw), F32)],
        compiler_params=_cparams(("parallel", "parallel")), name=name)(qkv, qkv, qkv, bias)


def _na_bwd(qkv, o, do, lse, bias, ch, name):
    s = qkv.shape[0]
    rows = s // GRID_W
    cw = ch * HEAD_DIM
    tq = NA_ROWBLOCK * GRID_W
    win = NA_ROWS * GRID_W
    scale = 1.0 / math.sqrt(HEAD_DIM)

    def body(q_ref, k_ref, v_ref, o_ref, do_ref, lse_ref, b_ref, dq_ref, dk_ref, dv_ref, db_ref):
        rb = pl.program_id(1)

        @pl.when(rb == 0)
        def _():
            dk_ref[...] = jnp.zeros_like(dk_ref)
            dv_ref[...] = jnp.zeros_like(dv_ref)
            db_ref[...] = jnp.zeros_like(db_ref)

        for i in range(NA_ROWBLOCK):
            st, var = _na_row(rb * NA_ROWBLOCK + i, rows)
            ws = pl.ds(st, win)
            kw, vw = k_ref[ws, :], v_ref[ws, :]
            qs = pl.ds(i * GRID_W, GRID_W)
            q, dov = q_ref[qs, :], do_ref[qs, :]
            sc = _nt(q, kw) * scale + b_ref[var]
            p = jnp.exp(sc - lse_ref[qs, :][:, :1])
            delta = jnp.sum(dov.astype(F32) * o_ref[qs, :].astype(F32), axis=-1, keepdims=True)
            ds = p * (_nt(dov, vw) - delta)
            dsb = ds.astype(BF16)
            dq_ref[qs, :] = (_nn(dsb, kw) * scale).astype(BF16)
            dk_ref[ws, :] += _tn(dsb, q) * scale
            dv_ref[ws, :] += _tn(p.astype(BF16), dov)
            db_ref[var] += ds

    qspec = pl.BlockSpec((tq, HEAD_DIM), lambda h, rb: (rb, h))
    kvspec = pl.BlockSpec((s, HEAD_DIM), lambda h, rb: (0, h))
    return pl.pallas_call(
        body, grid=(ch, rows // NA_ROWBLOCK),
        in_specs=[qspec, pl.BlockSpec((s, HEAD_DIM), lambda h, rb: (0, ch + h)),
                  pl.BlockSpec((s, HEAD_DIM), lambda h, rb: (0, 2 * ch + h)), qspec, qspec, qspec,
                  pl.BlockSpec((None, NA_ROWS, GRID_W, win), lambda h, rb: (h, 0, 0, 0))],
        out_specs=[qspec, kvspec, kvspec, pl.BlockSpec((None, NA_ROWS, GRID_W, win), lambda h, rb: (h, 0, 0, 0))],
        out_shape=[jax.ShapeDtypeStruct((s, cw), BF16), jax.ShapeDtypeStruct((s, cw), F32),
                   jax.ShapeDtypeStruct((s, cw), F32), jax.ShapeDtypeStruct((ch, NA_ROWS, GRID_W, win), F32)],
        compiler_params=_cparams(("parallel", "arbitrary")), name=name)(qkv, qkv, qkv, o, do, lse, bias)


def _rpb_grad(dbias):
    ch = dbias.shape[0]
    c = np.arange(GRID_W)
    col_idx = (c[None, :] - c[:, None] + NA_COLS - 1).reshape(-1)
    oh_col = (col_idx[:, None] == np.arange(LANES)[None, :]).astype(np.float32)
    d5 = dbias.reshape(ch, NA_ROWS, GRID_W, NA_ROWS, GRID_W)
    vals = jnp.transpose(d5, (0, 1, 3, 2, 4)).reshape(ch * NA_ROWS * NA_ROWS, GRID_W * GRID_W)
    by_col = _bucket_sum(vals, jnp.asarray(oh_col), "rpb_grad_cols")
    row_idx = (np.arange(NA_ROWS)[None, :] - np.arange(NA_ROWS)[:, None] + NA_ROWS - 1).reshape(-1)
    oh_row = (row_idx[:, None] == np.arange(LANES)[None, :]).astype(np.float32)
    vals2 = jnp.transpose(by_col.reshape(ch, NA_ROWS * NA_ROWS, LANES), (0, 2, 1)).reshape(ch * LANES, NA_ROWS * NA_ROWS)
    by_row = _bucket_sum(vals2, jnp.asarray(oh_row), "rpb_grad_rows")
    return jnp.transpose(by_row.reshape(ch, LANES, LANES), (0, 2, 1))[:, :2 * NA_ROWS - 1, :2 * NA_COLS - 1]


def _s5_discretize(lam_re, lam_im, log_step, b_re, b_im):
    step = jnp.exp(log_step.astype(F32))[:, None]
    lr = jnp.minimum(lam_re.astype(F32), -1e-4)
    li = lam_im.astype(F32)
    mag = jnp.exp(lr * step)
    ab_re = mag * jnp.cos(li * step)
    ab_im = mag * jnp.sin(li * step)
    den = lr * lr + li * li
    zr = ((ab_re - 1.0) * lr + ab_im * li) / den
    zi = (ab_im * lr - (ab_re - 1.0) * li) / den
    br = b_re.astype(F32)
    bi = b_im.astype(F32)
    return ab_re, ab_im, zr[..., None] * br - zi[..., None] * bi, zr[..., None] * bi + zi[..., None] * br


def _scan_tables(a_re, a_im, rev):
    ar, ai = a_re.reshape(-1), a_im.reshape(-1)
    pows = [(ar, ai)]
    for _ in range(SUBLANES - 1):
        pr, pi = pows[-1]
        pows.append((pr * ar - pi * ai, pr * ai + pi * ar))
    row = jnp.arange(SUBLANES)[:, None]
    tabs = []
    for k in (1, 2, 4):
        keep = (row < SUBLANES - k) if rev else (row >= k)
        tabs += [jnp.where(keep, pows[k - 1][0][None, :], 0.0), jnp.where(keep, pows[k - 1][1][None, :], 0.0)]
    order = list(range(SUBLANES - 1, -1, -1)) if rev else list(range(SUBLANES))
    tabs += [jnp.stack([pows[i][0] for i in order]), jnp.stack([pows[i][1] for i in order])]
    t = jnp.stack(tabs)
    nblk = t.shape[-1] // (4 * LANES)
    return jnp.transpose(t.reshape(8, SUBLANES, nblk, 4 * LANES), (2, 0, 1, 3))


def _block_diag(w):
    g, a, b = w.shape
    nblk = g // S5_GROUPS_PER_BLOCK
    eye = jnp.eye(S5_GROUPS_PER_BLOCK, dtype=w.dtype)
    w4 = w.reshape(nblk, S5_GROUPS_PER_BLOCK, a, b)
    return (w4[:, :, :, None, :] * eye[None, :, None, :, None]).reshape(nblk, S5_GROUPS_PER_BLOCK * a, S5_GROUPS_PER_BLOCK * b)


def _block_diag_take(w, a, b):
    nblk = w.shape[0]
    w5 = w.reshape(nblk, S5_GROUPS_PER_BLOCK, a, S5_GROUPS_PER_BLOCK, b)
    eye = jnp.eye(S5_GROUPS_PER_BLOCK, dtype=w.dtype)
    return jnp.sum(w5 * eye[None, :, None, :, None], axis=3).reshape(nblk * S5_GROUPS_PER_BLOCK, a, b)


def _scan_tile(r, i, tab_ref, carry, rev):
    for lvl, k in enumerate((1, 2, 4)):
        mr, mi = tab_ref[2 * lvl], tab_ref[2 * lvl + 1]
        sh = SUBLANES - k if rev else k
        rr, ri = pltpu.roll(r, sh, 0), pltpu.roll(i, sh, 0)
        r, i = r + (mr * rr - mi * ri), i + (mr * ri + mi * rr)
    pr, pi = tab_ref[6], tab_ref[7]
    cr, ci = carry
    return r + (pr * cr - pi * ci), i + (pr * ci + pi * cr)


def _s5_fwd(proj, ucol0, tabs, bre, bim, cre, cim, rev, final, name):
    s = proj.shape[0]
    nblk = tabs.shape[0]
    bw = nblk * LANES
    w = 4 * LANES
    t = _tile(s, S5_CHUNK)
    nc, nt = s // t, t // SUBLANES
    ub = ucol0 // LANES
    cm = (lambda c: nc - 1 - c) if rev else (lambda c: c)
    last = 0 if rev else SUBLANES - 1

    def body(u_ref, tab_ref, bre_ref, bim_ref, cre_ref, cim_ref, *rest):
        if final is not None:
            yo_ref, d_ref, y_ref, xr_ref, xi_ref, xr_s, xi_s, car_r, car_i = rest
        else:
            y_ref, xr_ref, xi_ref, xr_s, xi_s, car_r, car_i = rest

        @pl.when(pl.program_id(1) == 0)
        def _():
            car_r[...] = jnp.zeros_like(car_r)
            car_i[...] = jnp.zeros_like(car_i)

        u = u_ref[...]
        xr_s[...] = _nn(u, bre_ref[...])
        xi_s[...] = _nn(u, bim_ref[...])

        def tile(tt, carry):
            k = nt - 1 - tt if rev else tt
            rows = pl.ds(pl.multiple_of(k * SUBLANES, SUBLANES), SUBLANES)
            r, i = _scan_tile(xr_s[rows, :], xi_s[rows, :], tab_ref, carry, rev)
            xr_s[rows, :] = r
            xi_s[rows, :] = i
            return (jnp.broadcast_to(r[last:last + 1, :], r.shape), jnp.broadcast_to(i[last:last + 1, :], i.shape))

        carry = lax.fori_loop(0, nt, tile, (car_r[...], car_i[...]))
        car_r[...], car_i[...] = carry
        xr, xi = xr_s[...].astype(BF16), xi_s[...].astype(BF16)
        y = _nn(xr, cre_ref[...]) - _nn(xi, cim_ref[...])
        if final is not None:
            y = y + yo_ref[...] + d_ref[...] * u.astype(F32)
        y_ref[...] = y
        xr_ref[...] = xr
        xi_ref[...] = xi

    yspec = pl.BlockSpec((t, LANES), lambda j, c: (cm(c), j))
    xspec = pl.BlockSpec((t, w), lambda j, c: (cm(c), j))
    in_specs = [pl.BlockSpec((t, LANES), lambda j, c: (cm(c), ub + j)),
                pl.BlockSpec((None, 8, SUBLANES, w), lambda j, c: (j, 0, 0, 0)),
                pl.BlockSpec((None, LANES, w), lambda j, c: (j, 0, 0)), pl.BlockSpec((None, LANES, w), lambda j, c: (j, 0, 0)),
                pl.BlockSpec((None, w, LANES), lambda j, c: (j, 0, 0)), pl.BlockSpec((None, w, LANES), lambda j, c: (j, 0, 0))]
    args = [proj, tabs, bre, bim, cre, cim]
    if final is not None:
        in_specs += [yspec, pl.BlockSpec((1, LANES), lambda j, c: (0, j))]
        args += [final[0], final[1].reshape(1, bw)]
    return pl.pallas_call(
        body, grid=(nblk, nc), in_specs=in_specs, out_specs=[yspec, xspec, xspec],
        out_shape=[jax.ShapeDtypeStruct((s, bw), F32), jax.ShapeDtypeStruct((s, nblk * w), BF16),
                   jax.ShapeDtypeStruct((s, nblk * w), BF16)],
        scratch_shapes=[pltpu.VMEM((t, w), F32), pltpu.VMEM((t, w), F32), pltpu.VMEM((SUBLANES, w), F32),
                        pltpu.VMEM((SUBLANES, w), F32)],
        compiler_params=_cparams(("parallel", "arbitrary")), name=name)(*args)


def _s5_bwd(proj, ucol0, dy, xr, xi, gtabs, bre, bim, cre, cim, rev, final, name):
    s = proj.shape[0]
    nblk = gtabs.shape[0]
    bw = nblk * LANES
    w = 4 * LANES
    t = _tile(s, S5_CHUNK)
    nc, nt = s // t, t // SUBLANES
    ub = ucol0 // LANES
    grev = not rev
    cm = (lambda c: nc - 1 - c) if grev else (lambda c: c)
    last = 0 if grev else SUBLANES - 1
    nfin = 2 if final is not None else 0

    def body(dy_ref, u_ref, xr_ref, xi_ref, tab_ref, bre_ref, bim_ref, cre_ref, cim_ref, *rest):
        fin, rest = rest[:nfin], rest[nfin:]
        du_ref, dar_ref, dai_ref, dbr_ref, dbi_ref, dcr_ref, dci_ref = rest[:7]
        rest = rest[7:]
        if final is not None:
            dd_ref, rest = rest[0], rest[1:]
        gr_s, gi_s, xr_s, xi_s, car_r, car_i = rest

        @pl.when(pl.program_id(1) == 0)
        def _():
            for ref in (car_r, car_i, dar_ref, dai_ref, dbr_ref, dbi_ref, dcr_ref, dci_ref):
                ref[...] = jnp.zeros_like(ref)
            if final is not None:
                dd_ref[...] = jnp.zeros_like(dd_ref)

        dyv = dy_ref[...]
        dyb = dyv.astype(BF16)
        u = u_ref[...]
        xrb, xib = xr_ref[...], xi_ref[...]
        gr_s[...] = _nt(dyb, cre_ref[...])
        gi_s[...] = -_nt(dyb, cim_ref[...])
        xr_s[...] = xrb.astype(F32)
        xi_s[...] = xib.astype(F32)
        rowid = lax.broadcasted_iota(jnp.int32, (SUBLANES, w), 0)

        def tile(tt, carry):
            k = nt - 1 - tt if grev else tt
            rows = pl.ds(pl.multiple_of(k * SUBLANES, SUBLANES), SUBLANES)
            r, i = _scan_tile(gr_s[rows, :], gi_s[rows, :], tab_ref, carry, grev)
            gr_s[rows, :] = r
            gi_s[rows, :] = i
            if grev:
                er = jnp.where(rowid == SUBLANES - 1, carry[0], pltpu.roll(r, SUBLANES - 1, 0))
                ei = jnp.where(rowid == SUBLANES - 1, carry[1], pltpu.roll(i, SUBLANES - 1, 0))
            else:
                er = jnp.where(rowid == 0, carry[0], pltpu.roll(r, 1, 0))
                ei = jnp.where(rowid == 0, carry[1], pltpu.roll(i, 1, 0))
            sr, si = xr_s[rows, :], xi_s[rows, :]
            dar_ref[...] += er * sr + ei * si
            dai_ref[...] += ei * sr - er * si
            return (jnp.broadcast_to(r[last:last + 1, :], r.shape), jnp.broadcast_to(i[last:last + 1, :], i.shape))

        carry = lax.fori_loop(0, nt, tile, (car_r[...], car_i[...]))
        car_r[...], car_i[...] = carry
        gr, gi = gr_s[...].astype(BF16), gi_s[...].astype(BF16)
        du = _nt(gr, bre_ref[...]) + _nt(gi, bim_ref[...])
        if final is not None:
            du = du + fin[0][...] + fin[1][...] * dyv.astype(F32)
            dd_ref[...] += jnp.sum(dyv.astype(F32) * u.astype(F32), axis=0, keepdims=True)
        du_ref[...] = du
        dbr_ref[...] += _tn(u, gr)
        dbi_ref[...] += _tn(u, gi)
        dcr_ref[...] += _tn(xrb, dyb)
        dci_ref[...] -= _tn(xib, dyb)

    yspec = pl.BlockSpec((t, LANES), lambda j, c: (cm(c), j))
    xspec = pl.BlockSpec((t, w), lambda j, c: (cm(c), j))
    bspec = pl.BlockSpec((None, LANES, w), lambda j, c: (j, 0, 0))
    cspec = pl.BlockSpec((None, w, LANES), lambda j, c: (j, 0, 0))
    aspec = pl.BlockSpec((None, SUBLANES, w), lambda j, c: (j, 0, 0))
    dspec = pl.BlockSpec((1, LANES), lambda j, c: (0, j))
    in_specs = [yspec, pl.BlockSpec((t, LANES), lambda j, c: (cm(c), ub + j)), xspec, xspec,
                pl.BlockSpec((None, 8, SUBLANES, w), lambda j, c: (j, 0, 0, 0)), bspec, bspec, cspec, cspec]
    args = [dy, proj, xr, xi, gtabs, bre, bim, cre, cim]
    out_specs = [yspec, aspec, aspec, bspec, bspec, cspec, cspec]
    out_shape = [jax.ShapeDtypeStruct((s, bw), F32)] + [jax.ShapeDtypeStruct((nblk, SUBLANES, w), F32)] * 2 \
        + [jax.ShapeDtypeStruct((nblk, LANES, w), F32)] * 2 + [jax.ShapeDtypeStruct((nblk, w, LANES), F32)] * 2
    if final is not None:
        in_specs += [yspec, dspec]
        args += [final[0], final[1].reshape(1, bw)]
        out_specs.append(dspec)
        out_shape.append(jax.ShapeDtypeStruct((1, bw), F32))
    return pl.pallas_call(
        body, grid=(nblk, nc), in_specs=in_specs, out_specs=out_specs, out_shape=out_shape,
        scratch_shapes=[pltpu.VMEM((t, w), F32)] * 4 + [pltpu.VMEM((SUBLANES, w), F32)] * 2,
        compiler_params=_cparams(("parallel", "arbitrary")), name=name)(*args)


HBM_SPEC = pl.BlockSpec(memory_space=pltpu.HBM)


N_CHIPS = 4


def _place():
    mx, my, mc = lax.axis_index("x"), lax.axis_index("y"), lax.axis_index("c")
    return (mx, my, mc), (mx, my, 1 - mc), [(1 - mx, my), (mx, 1 - my), (1 - mx, 1 - my)]


def _all_gather(x, name):
    nl, r, c = x.shape

    def body(x_ref, out_ref, send_sems, recv_sems, local_sem):
        me, sibling, chips = _place()
        mc = me[2]

        def slot(px, py, pc):
            return out_ref.at[:, 4 * px + 2 * py + pc]

        def copy(k, block, to, src=None):
            return pltpu.make_async_remote_copy(src_ref=slot(*block) if src is None else src, dst_ref=slot(*block),
                                                send_sem=send_sems.at[k], recv_sem=recv_sems.at[k],
                                                device_id=to, device_id_type=MESH)

        mine = pltpu.make_async_copy(x_ref, slot(*me), local_sem)
        mine.start()
        first = [copy(0, me, sibling, src=x_ref)] + [copy(1 + j, me, (*chip, mc), src=x_ref) for j, chip in enumerate(chips)]
        for cp in first:
            cp.start()
        passed = [copy(4 + j, (*chip, mc), sibling) for j, chip in enumerate(chips)]
        for j, chip in enumerate(chips):
            copy(1 + j, (*chip, mc), me).wait_recv()
            passed[j].start()
        copy(0, sibling, me).wait_recv()
        for j, chip in enumerate(chips):
            copy(4 + j, (*chip, 1 - mc), me).wait_recv()
        for cp in first + passed:
            cp.wait_send()
        mine.wait()

    return pl.pallas_call(
        body, in_specs=[HBM_SPEC], out_specs=HBM_SPEC, out_shape=jax.ShapeDtypeStruct((nl, N_DEV, r, c), x.dtype),
        scratch_shapes=[pltpu.SemaphoreType.DMA((N_DEV - 1,)), pltpu.SemaphoreType.DMA((N_DEV - 1,)),
                        pltpu.SemaphoreType.DMA],
        compiler_params=pltpu.CompilerParams(has_side_effects=True), name=name)(x)


def _pair_exchange(gs, name):
    nl = len(gs)
    _, r, c = gs[0].shape

    def body(*refs):
        g_refs, out_ref, send_sems, recv_sems = refs[:nl], refs[nl], refs[nl + 1], refs[nl + 2]
        me, sibling, _ = _place()
        sends = []
        for q in range(N_CHIPS):
            for l in range(nl):
                cp = pltpu.make_async_remote_copy(src_ref=g_refs[l].at[2 * q + (1 - me[2])], dst_ref=out_ref.at[q, l],
                                                  send_sem=send_sems.at[q, l], recv_sem=recv_sems.at[q, l],
                                                  device_id=sibling, device_id_type=MESH)
                cp.start()
                sends.append(cp)
        for cp in sends:
            cp.wait_recv()
        for cp in sends:
            cp.wait_send()

    return pl.pallas_call(
        body, in_specs=[HBM_SPEC] * nl, out_specs=HBM_SPEC, out_shape=jax.ShapeDtypeStruct((N_CHIPS, nl, r, c), F32),
        scratch_shapes=[pltpu.SemaphoreType.DMA((N_CHIPS, nl)), pltpu.SemaphoreType.DMA((N_CHIPS, nl))],
        compiler_params=pltpu.CompilerParams(has_side_effects=True), name=name)(*gs)


def _pair_add(g, t, layer, core, name):
    _, r, c = g.shape
    tr = r
    while tr * c > 512 * 1024 and tr % 32 == 0:
        tr //= 2

    def body(core_ref, g_ref, t_ref, o_ref):
        o_ref[...] = (g_ref[...] + t_ref[...]).astype(BF16)

    return pl.pallas_call(
        body,
        grid_spec=pltpu.PrefetchScalarGridSpec(
            num_scalar_prefetch=1, grid=(N_CHIPS, r // tr),
            in_specs=[pl.BlockSpec((None, tr, c), lambda q, i, core_ref: (2 * q + core_ref[0], i, 0)),
                      pl.BlockSpec((None, None, tr, c), lambda q, i, core_ref: (q, layer, i, 0))],
            out_specs=pl.BlockSpec((None, tr, c), lambda q, i, core_ref: (q, i, 0))),
        out_shape=jax.ShapeDtypeStruct((N_CHIPS, r, c), BF16),
        compiler_params=_cparams(("parallel", "parallel")), name=name)(core, g, t)


def _chip_exchange(ps, name):
    nl = len(ps)
    _, r, c = ps[0].shape

    def body(*refs):
        p_refs, out_ref, send_sems, recv_sems, local_sems = refs[:nl], refs[nl], refs[nl + 1], refs[nl + 2], refs[nl + 3]
        me, _, chips = _place()
        mychip = 2 * me[0] + me[1]
        owns = [pltpu.make_async_copy(p_refs[l].at[mychip], out_ref.at[mychip, l], local_sems.at[l]) for l in range(nl)]
        for cp in owns:
            cp.start()
        sends = []
        for j, (px, py) in enumerate(chips):
            for l in range(nl):
                cp = pltpu.make_async_remote_copy(src_ref=p_refs[l].at[2 * px + py], dst_ref=out_ref.at[mychip, l],
                                                  send_sem=send_sems.at[j, l], recv_sem=recv_sems.at[j, l],
                                                  device_id=(px, py, me[2]), device_id_type=MESH)
                cp.start()
                sends.append(cp)
        for j, (px, py) in enumerate(chips):
            for l in range(nl):
                pltpu.make_async_remote_copy(src_ref=p_refs[l].at[2 * px + py], dst_ref=out_ref.at[2 * px + py, l],
                                             send_sem=send_sems.at[j, l], recv_sem=recv_sems.at[j, l],
                                             device_id=(px, py, me[2]), device_id_type=MESH).wait_recv()
        for cp in sends:
            cp.wait_send()
        for cp in owns:
            cp.wait()

    return pl.pallas_call(
        body, in_specs=[HBM_SPEC] * nl, out_specs=HBM_SPEC, out_shape=jax.ShapeDtypeStruct((N_CHIPS, nl, r, c), BF16),
        scratch_shapes=[pltpu.SemaphoreType.DMA((N_CHIPS - 1, nl)), pltpu.SemaphoreType.DMA((N_CHIPS - 1, nl)),
                        pltpu.SemaphoreType.DMA((nl,))],
        compiler_params=pltpu.CompilerParams(has_side_effects=True), name=name)(*ps)


def _adamw(w, gstack, m, v, name):
    r, c = w.shape
    nstack = gstack.shape[0]
    tr = r
    while tr * c > 128 * 1024 and tr % 32 == 0:
        tr //= 2
    c1 = 1.0 - ADAM_B1 ** ADAM_STEP
    c2 = 1.0 - ADAM_B2 ** ADAM_STEP

    def body(w_ref, g_ref, m_ref, v_ref, go_ref, d_ref, mo_ref, vo_ref):
        g = g_ref[0].astype(F32)
        for p in range(1, nstack):
            g = g + g_ref[p].astype(F32)
        mn = ADAM_B1 * m_ref[...] + (1.0 - ADAM_B1) * g
        vn = ADAM_B2 * v_ref[...] + (1.0 - ADAM_B2) * (g * g)
        go_ref[...] = g
        mo_ref[...] = mn
        vo_ref[...] = vn
        d_ref[...] = -ADAM_LR * ((mn / c1) / (jnp.sqrt(vn / c2) + ADAM_EPS) + ADAM_WD * w_ref[...])

    spec = pl.BlockSpec((tr, c), lambda i: (i, 0))
    return pl.pallas_call(
        body, grid=(r // tr,), in_specs=[spec, pl.BlockSpec((nstack, tr, c), lambda i: (0, i, 0)), spec, spec],
        out_specs=[spec] * 4, out_shape=[jax.ShapeDtypeStruct((r, c), F32)] * 4,
        compiler_params=_cparams(("parallel",)), name=name)(w, gstack, m, v)


def _s5_tables(lam_re, lam_im, log_step, b_re, b_im, c_re, c_im):
    out = []
    for d in range(2):
        ab_re, ab_im, bb_re, bb_im = _s5_discretize(lam_re[d], lam_im[d], log_step[d], b_re, b_im)
        rev = d == 1
        out.append(dict(
            tabs=_scan_tables(ab_re, ab_im, rev), gtabs=_scan_tables(ab_re, -ab_im, not rev),
            bre=_block_diag(jnp.transpose(bb_re, (0, 2, 1))).astype(BF16), bim=_block_diag(jnp.transpose(bb_im, (0, 2, 1))).astype(BF16),
            cre=_block_diag(jnp.transpose(c_re[d], (0, 2, 1))).astype(BF16), cim=_block_diag(jnp.transpose(c_im[d], (0, 2, 1))).astype(BF16)))
    return out


def _forward_backward(x, target, p, wg):
    s, d = x.shape
    depth = p["norm_mix"].shape[0]
    ah = p["t5_bias"].shape[1]
    aw = ah * HEAD_DIM
    ch = p["c_rpb"].shape[1]
    groups, pstate = p["s5_lam_re"].shape[2:]
    bw = groups * S5_GROUP
    assert aw + bw == d and ch * HEAD_DIM == d

    dil_bias = [_dil_bias(p["t5_bias"], dil) for _, dil in DILATED_BRANCHES]
    saved = []
    for i in range(depth):
        j = i // 2
        sv = dict(x=x)
        hn = _rms_fwd(x, p["norm_mix"][i], f"norm_mix_fwd{i}")
        sv["hn"] = hn
        if i % 2 == 0:
            proj = _mm_cols(f"ab_in_fwd{i}", hn, wg["ab_w_in"], j)
            outs = [_dil_fwd(proj, dil_bias[b][0], dil, ah, f"dil_fwd_d{dil}_{i}") for b, (_, dil) in enumerate(DILATED_BRANCHES)]
            o_a, lse = _dil_merge([o for o, _ in outs], [l for _, l in outs])
            tb = _s5_tables(p["s5_lam_re"][j], p["s5_lam_im"][j], p["s5_log_step"][j], p["s5_b_re"][j], p["s5_b_im"][j],
                            p["s5_c_re"][j], p["s5_c_im"][j])
            y0, x0r, x0i = _s5_fwd(proj, 3 * aw, tb[0]["tabs"], tb[0]["bre"], tb[0]["bim"], tb[0]["cre"], tb[0]["cim"],
                                   False, None, f"s5_fwd_a{i}")
            y_pre, x1r, x1i = _s5_fwd(proj, 3 * aw, tb[1]["tabs"], tb[1]["bre"], tb[1]["bim"], tb[1]["cre"], tb[1]["cim"],
                                      True, (y0, p["s5_d"][j]), f"s5_fwd_b{i}")
            o_b = _mm(f"glu_fwd{i}", y_pre, wg["s5_w_glu"], b_layer=j, a_fn=_gelu, extras=(y_pre,), out_dtypes=(BF16,),
                      epi=lambda acc, yp: (_gelu(yp) * jax.nn.sigmoid(acc),))[0]
            merged = jnp.concatenate([o_a, o_b], axis=1)
            x = _mm(f"ab_out_fwd{i}", merged, wg["ab_w_out"], b_layer=j, extras=(x,), epi=lambda acc, xr: (acc + xr,))[0]
            sv.update(proj=proj, o_a=o_a, lse=lse, tb=tb, states=((x0r, x0i), (x1r, x1i)), y_pre=y_pre, merged=merged)
        else:
            qkv = _mm_cols(f"c_qkv_fwd{i}", hn, wg["c_w_qkv"], j)
            nbias = _na_bias(p["c_rpb"][j])
            o, lse = _na_fwd(qkv, nbias, ch, f"na_fwd{i}")
            x = _mm(f"c_out_fwd{i}", o, wg["c_w_out"], b_layer=j, extras=(x,), epi=lambda acc, xr: (acc + xr,))[0]
            sv.update(qkv=qkv, o=o, lse=lse, nbias=nbias)
        sv["x_mid"] = x
        hn2 = _rms_fwd(x, p["norm_mlp"][i], f"norm_mlp_fwd{i}")
        h_pre = _mm_cols(f"mlp_up_fwd{i}", hn2, wg["mlp_w1"], i)
        x = _mm(f"mlp_down_fwd{i}", h_pre, wg["mlp_w2"], b_layer=i, a_fn=_relu_sq, extras=(x,), epi=lambda acc, xr: (acc + xr,))[0]
        sv.update(hn2=hn2, h_pre=h_pre)
        saved.append(sv)

    loss_sum, dx, g_final = _final_loss(x, p["norm_final"], target)

    g = {k: [None] * wg[k].shape[0] for k in BIG}
    g.update({k: [None] * p[k].shape[0] for k in ("norm_mix", "norm_mlp", "s5_lam_re", "s5_lam_im", "s5_log_step", "s5_b_re",
                                                  "s5_b_im", "s5_c_re", "s5_c_im", "s5_d", "c_rpb")})
    g_t5 = jnp.zeros_like(p["t5_bias"], dtype=F32)
    for i in reversed(range(depth)):
        j = i // 2
        sv = saved[i]
        dh = _mm(f"mlp_down_bwd{i}", dx, wg["mlp_w2"], b_layer=i, tb=True, extras=(sv["h_pre"],), out_dtypes=(BF16,),
                 epi=lambda acc, hp: (acc * (2.0 * jnp.maximum(hp.astype(F32), 0.0)),))[0]
        g["mlp_w2"][i] = _mm(f"mlp_w2_grad{i}", sv["h_pre"], dx, ta=True, a_fn=_relu_sq, tk=1024)[0]
        g["mlp_w1"][i] = _mm_cols_grad(f"mlp_w1_grad{i}", sv["hn2"], dh)
        dhn2 = _mm_cols_t(f"mlp_up_bwd{i}", dh, wg["mlp_w1"], i)
        dx, gn = _rms_bwd(sv["x_mid"], p["norm_mlp"][i], dhn2, dx, f"norm_mlp_bwd{i}")
        g["norm_mlp"][i] = gn[0]
        if i % 2 == 0:
            tb = sv["tb"]
            dmerged = _mm(f"ab_out_bwd{i}", dx, wg["ab_w_out"], b_layer=j, tb=True, out_dtypes=(BF16,))[0]
            g["ab_w_out"][j] = _mm(f"ab_w_out_grad{i}", sv["merged"], dx, ta=True, tk=1024)[0]
            def glu_epi(acc, yp, dob):
                sg = jax.nn.sigmoid(acc)
                dob = dob.astype(F32)
                return dob * _gelu(yp) * sg * (1.0 - sg), dob * sg
            dz, t1 = _mm(f"glu_bwd_z{i}", sv["y_pre"], wg["s5_w_glu"], b_layer=j, a_fn=_gelu, extras=(sv["y_pre"], dmerged),
                         extra_cols=(0, aw), epi=glu_epi, out_dtypes=(BF16, F32))
            dy_pre = _mm(f"glu_bwd_y{i}", dz, wg["s5_w_glu"], b_layer=j, tb=True, extras=(t1, sv["y_pre"]),
                         epi=lambda acc, t, yp: ((acc + t) * _gelu_grad(yp),), out_dtypes=(BF16,))[0]
            g["s5_w_glu"][j] = _mm(f"glu_w_grad{i}", sv["y_pre"], dz, ta=True, a_fn=_gelu, tk=1024)[0]
            r0 = _s5_bwd(sv["proj"], 3 * aw, dy_pre, *sv["states"][0], tb[0]["gtabs"], tb[0]["bre"], tb[0]["bim"],
                         tb[0]["cre"], tb[0]["cim"], False, None, f"s5_bwd_a{i}")
            r1 = _s5_bwd(sv["proj"], 3 * aw, dy_pre, *sv["states"][1], tb[1]["gtabs"], tb[1]["bre"], tb[1]["bim"],
                         tb[1]["cre"], tb[1]["cim"], True, (r0[0], p["s5_d"][j]), f"s5_bwd_b{i}")
            du = r1[0]
            g["s5_d"][j] = r1[7][0]
            gl_re, gl_im, gls, gb_re, gb_im, gc_re, gc_im = [], [], [], 0.0, 0.0, [], []
            for dnum, rr in enumerate((r0, r1)):
                da_re = jnp.sum(rr[1], axis=1).reshape(groups, pstate)
                da_im = jnp.sum(rr[2], axis=1).reshape(groups, pstate)
                dbb_re = jnp.transpose(_block_diag_take(rr[3], S5_GROUP, pstate), (0, 2, 1))
                dbb_im = jnp.transpose(_block_diag_take(rr[4], S5_GROUP, pstate), (0, 2, 1))
                _, vjp = jax.vjp(_s5_discretize, p["s5_lam_re"][j][dnum], p["s5_lam_im"][j][dnum], p["s5_log_step"][j][dnum],
                                 p["s5_b_re"][j], p["s5_b_im"][j])
                a, b, c, e, f = vjp((da_re, da_im, dbb_re, dbb_im))
                gl_re.append(a)
                gl_im.append(b)
                gls.append(c)
                gb_re, gb_im = gb_re + e, gb_im + f
                gc_re.append(jnp.transpose(_block_diag_take(rr[5], pstate, S5_GROUP), (0, 2, 1)))
                gc_im.append(jnp.transpose(_block_diag_take(rr[6], pstate, S5_GROUP), (0, 2, 1)))
            g["s5_lam_re"][j], g["s5_lam_im"][j], g["s5_log_step"][j] = jnp.stack(gl_re), jnp.stack(gl_im), jnp.stack(gls)
            g["s5_b_re"][j], g["s5_b_im"][j] = gb_re, gb_im
            g["s5_c_re"][j], g["s5_c_im"][j] = jnp.stack(gc_re), jnp.stack(gc_im)
            delta = _head_delta(dmerged, 0, sv["o_a"], f"dil_delta{i}")
            dq = dk = dv = 0.0
            for b, (_, dil) in enumerate(DILATED_BRANCHES):
                q1, k1, v1, db = _dil_bwd(sv["proj"], dmerged, sv["lse"], delta, dil_bias[b][0], dil_bias[b][1], dil, ah,
                                          f"dil_bwd_d{dil}_{i}")
                dq, dk, dv = dq + q1, dk + k1, dv + v1
                g_t5 = g_t5 + _t5_grad(db, dil)
            dproj = jnp.concatenate([dq, dk, dv, du], axis=1).astype(BF16)
            g["ab_w_in"][j] = _mm_cols_grad(f"ab_w_in_grad{i}", sv["hn"], dproj)
            dhn = _mm_cols_t(f"ab_in_bwd{i}", dproj, wg["ab_w_in"], j)
        else:
            do = _mm(f"c_out_bwd{i}", dx, wg["c_w_out"], b_layer=j, tb=True, out_dtypes=(BF16,))[0]
            g["c_w_out"][j] = _mm(f"c_w_out_grad{i}", sv["o"], dx, ta=True, tk=1024)[0]
            dq, dk, dv, db = _na_bwd(sv["qkv"], sv["o"], do, sv["lse"], sv["nbias"], ch, f"na_bwd{i}")
            g["c_rpb"][j] = _rpb_grad(db)
            dqkv = jnp.concatenate([dq, dk.astype(BF16), dv.astype(BF16)], axis=1)
            g["c_w_qkv"][j] = _mm_cols_grad(f"c_w_qkv_grad{i}", sv["hn"], dqkv)
            dhn = _mm_cols_t(f"c_qkv_bwd{i}", dqkv, wg["c_w_qkv"], j)
        dx, gn = _rms_bwd(sv["x"], p["norm_mix"][i], dhn, dx, f"norm_mix_bwd{i}")
        g["norm_mix"][i] = gn[0]
    g["t5_bias"] = g_t5
    g["norm_final"] = g_final[0]
    return loss_sum[0, 0], dx, g


BIG = ("ab_w_in", "ab_w_out", "s5_w_glu", "c_w_qkv", "c_w_out", "mlp_w1", "mlp_w2")
ROW_SHARDED = ("ab_w_out", "s5_w_glu", "c_w_out", "mlp_w2")
WEIGHTS = ("t5_bias", "ab_w_in", "ab_w_out", "s5_lam_re", "s5_lam_im", "s5_log_step", "s5_b_re", "s5_b_im", "s5_c_re",
           "s5_c_im", "s5_d", "s5_w_glu", "c_w_qkv", "c_w_out", "c_rpb", "norm_mix", "norm_mlp", "mlp_w1", "mlp_w2",
           "norm_final")


def _step(x, target, w, m, v):
    d = x.shape[-1]
    wg = {}
    for k in BIG:
        full = _all_gather(w[k].astype(BF16), f"gather_{k}")
        if k in ROW_SHARDED:
            full = full.reshape(full.shape[0], N_DEV * full.shape[2], full.shape[3])
        wg[k] = full
    small = {k: w[k] for k in WEIGHTS if k not in BIG}
    loss_sum, dx, g = _forward_backward(x[0], target[0], small, wg)
    loss = lax.psum(0.5 * loss_sum / d, ("x", "y", "c"))

    out = {}
    core = lax.axis_index("c").astype(jnp.int32).reshape(1)
    for k in BIG:
        nl, r, c = w[k].shape
        gs = [gl.reshape(N_DEV, r, c) for gl in g[k]]
        sib = _pair_exchange(gs, f"pair_exchange_{k}")
        ps = [_pair_add(gs[l], sib, l, core, f"pair_add_{k}{l}") for l in range(nl)]
        recv = _chip_exchange(ps, f"chip_exchange_{k}").reshape(N_CHIPS, nl * r, c)
        res = _adamw(w[k].reshape(nl * r, c), recv, m[k].reshape(nl * r, c), v[k].reshape(nl * r, c), f"adamw_{k}")
        out[k] = [a.reshape(nl, r, c) for a in res]
    names = [k for k in WEIGHTS if k not in BIG]
    def flat(tree):
        return jnp.concatenate([jnp.asarray(jnp.stack(tree[k]) if isinstance(tree[k], list) else tree[k], F32).reshape(-1)
                                for k in names])
    total = sum(int(np.prod(w[k].shape)) for k in names)
    rows = -(-total // LANES)
    rows = -(-rows // SUBLANES) * SUBLANES
    pad = rows * LANES - total
    def pack(tree):
        return jnp.pad(flat(tree), (0, pad)).reshape(rows, LANES)
    gall = _all_gather(pack(g)[None], "gather_small_grads")[0]
    res = _adamw(pack(w), gall, pack(m), pack(v), "adamw_small")
    off = 0
    for k in names:
        n = int(np.prod(w[k].shape))
        out[k] = [a.reshape(-1)[off:off + n].reshape(w[k].shape) for a in res]
        off += n
    return (loss, dx[None], *[out[k][0] for k in WEIGHTS], *[out[k][1] for k in WEIGHTS],
            *[out[k][2] for k in WEIGHTS], *[out[k][3] for k in WEIGHTS])


def kernel(x, t5_bias, ab_w_in, ab_w_out, s5_lam_re, s5_lam_im, s5_log_step, s5_b_re, s5_b_im, s5_c_re, s5_c_im, s5_d, s5_w_glu, c_w_qkv, c_w_out, c_rpb, norm_mix, norm_mlp, mlp_w1, mlp_w2, norm_final, loss_target, m_t5_bias, m_ab_w_in, m_ab_w_out, m_s5_lam_re, m_s5_lam_im, m_s5_log_step, m_s5_b_re, m_s5_b_im, m_s5_c_re, m_s5_c_im, m_s5_d, m_s5_w_glu, m_c_w_qkv, m_c_w_out, m_c_rpb, m_norm_mix, m_norm_mlp, m_mlp_w1, m_mlp_w2, m_norm_final, v_t5_bias, v_ab_w_in, v_ab_w_out, v_s5_lam_re, v_s5_lam_im, v_s5_log_step, v_s5_b_re, v_s5_b_im, v_s5_c_re, v_s5_c_im, v_s5_d, v_s5_w_glu, v_c_w_qkv, v_c_w_out, v_c_rpb, v_norm_mix, v_norm_mlp, v_mlp_w1, v_mlp_w2, v_norm_final):
    w = dict(t5_bias=t5_bias, ab_w_in=ab_w_in, ab_w_out=ab_w_out, s5_lam_re=s5_lam_re, s5_lam_im=s5_lam_im,
             s5_log_step=s5_log_step, s5_b_re=s5_b_re, s5_b_im=s5_b_im, s5_c_re=s5_c_re, s5_c_im=s5_c_im, s5_d=s5_d,
             s5_w_glu=s5_w_glu, c_w_qkv=c_w_qkv, c_w_out=c_w_out, c_rpb=c_rpb, norm_mix=norm_mix, norm_mlp=norm_mlp,
             mlp_w1=mlp_w1, mlp_w2=mlp_w2, norm_final=norm_final)
    m = dict(t5_bias=m_t5_bias, ab_w_in=m_ab_w_in, ab_w_out=m_ab_w_out, s5_lam_re=m_s5_lam_re, s5_lam_im=m_s5_lam_im,
             s5_log_step=m_s5_log_step, s5_b_re=m_s5_b_re, s5_b_im=m_s5_b_im, s5_c_re=m_s5_c_re, s5_c_im=m_s5_c_im,
             s5_d=m_s5_d, s5_w_glu=m_s5_w_glu, c_w_qkv=m_c_w_qkv, c_w_out=m_c_w_out, c_rpb=m_c_rpb, norm_mix=m_norm_mix,
             norm_mlp=m_norm_mlp, mlp_w1=m_mlp_w1, mlp_w2=m_mlp_w2, norm_final=m_norm_final)
    v = dict(t5_bias=v_t5_bias, ab_w_in=v_ab_w_in, ab_w_out=v_ab_w_out, s5_lam_re=v_s5_lam_re, s5_lam_im=v_s5_lam_im,
             s5_log_step=v_s5_log_step, s5_b_re=v_s5_b_re, s5_b_im=v_s5_b_im, s5_c_re=v_s5_c_re, s5_c_im=v_s5_c_im,
             s5_d=v_s5_d, s5_w_glu=v_s5_w_glu, c_w_qkv=v_c_w_qkv, c_w_out=v_c_w_out, c_rpb=v_c_rpb, norm_mix=v_norm_mix,
             norm_mlp=v_norm_mlp, mlp_w1=v_mlp_w1, mlp_w2=v_mlp_w2, norm_final=v_norm_final)
    return _step(x, loss_target, w, m, v)
```

```python
import math

import jax
import jax.numpy as jnp
import numpy as np
from jax import lax
from jax.experimental import pallas as pl
from jax.experimental.pallas import tpu as pltpu

F32 = jnp.float32
BF16 = jnp.bfloat16

N_DEV = 8
HEAD_DIM = 128
LANES = 128
QBLOCK = 128
DIL_HALF = 64
DILATED_BRANCHES = ((128, 1), (512, 4), (2048, 16))
S5_GROUP = 16
S5_GROUPS_PER_BLOCK = LANES // S5_GROUP
S5_CHUNK = 512
SUBLANES = 8
GRID_W = 64
NA_ROWS = 8
NA_COLS = 16
NA_GROUP = 4
NA_WIN = NA_GROUP + NA_ROWS - 1
NA_GROUPS_PER_STEP = 4
T5_BUCKETS = 32
T5_MAX_DISTANCE = 1024
RMS_EPS = 1e-6
NEG_INF = -1e30
ADAM_LR = 0.001
ADAM_B1 = 0.9
ADAM_B2 = 0.999
ADAM_EPS = 1e-08
ADAM_WD = 0.01
ADAM_STEP = 10
VMEM_LIMIT_BYTES = 56 * 1024 * 1024
MESH = pl.DeviceIdType.MESH


def _cparams(sem=None):
    return pltpu.CompilerParams(dimension_semantics=sem, vmem_limit_bytes=VMEM_LIMIT_BYTES)


def _tile(dim, pref):
    t = min(dim, pref)
    assert dim % t == 0, (dim, pref)
    return t


def _dot(a, b, ca, cb):
    return lax.dot_general(a, b, (((ca,), (cb,)), ((), ())), preferred_element_type=F32)


def _nn(a, b):
    return _dot(a, b, 1, 0)


def _nt(a, b):
    return _dot(a, b, 1, 1)


def _tn(a, b):
    return _dot(a, b, 0, 0)


HBM_SPEC = pl.BlockSpec(memory_space=pltpu.HBM)


def _split_comm_refs(comm, in_refs, out_refs, sem_refs):
    parts, i, o, s = [], 0, 0, 0
    for op in comm:
        ni, no, ns = len(op["ins"]), len(op["outs"]), len(op["sems"])
        parts.append((in_refs[i:i + ni], out_refs[o:o + no], sem_refs[s:s + ns]))
        i, o, s = i + ni, o + no, s + ns
    return parts


def _mm_call(name, a, b, a_spec, b_spec, grid, nk, out_shapes, out_specs, acc_shape,
             ta=False, tb=False, a_fn=None, epi=None, extras=(), extra_specs=(), comm=()):
    ne, no = len(extras), len(out_shapes)
    comm_ins = [x for op in comm for x in op["ins"]]
    comm_outs = [x for op in comm for x in op["outs"]]
    comm_sems = [x for op in comm for x in op["sems"]]
    nci, nco = len(comm_ins), len(comm_outs)
    total = grid[0] * grid[1] * grid[2]

    def body(a_ref, b_ref, *rest):
        ex, rest = rest[:ne], rest[ne:]
        cin, rest = rest[:nci], rest[nci:]
        outs, rest = rest[:no], rest[no:]
        cout, rest = rest[:nco], rest[nco:]
        acc, csem = rest[0], rest[1:]
        k = pl.program_id(2)
        step = (pl.program_id(0) * grid[1] + pl.program_id(1)) * grid[2] + k
        parts = _split_comm_refs(comm, cin, cout, csem)

        def phase(which, at):
            if comm:
                @pl.when(step == at)
                def _():
                    for op, refs in zip(comm, parts):
                        op["run"](which, *refs)

        phase("start", 0)
        phase("mid", total // 2)

        @pl.when(k == 0)
        def _():
            acc[...] = jnp.zeros_like(acc)

        av = a_ref[...]
        if a_fn is not None:
            av = a_fn(av)
        acc[...] += _dot(av.astype(BF16), b_ref[...].astype(BF16), 0 if ta else 1, 1 if tb else 0)

        @pl.when(k == nk - 1)
        def _():
            r = acc[...]
            res = epi(r, *[e[...] for e in ex]) if epi is not None else (r,)
            for o, v in zip(outs, res):
                o[...] = v.astype(o.dtype)

        phase("wait", total - 1)

    res = pl.pallas_call(
        body, grid=grid, in_specs=[a_spec, b_spec, *extra_specs] + [HBM_SPEC] * nci,
        out_specs=list(out_specs) + [HBM_SPEC] * nco, out_shape=list(out_shapes) + comm_outs,
        scratch_shapes=[pltpu.VMEM(acc_shape, F32)] + comm_sems,
        compiler_params=_cparams(("arbitrary",) * 3 if comm else ("parallel", "parallel", "arbitrary")), name=name,
    )(a, b, *extras, *comm_ins)
    o = no
    for op in comm:
        op["res"] = res[o:o + len(op["outs"])]
        o += len(op["outs"])
    return res[:no]


def _run_comm(comm, name):
    comm_ins = [x for op in comm for x in op["ins"]]
    comm_outs = [x for op in comm for x in op["outs"]]
    comm_sems = [x for op in comm for x in op["sems"]]
    nci, nco = len(comm_ins), len(comm_outs)

    def body(*refs):
        parts = _split_comm_refs(comm, refs[:nci], refs[nci:nci + nco], refs[nci + nco:])
        for which in ("start", "mid", "wait"):
            for op, r in zip(comm, parts):
                op["run"](which, *r)

    res = pl.pallas_call(
        body, in_specs=[HBM_SPEC] * nci, out_specs=[HBM_SPEC] * nco, out_shape=comm_outs, scratch_shapes=comm_sems,
        compiler_params=pltpu.CompilerParams(has_side_effects=True), name=name)(*comm_ins)
    o = 0
    for op in comm:
        op["res"] = res[o:o + len(op["outs"])]
        o += len(op["outs"])


def _mm(name, a, b, *, ta=False, tb=False, a_fn=None, epi=None, extras=(), extra_cols=None,
        out_dtypes=(F32,), tm=1024, tn=1024, tk=2048, comm=()):
    m, kdim = (a.shape[1], a.shape[0]) if ta else a.shape
    n = b.shape[0] if tb else b.shape[1]
    assert (b.shape[1] if tb else b.shape[0]) == kdim, (a.shape, b.shape)
    tm, tn, tk = _tile(m, tm), _tile(n, tn), _tile(kdim, tk)
    a_spec = pl.BlockSpec((tk, tm), lambda i, j, k: (k, i)) if ta else pl.BlockSpec((tm, tk), lambda i, j, k: (i, k))
    b_spec = pl.BlockSpec((tn, tk), lambda i, j, k: (j, k)) if tb else pl.BlockSpec((tk, tn), lambda i, j, k: (k, j))
    o_spec = pl.BlockSpec((tm, tn), lambda i, j, k: (i, j))
    extra_cols = extra_cols or (0,) * len(extras)
    especs = []
    for c0 in extra_cols:
        assert c0 % tn == 0
        cb = c0 // tn
        especs.append(pl.BlockSpec((tm, tn), lambda i, j, k, cb=cb: (i, cb + j)))
    return _mm_call(name, a, b, a_spec, b_spec, (m // tm, n // tn, kdim // tk), kdim // tk,
                    [jax.ShapeDtypeStruct((m, n), d) for d in out_dtypes], [o_spec] * len(out_dtypes), (tm, tn),
                    ta=ta, tb=tb, a_fn=a_fn, epi=epi, extras=extras, extra_specs=especs, comm=comm)


def _mm_cols(name, a, b3, *, out_dtype=BF16, tm=1024, tk=2048, comm=()):
    m, kdim = a.shape
    n = b3.shape[-1]
    tm, tk = _tile(m, tm), _tile(kdim, tk)
    return _mm_call(name, a, b3, pl.BlockSpec((tm, tk), lambda i, j, k: (i, k)),
                    pl.BlockSpec((None, tk, n), lambda i, j, k: (j, k, 0)),
                    (m // tm, N_DEV, kdim // tk), kdim // tk,
                    [jax.ShapeDtypeStruct((m, N_DEV * n), out_dtype)], [pl.BlockSpec((tm, n), lambda i, j, k: (i, j))],
                    (tm, n), comm=comm)[0]


def _mm_cols_t(name, a, b3, *, out_dtype=BF16, tm=1024, tn=1024, comm=()):
    m = a.shape[0]
    kout, n = b3.shape[-2:]
    tm, tn = _tile(m, tm), _tile(kout, tn)
    return _mm_call(name, a, b3, pl.BlockSpec((tm, n), lambda i, j, k: (i, k)),
                    pl.BlockSpec((None, tn, n), lambda i, j, k: (k, j, 0)),
                    (m // tm, kout // tn, N_DEV), N_DEV,
                    [jax.ShapeDtypeStruct((m, kout), out_dtype)], [pl.BlockSpec((tm, tn), lambda i, j, k: (i, j))],
                    (tm, tn), tb=True, comm=comm)[0]


def _mm_cols_grad(name, a, dy, *, tm=1024, tk=1024, comm=()):
    s, kout = a.shape
    n = dy.shape[1] // N_DEV
    tm, tk = _tile(kout, tm), _tile(s, tk)
    return _mm_call(name, a, dy, pl.BlockSpec((tk, tm), lambda i, j, k: (k, i)),
                    pl.BlockSpec((tk, n), lambda i, j, k: (k, j)),
                    (kout // tm, N_DEV, s // tk), s // tk,
                    [jax.ShapeDtypeStruct((N_DEV, kout, n), F32)], [pl.BlockSpec((None, tm, n), lambda i, j, k: (j, i, 0))],
                    (tm, n), ta=True, comm=comm)[0]


_GELU_C = math.sqrt(2.0 / math.pi)


def _gelu(x):
    return 0.5 * x * (1.0 + jnp.tanh(_GELU_C * (x + 0.044715 * x * x * x)))


def _gelu_grad(x):
    t = jnp.tanh(_GELU_C * (x + 0.044715 * x * x * x))
    return 0.5 * (1.0 + t) + 0.5 * x * (1.0 - t * t) * _GELU_C * (1.0 + 3.0 * 0.044715 * x * x)


def _relu_sq(x):
    r = jnp.maximum(x.astype(F32), 0.0)
    return r * r


def _rms_fwd(x, g, name):
    s, d = x.shape
    tr = _tile(s, 512)

    def body(x_ref, g_ref, o_ref):
        xv = x_ref[...]
        r = lax.rsqrt(jnp.mean(xv * xv, axis=-1, keepdims=True) + RMS_EPS)
        o_ref[...] = (xv * r * g_ref[...]).astype(BF16)

    return pl.pallas_call(
        body, grid=(s // tr,),
        in_specs=[pl.BlockSpec((tr, d), lambda i: (i, 0)), pl.BlockSpec((1, d), lambda i: (0, 0))],
        out_specs=pl.BlockSpec((tr, d), lambda i: (i, 0)), out_shape=jax.ShapeDtypeStruct((s, d), BF16),
        compiler_params=_cparams(("parallel",)), name=name)(x, g.reshape(1, d))


def _rms_bwd(x, g, dy, dres, name):
    s, d = x.shape
    tr = _tile(s, 512)

    def body(x_ref, g_ref, dy_ref, dres_ref, dx_ref, dg_ref):
        @pl.when(pl.program_id(0) == 0)
        def _():
            dg_ref[...] = jnp.zeros_like(dg_ref)

        xv = x_ref[...]
        dyv = dy_ref[...].astype(F32)
        r = lax.rsqrt(jnp.mean(xv * xv, axis=-1, keepdims=True) + RMS_EPS)
        xh = xv * r
        gdy = dyv * g_ref[...]
        dx_ref[...] = dres_ref[...] + r * (gdy - xh * jnp.mean(gdy * xh, axis=-1, keepdims=True))
        dg_ref[...] += jnp.sum(dyv * xh, axis=0, keepdims=True)

    return pl.pallas_call(
        body, grid=(s // tr,),
        in_specs=[pl.BlockSpec((tr, d), lambda i: (i, 0)), pl.BlockSpec((1, d), lambda i: (0, 0)),
                  pl.BlockSpec((tr, d), lambda i: (i, 0)), pl.BlockSpec((tr, d), lambda i: (i, 0))],
        out_specs=[pl.BlockSpec((tr, d), lambda i: (i, 0)), pl.BlockSpec((1, d), lambda i: (0, 0))],
        out_shape=[jax.ShapeDtypeStruct((s, d), F32), jax.ShapeDtypeStruct((1, d), F32)],
        compiler_params=_cparams(("arbitrary",)), name=name)(x, g.reshape(1, d), dy, dres)


def _final_loss(x, g, target):
    s, d = x.shape
    tr = _tile(s, 512)

    def body(x_ref, g_ref, t_ref, loss_ref, dx_ref, dg_ref):
        @pl.when(pl.program_id(0) == 0)
        def _():
            dg_ref[...] = jnp.zeros_like(dg_ref)
            loss_ref[...] = jnp.zeros_like(loss_ref)

        xv = x_ref[...]
        gv = g_ref[...]
        r = lax.rsqrt(jnp.mean(xv * xv, axis=-1, keepdims=True) + RMS_EPS)
        xh = xv * r
        err = xh * gv - t_ref[...]
        loss_ref[...] += jnp.sum(jnp.sum(err * err, axis=-1, keepdims=True), axis=0, keepdims=True)
        dyv = err * (1.0 / d)
        gdy = dyv * gv
        dx_ref[...] = r * (gdy - xh * jnp.mean(gdy * xh, axis=-1, keepdims=True))
        dg_ref[...] += jnp.sum(dyv * xh, axis=0, keepdims=True)

    return pl.pallas_call(
        body, grid=(s // tr,),
        in_specs=[pl.BlockSpec((tr, d), lambda i: (i, 0)), pl.BlockSpec((1, d), lambda i: (0, 0)),
                  pl.BlockSpec((tr, d), lambda i: (i, 0))],
        out_specs=[pl.BlockSpec((1, 1), lambda i: (0, 0)), pl.BlockSpec((tr, d), lambda i: (i, 0)),
                   pl.BlockSpec((1, d), lambda i: (0, 0))],
        out_shape=[jax.ShapeDtypeStruct((1, 1), F32), jax.ShapeDtypeStruct((s, d), F32),
                   jax.ShapeDtypeStruct((1, d), F32)],
        compiler_params=_cparams(("arbitrary",)), name="final_norm_loss")(x, g.reshape(1, d), target)


def _t5_bucket(rel):
    half = T5_BUCKETS // 2
    max_exact = half // 2
    n = jnp.abs(rel)
    nf = jnp.maximum(n, 1).astype(F32)
    large = max_exact + (jnp.log(nf / max_exact) / math.log(T5_MAX_DISTANCE / max_exact)
                         * (half - max_exact)).astype(jnp.int32)
    large = jnp.minimum(large, half - 1)
    return jnp.where(rel > 0, half, 0) + jnp.where(n < max_exact, n, large)


def _dil_offsets():
    i = jnp.arange(QBLOCK)[:, None]
    kk = jnp.arange(2 * QBLOCK)[None, :]
    return kk - DIL_HALF - i, (jnp.arange(QBLOCK)[None, :] + DIL_HALF) - jnp.arange(2 * QBLOCK)[:, None]


def _expand(table, onehot, name, tn=8192):
    r, n = table.shape[0], onehot.shape[1]
    tn = _tile(n, tn)

    def body(t_ref, oh_ref, o_ref):
        o_ref[...] = lax.dot_general(t_ref[...], oh_ref[...], (((1,), (0,)), ((), ())),
                                     precision=lax.Precision.HIGHEST, preferred_element_type=F32)

    return pl.pallas_call(
        body, grid=(n // tn,),
        in_specs=[pl.BlockSpec((r, LANES), lambda i: (0, 0)), pl.BlockSpec((LANES, tn), lambda i: (0, i))],
        out_specs=pl.BlockSpec((r, tn), lambda i: (0, i)), out_shape=jax.ShapeDtypeStruct((r, n), F32),
        compiler_params=_cparams(("parallel",)), name=name)(table, onehot)


def _pad_rows_lanes(t):
    r, c = t.shape
    return jnp.pad(t.astype(F32), ((0, -r % SUBLANES), (0, LANES - c)))


def _dil_bias(t5, dil):
    ah = t5.shape[1]
    off1, off2 = _dil_offsets()
    bucket = jnp.concatenate([_t5_bucket(off1 * dil).reshape(-1), _t5_bucket(off2 * dil).reshape(-1)])
    onehot = (jnp.arange(LANES)[:, None] == bucket[None, :]).astype(F32)
    b = _expand(_pad_rows_lanes(t5.T), onehot, f"t5_bias_d{dil}")[:ah]
    n1 = QBLOCK * 2 * QBLOCK
    return b[:, :n1].reshape(ah, QBLOCK, 2 * QBLOCK), b[:, n1:].reshape(ah, 2 * QBLOCK, QBLOCK)


def _window(p, c, n, cols=slice(None)):
    return jnp.concatenate([p[pl.ds(DIL_HALF, DIL_HALF), cols], c[:, cols], n[pl.ds(0, DIL_HALF), cols]], axis=0)


def _dil_specs(width, ncol_blocks, col_block, nb):
    def spec(dn):
        return pl.BlockSpec((QBLOCK, width), lambda r, n: (jnp.clip(n + dn, 0, nb - 1), r * ncol_blocks + col_block))
    return [spec(-1), spec(0), spec(1)]


def _dil_fwd(proj, bias1, dil, ah, name):
    s, wtot = proj.shape
    ln = s // dil
    nb = ln // QBLOCK
    assert nb * QBLOCK * dil == s
    aw = ah * HEAD_DIM
    wb = wtot // aw
    scale = 1.0 / math.sqrt(HEAD_DIM)
    pv = proj.reshape(ln, dil * wtot)

    def body(q_ref, kp, kc, kn, vp, vc, vn, b_ref, o_ref, lse_ref):
        n = pl.program_id(1)
        ii = lax.broadcasted_iota(jnp.int32, (QBLOCK, 2 * QBLOCK), 0)
        jj = lax.broadcasted_iota(jnp.int32, (QBLOCK, 2 * QBLOCK), 1)
        kpos = n * QBLOCK + jj - DIL_HALF
        valid = (jnp.abs(jj - DIL_HALF - ii) <= DIL_HALF) & (kpos >= 0) & (kpos < ln)
        for h in range(ah):
            cs = pl.ds(h * HEAD_DIM, HEAD_DIM)
            kw, vw = _window(kp, kc, kn, cs), _window(vp, vc, vn, cs)
            sc = jnp.where(valid, _nt(q_ref[:, cs], kw) * scale + b_ref[h], NEG_INF)
            m = jnp.max(sc, axis=-1, keepdims=True)
            p = jnp.exp(sc - m)
            l = jnp.sum(p, axis=-1, keepdims=True)
            o_ref[:, cs] = (_nn(p.astype(BF16), vw) / l).astype(BF16)
            lse_ref[:, cs] = jnp.broadcast_to(m + jnp.log(l), (QBLOCK, HEAD_DIM))

    ospec = pl.BlockSpec((QBLOCK, aw), lambda r, n: (n, r))
    o, lse = pl.pallas_call(
        body, grid=(dil, nb),
        in_specs=[_dil_specs(aw, wb, 0, nb)[1], *_dil_specs(aw, wb, 1, nb), *_dil_specs(aw, wb, 2, nb),
                  pl.BlockSpec((ah, QBLOCK, 2 * QBLOCK), lambda r, n: (0, 0, 0))],
        out_specs=[ospec, ospec],
        out_shape=[jax.ShapeDtypeStruct((ln, dil * aw), BF16), jax.ShapeDtypeStruct((ln, dil * aw), F32)],
        compiler_params=_cparams(("parallel", "parallel")), name=name,
    )(pv, pv, pv, pv, pv, pv, pv, bias1)
    return o.reshape(s, aw), lse.reshape(s, aw)


def _dil_merge(outs, lses):
    s, aw = outs[0].shape
    tr = _tile(s, 512)

    def body(o1, l1, o2, l2, o3, l3, o_ref, lse_ref):
        a, b, c = l1[...], l2[...], l3[...]
        m = jnp.maximum(jnp.maximum(a, b), c)
        w1, w2, w3 = jnp.exp(a - m), jnp.exp(b - m), jnp.exp(c - m)
        tot = w1 + w2 + w3
        o_ref[...] = ((w1 * o1[...].astype(F32) + w2 * o2[...].astype(F32) + w3 * o3[...].astype(F32)) / tot).astype(BF16)
        lse_ref[...] = m + jnp.log(tot)

    spec = pl.BlockSpec((tr, aw), lambda i: (i, 0))
    return pl.pallas_call(
        body, grid=(s // tr,), in_specs=[spec] * 6, out_specs=[spec, spec],
        out_shape=[jax.ShapeDtypeStruct((s, aw), BF16), jax.ShapeDtypeStruct((s, aw), F32)],
        compiler_params=_cparams(("parallel",)), name="dil_merge",
    )(outs[0], lses[0], outs[1], lses[1], outs[2], lses[2])


def _head_delta(do, do_col0, o, name):
    s, w = o.shape
    tr = _tile(s, 512)
    cb = do_col0 // HEAD_DIM

    def body(do_ref, o_ref, d_ref):
        d = jnp.sum(do_ref[...].astype(F32) * o_ref[...].astype(F32), axis=-1, keepdims=True)
        d_ref[...] = jnp.broadcast_to(d, d_ref.shape)

    return pl.pallas_call(
        body, grid=(s // tr, w // HEAD_DIM),
        in_specs=[pl.BlockSpec((tr, HEAD_DIM), lambda i, h: (i, cb + h)), pl.BlockSpec((tr, HEAD_DIM), lambda i, h: (i, h))],
        out_specs=pl.BlockSpec((tr, HEAD_DIM), lambda i, h: (i, h)), out_shape=jax.ShapeDtypeStruct((s, w), F32),
        compiler_params=_cparams(("parallel", "parallel")), name=name)(do, o)


def _dil_bwd(proj, dmerged, lse, delta, bias1, bias2, dil, ah, name):
    s, wtot = proj.shape
    ln = s // dil
    nb = ln // QBLOCK
    aw = ah * HEAD_DIM
    wb = wtot // aw
    wd = dmerged.shape[1] // aw
    scale = 1.0 / math.sqrt(HEAD_DIM)
    pv = proj.reshape(ln, dil * wtot)
    dov = dmerged.reshape(ln, dil * dmerged.shape[1])
    lv = lse.reshape(ln, dil * aw)
    dlv = delta.reshape(ln, dil * aw)

    def body(qp, qc, qn, kp, kc, kn, vp, vc, vn, dop, doc, don, lp, lc, lnx, dp, dc, dn, b1_ref, b2_ref,
             dq_ref, dk_ref, dv_ref, db_ref):
        n = pl.program_id(1)

        @pl.when((pl.program_id(0) == 0) & (n == 0))
        def _():
            db_ref[...] = jnp.zeros_like(db_ref)

        ii = lax.broadcasted_iota(jnp.int32, (QBLOCK, 2 * QBLOCK), 0)
        jj = lax.broadcasted_iota(jnp.int32, (QBLOCK, 2 * QBLOCK), 1)
        kpos = n * QBLOCK + jj - DIL_HALF
        valid = (jnp.abs(jj - DIL_HALF - ii) <= DIL_HALF) & (kpos >= 0) & (kpos < ln)
        ww = lax.broadcasted_iota(jnp.int32, (2 * QBLOCK, QBLOCK), 0)
        cc = lax.broadcasted_iota(jnp.int32, (2 * QBLOCK, QBLOCK), 1)
        qpos = n * QBLOCK - DIL_HALF + ww
        valid2 = (jnp.abs(cc + DIL_HALF - ww) <= DIL_HALF) & (qpos >= 0) & (qpos < ln)
        for h in range(ah):
            cs = pl.ds(h * HEAD_DIM, HEAD_DIM)
            kw, vw = _window(kp, kc, kn, cs), _window(vp, vc, vn, cs)
            sc = _nt(qc[:, cs], kw) * scale + b1_ref[h]
            lse2 = jnp.concatenate([lc[:, cs], lc[:, cs]], axis=1)
            p = jnp.where(valid, jnp.exp(jnp.where(valid, sc - lse2, 0.0)), 0.0)
            ds = p * (_nt(doc[:, cs], vw) - jnp.concatenate([dc[:, cs], dc[:, cs]], axis=1))
            dq_ref[:, cs] = _nn(ds.astype(BF16), kw) * scale
            db_ref[h] += ds
            qw, dow = _window(qp, qc, qn, cs), _window(dop, doc, don, cs)
            sc2 = _nt(qw, kc[:, cs]) * scale + b2_ref[h]
            p2 = jnp.where(valid2, jnp.exp(jnp.where(valid2, sc2 - _window(lp, lc, lnx, cs), 0.0)), 0.0)
            dv_ref[:, cs] = _tn(p2.astype(BF16), dow)
            ds2 = p2 * (_nt(dow, vc[:, cs]) - _window(dp, dc, dn, cs))
            dk_ref[:, cs] = _tn(ds2.astype(BF16), qw) * scale

    ospec = pl.BlockSpec((QBLOCK, aw), lambda r, n: (n, r))
    dq, dk, dv, db = pl.pallas_call(
        body, grid=(dil, nb),
        in_specs=[*_dil_specs(aw, wb, 0, nb), *_dil_specs(aw, wb, 1, nb), *_dil_specs(aw, wb, 2, nb),
                  *_dil_specs(aw, wd, 0, nb), *_dil_specs(aw, 1, 0, nb), *_dil_specs(aw, 1, 0, nb),
                  pl.BlockSpec((ah, QBLOCK, 2 * QBLOCK), lambda r, n: (0, 0, 0)),
                  pl.BlockSpec((ah, 2 * QBLOCK, QBLOCK), lambda r, n: (0, 0, 0))],
        out_specs=[ospec, ospec, ospec, pl.BlockSpec((ah, QBLOCK, 2 * QBLOCK), lambda r, n: (0, 0, 0))],
        out_shape=[jax.ShapeDtypeStruct((ln, dil * aw), F32)] * 3 + [jax.ShapeDtypeStruct((ah, QBLOCK, 2 * QBLOCK), F32)],
        compiler_params=_cparams(("arbitrary", "arbitrary")), name=name,
    )(pv, pv, pv, pv, pv, pv, pv, pv, pv, dov, dov, dov, lv, lv, lv, dlv, dlv, dlv, bias1, bias2)
    return dq.reshape(s, aw), dk.reshape(s, aw), dv.reshape(s, aw), db


def _bucket_sum(vals, onehot, name):
    r, n = vals.shape
    b = onehot.shape[1]

    def body(v_ref, oh_ref, o_ref):
        o_ref[...] = lax.dot_general(v_ref[...], oh_ref[...], (((1,), (0,)), ((), ())),
                                     precision=lax.Precision.HIGHEST, preferred_element_type=F32)

    tr = max(t for t in range(SUBLANES, 257, SUBLANES) if r % t == 0)
    return pl.pallas_call(
        body, grid=(r // tr,),
        in_specs=[pl.BlockSpec((tr, n), lambda i: (i, 0)), pl.BlockSpec((n, b), lambda i: (0, 0))],
        out_specs=pl.BlockSpec((tr, b), lambda i: (i, 0)), out_shape=jax.ShapeDtypeStruct((r, b), F32),
        compiler_params=_cparams(("parallel",)), name=name)(vals, onehot)


def _t5_grad(dbias, dil):
    ah = dbias.shape[0]
    off1, _ = _dil_offsets()
    bucket = _t5_bucket(off1 * dil).reshape(-1)
    inside = (jnp.abs(off1) <= DIL_HALF).reshape(-1)
    onehot = ((bucket[:, None] == jnp.arange(LANES)[None, :]) & inside[:, None]).astype(F32)
    vals = jnp.pad(dbias.reshape(ah, -1), ((0, -ah % SUBLANES), (0, 0)))
    return _bucket_sum(vals, onehot, f"t5_grad_d{dil}")[:ah, :T5_BUCKETS].T


def _na_table_rows():
    ro = -np.ones((3, NA_GROUP, NA_WIN), np.int64)
    for i in range(NA_GROUP):
        for j in range(NA_WIN):
            if j < NA_ROWS:
                ro[0, i, j] = j - i + NA_ROWS - 1
            if i <= j < i + NA_ROWS:
                ro[1, i, j] = j - i + NA_ROWS // 2 - 1
            if j >= NA_WIN - NA_ROWS:
                ro[2, i, j] = j - i
    return ro


def _na_bias(rpb):
    ch, nro, nco = rpb.shape
    c = np.arange(GRID_W)
    col_start = np.clip(c - NA_COLS // 2, 0, GRID_W - NA_COLS)
    col_ok = (c[None, :] >= col_start[:, None]) & (c[None, :] < col_start[:, None] + NA_COLS)
    col_idx = np.clip(c[None, :] - c[:, None] + NA_COLS - 1, 0, 2 * NA_COLS - 2).reshape(-1)
    onehot = (np.arange(LANES)[:, None] == col_idx[None, :]).astype(np.float32)
    table = jnp.pad(rpb.astype(F32).reshape(ch * nro, nco), ((0, -(ch * nro) % SUBLANES), (0, LANES - nco)))
    by_row = _expand(table, jnp.asarray(onehot), "rpb_bias", tn=GRID_W * GRID_W)[:ch * nro]
    by_row = jnp.where(jnp.asarray(col_ok.reshape(-1))[None, :], by_row, NEG_INF).reshape(ch, nro, GRID_W, GRID_W)
    neg = jnp.full((ch, GRID_W, GRID_W), NEG_INF, F32)
    tiles = [by_row[:, r] if r >= 0 else neg for r in _na_table_rows().reshape(-1)]
    b = jnp.stack(tiles, axis=1).reshape(ch, 3, NA_GROUP, NA_WIN, GRID_W, GRID_W)
    return jnp.transpose(b, (0, 1, 2, 4, 3, 5)).reshape(ch, 3, NA_GROUP * GRID_W, NA_WIN * GRID_W)


def _na_group(g, rows):
    ngroups = rows // NA_GROUP
    ws = jnp.clip(g * NA_GROUP - NA_ROWS // 2, 0, rows - NA_WIN)
    return pl.multiple_of(ws * GRID_W, GRID_W), jnp.where(g == 0, 0, jnp.where(g == ngroups - 1, 2, 1))


def _na_fwd(qkv, bias, ch, name):
    s = qkv.shape[0]
    rows = s // GRID_W
    assert rows >= NA_WIN and rows % (NA_GROUP * NA_GROUPS_PER_STEP) == 0
    cw = ch * HEAD_DIM
    tg = NA_GROUP * GRID_W
    tq = NA_GROUPS_PER_STEP * tg
    win = NA_WIN * GRID_W
    scale = 1.0 / math.sqrt(HEAD_DIM)

    def body(q_ref, k_ref, v_ref, b_ref, o_ref, lse_ref):
        gb = pl.program_id(1)
        for i in range(NA_GROUPS_PER_STEP):
            st, var = _na_group(gb * NA_GROUPS_PER_STEP + i, rows)
            kw, vw = k_ref[pl.ds(st, win), :], v_ref[pl.ds(st, win), :]
            qs = pl.ds(i * tg, tg)
            sc = _nt(q_ref[qs, :], kw) * scale + b_ref[var]
            m = jnp.max(sc, axis=-1, keepdims=True)
            p = jnp.exp(sc - m)
            l = jnp.sum(p, axis=-1, keepdims=True)
            o_ref[qs, :] = (_nn(p.astype(BF16), vw) / l).astype(BF16)
            lse_ref[qs, :] = jnp.broadcast_to(m + jnp.log(l), (tg, HEAD_DIM))

    ospec = pl.BlockSpec((tq, HEAD_DIM), lambda h, gb: (gb, h))
    return pl.pallas_call(
        body, grid=(ch, s // tq),
        in_specs=[pl.BlockSpec((tq, HEAD_DIM), lambda h, gb: (gb, h)),
                  pl.BlockSpec((s, HEAD_DIM), lambda h, gb: (0, ch + h)),
                  pl.BlockSpec((s, HEAD_DIM), lambda h, gb: (0, 2 * ch + h)),
                  pl.BlockSpec((None, 3, tg, win), lambda h, gb: (h, 0, 0, 0))],
        out_specs=[ospec, ospec],
        out_shape=[jax.ShapeDtypeStruct((s, cw), BF16), jax.ShapeDtypeStruct((s, cw), F32)],
        compiler_params=_cparams(("parallel", "parallel")), name=name)(qkv, qkv, qkv, bias)


def _na_bwd(qkv, o, do, lse, bias, ch, name):
    s = qkv.shape[0]
    rows = s // GRID_W
    cw = ch * HEAD_DIM
    tg = NA_GROUP * GRID_W
    tq = NA_GROUPS_PER_STEP * tg
    win = NA_WIN * GRID_W
    scale = 1.0 / math.sqrt(HEAD_DIM)

    def body(q_ref, k_ref, v_ref, o_ref, do_ref, lse_ref, b_ref, dq_ref, dk_ref, dv_ref, db_ref):
        gb = pl.program_id(1)

        @pl.when(gb == 0)
        def _():
            dk_ref[...] = jnp.zeros_like(dk_ref)
            dv_ref[...] = jnp.zeros_like(dv_ref)
            db_ref[...] = jnp.zeros_like(db_ref)

        for i in range(NA_GROUPS_PER_STEP):
            st, var = _na_group(gb * NA_GROUPS_PER_STEP + i, rows)
            ws = pl.ds(st, win)
            kw, vw = k_ref[ws, :], v_ref[ws, :]
            qs = pl.ds(i * tg, tg)
            q, dov = q_ref[qs, :], do_ref[qs, :]
            sc = _nt(q, kw) * scale + b_ref[var]
            p = jnp.exp(sc - lse_ref[qs, :][:, :1])
            delta = jnp.sum(dov.astype(F32) * o_ref[qs, :].astype(F32), axis=-1, keepdims=True)
            ds = p * (_nt(dov, vw) - delta)
            dsb = ds.astype(BF16)
            dq_ref[qs, :] = (_nn(dsb, kw) * scale).astype(BF16)
            dk_ref[ws, :] += _tn(dsb, q) * scale
            dv_ref[ws, :] += _tn(p.astype(BF16), dov)
            db_ref[var] += ds

    qspec = pl.BlockSpec((tq, HEAD_DIM), lambda h, gb: (gb, h))
    kvspec = pl.BlockSpec((s, HEAD_DIM), lambda h, gb: (0, h))
    bspec = pl.BlockSpec((None, 3, tg, win), lambda h, gb: (h, 0, 0, 0))
    return pl.pallas_call(
        body, grid=(ch, s // tq),
        in_specs=[qspec, pl.BlockSpec((s, HEAD_DIM), lambda h, gb: (0, ch + h)),
                  pl.BlockSpec((s, HEAD_DIM), lambda h, gb: (0, 2 * ch + h)), qspec, qspec, qspec, bspec],
        out_specs=[qspec, kvspec, kvspec, bspec],
        out_shape=[jax.ShapeDtypeStruct((s, cw), BF16), jax.ShapeDtypeStruct((s, cw), F32),
                   jax.ShapeDtypeStruct((s, cw), F32), jax.ShapeDtypeStruct((ch, 3, tg, win), F32)],
        compiler_params=_cparams(("parallel", "arbitrary")), name=name)(qkv, qkv, qkv, o, do, lse, bias)


def _rpb_grad(dbias):
    ch = dbias.shape[0]
    ntile = 3 * NA_GROUP * NA_WIN
    c = np.arange(GRID_W)
    col_idx = (c[None, :] - c[:, None] + NA_COLS - 1).reshape(-1)
    oh_col = (col_idx[:, None] == np.arange(LANES)[None, :]).astype(np.float32)
    d6 = dbias.reshape(ch, 3, NA_GROUP, GRID_W, NA_WIN, GRID_W)
    vals = jnp.transpose(d6, (0, 1, 2, 4, 3, 5)).reshape(ch * ntile, GRID_W * GRID_W)
    by_col = _bucket_sum(vals, jnp.asarray(oh_col), "rpb_grad_cols")
    npad = 2 * LANES
    oh_row = np.zeros((npad, LANES), np.float32)
    for t, r in enumerate(_na_table_rows().reshape(-1)):
        if r >= 0:
            oh_row[t, r] = 1.0
    by_col = jnp.pad(by_col.reshape(ch, ntile, LANES), ((0, 0), (0, npad - ntile), (0, 0)))
    vals2 = jnp.transpose(by_col, (0, 2, 1)).reshape(ch * LANES, npad)
    by_row = _bucket_sum(vals2, jnp.asarray(oh_row), "rpb_grad_rows")
    return jnp.transpose(by_row.reshape(ch, LANES, LANES), (0, 2, 1))[:, :2 * NA_ROWS - 1, :2 * NA_COLS - 1]


def _s5_discretize(lam_re, lam_im, log_step, b_re, b_im):
    step = jnp.exp(log_step.astype(F32))[:, None]
    lr = jnp.minimum(lam_re.astype(F32), -1e-4)
    li = lam_im.astype(F32)
    mag = jnp.exp(lr * step)
    ab_re = mag * jnp.cos(li * step)
    ab_im = mag * jnp.sin(li * step)
    den = lr * lr + li * li
    zr = ((ab_re - 1.0) * lr + ab_im * li) / den
    zi = (ab_im * lr - (ab_re - 1.0) * li) / den
    br = b_re.astype(F32)
    bi = b_im.astype(F32)
    return ab_re, ab_im, zr[..., None] * br - zi[..., None] * bi, zr[..., None] * bi + zi[..., None] * br


def _scan_tables(a_re, a_im, rev):
    ar, ai = a_re.reshape(-1), a_im.reshape(-1)
    pows = [(ar, ai)]
    for _ in range(SUBLANES - 1):
        pr, pi = pows[-1]
        pows.append((pr * ar - pi * ai, pr * ai + pi * ar))
    row = jnp.arange(SUBLANES)[:, None]
    tabs = []
    for k in (1, 2, 4):
        keep = (row < SUBLANES - k) if rev else (row >= k)
        tabs += [jnp.where(keep, pows[k - 1][0][None, :], 0.0), jnp.where(keep, pows[k - 1][1][None, :], 0.0)]
    order = list(range(SUBLANES - 1, -1, -1)) if rev else list(range(SUBLANES))
    tabs += [jnp.stack([pows[i][0] for i in order]), jnp.stack([pows[i][1] for i in order])]
    t = jnp.stack(tabs)
    nblk = t.shape[-1] // (4 * LANES)
    return jnp.transpose(t.reshape(8, SUBLANES, nblk, 4 * LANES), (2, 0, 1, 3))


def _block_diag(w):
    g, a, b = w.shape
    nblk = g // S5_GROUPS_PER_BLOCK
    eye = jnp.eye(S5_GROUPS_PER_BLOCK, dtype=w.dtype)
    w4 = w.reshape(nblk, S5_GROUPS_PER_BLOCK, a, b)
    return (w4[:, :, :, None, :] * eye[None, :, None, :, None]).reshape(nblk, S5_GROUPS_PER_BLOCK * a, S5_GROUPS_PER_BLOCK * b)


def _block_diag_take(w, a, b):
    nblk = w.shape[0]
    w5 = w.reshape(nblk, S5_GROUPS_PER_BLOCK, a, S5_GROUPS_PER_BLOCK, b)
    eye = jnp.eye(S5_GROUPS_PER_BLOCK, dtype=w.dtype)
    return jnp.sum(w5 * eye[None, :, None, :, None], axis=3).reshape(nblk * S5_GROUPS_PER_BLOCK, a, b)


def _scan_tile(r, i, tab_ref, carry, rev):
    for lvl, k in enumerate((1, 2, 4)):
        mr, mi = tab_ref[2 * lvl], tab_ref[2 * lvl + 1]
        sh = SUBLANES - k if rev else k
        rr, ri = pltpu.roll(r, sh, 0), pltpu.roll(i, sh, 0)
        r, i = r + (mr * rr - mi * ri), i + (mr * ri + mi * rr)
    pr, pi = tab_ref[6], tab_ref[7]
    cr, ci = carry
    return r + (pr * cr - pi * ci), i + (pr * ci + pi * cr)


def _s5_fwd(proj, ucol0, tabs, bre, bim, cre, cim, rev, final, name):
    s = proj.shape[0]
    nblk = tabs.shape[0]
    bw = nblk * LANES
    w = 4 * LANES
    t = _tile(s, S5_CHUNK)
    nc, nt = s // t, t // SUBLANES
    ub = ucol0 // LANES
    cm = (lambda c: nc - 1 - c) if rev else (lambda c: c)
    last = 0 if rev else SUBLANES - 1

    def body(u_ref, tab_ref, bre_ref, bim_ref, cre_ref, cim_ref, *rest):
        if final is not None:
            yo_ref, d_ref, y_ref, xr_ref, xi_ref, xr_s, xi_s, car_r, car_i = rest
        else:
            y_ref, xr_ref, xi_ref, xr_s, xi_s, car_r, car_i = rest

        @pl.when(pl.program_id(1) == 0)
        def _():
            car_r[...] = jnp.zeros_like(car_r)
            car_i[...] = jnp.zeros_like(car_i)

        u = u_ref[...]
        xr_s[...] = _nn(u, bre_ref[...])
        xi_s[...] = _nn(u, bim_ref[...])

        def tile(tt, carry):
            k = nt - 1 - tt if rev else tt
            rows = pl.ds(pl.multiple_of(k * SUBLANES, SUBLANES), SUBLANES)
            r, i = _scan_tile(xr_s[rows, :], xi_s[rows, :], tab_ref, carry, rev)
            xr_s[rows, :] = r
            xi_s[rows, :] = i
            return (jnp.broadcast_to(r[last:last + 1, :], r.shape), jnp.broadcast_to(i[last:last + 1, :], i.shape))

        carry = lax.fori_loop(0, nt, tile, (car_r[...], car_i[...]))
        car_r[...], car_i[...] = carry
        xr, xi = xr_s[...].astype(BF16), xi_s[...].astype(BF16)
        y = _nn(xr, cre_ref[...]) - _nn(xi, cim_ref[...])
        if final is not None:
            y = y + yo_ref[...] + d_ref[...] * u.astype(F32)
        y_ref[...] = y
        xr_ref[...] = xr
        xi_ref[...] = xi

    yspec = pl.BlockSpec((t, LANES), lambda j, c: (cm(c), j))
    xspec = pl.BlockSpec((t, w), lambda j, c: (cm(c), j))
    in_specs = [pl.BlockSpec((t, LANES), lambda j, c: (cm(c), ub + j)),
                pl.BlockSpec((None, 8, SUBLANES, w), lambda j, c: (j, 0, 0, 0)),
                pl.BlockSpec((None, LANES, w), lambda j, c: (j, 0, 0)), pl.BlockSpec((None, LANES, w), lambda j, c: (j, 0, 0)),
                pl.BlockSpec((None, w, LANES), lambda j, c: (j, 0, 0)), pl.BlockSpec((None, w, LANES), lambda j, c: (j, 0, 0))]
    args = [proj, tabs, bre, bim, cre, cim]
    if final is not None:
        in_specs += [yspec, pl.BlockSpec((1, LANES), lambda j, c: (0, j))]
        args += [final[0], final[1].reshape(1, bw)]
    return pl.pallas_call(
        body, grid=(nblk, nc), in_specs=in_specs, out_specs=[yspec, xspec, xspec],
        out_shape=[jax.ShapeDtypeStruct((s, bw), F32), jax.ShapeDtypeStruct((s, nblk * w), BF16),
                   jax.ShapeDtypeStruct((s, nblk * w), BF16)],
        scratch_shapes=[pltpu.VMEM((t, w), F32), pltpu.VMEM((t, w), F32), pltpu.VMEM((SUBLANES, w), F32),
                        pltpu.VMEM((SUBLANES, w), F32)],
        compiler_params=_cparams(("parallel", "arbitrary")), name=name)(*args)


def _s5_bwd(proj, ucol0, dy, xr, xi, gtabs, bre, bim, cre, cim, rev, final, name):
    s = proj.shape[0]
    nblk = gtabs.shape[0]
    bw = nblk * LANES
    w = 4 * LANES
    t = _tile(s, S5_CHUNK)
    nc, nt = s // t, t // SUBLANES
    ub = ucol0 // LANES
    grev = not rev
    cm = (lambda c: nc - 1 - c) if grev else (lambda c: c)
    last = 0 if grev else SUBLANES - 1
    nfin = 2 if final is not None else 0

    def body(dy_ref, u_ref, xr_ref, xi_ref, tab_ref, bre_ref, bim_ref, cre_ref, cim_ref, *rest):
        fin, rest = rest[:nfin], rest[nfin:]
        du_ref, dar_ref, dai_ref, dbr_ref, dbi_ref, dcr_ref, dci_ref = rest[:7]
        rest = rest[7:]
        if final is not None:
            dd_ref, rest = rest[0], rest[1:]
        gr_s, gi_s, xr_s, xi_s, car_r, car_i = rest

        @pl.when(pl.program_id(1) == 0)
        def _():
            for ref in (car_r, car_i, dar_ref, dai_ref, dbr_ref, dbi_ref, dcr_ref, dci_ref):
                ref[...] = jnp.zeros_like(ref)
            if final is not None:
                dd_ref[...] = jnp.zeros_like(dd_ref)

        dyv = dy_ref[...]
        dyb = dyv.astype(BF16)
        u = u_ref[...]
        xrb, xib = xr_ref[...], xi_ref[...]
        gr_s[...] = _nt(dyb, cre_ref[...])
        gi_s[...] = -_nt(dyb, cim_ref[...])
        xr_s[...] = xrb.astype(F32)
        xi_s[...] = xib.astype(F32)
        rowid = lax.broadcasted_iota(jnp.int32, (SUBLANES, w), 0)

        def tile(tt, carry):
            k = nt - 1 - tt if grev else tt
            rows = pl.ds(pl.multiple_of(k * SUBLANES, SUBLANES), SUBLANES)
            r, i = _scan_tile(gr_s[rows, :], gi_s[rows, :], tab_ref, carry, grev)
            gr_s[rows, :] = r
            gi_s[rows, :] = i
            if grev:
                er = jnp.where(rowid == SUBLANES - 1, carry[0], pltpu.roll(r, SUBLANES - 1, 0))
                ei = jnp.where(rowid == SUBLANES - 1, carry[1], pltpu.roll(i, SUBLANES - 1, 0))
            else:
                er = jnp.where(rowid == 0, carry[0], pltpu.roll(r, 1, 0))
                ei = jnp.where(rowid == 0, carry[1], pltpu.roll(i, 1, 0))
            sr, si = xr_s[rows, :], xi_s[rows, :]
            dar_ref[...] += er * sr + ei * si
            dai_ref[...] += ei * sr - er * si
            return (jnp.broadcast_to(r[last:last + 1, :], r.shape), jnp.broadcast_to(i[last:last + 1, :], i.shape))

        carry = lax.fori_loop(0, nt, tile, (car_r[...], car_i[...]))
        car_r[...], car_i[...] = carry
        gr, gi = gr_s[...].astype(BF16), gi_s[...].astype(BF16)
        du = _nt(gr, bre_ref[...]) + _nt(gi, bim_ref[...])
        if final is not None:
            du = du + fin[0][...] + fin[1][...] * dyv.astype(F32)
            dd_ref[...] += jnp.sum(dyv.astype(F32) * u.astype(F32), axis=0, keepdims=True)
        du_ref[...] = du
        dbr_ref[...] += _tn(u, gr)
        dbi_ref[...] += _tn(u, gi)
        dcr_ref[...] += _tn(xrb, dyb)
        dci_ref[...] -= _tn(xib, dyb)

    yspec = pl.BlockSpec((t, LANES), lambda j, c: (cm(c), j))
    xspec = pl.BlockSpec((t, w), lambda j, c: (cm(c), j))
    bspec = pl.BlockSpec((None, LANES, w), lambda j, c: (j, 0, 0))
    cspec = pl.BlockSpec((None, w, LANES), lambda j, c: (j, 0, 0))
    aspec = pl.BlockSpec((None, SUBLANES, w), lambda j, c: (j, 0, 0))
    dspec = pl.BlockSpec((1, LANES), lambda j, c: (0, j))
    in_specs = [yspec, pl.BlockSpec((t, LANES), lambda j, c: (cm(c), ub + j)), xspec, xspec,
                pl.BlockSpec((None, 8, SUBLANES, w), lambda j, c: (j, 0, 0, 0)), bspec, bspec, cspec, cspec]
    args = [dy, proj, xr, xi, gtabs, bre, bim, cre, cim]
    out_specs = [yspec, aspec, aspec, bspec, bspec, cspec, cspec]
    out_shape = [jax.ShapeDtypeStruct((s, bw), F32)] + [jax.ShapeDtypeStruct((nblk, SUBLANES, w), F32)] * 2 \
        + [jax.ShapeDtypeStruct((nblk, LANES, w), F32)] * 2 + [jax.ShapeDtypeStruct((nblk, w, LANES), F32)] * 2
    if final is not None:
        in_specs += [yspec, dspec]
        args += [final[0], final[1].reshape(1, bw)]
        out_specs.append(dspec)
        out_shape.append(jax.ShapeDtypeStruct((1, bw), F32))
    return pl.pallas_call(
        body, grid=(nblk, nc), in_specs=in_specs, out_specs=out_specs, out_shape=out_shape,
        scratch_shapes=[pltpu.VMEM((t, w), F32)] * 4 + [pltpu.VMEM((SUBLANES, w), F32)] * 2,
        compiler_params=_cparams(("parallel", "arbitrary")), name=name)(*args)


N_CHIPS = 4


def _place():
    mx, my, mc = lax.axis_index("x"), lax.axis_index("y"), lax.axis_index("c")
    return (mx, my, mc), (mx, my, 1 - mc), [(1 - mx, my), (mx, 1 - my), (1 - mx, 1 - my)]


def _gather_op(x):
    r, c = x.shape

    def run(which, ins, outs, sems):
        (x_ref,), (out_ref,), (send_sems, recv_sems, local_sem) = ins, outs, sems
        me, sibling, chips = _place()
        mc = me[2]

        def slot(px, py, pc):
            return out_ref.at[4 * px + 2 * py + pc]

        def copy(k, block, to, src=None):
            return pltpu.make_async_remote_copy(src_ref=slot(*block) if src is None else src, dst_ref=slot(*block),
                                                send_sem=send_sems.at[k], recv_sem=recv_sems.at[k],
                                                device_id=to, device_id_type=MESH)

        mine = pltpu.make_async_copy(x_ref, slot(*me), local_sem)
        first = [copy(0, me, sibling, src=x_ref)] + [copy(1 + j, me, (*chip, mc), src=x_ref) for j, chip in enumerate(chips)]
        passed = [copy(4 + j, (*chip, mc), sibling) for j, chip in enumerate(chips)]
        if which == "start":
            mine.start()
            for cp in first:
                cp.start()
        elif which == "mid":
            for j, chip in enumerate(chips):
                copy(1 + j, (*chip, mc), me).wait_recv()
                passed[j].start()
        else:
            copy(0, sibling, me).wait_recv()
            for j, chip in enumerate(chips):
                copy(4 + j, (*chip, 1 - mc), me).wait_recv()
            for cp in first + passed:
                cp.wait_send()
            mine.wait()

    return dict(ins=[x], outs=[jax.ShapeDtypeStruct((N_DEV, r, c), x.dtype)], run=run,
                sems=[pltpu.SemaphoreType.DMA((N_DEV - 1,)), pltpu.SemaphoreType.DMA((N_DEV - 1,)),
                      pltpu.SemaphoreType.DMA])


def _pair_op(gs):
    nt = len(gs)

    def run(which, g_refs, out_refs, sems):
        send_sems, recv_sems = sems
        me, sibling, _ = _place()
        copies = [pltpu.make_async_remote_copy(src_ref=g_refs[t].at[2 * q + (1 - me[2])], dst_ref=out_refs[t].at[q],
                                               send_sem=send_sems.at[t, q], recv_sem=recv_sems.at[t, q],
                                               device_id=sibling, device_id_type=MESH)
                  for t in range(nt) for q in range(N_CHIPS)]
        if which == "start":
            for cp in copies:
                cp.start()
        elif which == "wait":
            for cp in copies:
                cp.wait_recv()
            for cp in copies:
                cp.wait_send()

    return dict(ins=list(gs), outs=[jax.ShapeDtypeStruct((N_CHIPS,) + g.shape[1:], F32) for g in gs], run=run,
                sems=[pltpu.SemaphoreType.DMA((nt, N_CHIPS)), pltpu.SemaphoreType.DMA((nt, N_CHIPS))])


def _pair_add(g, t, core, name):
    _, r, c = g.shape
    tr = r
    while tr * c > 512 * 1024 and tr % 32 == 0:
        tr //= 2

    def body(core_ref, g_ref, t_ref, o_ref):
        o_ref[...] = (g_ref[...] + t_ref[...]).astype(BF16)

    return pl.pallas_call(
        body,
        grid_spec=pltpu.PrefetchScalarGridSpec(
            num_scalar_prefetch=1, grid=(N_CHIPS, r // tr),
            in_specs=[pl.BlockSpec((None, tr, c), lambda q, i, core_ref: (2 * q + core_ref[0], i, 0)),
                      pl.BlockSpec((None, tr, c), lambda q, i, core_ref: (q, i, 0))],
            out_specs=pl.BlockSpec((None, tr, c), lambda q, i, core_ref: (q, i, 0))),
        out_shape=jax.ShapeDtypeStruct((N_CHIPS, r, c), BF16),
        compiler_params=_cparams(("parallel", "parallel")), name=name)(core, g, t)


def _chip_op(ps):
    nt = len(ps)

    def run(which, p_refs, out_refs, sems):
        send_sems, recv_sems, local_sems = sems
        me, _, chips = _place()
        mychip = 2 * me[0] + me[1]
        owns = [pltpu.make_async_copy(p_refs[t].at[mychip], out_refs[t].at[mychip], local_sems.at[t]) for t in range(nt)]
        sends = [pltpu.make_async_remote_copy(src_ref=p_refs[t].at[2 * px + py], dst_ref=out_refs[t].at[mychip],
                                              send_sem=send_sems.at[t, j], recv_sem=recv_sems.at[t, j],
                                              device_id=(px, py, me[2]), device_id_type=MESH)
                 for t in range(nt) for j, (px, py) in enumerate(chips)]
        if which == "start":
            for cp in owns + sends:
                cp.start()
        elif which == "wait":
            for t in range(nt):
                for j, (px, py) in enumerate(chips):
                    pltpu.make_async_remote_copy(src_ref=p_refs[t].at[2 * px + py], dst_ref=out_refs[t].at[2 * px + py],
                                                 send_sem=send_sems.at[t, j], recv_sem=recv_sems.at[t, j],
                                                 device_id=(px, py, me[2]), device_id_type=MESH).wait_recv()
            for cp in sends:
                cp.wait_send()
            for cp in owns:
                cp.wait()

    return dict(ins=list(ps), outs=[jax.ShapeDtypeStruct(p.shape, p.dtype) for p in ps], run=run,
                sems=[pltpu.SemaphoreType.DMA((nt, N_CHIPS - 1)), pltpu.SemaphoreType.DMA((nt, N_CHIPS - 1)),
                      pltpu.SemaphoreType.DMA((nt,))])


def _adamw(w, gstack, m, v, name, layer=None, prev=None):
    r, c = w.shape[-2:]
    nstack = gstack.shape[0]
    tr = r
    while tr * c > 128 * 1024 and tr % 32 == 0:
        tr //= 2
    c1 = 1.0 - ADAM_B1 ** ADAM_STEP
    c2 = 1.0 - ADAM_B2 ** ADAM_STEP

    def body(w_ref, g_ref, m_ref, v_ref, *rest):
        go_ref, d_ref, mo_ref, vo_ref = rest[-4:]
        g = g_ref[0].astype(F32)
        for p in range(1, nstack):
            g = g + g_ref[p].astype(F32)
        mn = ADAM_B1 * m_ref[...] + (1.0 - ADAM_B1) * g
        vn = ADAM_B2 * v_ref[...] + (1.0 - ADAM_B2) * (g * g)
        go_ref[...] = g
        mo_ref[...] = mn
        vo_ref[...] = vn
        d_ref[...] = -ADAM_LR * ((mn / c1) / (jnp.sqrt(vn / c2) + ADAM_EPS) + ADAM_WD * w_ref[...])

    if layer is None:
        spec = pl.BlockSpec((tr, c), lambda i: (i, 0))
        full = (r, c)
    else:
        spec = pl.BlockSpec((None, tr, c), lambda i: (layer, i, 0))
        full = w.shape
    prev = list(prev) if prev is not None else []
    return pl.pallas_call(
        body, grid=(r // tr,),
        in_specs=[spec, pl.BlockSpec((nstack, tr, c), lambda i: (0, i, 0)), spec, spec] + [pl.BlockSpec(memory_space=pl.ANY)] * len(prev),
        out_specs=[spec] * 4, out_shape=[jax.ShapeDtypeStruct(full, F32)] * 4,
        input_output_aliases={4 + n: n for n in range(len(prev))},
        compiler_params=_cparams(("parallel",)), name=name)(w, gstack, m, v, *prev)


def _s5_tables(lam_re, lam_im, log_step, b_re, b_im, c_re, c_im):
    out = []
    for d in range(2):
        ab_re, ab_im, bb_re, bb_im = _s5_discretize(lam_re[d], lam_im[d], log_step[d], b_re, b_im)
        rev = d == 1
        out.append(dict(
            tabs=_scan_tables(ab_re, ab_im, rev), gtabs=_scan_tables(ab_re, -ab_im, not rev),
            bre=_block_diag(jnp.transpose(bb_re, (0, 2, 1))).astype(BF16), bim=_block_diag(jnp.transpose(bb_im, (0, 2, 1))).astype(BF16),
            cre=_block_diag(jnp.transpose(c_re[d], (0, 2, 1))).astype(BF16), cim=_block_diag(jnp.transpose(c_im[d], (0, 2, 1))).astype(BF16)))
    return out


def _layer_tensors(i):
    j = i // 2
    mixer = [("ab_w_in", j), ("ab_w_out", j), ("s5_w_glu", j)] if i % 2 == 0 else [("c_w_qkv", j), ("c_w_out", j)]
    return mixer + [("mlp_w1", i), ("mlp_w2", i)]


class _WeightGather:
    def __init__(self, shards, depth):
        self.shards, self.depth, self.ops = shards, depth, {}

    def _op(self, key):
        self.ops[key] = _gather_op(self.shards[key[0]][key[1]])
        return self.ops[key]

    def start(self):
        _run_comm([self._op(key) for key in _layer_tensors(0)], "gather_layer0")

    def carry(self, i, slot):
        if i + 1 >= self.depth:
            return ()
        t = _layer_tensors(i + 1)
        plan = {"in": t[-2:-1], "out": t[1:-2], "up": t[-1:], "down": t[:1]}
        return [self._op(key) for key in plan[slot]]

    def get(self, name, l):
        full = self.ops[(name, l)]["res"][0]
        return full.reshape(-1, full.shape[-1]) if name in ROW_SHARDED else full


class _GradExchange:
    def __init__(self, core, depth):
        self.core, self.depth, self.g, self.recv_ops = core, depth, {}, []

    def put(self, name, l, g):
        self.g[(name, l)] = g.reshape(N_DEV, -1, g.shape[-1])

    def _sums(self, keys, pair):
        return {key: _pair_add(self.g[key], pair["res"][n], self.core, f"pair_add_{key[0]}{key[1]}") for n, key in enumerate(keys)}

    def _chip(self, keys):
        op = _chip_op([self.ps[key] for key in keys])
        self.recv_ops.append((keys, op))
        return [op]

    def carry(self, i, slot):
        if i + 1 >= self.depth:
            return ()
        keys = _layer_tensors(i + 1)
        if slot == "down_bwd":
            self.pair = _pair_op([self.g[key] for key in keys])
            return [self.pair]
        if slot == "w1_grad":
            self.ps = self._sums(keys, self.pair)
            return self._chip(keys[-2:])
        return self._chip(keys[:-2])

    def finish(self):
        keys = _layer_tensors(0)
        pair = _pair_op([self.g[key] for key in keys])
        _run_comm([pair], "pair_exchange_layer0")
        self.ps = self._sums(keys, pair)
        _run_comm(self._chip(keys), "chip_exchange_layer0")
        return {key: op["res"][n] for keys, op in self.recv_ops for n, key in enumerate(keys)}


def _forward_backward(x, target, p, wsrc, gsink):
    s, d = x.shape
    depth = p["norm_mix"].shape[0]
    ah = p["t5_bias"].shape[1]
    aw = ah * HEAD_DIM
    ch = p["c_rpb"].shape[1]
    groups, pstate = p["s5_lam_re"].shape[2:]
    bw = groups * S5_GROUP
    assert aw + bw == d and ch * HEAD_DIM == d

    dil_bias = [_dil_bias(p["t5_bias"], dil) for _, dil in DILATED_BRANCHES]
    saved = []
    for i in range(depth):
        j = i // 2
        sv = dict(x=x)
        hn = _rms_fwd(x, p["norm_mix"][i], f"norm_mix_fwd{i}")
        sv["hn"] = hn
        if i % 2 == 0:
            proj = _mm_cols(f"ab_in_fwd{i}", hn, wsrc.get("ab_w_in", j), comm=wsrc.carry(i, "in"))
            outs = [_dil_fwd(proj, dil_bias[b][0], dil, ah, f"dil_fwd_d{dil}_{i}") for b, (_, dil) in enumerate(DILATED_BRANCHES)]
            o_a, lse = _dil_merge([o for o, _ in outs], [l for _, l in outs])
            tb = _s5_tables(p["s5_lam_re"][j], p["s5_lam_im"][j], p["s5_log_step"][j], p["s5_b_re"][j], p["s5_b_im"][j],
                            p["s5_c_re"][j], p["s5_c_im"][j])
            y0, x0r, x0i = _s5_fwd(proj, 3 * aw, tb[0]["tabs"], tb[0]["bre"], tb[0]["bim"], tb[0]["cre"], tb[0]["cim"],
                                   False, None, f"s5_fwd_a{i}")
            y_pre, x1r, x1i = _s5_fwd(proj, 3 * aw, tb[1]["tabs"], tb[1]["bre"], tb[1]["bim"], tb[1]["cre"], tb[1]["cim"],
                                      True, (y0, p["s5_d"][j]), f"s5_fwd_b{i}")
            o_b = _mm(f"glu_fwd{i}", y_pre, wsrc.get("s5_w_glu", j), a_fn=_gelu, extras=(y_pre,), out_dtypes=(BF16,),
                      epi=lambda acc, yp: (_gelu(yp) * jax.nn.sigmoid(acc),))[0]
            merged = jnp.concatenate([o_a, o_b], axis=1)
            x = _mm(f"ab_out_fwd{i}", merged, wsrc.get("ab_w_out", j), extras=(x,), epi=lambda acc, xr: (acc + xr,),
                    comm=wsrc.carry(i, "out"))[0]
            sv.update(proj=proj, o_a=o_a, lse=lse, tb=tb, states=((x0r, x0i), (x1r, x1i)), y_pre=y_pre, merged=merged)
        else:
            qkv = _mm_cols(f"c_qkv_fwd{i}", hn, wsrc.get("c_w_qkv", j), comm=wsrc.carry(i, "in"))
            nbias = _na_bias(p["c_rpb"][j])
            o, lse = _na_fwd(qkv, nbias, ch, f"na_fwd{i}")
            x = _mm(f"c_out_fwd{i}", o, wsrc.get("c_w_out", j), extras=(x,), epi=lambda acc, xr: (acc + xr,),
                    comm=wsrc.carry(i, "out"))[0]
            sv.update(qkv=qkv, o=o, lse=lse, nbias=nbias)
        sv["x_mid"] = x
        hn2 = _rms_fwd(x, p["norm_mlp"][i], f"norm_mlp_fwd{i}")
        h_pre = _mm_cols(f"mlp_up_fwd{i}", hn2, wsrc.get("mlp_w1", i), comm=wsrc.carry(i, "up"))
        x = _mm(f"mlp_down_fwd{i}", h_pre, wsrc.get("mlp_w2", i), a_fn=_relu_sq, extras=(x,), epi=lambda acc, xr: (acc + xr,),
                comm=wsrc.carry(i, "down"))[0]
        sv.update(hn2=hn2, h_pre=h_pre)
        saved.append(sv)

    loss_sum, dx, g_final = _final_loss(x, p["norm_final"], target)

    g = ({k: [None] * p[k].shape[0] for k in ("norm_mix", "norm_mlp", "s5_lam_re", "s5_lam_im", "s5_log_step", "s5_b_re",
                                                  "s5_b_im", "s5_c_re", "s5_c_im", "s5_d", "c_rpb")})
    g_t5 = jnp.zeros_like(p["t5_bias"], dtype=F32)
    for i in reversed(range(depth)):
        j = i // 2
        sv = saved[i]
        dh = _mm(f"mlp_down_bwd{i}", dx, wsrc.get("mlp_w2", i), tb=True, extras=(sv["h_pre"],), out_dtypes=(BF16,),
                 epi=lambda acc, hp: (acc * (2.0 * jnp.maximum(hp.astype(F32), 0.0)),), comm=gsink.carry(i, "down_bwd"))[0]
        gsink.put("mlp_w2", i, _mm(f"mlp_w2_grad{i}", sv["h_pre"], dx, ta=True, a_fn=_relu_sq, tk=1024)[0])
        gsink.put("mlp_w1", i, _mm_cols_grad(f"mlp_w1_grad{i}", sv["hn2"], dh, comm=gsink.carry(i, "w1_grad")))
        dhn2 = _mm_cols_t(f"mlp_up_bwd{i}", dh, wsrc.get("mlp_w1", i), comm=gsink.carry(i, "up_bwd"))
        dx, gn = _rms_bwd(sv["x_mid"], p["norm_mlp"][i], dhn2, dx, f"norm_mlp_bwd{i}")
        g["norm_mlp"][i] = gn[0]
        if i % 2 == 0:
            tb = sv["tb"]
            dmerged = _mm(f"ab_out_bwd{i}", dx, wsrc.get("ab_w_out", j), tb=True, out_dtypes=(BF16,))[0]
            gsink.put("ab_w_out", j, _mm(f"ab_w_out_grad{i}", sv["merged"], dx, ta=True, tk=1024)[0])
            def glu_epi(acc, yp, dob):
                sg = jax.nn.sigmoid(acc)
                dob = dob.astype(F32)
                return dob * _gelu(yp) * sg * (1.0 - sg), dob * sg
            dz, t1 = _mm(f"glu_bwd_z{i}", sv["y_pre"], wsrc.get("s5_w_glu", j), a_fn=_gelu, extras=(sv["y_pre"], dmerged),
                         extra_cols=(0, aw), epi=glu_epi, out_dtypes=(BF16, F32))
            dy_pre = _mm(f"glu_bwd_y{i}", dz, wsrc.get("s5_w_glu", j), tb=True, extras=(t1, sv["y_pre"]),
                         epi=lambda acc, t, yp: ((acc + t) * _gelu_grad(yp),), out_dtypes=(BF16,))[0]
            gsink.put("s5_w_glu", j, _mm(f"glu_w_grad{i}", sv["y_pre"], dz, ta=True, a_fn=_gelu, tk=1024)[0])
            r0 = _s5_bwd(sv["proj"], 3 * aw, dy_pre, *sv["states"][0], tb[0]["gtabs"], tb[0]["bre"], tb[0]["bim"],
                         tb[0]["cre"], tb[0]["cim"], False, None, f"s5_bwd_a{i}")
            r1 = _s5_bwd(sv["proj"], 3 * aw, dy_pre, *sv["states"][1], tb[1]["gtabs"], tb[1]["bre"], tb[1]["bim"],
                         tb[1]["cre"], tb[1]["cim"], True, (r0[0], p["s5_d"][j]), f"s5_bwd_b{i}")
            du = r1[0]
            g["s5_d"][j] = r1[7][0]
            gl_re, gl_im, gls, gb_re, gb_im, gc_re, gc_im = [], [], [], 0.0, 0.0, [], []
            for dnum, rr in enumerate((r0, r1)):
                da_re = jnp.sum(rr[1], axis=1).reshape(groups, pstate)
                da_im = jnp.sum(rr[2], axis=1).reshape(groups, pstate)
                dbb_re = jnp.transpose(_block_diag_take(rr[3], S5_GROUP, pstate), (0, 2, 1))
                dbb_im = jnp.transpose(_block_diag_take(rr[4], S5_GROUP, pstate), (0, 2, 1))
                _, vjp = jax.vjp(_s5_discretize, p["s5_lam_re"][j][dnum], p["s5_lam_im"][j][dnum], p["s5_log_step"][j][dnum],
                                 p["s5_b_re"][j], p["s5_b_im"][j])
                a, b, c, e, f = vjp((da_re, da_im, dbb_re, dbb_im))
                gl_re.append(a)
                gl_im.append(b)
                gls.append(c)
                gb_re, gb_im = gb_re + e, gb_im + f
                gc_re.append(jnp.transpose(_block_diag_take(rr[5], pstate, S5_GROUP), (0, 2, 1)))
                gc_im.append(jnp.transpose(_block_diag_take(rr[6], pstate, S5_GROUP), (0, 2, 1)))
            g["s5_lam_re"][j], g["s5_lam_im"][j], g["s5_log_step"][j] = jnp.stack(gl_re), jnp.stack(gl_im), jnp.stack(gls)
            g["s5_b_re"][j], g["s5_b_im"][j] = gb_re, gb_im
            g["s5_c_re"][j], g["s5_c_im"][j] = jnp.stack(gc_re), jnp.stack(gc_im)
            delta = _head_delta(dmerged, 0, sv["o_a"], f"dil_delta{i}")
            dq = dk = dv = 0.0
            for b, (_, dil) in enumerate(DILATED_BRANCHES):
                q1, k1, v1, db = _dil_bwd(sv["proj"], dmerged, sv["lse"], delta, dil_bias[b][0], dil_bias[b][1], dil, ah,
                                          f"dil_bwd_d{dil}_{i}")
                dq, dk, dv = dq + q1, dk + k1, dv + v1
                g_t5 = g_t5 + _t5_grad(db, dil)
            dproj = jnp.concatenate([dq, dk, dv, du], axis=1).astype(BF16)
            gsink.put("ab_w_in", j, _mm_cols_grad(f"ab_w_in_grad{i}", sv["hn"], dproj))
            dhn = _mm_cols_t(f"ab_in_bwd{i}", dproj, wsrc.get("ab_w_in", j))
        else:
            do = _mm(f"c_out_bwd{i}", dx, wsrc.get("c_w_out", j), tb=True, out_dtypes=(BF16,))[0]
            gsink.put("c_w_out", j, _mm(f"c_w_out_grad{i}", sv["o"], dx, ta=True, tk=1024)[0])
            dq, dk, dv, db = _na_bwd(sv["qkv"], sv["o"], do, sv["lse"], sv["nbias"], ch, f"na_bwd{i}")
            g["c_rpb"][j] = _rpb_grad(db)
            dqkv = jnp.concatenate([dq, dk.astype(BF16), dv.astype(BF16)], axis=1)
            gsink.put("c_w_qkv", j, _mm_cols_grad(f"c_w_qkv_grad{i}", sv["hn"], dqkv))
            dhn = _mm_cols_t(f"c_qkv_bwd{i}", dqkv, wsrc.get("c_w_qkv", j))
        dx, gn = _rms_bwd(sv["x"], p["norm_mix"][i], dhn, dx, f"norm_mix_bwd{i}")
        g["norm_mix"][i] = gn[0]
    g["t5_bias"] = g_t5
    g["norm_final"] = g_final[0]
    return loss_sum[0, 0], dx, g


BIG = ("ab_w_in", "ab_w_out", "s5_w_glu", "c_w_qkv", "c_w_out", "mlp_w1", "mlp_w2")
ROW_SHARDED = ("ab_w_out", "s5_w_glu", "c_w_out", "mlp_w2")
WEIGHTS = ("t5_bias", "ab_w_in", "ab_w_out", "s5_lam_re", "s5_lam_im", "s5_log_step", "s5_b_re", "s5_b_im", "s5_c_re",
           "s5_c_im", "s5_d", "s5_w_glu", "c_w_qkv", "c_w_out", "c_rpb", "norm_mix", "norm_mlp", "mlp_w1", "mlp_w2",
           "norm_final")


def _step(x, target, w, m, v):
    d = x.shape[-1]
    depth = w["norm_mix"].shape[0]
    wsrc = _WeightGather({k: w[k].astype(BF16) for k in BIG}, depth)
    wsrc.start()
    gsink = _GradExchange(lax.axis_index("c").astype(jnp.int32).reshape(1), depth)
    small = {k: w[k] for k in WEIGHTS if k not in BIG}
    loss_sum, dx, g = _forward_backward(x[0], target[0], small, wsrc, gsink)
    loss = lax.psum(0.5 * loss_sum / d, ("x", "y", "c"))

    out = {}
    recv = gsink.finish()
    for k in BIG:
        res = None
        for l in range(w[k].shape[0]):
            res = _adamw(w[k], recv[(k, l)], m[k], v[k], f"adamw_{k}{l}", layer=l, prev=res)
        out[k] = res
    names = [k for k in WEIGHTS if k not in BIG]
    def flat(tree):
        return jnp.concatenate([jnp.asarray(jnp.stack(tree[k]) if isinstance(tree[k], list) else tree[k], F32).reshape(-1)
                                for k in names])
    total = sum(int(np.prod(w[k].shape)) for k in names)
    rows = -(-total // LANES)
    rows = -(-rows // SUBLANES) * SUBLANES
    pad = rows * LANES - total
    def pack(tree):
        return jnp.pad(flat(tree), (0, pad)).reshape(rows, LANES)
    small_op = _gather_op(pack(g))
    _run_comm([small_op], "gather_small_grads")
    res = _adamw(pack(w), small_op["res"][0], pack(m), pack(v), "adamw_small")
    off = 0
    for k in names:
        n = int(np.prod(w[k].shape))
        out[k] = [a.reshape(-1)[off:off + n].reshape(w[k].shape) for a in res]
        off += n
    return (loss, dx[None], *[out[k][0] for k in WEIGHTS], *[out[k][1] for k in WEIGHTS],
            *[out[k][2] for k in WEIGHTS], *[out[k][3] for k in WEIGHTS])


def kernel(x, t5_bias, ab_w_in, ab_w_out, s5_lam_re, s5_lam_im, s5_log_step, s5_b_re, s5_b_im, s5_c_re, s5_c_im, s5_d, s5_w_glu, c_w_qkv, c_w_out, c_rpb, norm_mix, norm_mlp, mlp_w1, mlp_w2, norm_final, loss_target, m_t5_bias, m_ab_w_in, m_ab_w_out, m_s5_lam_re, m_s5_lam_im, m_s5_log_step, m_s5_b_re, m_s5_b_im, m_s5_c_re, m_s5_c_im, m_s5_d, m_s5_w_glu, m_c_w_qkv, m_c_w_out, m_c_rpb, m_norm_mix, m_norm_mlp, m_mlp_w1, m_mlp_w2, m_norm_final, v_t5_bias, v_ab_w_in, v_ab_w_out, v_s5_lam_re, v_s5_lam_im, v_s5_log_step, v_s5_b_re, v_s5_b_im, v_s5_c_re, v_s5_c_im, v_s5_d, v_s5_w_glu, v_c_w_qkv, v_c_w_out, v_c_rpb, v_norm_mix, v_norm_mlp, v_mlp_w1, v_mlp_w2, v_norm_final):
    w = dict(t5_bias=t5_bias, ab_w_in=ab_w_in, ab_w_out=ab_w_out, s5_lam_re=s5_lam_re, s5_lam_im=s5_lam_im,
             s5_log_step=s5_log_step, s5_b_re=s5_b_re, s5_b_im=s5_b_im, s5_c_re=s5_c_re, s5_c_im=s5_c_im, s5_d=s5_d,
             s5_w_glu=s5_w_glu, c_w_qkv=c_w_qkv, c_w_out=c_w_out, c_rpb=c_rpb, norm_mix=norm_mix, norm_mlp=norm_mlp,
             mlp_w1=mlp_w1, mlp_w2=mlp_w2, norm_final=norm_final)
    m = dict(t5_bias=m_t5_bias, ab_w_in=m_ab_w_in, ab_w_out=m_ab_w_out, s5_lam_re=m_s5_lam_re, s5_lam_im=m_s5_lam_im,
             s5_log_step=m_s5_log_step, s5_b_re=m_s5_b_re, s5_b_im=m_s5_b_im, s5_c_re=m_s5_c_re, s5_c_im=m_s5_c_im,
             s5_d=m_s5_d, s5_w_glu=m_s5_w_glu, c_w_qkv=m_c_w_qkv, c_w_out=m_c_w_out, c_rpb=m_c_rpb, norm_mix=m_norm_mix,
             norm_mlp=m_norm_mlp, mlp_w1=m_mlp_w1, mlp_w2=m_mlp_w2, norm_final=m_norm_final)
    v = dict(t5_bias=v_t5_bias, ab_w_in=v_ab_w_in, ab_w_out=v_ab_w_out, s5_lam_re=v_s5_lam_re, s5_lam_im=v_s5_lam_im,
             s5_log_step=v_s5_log_step, s5_b_re=v_s5_b_re, s5_b_im=v_s5_b_im, s5_c_re=v_s5_c_re, s5_c_im=v_s5_c_im,
             s5_d=v_s5_d, s5_w_glu=v_s5_w_glu, c_w_qkv=v_c_w_qkv, c_w_out=v_c_w_out, c_rpb=v_c_rpb, norm_mix=v_norm_mix,
             norm_mlp=v_norm_mlp, mlp_w1=v_mlp_w1, mlp_w2=v_mlp_w2, norm_final=v_norm_final)
    return _step(x, loss_target, w, m, v)
```

```python
import math

import jax
import jax.numpy as jnp
import numpy as np
from jax import lax
from jax.experimental import pallas as pl
from jax.experimental.pallas import tpu as pltpu

F32 = jnp.float32
BF16 = jnp.bfloat16

N_DEV = 8
HEAD_DIM = 128
LANES = 128
QBLOCK = 128
DIL_HALF = 64
DILATED_BRANCHES = ((128, 1), (512, 4), (2048, 16))
S5_GROUP = 16
S5_GROUPS_PER_BLOCK = LANES // S5_GROUP
S5_CHUNK = 512
SUBLANES = 8
GRID_W = 64
NA_ROWS = 8
NA_COLS = 16
NA_GROUP = 4
NA_WIN = NA_GROUP + NA_ROWS - 1
NA_GROUPS_PER_STEP = 4
T5_BUCKETS = 32
T5_MAX_DISTANCE = 1024
RMS_EPS = 1e-6
NEG_INF = -1e30
ADAM_LR = 0.001
ADAM_B1 = 0.9
ADAM_B2 = 0.999
ADAM_EPS = 1e-08
ADAM_WD = 0.01
ADAM_STEP = 10
VMEM_LIMIT_BYTES = 56 * 1024 * 1024
MESH = pl.DeviceIdType.MESH


def _cparams(sem=None):
    return pltpu.CompilerParams(dimension_semantics=sem, vmem_limit_bytes=VMEM_LIMIT_BYTES)


def _tile(dim, pref):
    t = min(dim, pref)
    assert dim % t == 0, (dim, pref)
    return t


def _dot(a, b, ca, cb):
    return lax.dot_general(a, b, (((ca,), (cb,)), ((), ())), preferred_element_type=F32)


def _nn(a, b):
    return _dot(a, b, 1, 0)


def _nt(a, b):
    return _dot(a, b, 1, 1)


def _tn(a, b):
    return _dot(a, b, 0, 0)


HBM_SPEC = pl.BlockSpec(memory_space=pltpu.HBM)


def _split_comm_refs(comm, in_refs, out_refs, sem_refs):
    parts, i, o, s = [], 0, 0, 0
    for op in comm:
        ni, no, ns = len(op["ins"]), len(op["outs"]), len(op["sems"])
        parts.append((in_refs[i:i + ni], out_refs[o:o + no], sem_refs[s:s + ns]))
        i, o, s = i + ni, o + no, s + ns
    return parts


def _mm_call(name, a, b, a_spec, b_spec, grid, nk, out_shapes, out_specs, acc_shape,
             ta=False, tb=False, a_fn=None, epi=None, extras=(), extra_specs=(), comm=()):
    ne, no = len(extras), len(out_shapes)
    comm_ins = [x for op in comm for x in op["ins"]]
    comm_outs = [x for op in comm for x in op["outs"]]
    comm_sems = [x for op in comm for x in op["sems"]]
    nci, nco = len(comm_ins), len(comm_outs)
    total = grid[0] * grid[1] * grid[2]

    def body(a_ref, b_ref, *rest):
        ex, rest = rest[:ne], rest[ne:]
        cin, rest = rest[:nci], rest[nci:]
        outs, rest = rest[:no], rest[no:]
        cout, rest = rest[:nco], rest[nco:]
        acc, csem = rest[0], rest[1:]
        k = pl.program_id(2)
        step = (pl.program_id(0) * grid[1] + pl.program_id(1)) * grid[2] + k
        parts = _split_comm_refs(comm, cin, cout, csem)

        def phase(which, at):
            if comm:
                @pl.when(step == at)
                def _():
                    for op, refs in zip(comm, parts):
                        op["run"](which, *refs)

        phase("start", 0)
        phase("mid", total - 1 - total // 8)

        @pl.when(k == 0)
        def _():
            acc[...] = jnp.zeros_like(acc)

        av = a_ref[...]
        if a_fn is not None:
            av = a_fn(av)
        acc[...] += _dot(av.astype(BF16), b_ref[...].astype(BF16), 0 if ta else 1, 1 if tb else 0)

        @pl.when(k == nk - 1)
        def _():
            r = acc[...]
            res = epi(r, *[e[...] for e in ex]) if epi is not None else (r,)
            for o, v in zip(outs, res):
                o[...] = v.astype(o.dtype)

        phase("wait", total - 1)

    res = pl.pallas_call(
        body, grid=grid, in_specs=[a_spec, b_spec, *extra_specs] + [HBM_SPEC] * nci,
        out_specs=list(out_specs) + [HBM_SPEC] * nco, out_shape=list(out_shapes) + comm_outs,
        scratch_shapes=[pltpu.VMEM(acc_shape, F32)] + comm_sems,
        compiler_params=_cparams(("arbitrary",) * 3 if comm else ("parallel", "parallel", "arbitrary")), name=name,
    )(a, b, *extras, *comm_ins)
    o = no
    for op in comm:
        op["res"] = res[o:o + len(op["outs"])]
        o += len(op["outs"])
    return res[:no]


def _run_comm(comm, name):
    comm_ins = [x for op in comm for x in op["ins"]]
    comm_outs = [x for op in comm for x in op["outs"]]
    comm_sems = [x for op in comm for x in op["sems"]]
    nci, nco = len(comm_ins), len(comm_outs)

    def body(*refs):
        parts = _split_comm_refs(comm, refs[:nci], refs[nci:nci + nco], refs[nci + nco:])
        for which in ("start", "mid", "wait"):
            for op, r in zip(comm, parts):
                op["run"](which, *r)

    res = pl.pallas_call(
        body, in_specs=[HBM_SPEC] * nci, out_specs=[HBM_SPEC] * nco, out_shape=comm_outs, scratch_shapes=comm_sems,
        compiler_params=pltpu.CompilerParams(has_side_effects=True), name=name)(*comm_ins)
    o = 0
    for op in comm:
        op["res"] = res[o:o + len(op["outs"])]
        o += len(op["outs"])


def _mm(name, a, b, *, ta=False, tb=False, a_fn=None, epi=None, extras=(), extra_cols=None,
        out_dtypes=(F32,), tm=1024, tn=1024, tk=2048, comm=()):
    m, kdim = (a.shape[1], a.shape[0]) if ta else a.shape
    n = b.shape[0] if tb else b.shape[1]
    assert (b.shape[1] if tb else b.shape[0]) == kdim, (a.shape, b.shape)
    tm, tn, tk = _tile(m, tm), _tile(n, tn), _tile(kdim, tk)
    a_spec = pl.BlockSpec((tk, tm), lambda i, j, k: (k, i)) if ta else pl.BlockSpec((tm, tk), lambda i, j, k: (i, k))
    b_spec = pl.BlockSpec((tn, tk), lambda i, j, k: (j, k)) if tb else pl.BlockSpec((tk, tn), lambda i, j, k: (k, j))
    o_spec = pl.BlockSpec((tm, tn), lambda i, j, k: (i, j))
    extra_cols = extra_cols or (0,) * len(extras)
    especs = []
    for c0 in extra_cols:
        assert c0 % tn == 0
        cb = c0 // tn
        especs.append(pl.BlockSpec((tm, tn), lambda i, j, k, cb=cb: (i, cb + j)))
    return _mm_call(name, a, b, a_spec, b_spec, (m // tm, n // tn, kdim // tk), kdim // tk,
                    [jax.ShapeDtypeStruct((m, n), d) for d in out_dtypes], [o_spec] * len(out_dtypes), (tm, tn),
                    ta=ta, tb=tb, a_fn=a_fn, epi=epi, extras=extras, extra_specs=especs, comm=comm)


def _mm_cols(name, a, b3, *, out_dtype=BF16, tm=1024, tk=2048, comm=()):
    m, kdim = a.shape
    n = b3.shape[-1]
    tm, tk = _tile(m, tm), _tile(kdim, tk)
    return _mm_call(name, a, b3, pl.BlockSpec((tm, tk), lambda i, j, k: (i, k)),
                    pl.BlockSpec((None, tk, n), lambda i, j, k: (j, k, 0)),
                    (m // tm, N_DEV, kdim // tk), kdim // tk,
                    [jax.ShapeDtypeStruct((m, N_DEV * n), out_dtype)], [pl.BlockSpec((tm, n), lambda i, j, k: (i, j))],
                    (tm, n), comm=comm)[0]


def _mm_cols_t(name, a, b3, *, out_dtype=BF16, tm=1024, tn=1024, comm=()):
    m = a.shape[0]
    kout, n = b3.shape[-2:]
    tm, tn = _tile(m, tm), _tile(kout, tn)
    return _mm_call(name, a, b3, pl.BlockSpec((tm, n), lambda i, j, k: (i, k)),
                    pl.BlockSpec((None, tn, n), lambda i, j, k: (k, j, 0)),
                    (m // tm, kout // tn, N_DEV), N_DEV,
                    [jax.ShapeDtypeStruct((m, kout), out_dtype)], [pl.BlockSpec((tm, tn), lambda i, j, k: (i, j))],
                    (tm, tn), tb=True, comm=comm)[0]


def _mm_cols_grad(name, a, dy, *, tm=1024, tk=1024, comm=()):
    s, kout = a.shape
    n = dy.shape[1] // N_DEV
    tm, tk = _tile(kout, tm), _tile(s, tk)
    return _mm_call(name, a, dy, pl.BlockSpec((tk, tm), lambda i, j, k: (k, i)),
                    pl.BlockSpec((tk, n), lambda i, j, k: (k, j)),
                    (kout // tm, N_DEV, s // tk), s // tk,
                    [jax.ShapeDtypeStruct((N_DEV, kout, n), F32)], [pl.BlockSpec((None, tm, n), lambda i, j, k: (j, i, 0))],
                    (tm, n), ta=True, comm=comm)[0]


_GELU_C = math.sqrt(2.0 / math.pi)


def _gelu(x):
    return 0.5 * x * (1.0 + jnp.tanh(_GELU_C * (x + 0.044715 * x * x * x)))


def _gelu_grad(x):
    t = jnp.tanh(_GELU_C * (x + 0.044715 * x * x * x))
    return 0.5 * (1.0 + t) + 0.5 * x * (1.0 - t * t) * _GELU_C * (1.0 + 3.0 * 0.044715 * x * x)


def _relu_sq(x):
    r = jnp.maximum(x.astype(F32), 0.0)
    return r * r


def _rms_fwd(x, g, name):
    s, d = x.shape
    tr = _tile(s, 512)

    def body(x_ref, g_ref, o_ref):
        xv = x_ref[...]
        r = lax.rsqrt(jnp.mean(xv * xv, axis=-1, keepdims=True) + RMS_EPS)
        o_ref[...] = (xv * r * g_ref[...]).astype(BF16)

    return pl.pallas_call(
        body, grid=(s // tr,),
        in_specs=[pl.BlockSpec((tr, d), lambda i: (i, 0)), pl.BlockSpec((1, d), lambda i: (0, 0))],
        out_specs=pl.BlockSpec((tr, d), lambda i: (i, 0)), out_shape=jax.ShapeDtypeStruct((s, d), BF16),
        compiler_params=_cparams(("parallel",)), name=name)(x, g.reshape(1, d))


def _rms_bwd(x, g, dy, dres, name):
    s, d = x.shape
    tr = _tile(s, 512)

    def body(x_ref, g_ref, dy_ref, dres_ref, dx_ref, dg_ref):
        @pl.when(pl.program_id(0) == 0)
        def _():
            dg_ref[...] = jnp.zeros_like(dg_ref)

        xv = x_ref[...]
        dyv = dy_ref[...].astype(F32)
        r = lax.rsqrt(jnp.mean(xv * xv, axis=-1, keepdims=True) + RMS_EPS)
        xh = xv * r
        gdy = dyv * g_ref[...]
        dx_ref[...] = dres_ref[...] + r * (gdy - xh * jnp.mean(gdy * xh, axis=-1, keepdims=True))
        dg_ref[...] += jnp.sum(dyv * xh, axis=0, keepdims=True)

    return pl.pallas_call(
        body, grid=(s // tr,),
        in_specs=[pl.BlockSpec((tr, d), lambda i: (i, 0)), pl.BlockSpec((1, d), lambda i: (0, 0)),
                  pl.BlockSpec((tr, d), lambda i: (i, 0)), pl.BlockSpec((tr, d), lambda i: (i, 0))],
        out_specs=[pl.BlockSpec((tr, d), lambda i: (i, 0)), pl.BlockSpec((1, d), lambda i: (0, 0))],
        out_shape=[jax.ShapeDtypeStruct((s, d), F32), jax.ShapeDtypeStruct((1, d), F32)],
        compiler_params=_cparams(("arbitrary",)), name=name)(x, g.reshape(1, d), dy, dres)


def _final_loss(x, g, target):
    s, d = x.shape
    tr = _tile(s, 512)

    def body(x_ref, g_ref, t_ref, loss_ref, dx_ref, dg_ref):
        @pl.when(pl.program_id(0) == 0)
        def _():
            dg_ref[...] = jnp.zeros_like(dg_ref)
            loss_ref[...] = jnp.zeros_like(loss_ref)

        xv = x_ref[...]
        gv = g_ref[...]
        r = lax.rsqrt(jnp.mean(xv * xv, axis=-1, keepdims=True) + RMS_EPS)
        xh = xv * r
        err = xh * gv - t_ref[...]
        loss_ref[...] += jnp.sum(jnp.sum(err * err, axis=-1, keepdims=True), axis=0, keepdims=True)
        dyv = err * (1.0 / d)
        gdy = dyv * gv
        dx_ref[...] = r * (gdy - xh * jnp.mean(gdy * xh, axis=-1, keepdims=True))
        dg_ref[...] += jnp.sum(dyv * xh, axis=0, keepdims=True)

    return pl.pallas_call(
        body, grid=(s // tr,),
        in_specs=[pl.BlockSpec((tr, d), lambda i: (i, 0)), pl.BlockSpec((1, d), lambda i: (0, 0)),
                  pl.BlockSpec((tr, d), lambda i: (i, 0))],
        out_specs=[pl.BlockSpec((1, 1), lambda i: (0, 0)), pl.BlockSpec((tr, d), lambda i: (i, 0)),
                   pl.BlockSpec((1, d), lambda i: (0, 0))],
        out_shape=[jax.ShapeDtypeStruct((1, 1), F32), jax.ShapeDtypeStruct((s, d), F32),
                   jax.ShapeDtypeStruct((1, d), F32)],
        compiler_params=_cparams(("arbitrary",)), name="final_norm_loss")(x, g.reshape(1, d), target)


def _t5_bucket(rel):
    half = T5_BUCKETS // 2
    max_exact = half // 2
    n = jnp.abs(rel)
    nf = jnp.maximum(n, 1).astype(F32)
    large = max_exact + (jnp.log(nf / max_exact) / math.log(T5_MAX_DISTANCE / max_exact)
                         * (half - max_exact)).astype(jnp.int32)
    large = jnp.minimum(large, half - 1)
    return jnp.where(rel > 0, half, 0) + jnp.where(n < max_exact, n, large)


def _dil_offsets():
    i = jnp.arange(QBLOCK)[:, None]
    kk = jnp.arange(2 * QBLOCK)[None, :]
    return kk - DIL_HALF - i, (jnp.arange(QBLOCK)[None, :] + DIL_HALF) - jnp.arange(2 * QBLOCK)[:, None]


def _expand(table, onehot, name, tn=8192):
    r, n = table.shape[0], onehot.shape[1]
    tn = _tile(n, tn)

    def body(t_ref, oh_ref, o_ref):
        o_ref[...] = lax.dot_general(t_ref[...], oh_ref[...], (((1,), (0,)), ((), ())),
                                     precision=lax.Precision.HIGHEST, preferred_element_type=F32)

    return pl.pallas_call(
        body, grid=(n // tn,),
        in_specs=[pl.BlockSpec((r, LANES), lambda i: (0, 0)), pl.BlockSpec((LANES, tn), lambda i: (0, i))],
        out_specs=pl.BlockSpec((r, tn), lambda i: (0, i)), out_shape=jax.ShapeDtypeStruct((r, n), F32),
        compiler_params=_cparams(("parallel",)), name=name)(table, onehot)


def _pad_rows_lanes(t):
    r, c = t.shape
    return jnp.pad(t.astype(F32), ((0, -r % SUBLANES), (0, LANES - c)))


def _dil_bias(t5, dil):
    ah = t5.shape[1]
    off1, off2 = _dil_offsets()
    bucket = jnp.concatenate([_t5_bucket(off1 * dil).reshape(-1), _t5_bucket(off2 * dil).reshape(-1)])
    onehot = (jnp.arange(LANES)[:, None] == bucket[None, :]).astype(F32)
    b = _expand(_pad_rows_lanes(t5.T), onehot, f"t5_bias_d{dil}")[:ah]
    n1 = QBLOCK * 2 * QBLOCK
    return b[:, :n1].reshape(ah, QBLOCK, 2 * QBLOCK), b[:, n1:].reshape(ah, 2 * QBLOCK, QBLOCK)


def _window(p, c, n, cols=slice(None)):
    return jnp.concatenate([p[pl.ds(DIL_HALF, DIL_HALF), cols], c[:, cols], n[pl.ds(0, DIL_HALF), cols]], axis=0)


def _dil_specs(width, ncol_blocks, col_block, nb):
    def spec(dn):
        return pl.BlockSpec((QBLOCK, width), lambda r, n: (jnp.clip(n + dn, 0, nb - 1), r * ncol_blocks + col_block))
    return [spec(-1), spec(0), spec(1)]


def _dil_fwd(proj, bias1, dil, ah, name):
    s, wtot = proj.shape
    ln = s // dil
    nb = ln // QBLOCK
    assert nb * QBLOCK * dil == s
    aw = ah * HEAD_DIM
    wb = wtot // aw
    scale = 1.0 / math.sqrt(HEAD_DIM)
    pv = proj.reshape(ln, dil * wtot)

    def body(q_ref, kp, kc, kn, vp, vc, vn, b_ref, o_ref, lse_ref):
        n = pl.program_id(1)
        ii = lax.broadcasted_iota(jnp.int32, (QBLOCK, 2 * QBLOCK), 0)
        jj = lax.broadcasted_iota(jnp.int32, (QBLOCK, 2 * QBLOCK), 1)
        kpos = n * QBLOCK + jj - DIL_HALF
        valid = (jnp.abs(jj - DIL_HALF - ii) <= DIL_HALF) & (kpos >= 0) & (kpos < ln)
        for h in range(ah):
            cs = pl.ds(h * HEAD_DIM, HEAD_DIM)
            kw, vw = _window(kp, kc, kn, cs), _window(vp, vc, vn, cs)
            sc = jnp.where(valid, _nt(q_ref[:, cs], kw) * scale + b_ref[h], NEG_INF)
            m = jnp.max(sc, axis=-1, keepdims=True)
            p = jnp.exp(sc - m)
            l = jnp.sum(p, axis=-1, keepdims=True)
            o_ref[:, cs] = (_nn(p.astype(BF16), vw) / l).astype(BF16)
            lse_ref[:, cs] = jnp.broadcast_to(m + jnp.log(l), (QBLOCK, HEAD_DIM))

    ospec = pl.BlockSpec((QBLOCK, aw), lambda r, n: (n, r))
    o, lse = pl.pallas_call(
        body, grid=(dil, nb),
        in_specs=[_dil_specs(aw, wb, 0, nb)[1], *_dil_specs(aw, wb, 1, nb), *_dil_specs(aw, wb, 2, nb),
                  pl.BlockSpec((ah, QBLOCK, 2 * QBLOCK), lambda r, n: (0, 0, 0))],
        out_specs=[ospec, ospec],
        out_shape=[jax.ShapeDtypeStruct((ln, dil * aw), BF16), jax.ShapeDtypeStruct((ln, dil * aw), F32)],
        compiler_params=_cparams(("parallel", "parallel")), name=name,
    )(pv, pv, pv, pv, pv, pv, pv, bias1)
    return o.reshape(s, aw), lse.reshape(s, aw)


def _dil_merge(outs, lses):
    s, aw = outs[0].shape
    tr = _tile(s, 512)

    def body(o1, l1, o2, l2, o3, l3, o_ref, lse_ref):
        a, b, c = l1[...], l2[...], l3[...]
        m = jnp.maximum(jnp.maximum(a, b), c)
        w1, w2, w3 = jnp.exp(a - m), jnp.exp(b - m), jnp.exp(c - m)
        tot = w1 + w2 + w3
        o_ref[...] = ((w1 * o1[...].astype(F32) + w2 * o2[...].astype(F32) + w3 * o3[...].astype(F32)) / tot).astype(BF16)
        lse_ref[...] = m + jnp.log(tot)

    spec = pl.BlockSpec((tr, aw), lambda i: (i, 0))
    return pl.pallas_call(
        body, grid=(s // tr,), in_specs=[spec] * 6, out_specs=[spec, spec],
        out_shape=[jax.ShapeDtypeStruct((s, aw), BF16), jax.ShapeDtypeStruct((s, aw), F32)],
        compiler_params=_cparams(("parallel",)), name="dil_merge",
    )(outs[0], lses[0], outs[1], lses[1], outs[2], lses[2])


def _head_delta(do, do_col0, o, name):
    s, w = o.shape
    tr = _tile(s, 512)
    cb = do_col0 // HEAD_DIM

    def body(do_ref, o_ref, d_ref):
        d = jnp.sum(do_ref[...].astype(F32) * o_ref[...].astype(F32), axis=-1, keepdims=True)
        d_ref[...] = jnp.broadcast_to(d, d_ref.shape)

    return pl.pallas_call(
        body, grid=(s // tr, w // HEAD_DIM),
        in_specs=[pl.BlockSpec((tr, HEAD_DIM), lambda i, h: (i, cb + h)), pl.BlockSpec((tr, HEAD_DIM), lambda i, h: (i, h))],
        out_specs=pl.BlockSpec((tr, HEAD_DIM), lambda i, h: (i, h)), out_shape=jax.ShapeDtypeStruct((s, w), F32),
        compiler_params=_cparams(("parallel", "parallel")), name=name)(do, o)


def _dil_bwd(proj, dmerged, lse, delta, bias1, bias2, dil, ah, name):
    s, wtot = proj.shape
    ln = s // dil
    nb = ln // QBLOCK
    aw = ah * HEAD_DIM
    wb = wtot // aw
    wd = dmerged.shape[1] // aw
    scale = 1.0 / math.sqrt(HEAD_DIM)
    pv = proj.reshape(ln, dil * wtot)
    dov = dmerged.reshape(ln, dil * dmerged.shape[1])
    lv = lse.reshape(ln, dil * aw)
    dlv = delta.reshape(ln, dil * aw)

    def body(qp, qc, qn, kp, kc, kn, vp, vc, vn, dop, doc, don, lp, lc, lnx, dp, dc, dn, b1_ref, b2_ref,
             dq_ref, dk_ref, dv_ref, db_ref):
        n = pl.program_id(1)

        @pl.when((pl.program_id(0) == 0) & (n == 0))
        def _():
            db_ref[...] = jnp.zeros_like(db_ref)

        ii = lax.broadcasted_iota(jnp.int32, (QBLOCK, 2 * QBLOCK), 0)
        jj = lax.broadcasted_iota(jnp.int32, (QBLOCK, 2 * QBLOCK), 1)
        kpos = n * QBLOCK + jj - DIL_HALF
        valid = (jnp.abs(jj - DIL_HALF - ii) <= DIL_HALF) & (kpos >= 0) & (kpos < ln)
        ww = lax.broadcasted_iota(jnp.int32, (2 * QBLOCK, QBLOCK), 0)
        cc = lax.broadcasted_iota(jnp.int32, (2 * QBLOCK, QBLOCK), 1)
        qpos = n * QBLOCK - DIL_HALF + ww
        valid2 = (jnp.abs(cc + DIL_HALF - ww) <= DIL_HALF) & (qpos >= 0) & (qpos < ln)
        for h in range(ah):
            cs = pl.ds(h * HEAD_DIM, HEAD_DIM)
            kw, vw = _window(kp, kc, kn, cs), _window(vp, vc, vn, cs)
            sc = _nt(qc[:, cs], kw) * scale + b1_ref[h]
            lse2 = jnp.concatenate([lc[:, cs], lc[:, cs]], axis=1)
            p = jnp.where(valid, jnp.exp(jnp.where(valid, sc - lse2, 0.0)), 0.0)
            ds = p * (_nt(doc[:, cs], vw) - jnp.concatenate([dc[:, cs], dc[:, cs]], axis=1))
            dq_ref[:, cs] = _nn(ds.astype(BF16), kw) * scale
            db_ref[h] += ds
            qw, dow = _window(qp, qc, qn, cs), _window(dop, doc, don, cs)
            sc2 = _nt(qw, kc[:, cs]) * scale + b2_ref[h]
            p2 = jnp.where(valid2, jnp.exp(jnp.where(valid2, sc2 - _window(lp, lc, lnx, cs), 0.0)), 0.0)
            dv_ref[:, cs] = _tn(p2.astype(BF16), dow)
            ds2 = p2 * (_nt(dow, vc[:, cs]) - _window(dp, dc, dn, cs))
            dk_ref[:, cs] = _tn(ds2.astype(BF16), qw) * scale

    ospec = pl.BlockSpec((QBLOCK, aw), lambda r, n: (n, r))
    dq, dk, dv, db = pl.pallas_call(
        body, grid=(dil, nb),
        in_specs=[*_dil_specs(aw, wb, 0, nb), *_dil_specs(aw, wb, 1, nb), *_dil_specs(aw, wb, 2, nb),
                  *_dil_specs(aw, wd, 0, nb), *_dil_specs(aw, 1, 0, nb), *_dil_specs(aw, 1, 0, nb),
                  pl.BlockSpec((ah, QBLOCK, 2 * QBLOCK), lambda r, n: (0, 0, 0)),
                  pl.BlockSpec((ah, 2 * QBLOCK, QBLOCK), lambda r, n: (0, 0, 0))],
        out_specs=[ospec, ospec, ospec, pl.BlockSpec((ah, QBLOCK, 2 * QBLOCK), lambda r, n: (0, 0, 0))],
        out_shape=[jax.ShapeDtypeStruct((ln, dil * aw), F32)] * 3 + [jax.ShapeDtypeStruct((ah, QBLOCK, 2 * QBLOCK), F32)],
        compiler_params=_cparams(("arbitrary", "arbitrary")), name=name,
    )(pv, pv, pv, pv, pv, pv, pv, pv, pv, dov, dov, dov, lv, lv, lv, dlv, dlv, dlv, bias1, bias2)
    return dq.reshape(s, aw), dk.reshape(s, aw), dv.reshape(s, aw), db


def _strided_specs(rows, col_block, nsb):
    half = rows // 2
    return [pl.BlockSpec((half, HEAD_DIM), lambda h, n: (jnp.clip(2 * n - 1, 0, 2 * nsb - 1), col_block + h)),
            pl.BlockSpec((rows, HEAD_DIM), lambda h, n: (n, col_block + h)),
            pl.BlockSpec((half, HEAD_DIM), lambda h, n: (jnp.clip(2 * n + 2, 0, 2 * nsb - 1), col_block + h))]


def _fill_window(dst, p, c, n):
    half, rows = p.shape[0], c.shape[0]
    dst[pl.ds(0, half), :] = p[...].astype(F32)
    dst[pl.ds(half, rows), :] = c[...].astype(F32)
    dst[pl.ds(half + rows, half), :] = n[...].astype(F32)


def _dil_masks(n, ln):
    ii = lax.broadcasted_iota(jnp.int32, (QBLOCK, 2 * QBLOCK), 0)
    jj = lax.broadcasted_iota(jnp.int32, (QBLOCK, 2 * QBLOCK), 1)
    kpos = n * QBLOCK + jj - DIL_HALF
    valid = (jnp.abs(jj - DIL_HALF - ii) <= DIL_HALF) & (kpos >= 0) & (kpos < ln)
    ww = lax.broadcasted_iota(jnp.int32, (2 * QBLOCK, QBLOCK), 0)
    cc = lax.broadcasted_iota(jnp.int32, (2 * QBLOCK, QBLOCK), 1)
    qpos = n * QBLOCK - DIL_HALF + ww
    valid2 = (jnp.abs(cc + DIL_HALF - ww) <= DIL_HALF) & (qpos >= 0) & (qpos < ln)
    return valid, valid2


def _dil_fwd_strided(proj, bias1, dil, ah, name):
    s, wtot = proj.shape
    ln = s // dil
    nsb = ln // QBLOCK
    assert nsb * QBLOCK * dil == s
    aw = ah * HEAD_DIM
    rows = QBLOCK * dil
    half = rows // 2
    scale = 1.0 / math.sqrt(HEAD_DIM)

    def body(q_ref, kp, kc, kn, vp, vc, vn, b_ref, o_ref, lse_ref, qf, kf, vf, of):
        valid, _ = _dil_masks(pl.program_id(1), ln)
        qf[...] = q_ref[...].astype(F32)
        _fill_window(kf, kp, kc, kn)
        _fill_window(vf, vp, vc, vn)
        bias = b_ref[...]

        def residue(r, carry):
            qs = pl.ds(r, QBLOCK, stride=dil)
            ws = pl.ds(r, 2 * QBLOCK, stride=dil)
            kw, vw = kf[ws, :].astype(BF16), vf[ws, :].astype(BF16)
            sc = jnp.where(valid, _nt(qf[qs, :].astype(BF16), kw) * scale + bias, NEG_INF)
            m = jnp.max(sc, axis=-1, keepdims=True)
            p = jnp.exp(sc - m)
            l = jnp.sum(p, axis=-1, keepdims=True)
            of[qs, :] = _nn(p.astype(BF16), vw) / l
            lse_ref[qs, :] = jnp.broadcast_to(m + jnp.log(l), (QBLOCK, HEAD_DIM))
            return carry

        lax.fori_loop(0, dil, residue, 0, unroll=4)
        o_ref[...] = of[...].astype(BF16)

    ospec = pl.BlockSpec((rows, HEAD_DIM), lambda h, n: (n, h))
    return pl.pallas_call(
        body, grid=(ah, nsb),
        in_specs=[_strided_specs(rows, 0, nsb)[1], *_strided_specs(rows, ah, nsb), *_strided_specs(rows, 2 * ah, nsb),
                  pl.BlockSpec((None, QBLOCK, 2 * QBLOCK), lambda h, n: (h, 0, 0))],
        out_specs=[ospec, ospec],
        out_shape=[jax.ShapeDtypeStruct((s, aw), BF16), jax.ShapeDtypeStruct((s, aw), F32)],
        scratch_shapes=[pltpu.VMEM((rows, HEAD_DIM), F32), pltpu.VMEM((2 * rows, HEAD_DIM), F32),
                        pltpu.VMEM((2 * rows, HEAD_DIM), F32), pltpu.VMEM((rows, HEAD_DIM), F32)],
        compiler_params=_cparams(("parallel", "parallel")), name=name,
    )(proj, proj, proj, proj, proj, proj, proj, bias1)


def _dil_bwd_strided(proj, dmerged, lse, delta, bias1, bias2, dil, ah, name):
    s, wtot = proj.shape
    ln = s // dil
    nsb = ln // QBLOCK
    aw = ah * HEAD_DIM
    rows = QBLOCK * dil
    half = rows // 2
    scale = 1.0 / math.sqrt(HEAD_DIM)

    def body(qp, qc, qn, kp, kc, kn, vp, vc, vn, dop, doc, don, lp, lc, lnx, dp, dc, dn, b1_ref, b2_ref,
             dq_ref, dk_ref, dv_ref, db_ref, qf, kf, vf, dof):
        n = pl.program_id(1)

        @pl.when(n == 0)
        def _():
            db_ref[...] = jnp.zeros_like(db_ref)

        valid, valid2 = _dil_masks(n, ln)
        _fill_window(qf, qp, qc, qn)
        _fill_window(kf, kp, kc, kn)
        _fill_window(vf, vp, vc, vn)
        _fill_window(dof, dop, doc, don)
        b1, b2 = b1_ref[...], b2_ref[...]

        def stat_window(p, c, nx, r):
            return jnp.concatenate([p[pl.ds(r, DIL_HALF, stride=dil), :], c[pl.ds(r, QBLOCK, stride=dil), :],
                                    nx[pl.ds(r, DIL_HALF, stride=dil), :]], axis=0)

        def residue(r, carry):
            cs = pl.ds(half + r, QBLOCK, stride=dil)
            ws = pl.ds(r, 2 * QBLOCK, stride=dil)
            os = pl.ds(r, QBLOCK, stride=dil)
            qw, kw, vw, dow = (t[ws, :].astype(BF16) for t in (qf, kf, vf, dof))
            q, k, v, do = (t[cs, :].astype(BF16) for t in (qf, kf, vf, dof))
            lse_c, delta_c = lc[os, :], dc[os, :]
            sc = _nt(q, kw) * scale + b1
            p = jnp.where(valid, jnp.exp(jnp.where(valid, sc - jnp.concatenate([lse_c, lse_c], axis=1), 0.0)), 0.0)
            ds = p * (_nt(do, vw) - jnp.concatenate([delta_c, delta_c], axis=1))
            dq_ref[os, :] = _nn(ds.astype(BF16), kw) * scale
            db_ref[...] += ds
            sc2 = _nt(qw, k) * scale + b2
            p2 = jnp.where(valid2, jnp.exp(jnp.where(valid2, sc2 - stat_window(lp, lc, lnx, r), 0.0)), 0.0)
            dv_ref[os, :] = _tn(p2.astype(BF16), dow)
            ds2 = p2 * (_nt(dow, v) - stat_window(dp, dc, dn, r))
            dk_ref[os, :] = _tn(ds2.astype(BF16), qw) * scale
            return carry

        lax.fori_loop(0, dil, residue, 0, unroll=4)

    ospec = pl.BlockSpec((rows, HEAD_DIM), lambda h, n: (n, h))
    win = pltpu.VMEM((2 * rows, HEAD_DIM), F32)
    return pl.pallas_call(
        body, grid=(ah, nsb),
        in_specs=[*_strided_specs(rows, 0, nsb), *_strided_specs(rows, ah, nsb), *_strided_specs(rows, 2 * ah, nsb),
                  *_strided_specs(rows, 0, nsb), *_strided_specs(rows, 0, nsb), *_strided_specs(rows, 0, nsb),
                  pl.BlockSpec((None, QBLOCK, 2 * QBLOCK), lambda h, n: (h, 0, 0)),
                  pl.BlockSpec((None, 2 * QBLOCK, QBLOCK), lambda h, n: (h, 0, 0))],
        out_specs=[ospec, ospec, ospec, pl.BlockSpec((None, QBLOCK, 2 * QBLOCK), lambda h, n: (h, 0, 0))],
        out_shape=[jax.ShapeDtypeStruct((s, aw), F32)] * 3 + [jax.ShapeDtypeStruct((ah, QBLOCK, 2 * QBLOCK), F32)],
        scratch_shapes=[win, win, win, win],
        compiler_params=_cparams(("parallel", "arbitrary")), name=name,
    )(proj, proj, proj, proj, proj, proj, proj, proj, proj, dmerged, dmerged, dmerged, lse, lse, lse,
      delta, delta, delta, bias1, bias2)


def _bucket_sum(vals, onehot, name):
    r, n = vals.shape
    b = onehot.shape[1]

    def body(v_ref, oh_ref, o_ref):
        o_ref[...] = lax.dot_general(v_ref[...], oh_ref[...], (((1,), (0,)), ((), ())),
                                     precision=lax.Precision.HIGHEST, preferred_element_type=F32)

    tr = max(t for t in range(SUBLANES, 257, SUBLANES) if r % t == 0)
    return pl.pallas_call(
        body, grid=(r // tr,),
        in_specs=[pl.BlockSpec((tr, n), lambda i: (i, 0)), pl.BlockSpec((n, b), lambda i: (0, 0))],
        out_specs=pl.BlockSpec((tr, b), lambda i: (i, 0)), out_shape=jax.ShapeDtypeStruct((r, b), F32),
        compiler_params=_cparams(("parallel",)), name=name)(vals, onehot)


def _t5_grad(dbias, dil):
    ah = dbias.shape[0]
    off1, _ = _dil_offsets()
    bucket = _t5_bucket(off1 * dil).reshape(-1)
    inside = (jnp.abs(off1) <= DIL_HALF).reshape(-1)
    onehot = ((bucket[:, None] == jnp.arange(LANES)[None, :]) & inside[:, None]).astype(F32)
    vals = jnp.pad(dbias.reshape(ah, -1), ((0, -ah % SUBLANES), (0, 0)))
    return _bucket_sum(vals, onehot, f"t5_grad_d{dil}")[:ah, :T5_BUCKETS].T


def _na_table_rows():
    ro = -np.ones((3, NA_GROUP, NA_WIN), np.int64)
    for i in range(NA_GROUP):
        for j in range(NA_WIN):
            if j < NA_ROWS:
                ro[0, i, j] = j - i + NA_ROWS - 1
            if i <= j < i + NA_ROWS:
                ro[1, i, j] = j - i + NA_ROWS // 2 - 1
            if j >= NA_WIN - NA_ROWS:
                ro[2, i, j] = j - i
    return ro


def _na_bias(rpb):
    ch, nro, nco = rpb.shape
    c = np.arange(GRID_W)
    col_start = np.clip(c - NA_COLS // 2, 0, GRID_W - NA_COLS)
    col_ok = (c[None, :] >= col_start[:, None]) & (c[None, :] < col_start[:, None] + NA_COLS)
    col_idx = np.clip(c[None, :] - c[:, None] + NA_COLS - 1, 0, 2 * NA_COLS - 2).reshape(-1)
    onehot = (np.arange(LANES)[:, None] == col_idx[None, :]).astype(np.float32)
    table = jnp.pad(rpb.astype(F32).reshape(ch * nro, nco), ((0, -(ch * nro) % SUBLANES), (0, LANES - nco)))
    by_row = _expand(table, jnp.asarray(onehot), "rpb_bias", tn=GRID_W * GRID_W)[:ch * nro]
    by_row = jnp.where(jnp.asarray(col_ok.reshape(-1))[None, :], by_row, NEG_INF).reshape(ch, nro, GRID_W, GRID_W)
    neg = jnp.full((ch, GRID_W, GRID_W), NEG_INF, F32)
    tiles = [by_row[:, r] if r >= 0 else neg for r in _na_table_rows().reshape(-1)]
    b = jnp.stack(tiles, axis=1).reshape(ch, 3, NA_GROUP, NA_WIN, GRID_W, GRID_W)
    return jnp.transpose(b, (0, 1, 2, 4, 3, 5)).reshape(ch, 3, NA_GROUP * GRID_W, NA_WIN * GRID_W)


def _na_group(g, rows):
    ngroups = rows // NA_GROUP
    ws = jnp.clip(g * NA_GROUP - NA_ROWS // 2, 0, rows - NA_WIN)
    return pl.multiple_of(ws * GRID_W, GRID_W), jnp.where(g == 0, 0, jnp.where(g == ngroups - 1, 2, 1))


def _na_fwd(qkv, bias, ch, name):
    s = qkv.shape[0]
    rows = s // GRID_W
    assert rows >= NA_WIN and rows % (NA_GROUP * NA_GROUPS_PER_STEP) == 0
    cw = ch * HEAD_DIM
    tg = NA_GROUP * GRID_W
    tq = NA_GROUPS_PER_STEP * tg
    win = NA_WIN * GRID_W
    scale = 1.0 / math.sqrt(HEAD_DIM)

    def body(q_ref, k_ref, v_ref, b_ref, o_ref, lse_ref):
        gb = pl.program_id(1)
        for i in range(NA_GROUPS_PER_STEP):
            st, var = _na_group(gb * NA_GROUPS_PER_STEP + i, rows)
            kw, vw = k_ref[pl.ds(st, win), :], v_ref[pl.ds(st, win), :]
            qs = pl.ds(i * tg, tg)
            sc = _nt(q_ref[qs, :], kw) * scale + b_ref[var]
            m = jnp.max(sc, axis=-1, keepdims=True)
            p = jnp.exp(sc - m)
            l = jnp.sum(p, axis=-1, keepdims=True)
            o_ref[qs, :] = (_nn(p.astype(BF16), vw) / l).astype(BF16)
            lse_ref[qs, :] = jnp.broadcast_to(m + jnp.log(l), (tg, HEAD_DIM))

    ospec = pl.BlockSpec((tq, HEAD_DIM), lambda h, gb: (gb, h))
    return pl.pallas_call(
        body, grid=(ch, s // tq),
        in_specs=[pl.BlockSpec((tq, HEAD_DIM), lambda h, gb: (gb, h)),
                  pl.BlockSpec((s, HEAD_DIM), lambda h, gb: (0, ch + h)),
                  pl.BlockSpec((s, HEAD_DIM), lambda h, gb: (0, 2 * ch + h)),
                  pl.BlockSpec((None, 3, tg, win), lambda h, gb: (h, 0, 0, 0))],
        out_specs=[ospec, ospec],
        out_shape=[jax.ShapeDtypeStruct((s, cw), BF16), jax.ShapeDtypeStruct((s, cw), F32)],
        compiler_params=_cparams(("parallel", "parallel")), name=name)(qkv, qkv, qkv, bias)


def _na_bwd(qkv, o, do, lse, bias, ch, name):
    s = qkv.shape[0]
    rows = s // GRID_W
    cw = ch * HEAD_DIM
    tg = NA_GROUP * GRID_W
    tq = NA_GROUPS_PER_STEP * tg
    win = NA_WIN * GRID_W
    scale = 1.0 / math.sqrt(HEAD_DIM)

    def body(q_ref, k_ref, v_ref, o_ref, do_ref, lse_ref, b_ref, dq_ref, dk_ref, dv_ref, db_ref):
        gb = pl.program_id(1)

        @pl.when(gb == 0)
        def _():
            dk_ref[...] = jnp.zeros_like(dk_ref)
            dv_ref[...] = jnp.zeros_like(dv_ref)
            db_ref[...] = jnp.zeros_like(db_ref)

        for i in range(NA_GROUPS_PER_STEP):
            st, var = _na_group(gb * NA_GROUPS_PER_STEP + i, rows)
            ws = pl.ds(st, win)
            kw, vw = k_ref[ws, :], v_ref[ws, :]
            qs = pl.ds(i * tg, tg)
            q, dov = q_ref[qs, :], do_ref[qs, :]
            sc = _nt(q, kw) * scale + b_ref[var]
            p = jnp.exp(sc - lse_ref[qs, :][:, :1])
            delta = jnp.sum(dov.astype(F32) * o_ref[qs, :].astype(F32), axis=-1, keepdims=True)
            ds = p * (_nt(dov, vw) - delta)
            dsb = ds.astype(BF16)
            dq_ref[qs, :] = (_nn(dsb, kw) * scale).astype(BF16)
            dk_ref[ws, :] += _tn(dsb, q) * scale
            dv_ref[ws, :] += _tn(p.astype(BF16), dov)
            db_ref[var] += ds

    qspec = pl.BlockSpec((tq, HEAD_DIM), lambda h, gb: (gb, h))
    kvspec = pl.BlockSpec((s, HEAD_DIM), lambda h, gb: (0, h))
    bspec = pl.BlockSpec((None, 3, tg, win), lambda h, gb: (h, 0, 0, 0))
    return pl.pallas_call(
        body, grid=(ch, s // tq),
        in_specs=[qspec, pl.BlockSpec((s, HEAD_DIM), lambda h, gb: (0, ch + h)),
                  pl.BlockSpec((s, HEAD_DIM), lambda h, gb: (0, 2 * ch + h)), qspec, qspec, qspec, bspec],
        out_specs=[qspec, kvspec, kvspec, bspec],
        out_shape=[jax.ShapeDtypeStruct((s, cw), BF16), jax.ShapeDtypeStruct((s, cw), F32),
                   jax.ShapeDtypeStruct((s, cw), F32), jax.ShapeDtypeStruct((ch, 3, tg, win), F32)],
        compiler_params=_cparams(("parallel", "arbitrary")), name=name)(qkv, qkv, qkv, o, do, lse, bias)


def _rpb_grad(dbias):
    ch = dbias.shape[0]
    ntile = 3 * NA_GROUP * NA_WIN
    c = np.arange(GRID_W)
    col_idx = (c[None, :] - c[:, None] + NA_COLS - 1).reshape(-1)
    oh_col = (col_idx[:, None] == np.arange(LANES)[None, :]).astype(np.float32)
    d6 = dbias.reshape(ch, 3, NA_GROUP, GRID_W, NA_WIN, GRID_W)
    vals = jnp.transpose(d6, (0, 1, 2, 4, 3, 5)).reshape(ch * ntile, GRID_W * GRID_W)
    by_col = _bucket_sum(vals, jnp.asarray(oh_col), "rpb_grad_cols")
    npad = 2 * LANES
    oh_row = np.zeros((npad, LANES), np.float32)
    for t, r in enumerate(_na_table_rows().reshape(-1)):
        if r >= 0:
            oh_row[t, r] = 1.0
    by_col = jnp.pad(by_col.reshape(ch, ntile, LANES), ((0, 0), (0, npad - ntile), (0, 0)))
    vals2 = jnp.transpose(by_col, (0, 2, 1)).reshape(ch * LANES, npad)
    by_row = _bucket_sum(vals2, jnp.asarray(oh_row), "rpb_grad_rows")
    return jnp.transpose(by_row.reshape(ch, LANES, LANES), (0, 2, 1))[:, :2 * NA_ROWS - 1, :2 * NA_COLS - 1]


def _s5_discretize(lam_re, lam_im, log_step, b_re, b_im):
    step = jnp.exp(log_step.astype(F32))[:, None]
    lr = jnp.minimum(lam_re.astype(F32), -1e-4)
    li = lam_im.astype(F32)
    mag = jnp.exp(lr * step)
    ab_re = mag * jnp.cos(li * step)
    ab_im = mag * jnp.sin(li * step)
    den = lr * lr + li * li
    zr = ((ab_re - 1.0) * lr + ab_im * li) / den
    zi = (ab_im * lr - (ab_re - 1.0) * li) / den
    br = b_re.astype(F32)
    bi = b_im.astype(F32)
    return ab_re, ab_im, zr[..., None] * br - zi[..., None] * bi, zr[..., None] * bi + zi[..., None] * br


def _scan_tables(a_re, a_im, rev):
    ar, ai = a_re.reshape(-1), a_im.reshape(-1)
    pows = [(ar, ai)]
    for _ in range(SUBLANES - 1):
        pr, pi = pows[-1]
        pows.append((pr * ar - pi * ai, pr * ai + pi * ar))
    row = jnp.arange(SUBLANES)[:, None]
    tabs = []
    for k in (1, 2, 4):
        keep = (row < SUBLANES - k) if rev else (row >= k)
        tabs += [jnp.where(keep, pows[k - 1][0][None, :], 0.0), jnp.where(keep, pows[k - 1][1][None, :], 0.0)]
    order = list(range(SUBLANES - 1, -1, -1)) if rev else list(range(SUBLANES))
    tabs += [jnp.stack([pows[i][0] for i in order]), jnp.stack([pows[i][1] for i in order])]
    t = jnp.stack(tabs)
    nblk = t.shape[-1] // (4 * LANES)
    return jnp.transpose(t.reshape(8, SUBLANES, nblk, 4 * LANES), (2, 0, 1, 3))


def _block_diag(w):
    g, a, b = w.shape
    nblk = g // S5_GROUPS_PER_BLOCK
    eye = jnp.eye(S5_GROUPS_PER_BLOCK, dtype=w.dtype)
    w4 = w.reshape(nblk, S5_GROUPS_PER_BLOCK, a, b)
    return (w4[:, :, :, None, :] * eye[None, :, None, :, None]).reshape(nblk, S5_GROUPS_PER_BLOCK * a, S5_GROUPS_PER_BLOCK * b)


def _block_diag_take(w, a, b):
    nblk = w.shape[0]
    w5 = w.reshape(nblk, S5_GROUPS_PER_BLOCK, a, S5_GROUPS_PER_BLOCK, b)
    eye = jnp.eye(S5_GROUPS_PER_BLOCK, dtype=w.dtype)
    return jnp.sum(w5 * eye[None, :, None, :, None], axis=3).reshape(nblk * S5_GROUPS_PER_BLOCK, a, b)


def _scan_tile(r, i, tab_ref, carry, rev):
    for lvl, k in enumerate((1, 2, 4)):
        mr, mi = tab_ref[2 * lvl], tab_ref[2 * lvl + 1]
        sh = SUBLANES - k if rev else k
        rr, ri = pltpu.roll(r, sh, 0), pltpu.roll(i, sh, 0)
        r, i = r + (mr * rr - mi * ri), i + (mr * ri + mi * rr)
    pr, pi = tab_ref[6], tab_ref[7]
    cr, ci = carry
    return r + (pr * cr - pi * ci), i + (pr * ci + pi * cr)


def _s5_fwd(proj, ucol0, tabs, bre, bim, cre, cim, rev, final, name):
    s = proj.shape[0]
    nblk = tabs.shape[0]
    bw = nblk * LANES
    w = 4 * LANES
    t = _tile(s, S5_CHUNK)
    nc, nt = s // t, t // SUBLANES
    ub = ucol0 // LANES
    cm = (lambda c: nc - 1 - c) if rev else (lambda c: c)
    last = 0 if rev else SUBLANES - 1

    def body(u_ref, tab_ref, bre_ref, bim_ref, cre_ref, cim_ref, *rest):
        if final is not None:
            yo_ref, d_ref, y_ref, xr_ref, xi_ref, xr_s, xi_s, car_r, car_i = rest
        else:
            y_ref, xr_ref, xi_ref, xr_s, xi_s, car_r, car_i = rest

        @pl.when(pl.program_id(1) == 0)
        def _():
            car_r[...] = jnp.zeros_like(car_r)
            car_i[...] = jnp.zeros_like(car_i)

        u = u_ref[...]
        xr_s[...] = _nn(u, bre_ref[...])
        xi_s[...] = _nn(u, bim_ref[...])

        def tile(tt, carry):
            k = nt - 1 - tt if rev else tt
            rows = pl.ds(pl.multiple_of(k * SUBLANES, SUBLANES), SUBLANES)
            r, i = _scan_tile(xr_s[rows, :], xi_s[rows, :], tab_ref, carry, rev)
            xr_s[rows, :] = r
            xi_s[rows, :] = i
            return (jnp.broadcast_to(r[last:last + 1, :], r.shape), jnp.broadcast_to(i[last:last + 1, :], i.shape))

        carry = lax.fori_loop(0, nt, tile, (car_r[...], car_i[...]))
        car_r[...], car_i[...] = carry
        xr, xi = xr_s[...].astype(BF16), xi_s[...].astype(BF16)
        y = _nn(xr, cre_ref[...]) - _nn(xi, cim_ref[...])
        if final is not None:
            y = y + yo_ref[...] + d_ref[...] * u.astype(F32)
        y_ref[...] = y
        xr_ref[...] = xr
        xi_ref[...] = xi

    yspec = pl.BlockSpec((t, LANES), lambda j, c: (cm(c), j))
    xspec = pl.BlockSpec((t, w), lambda j, c: (cm(c), j))
    in_specs = [pl.BlockSpec((t, LANES), lambda j, c: (cm(c), ub + j)),
                pl.BlockSpec((None, 8, SUBLANES, w), lambda j, c: (j, 0, 0, 0)),
                pl.BlockSpec((None, LANES, w), lambda j, c: (j, 0, 0)), pl.BlockSpec((None, LANES, w), lambda j, c: (j, 0, 0)),
                pl.BlockSpec((None, w, LANES), lambda j, c: (j, 0, 0)), pl.BlockSpec((None, w, LANES), lambda j, c: (j, 0, 0))]
    args = [proj, tabs, bre, bim, cre, cim]
    if final is not None:
        in_specs += [yspec, pl.BlockSpec((1, LANES), lambda j, c: (0, j))]
        args += [final[0], final[1].reshape(1, bw)]
    return pl.pallas_call(
        body, grid=(nblk, nc), in_specs=in_specs, out_specs=[yspec, xspec, xspec],
        out_shape=[jax.ShapeDtypeStruct((s, bw), F32), jax.ShapeDtypeStruct((s, nblk * w), BF16),
                   jax.ShapeDtypeStruct((s, nblk * w), BF16)],
        scratch_shapes=[pltpu.VMEM((t, w), F32), pltpu.VMEM((t, w), F32), pltpu.VMEM((SUBLANES, w), F32),
                        pltpu.VMEM((SUBLANES, w), F32)],
        compiler_params=_cparams(("parallel", "arbitrary")), name=name)(*args)


def _s5_bwd(proj, ucol0, dy, xr, xi, gtabs, bre, bim, cre, cim, rev, final, name):
    s = proj.shape[0]
    nblk = gtabs.shape[0]
    bw = nblk * LANES
    w = 4 * LANES
    t = _tile(s, S5_CHUNK)
    nc, nt = s // t, t // SUBLANES
    ub = ucol0 // LANES
    grev = not rev
    cm = (lambda c: nc - 1 - c) if grev else (lambda c: c)
    last = 0 if grev else SUBLANES - 1
    nfin = 2 if final is not None else 0

    def body(dy_ref, u_ref, xr_ref, xi_ref, tab_ref, bre_ref, bim_ref, cre_ref, cim_ref, *rest):
        fin, rest = rest[:nfin], rest[nfin:]
        du_ref, dar_ref, dai_ref, dbr_ref, dbi_ref, dcr_ref, dci_ref = rest[:7]
        rest = rest[7:]
        if final is not None:
            dd_ref, rest = rest[0], rest[1:]
        gr_s, gi_s, xr_s, xi_s, car_r, car_i = rest

        @pl.when(pl.program_id(1) == 0)
        def _():
            for ref in (car_r, car_i, dar_ref, dai_ref, dbr_ref, dbi_ref, dcr_ref, dci_ref):
                ref[...] = jnp.zeros_like(ref)
            if final is not None:
                dd_ref[...] = jnp.zeros_like(dd_ref)

        dyv = dy_ref[...]
        dyb = dyv.astype(BF16)
        u = u_ref[...]
        xrb, xib = xr_ref[...], xi_ref[...]
        gr_s[...] = _nt(dyb, cre_ref[...])
        gi_s[...] = -_nt(dyb, cim_ref[...])
        xr_s[...] = xrb.astype(F32)
        xi_s[...] = xib.astype(F32)
        rowid = lax.broadcasted_iota(jnp.int32, (SUBLANES, w), 0)

        def tile(tt, carry):
            k = nt - 1 - tt if grev else tt
            rows = pl.ds(pl.multiple_of(k * SUBLANES, SUBLANES), SUBLANES)
            r, i = _scan_tile(gr_s[rows, :], gi_s[rows, :], tab_ref, carry, grev)
            gr_s[rows, :] = r
            gi_s[rows, :] = i
            if grev:
                er = jnp.where(rowid == SUBLANES - 1, carry[0], pltpu.roll(r, SUBLANES - 1, 0))
                ei = jnp.where(rowid == SUBLANES - 1, carry[1], pltpu.roll(i, SUBLANES - 1, 0))
            else:
                er = jnp.where(rowid == 0, carry[0], pltpu.roll(r, 1, 0))
                ei = jnp.where(rowid == 0, carry[1], pltpu.roll(i, 1, 0))
            sr, si = xr_s[rows, :], xi_s[rows, :]
            dar_ref[...] += er * sr + ei * si
            dai_ref[...] += ei * sr - er * si
            return (jnp.broadcast_to(r[last:last + 1, :], r.shape), jnp.broadcast_to(i[last:last + 1, :], i.shape))

        carry = lax.fori_loop(0, nt, tile, (car_r[...], car_i[...]))
        car_r[...], car_i[...] = carry
        gr, gi = gr_s[...].astype(BF16), gi_s[...].astype(BF16)
        du = _nt(gr, bre_ref[...]) + _nt(gi, bim_ref[...])
        if final is not None:
            du = du + fin[0][...] + fin[1][...] * dyv.astype(F32)
            dd_ref[...] += jnp.sum(dyv.astype(F32) * u.astype(F32), axis=0, keepdims=True)
        du_ref[...] = du
        dbr_ref[...] += _tn(u, gr)
        dbi_ref[...] += _tn(u, gi)
        dcr_ref[...] += _tn(xrb, dyb)
        dci_ref[...] -= _tn(xib, dyb)

    yspec = pl.BlockSpec((t, LANES), lambda j, c: (cm(c), j))
    xspec = pl.BlockSpec((t, w), lambda j, c: (cm(c), j))
    bspec = pl.BlockSpec((None, LANES, w), lambda j, c: (j, 0, 0))
    cspec = pl.BlockSpec((None, w, LANES), lambda j, c: (j, 0, 0))
    aspec = pl.BlockSpec((None, SUBLANES, w), lambda j, c: (j, 0, 0))
    dspec = pl.BlockSpec((1, LANES), lambda j, c: (0, j))
    in_specs = [yspec, pl.BlockSpec((t, LANES), lambda j, c: (cm(c), ub + j)), xspec, xspec,
                pl.BlockSpec((None, 8, SUBLANES, w), lambda j, c: (j, 0, 0, 0)), bspec, bspec, cspec, cspec]
    args = [dy, proj, xr, xi, gtabs, bre, bim, cre, cim]
    out_specs = [yspec, aspec, aspec, bspec, bspec, cspec, cspec]
    out_shape = [jax.ShapeDtypeStruct((s, bw), F32)] + [jax.ShapeDtypeStruct((nblk, SUBLANES, w), F32)] * 2 \
        + [jax.ShapeDtypeStruct((nblk, LANES, w), F32)] * 2 + [jax.ShapeDtypeStruct((nblk, w, LANES), F32)] * 2
    if final is not None:
        in_specs += [yspec, dspec]
        args += [final[0], final[1].reshape(1, bw)]
        out_specs.append(dspec)
        out_shape.append(jax.ShapeDtypeStruct((1, bw), F32))
    return pl.pallas_call(
        body, grid=(nblk, nc), in_specs=in_specs, out_specs=out_specs, out_shape=out_shape,
        scratch_shapes=[pltpu.VMEM((t, w), F32)] * 4 + [pltpu.VMEM((SUBLANES, w), F32)] * 2,
        compiler_params=_cparams(("parallel", "arbitrary")), name=name)(*args)


N_CHIPS = 4


def _place():
    mx, my, mc = lax.axis_index("x"), lax.axis_index("y"), lax.axis_index("c")
    return (mx, my, mc), (mx, my, 1 - mc), [(1 - mx, my), (mx, 1 - my), (1 - mx, 1 - my)]


def _gather_op(x):
    r, c = x.shape

    def run(which, ins, outs, sems):
        (x_ref,), (out_ref,), (send_sems, recv_sems, local_sem) = ins, outs, sems
        me, sibling, chips = _place()
        mc = me[2]

        def slot(px, py, pc):
            return out_ref.at[4 * px + 2 * py + pc]

        def copy(k, block, to, src=None):
            return pltpu.make_async_remote_copy(src_ref=slot(*block) if src is None else src, dst_ref=slot(*block),
                                                send_sem=send_sems.at[k], recv_sem=recv_sems.at[k],
                                                device_id=to, device_id_type=MESH)

        def mine():
            return pltpu.make_async_copy(x_ref, slot(*me), local_sem)

        def first():
            return [copy(0, me, sibling, src=x_ref)] + [copy(1 + j, me, (*chip, mc), src=x_ref) for j, chip in enumerate(chips)]

        def passed():
            return [copy(4 + j, (*chip, mc), sibling) for j, chip in enumerate(chips)]

        if which == "start":
            mine().start()
            for cp in first():
                cp.start()
        elif which == "mid":
            for j, (chip, cp) in enumerate(zip(chips, passed())):
                copy(1 + j, (*chip, mc), me).wait_recv()
                cp.start()
        else:
            copy(0, sibling, me).wait_recv()
            for j, chip in enumerate(chips):
                copy(4 + j, (*chip, 1 - mc), me).wait_recv()
            for cp in first() + passed():
                cp.wait_send()
            mine().wait()

    return dict(ins=[x], outs=[jax.ShapeDtypeStruct((N_DEV, r, c), x.dtype)], run=run,
                sems=[pltpu.SemaphoreType.DMA((N_DEV - 1,)), pltpu.SemaphoreType.DMA((N_DEV - 1,)),
                      pltpu.SemaphoreType.DMA])


def _pair_op(gs):
    nt = len(gs)

    def run(which, g_refs, out_refs, sems):
        if which == "mid":
            return
        send_sems, recv_sems = sems
        me, sibling, _ = _place()
        copies = [pltpu.make_async_remote_copy(src_ref=g_refs[t].at[2 * q + (1 - me[2])], dst_ref=out_refs[t].at[q],
                                               send_sem=send_sems.at[t, q], recv_sem=recv_sems.at[t, q],
                                               device_id=sibling, device_id_type=MESH)
                  for t in range(nt) for q in range(N_CHIPS)]
        if which == "start":
            for cp in copies:
                cp.start()
        elif which == "wait":
            for cp in copies:
                cp.wait_recv()
            for cp in copies:
                cp.wait_send()

    return dict(ins=list(gs), outs=[jax.ShapeDtypeStruct((N_CHIPS,) + g.shape[1:], F32) for g in gs], run=run,
                sems=[pltpu.SemaphoreType.DMA((nt, N_CHIPS)), pltpu.SemaphoreType.DMA((nt, N_CHIPS))])


def _pair_add(g, t, core, name):
    _, r, c = g.shape
    tr = r
    while tr * c > 512 * 1024 and tr % 32 == 0:
        tr //= 2

    def body(core_ref, g_ref, t_ref, o_ref):
        o_ref[...] = (g_ref[...] + t_ref[...]).astype(BF16)

    return pl.pallas_call(
        body,
        grid_spec=pltpu.PrefetchScalarGridSpec(
            num_scalar_prefetch=1, grid=(N_CHIPS, r // tr),
            in_specs=[pl.BlockSpec((None, tr, c), lambda q, i, core_ref: (2 * q + core_ref[0], i, 0)),
                      pl.BlockSpec((None, tr, c), lambda q, i, core_ref: (q, i, 0))],
            out_specs=pl.BlockSpec((None, tr, c), lambda q, i, core_ref: (q, i, 0))),
        out_shape=jax.ShapeDtypeStruct((N_CHIPS, r, c), BF16),
        compiler_params=_cparams(("parallel", "parallel")), name=name)(core, g, t)


def _chip_op(ps):
    nt = len(ps)

    def run(which, p_refs, out_refs, sems):
        if which == "mid":
            return
        send_sems, recv_sems, local_sems = sems
        me, _, chips = _place()
        mychip = 2 * me[0] + me[1]
        owns = [pltpu.make_async_copy(p_refs[t].at[mychip], out_refs[t].at[mychip], local_sems.at[t]) for t in range(nt)]
        sends = [pltpu.make_async_remote_copy(src_ref=p_refs[t].at[2 * px + py], dst_ref=out_refs[t].at[mychip],
                                              send_sem=send_sems.at[t, j], recv_sem=recv_sems.at[t, j],
                                              device_id=(px, py, me[2]), device_id_type=MESH)
                 for t in range(nt) for j, (px, py) in enumerate(chips)]
        if which == "start":
            for cp in owns + sends:
                cp.start()
        elif which == "wait":
            for t in range(nt):
                for j, (px, py) in enumerate(chips):
                    pltpu.make_async_remote_copy(src_ref=p_refs[t].at[2 * px + py], dst_ref=out_refs[t].at[2 * px + py],
                                                 send_sem=send_sems.at[t, j], recv_sem=recv_sems.at[t, j],
                                                 device_id=(px, py, me[2]), device_id_type=MESH).wait_recv()
            for cp in sends:
                cp.wait_send()
            for cp in owns:
                cp.wait()

    return dict(ins=list(ps), outs=[jax.ShapeDtypeStruct(p.shape, p.dtype) for p in ps], run=run,
                sems=[pltpu.SemaphoreType.DMA((nt, N_CHIPS - 1)), pltpu.SemaphoreType.DMA((nt, N_CHIPS - 1)),
                      pltpu.SemaphoreType.DMA((nt,))])


def _adamw(w, gstack, m, v, name, layer=None, prev=None):
    r, c = w.shape[-2:]
    nstack = gstack.shape[0]
    tr = r
    while tr * c > 128 * 1024 and tr % 32 == 0:
        tr //= 2
    c1 = 1.0 - ADAM_B1 ** ADAM_STEP
    c2 = 1.0 - ADAM_B2 ** ADAM_STEP

    def body(w_ref, g_ref, m_ref, v_ref, *rest):
        go_ref, d_ref, mo_ref, vo_ref = rest[-4:]
        g = g_ref[0].astype(F32)
        for p in range(1, nstack):
            g = g + g_ref[p].astype(F32)
        mn = ADAM_B1 * m_ref[...] + (1.0 - ADAM_B1) * g
        vn = ADAM_B2 * v_ref[...] + (1.0 - ADAM_B2) * (g * g)
        go_ref[...] = g
        mo_ref[...] = mn
        vo_ref[...] = vn
        d_ref[...] = -ADAM_LR * ((mn / c1) / (jnp.sqrt(vn / c2) + ADAM_EPS) + ADAM_WD * w_ref[...])

    if layer is None:
        spec = pl.BlockSpec((tr, c), lambda i: (i, 0))
        full = (r, c)
    else:
        spec = pl.BlockSpec((None, tr, c), lambda i: (layer, i, 0))
        full = w.shape
    prev = list(prev) if prev is not None else []
    return pl.pallas_call(
        body, grid=(r // tr,),
        in_specs=[spec, pl.BlockSpec((nstack, tr, c), lambda i: (0, i, 0)), spec, spec] + [pl.BlockSpec(memory_space=pl.ANY)] * len(prev),
        out_specs=[spec] * 4, out_shape=[jax.ShapeDtypeStruct(full, F32)] * 4,
        input_output_aliases={4 + n: n for n in range(len(prev))},
        compiler_params=_cparams(("parallel",)), name=name)(w, gstack, m, v, *prev)


def _s5_tables(lam_re, lam_im, log_step, b_re, b_im, c_re, c_im):
    out = []
    for d in range(2):
        ab_re, ab_im, bb_re, bb_im = _s5_discretize(lam_re[d], lam_im[d], log_step[d], b_re, b_im)
        rev = d == 1
        out.append(dict(
            tabs=_scan_tables(ab_re, ab_im, rev), gtabs=_scan_tables(ab_re, -ab_im, not rev),
            bre=_block_diag(jnp.transpose(bb_re, (0, 2, 1))).astype(BF16), bim=_block_diag(jnp.transpose(bb_im, (0, 2, 1))).astype(BF16),
            cre=_block_diag(jnp.transpose(c_re[d], (0, 2, 1))).astype(BF16), cim=_block_diag(jnp.transpose(c_im[d], (0, 2, 1))).astype(BF16)))
    return out


def _layer_tensors(i):
    j = i // 2
    mixer = [("ab_w_in", j), ("ab_w_out", j), ("s5_w_glu", j)] if i % 2 == 0 else [("c_w_qkv", j), ("c_w_out", j)]
    return mixer + [("mlp_w1", i), ("mlp_w2", i)]


class _WeightGather:
    def __init__(self, shards, depth):
        self.shards, self.depth, self.ops = shards, depth, {}

    def _op(self, key):
        self.ops[key] = _gather_op(self.shards[key[0]][key[1]])
        return self.ops[key]

    def start(self):
        _run_comm([self._op(key) for key in _layer_tensors(0)[:-2]], "gather_mixer0")

    def carry(self, i, slot):
        t = _layer_tensors(i)
        nxt = _layer_tensors(i + 1)[:-2] if i + 1 < self.depth else []
        plan = {"in": t[-2:-1], "up": t[-1:], "down": nxt}
        return [self._op(key) for key in plan[slot]]

    def get(self, name, l):
        full = self.ops[(name, l)]["res"][0]
        return full.reshape(-1, full.shape[-1]) if name in ROW_SHARDED else full


class _GradExchange:
    def __init__(self, core, depth):
        self.core, self.depth, self.g, self.recv_ops, self.pairs, self.ps = core, depth, {}, [], {}, {}

    def put(self, name, l, g):
        self.g[(name, l)] = g.reshape(N_DEV, -1, g.shape[-1])

    def _pair(self, keys):
        op = _pair_op([self.g[key] for key in keys])
        for n, key in enumerate(keys):
            self.pairs[key] = (op, n)
        return [op]

    def _chip(self, keys):
        for key in keys:
            op, n = self.pairs[key]
            self.ps[key] = _pair_add(self.g[key], op["res"][n], self.core, f"pair_add_{key[0]}{key[1]}")
        op = _chip_op([self.ps[key] for key in keys])
        self.recv_ops.append((keys, op))
        return [op]

    def carry(self, i, slot):
        t = _layer_tensors(i)
        later = _layer_tensors(i + 1)[:-2] if i + 1 < self.depth else []
        if slot == "up_bwd":
            return self._pair(t[-2:])
        if slot == "in_grad":
            return self._chip(t[-2:-1])
        if slot == "in_bwd":
            return self._chip(t[-1:])
        if slot == "down_bwd":
            return self._pair(later) if later else ()
        return self._chip(later) if later else ()

    def finish(self):
        keys = _layer_tensors(0)[:-2]
        _run_comm(self._pair(keys), "pair_exchange_mixer0")
        _run_comm(self._chip(keys), "chip_exchange_mixer0")
        return {key: op["res"][n] for keys, op in self.recv_ops for n, key in enumerate(keys)}


def _forward_backward(x, target, p, wsrc, gsink):
    s, d = x.shape
    depth = p["norm_mix"].shape[0]
    ah = p["t5_bias"].shape[1]
    aw = ah * HEAD_DIM
    ch = p["c_rpb"].shape[1]
    groups, pstate = p["s5_lam_re"].shape[2:]
    bw = groups * S5_GROUP
    assert aw + bw == d and ch * HEAD_DIM == d

    dil_bias = [_dil_bias(p["t5_bias"], dil) for _, dil in DILATED_BRANCHES]
    saved = []
    for i in range(depth):
        j = i // 2
        sv = dict(x=x)
        hn = _rms_fwd(x, p["norm_mix"][i], f"norm_mix_fwd{i}")
        sv["hn"] = hn
        if i % 2 == 0:
            proj = _mm_cols(f"ab_in_fwd{i}", hn, wsrc.get("ab_w_in", j), comm=wsrc.carry(i, "in"))
            outs = [(_dil_fwd if dil == 1 else _dil_fwd_strided)(proj, dil_bias[b][0], dil, ah, f"dil_fwd_d{dil}_{i}")
                    for b, (_, dil) in enumerate(DILATED_BRANCHES)]
            o_a, lse = _dil_merge([o for o, _ in outs], [l for _, l in outs])
            tb = _s5_tables(p["s5_lam_re"][j], p["s5_lam_im"][j], p["s5_log_step"][j], p["s5_b_re"][j], p["s5_b_im"][j],
                            p["s5_c_re"][j], p["s5_c_im"][j])
            y0, x0r, x0i = _s5_fwd(proj, 3 * aw, tb[0]["tabs"], tb[0]["bre"], tb[0]["bim"], tb[0]["cre"], tb[0]["cim"],
                                   False, None, f"s5_fwd_a{i}")
            y_pre, x1r, x1i = _s5_fwd(proj, 3 * aw, tb[1]["tabs"], tb[1]["bre"], tb[1]["bim"], tb[1]["cre"], tb[1]["cim"],
                                      True, (y0, p["s5_d"][j]), f"s5_fwd_b{i}")
            o_b = _mm(f"glu_fwd{i}", y_pre, wsrc.get("s5_w_glu", j), a_fn=_gelu, extras=(y_pre,), out_dtypes=(BF16,),
                      epi=lambda acc, yp: (_gelu(yp) * jax.nn.sigmoid(acc),))[0]
            merged = jnp.concatenate([o_a, o_b], axis=1)
            x = _mm(f"ab_out_fwd{i}", merged, wsrc.get("ab_w_out", j), extras=(x,), epi=lambda acc, xr: (acc + xr,))[0]
            sv.update(proj=proj, o_a=o_a, lse=lse, tb=tb, states=((x0r, x0i), (x1r, x1i)), y_pre=y_pre, merged=merged)
        else:
            qkv = _mm_cols(f"c_qkv_fwd{i}", hn, wsrc.get("c_w_qkv", j), comm=wsrc.carry(i, "in"))
            nbias = _na_bias(p["c_rpb"][j])
            o, lse = _na_fwd(qkv, nbias, ch, f"na_fwd{i}")
            x = _mm(f"c_out_fwd{i}", o, wsrc.get("c_w_out", j), extras=(x,), epi=lambda acc, xr: (acc + xr,))[0]
            sv.update(qkv=qkv, o=o, lse=lse, nbias=nbias)
        sv["x_mid"] = x
        hn2 = _rms_fwd(x, p["norm_mlp"][i], f"norm_mlp_fwd{i}")
        h_pre = _mm_cols(f"mlp_up_fwd{i}", hn2, wsrc.get("mlp_w1", i), comm=wsrc.carry(i, "up"))
        x = _mm(f"mlp_down_fwd{i}", h_pre, wsrc.get("mlp_w2", i), a_fn=_relu_sq, extras=(x,), epi=lambda acc, xr: (acc + xr,),
                comm=wsrc.carry(i, "down"))[0]
        sv.update(hn2=hn2, h_pre=h_pre)
        saved.append(sv)

    loss_sum, dx, g_final = _final_loss(x, p["norm_final"], target)

    g = ({k: [None] * p[k].shape[0] for k in ("norm_mix", "norm_mlp", "s5_lam_re", "s5_lam_im", "s5_log_step", "s5_b_re",
                                                  "s5_b_im", "s5_c_re", "s5_c_im", "s5_d", "c_rpb")})
    g_t5 = jnp.zeros_like(p["t5_bias"], dtype=F32)
    for i in reversed(range(depth)):
        j = i // 2
        sv = saved[i]
        dh = _mm(f"mlp_down_bwd{i}", dx, wsrc.get("mlp_w2", i), tb=True, extras=(sv["h_pre"],), out_dtypes=(BF16,),
                 epi=lambda acc, hp: (acc * (2.0 * jnp.maximum(hp.astype(F32), 0.0)),), comm=gsink.carry(i, "down_bwd"))[0]
        gsink.put("mlp_w2", i, _mm(f"mlp_w2_grad{i}", sv["h_pre"], dx, ta=True, a_fn=_relu_sq, tk=1024)[0])
        gsink.put("mlp_w1", i, _mm_cols_grad(f"mlp_w1_grad{i}", sv["hn2"], dh, comm=gsink.carry(i, "w1_grad")))
        dhn2 = _mm_cols_t(f"mlp_up_bwd{i}", dh, wsrc.get("mlp_w1", i), comm=gsink.carry(i, "up_bwd"))
        dx, gn = _rms_bwd(sv["x_mid"], p["norm_mlp"][i], dhn2, dx, f"norm_mlp_bwd{i}")
        g["norm_mlp"][i] = gn[0]
        if i % 2 == 0:
            tb = sv["tb"]
            dmerged = _mm(f"ab_out_bwd{i}", dx, wsrc.get("ab_w_out", j), tb=True, out_dtypes=(BF16,))[0]
            gsink.put("ab_w_out", j, _mm(f"ab_w_out_grad{i}", sv["merged"], dx, ta=True, tk=1024)[0])
            def glu_epi(acc, yp, dob):
                sg = jax.nn.sigmoid(acc)
                dob = dob.astype(F32)
                return dob * _gelu(yp) * sg * (1.0 - sg), dob * sg
            dz, t1 = _mm(f"glu_bwd_z{i}", sv["y_pre"], wsrc.get("s5_w_glu", j), a_fn=_gelu, extras=(sv["y_pre"], dmerged),
                         extra_cols=(0, aw), epi=glu_epi, out_dtypes=(BF16, F32))
            dy_pre = _mm(f"glu_bwd_y{i}", dz, wsrc.get("s5_w_glu", j), tb=True, extras=(t1, sv["y_pre"]),
                         epi=lambda acc, t, yp: ((acc + t) * _gelu_grad(yp),), out_dtypes=(BF16,))[0]
            gsink.put("s5_w_glu", j, _mm(f"glu_w_grad{i}", sv["y_pre"], dz, ta=True, a_fn=_gelu, tk=1024)[0])
            r0 = _s5_bwd(sv["proj"], 3 * aw, dy_pre, *sv["states"][0], tb[0]["gtabs"], tb[0]["bre"], tb[0]["bim"],
                         tb[0]["cre"], tb[0]["cim"], False, None, f"s5_bwd_a{i}")
            r1 = _s5_bwd(sv["proj"], 3 * aw, dy_pre, *sv["states"][1], tb[1]["gtabs"], tb[1]["bre"], tb[1]["bim"],
                         tb[1]["cre"], tb[1]["cim"], True, (r0[0], p["s5_d"][j]), f"s5_bwd_b{i}")
            du = r1[0]
            g["s5_d"][j] = r1[7][0]
            gl_re, gl_im, gls, gb_re, gb_im, gc_re, gc_im = [], [], [], 0.0, 0.0, [], []
            for dnum, rr in enumerate((r0, r1)):
                da_re = jnp.sum(rr[1], axis=1).reshape(groups, pstate)
                da_im = jnp.sum(rr[2], axis=1).reshape(groups, pstate)
                dbb_re = jnp.transpose(_block_diag_take(rr[3], S5_GROUP, pstate), (0, 2, 1))
                dbb_im = jnp.transpose(_block_diag_take(rr[4], S5_GROUP, pstate), (0, 2, 1))
                _, vjp = jax.vjp(_s5_discretize, p["s5_lam_re"][j][dnum], p["s5_lam_im"][j][dnum], p["s5_log_step"][j][dnum],
                                 p["s5_b_re"][j], p["s5_b_im"][j])
                a, b, c, e, f = vjp((da_re, da_im, dbb_re, dbb_im))
                gl_re.append(a)
                gl_im.append(b)
                gls.append(c)
                gb_re, gb_im = gb_re + e, gb_im + f
                gc_re.append(jnp.transpose(_block_diag_take(rr[5], pstate, S5_GROUP), (0, 2, 1)))
                gc_im.append(jnp.transpose(_block_diag_take(rr[6], pstate, S5_GROUP), (0, 2, 1)))
            g["s5_lam_re"][j], g["s5_lam_im"][j], g["s5_log_step"][j] = jnp.stack(gl_re), jnp.stack(gl_im), jnp.stack(gls)
            g["s5_b_re"][j], g["s5_b_im"][j] = gb_re, gb_im
            g["s5_c_re"][j], g["s5_c_im"][j] = jnp.stack(gc_re), jnp.stack(gc_im)
            delta = _head_delta(dmerged, 0, sv["o_a"], f"dil_delta{i}")
            dq = dk = dv = 0.0
            for b, (_, dil) in enumerate(DILATED_BRANCHES):
                q1, k1, v1, db = (_dil_bwd if dil == 1 else _dil_bwd_strided)(
                    sv["proj"], dmerged, sv["lse"], delta, dil_bias[b][0], dil_bias[b][1], dil, ah,
                                          f"dil_bwd_d{dil}_{i}")
                dq, dk, dv = dq + q1, dk + k1, dv + v1
                g_t5 = g_t5 + _t5_grad(db, dil)
            dproj = jnp.concatenate([dq, dk, dv, du], axis=1).astype(BF16)
            gsink.put("ab_w_in", j, _mm_cols_grad(f"ab_w_in_grad{i}", sv["hn"], dproj, comm=gsink.carry(i, "in_grad")))
            dhn = _mm_cols_t(f"ab_in_bwd{i}", dproj, wsrc.get("ab_w_in", j), comm=gsink.carry(i, "in_bwd"))
        else:
            do = _mm(f"c_out_bwd{i}", dx, wsrc.get("c_w_out", j), tb=True, out_dtypes=(BF16,))[0]
            gsink.put("c_w_out", j, _mm(f"c_w_out_grad{i}", sv["o"], dx, ta=True, tk=1024)[0])
            dq, dk, dv, db = _na_bwd(sv["qkv"], sv["o"], do, sv["lse"], sv["nbias"], ch, f"na_bwd{i}")
            g["c_rpb"][j] = _rpb_grad(db)
            dqkv = jnp.concatenate([dq, dk.astype(BF16), dv.astype(BF16)], axis=1)
            gsink.put("c_w_qkv", j, _mm_cols_grad(f"c_w_qkv_grad{i}", sv["hn"], dqkv, comm=gsink.carry(i, "in_grad")))
            dhn = _mm_cols_t(f"c_qkv_bwd{i}", dqkv, wsrc.get("c_w_qkv", j), comm=gsink.carry(i, "in_bwd"))
        dx, gn = _rms_bwd(sv["x"], p["norm_mix"][i], dhn, dx, f"norm_mix_bwd{i}")
        g["norm_mix"][i] = gn[0]
    g["t5_bias"] = g_t5
    g["norm_final"] = g_final[0]
    return loss_sum[0, 0], dx, g


BIG = ("ab_w_in", "ab_w_out", "s5_w_glu", "c_w_qkv", "c_w_out", "mlp_w1", "mlp_w2")
ROW_SHARDED = ("ab_w_out", "s5_w_glu", "c_w_out", "mlp_w2")
WEIGHTS = ("t5_bias", "ab_w_in", "ab_w_out", "s5_lam_re", "s5_lam_im", "s5_log_step", "s5_b_re", "s5_b_im", "s5_c_re",
           "s5_c_im", "s5_d", "s5_w_glu", "c_w_qkv", "c_w_out", "c_rpb", "norm_mix", "norm_mlp", "mlp_w1", "mlp_w2",
           "norm_final")


def _step(x, target, w, m, v):
    d = x.shape[-1]
    depth = w["norm_mix"].shape[0]
    wsrc = _WeightGather({k: w[k].astype(BF16) for k in BIG}, depth)
    wsrc.start()
    gsink = _GradExchange(lax.axis_index("c").astype(jnp.int32).reshape(1), depth)
    small = {k: w[k] for k in WEIGHTS if k not in BIG}
    loss_sum, dx, g = _forward_backward(x[0], target[0], small, wsrc, gsink)
    loss = lax.psum(0.5 * loss_sum / d, ("x", "y", "c"))

    out = {}
    recv = gsink.finish()
    for k in BIG:
        res = None
        for l in range(w[k].shape[0]):
            res = _adamw(w[k], recv[(k, l)], m[k], v[k], f"adamw_{k}{l}", layer=l, prev=res)
        out[k] = res
    names = [k for k in WEIGHTS if k not in BIG]
    def flat(tree):
        return jnp.concatenate([jnp.asarray(jnp.stack(tree[k]) if isinstance(tree[k], list) else tree[k], F32).reshape(-1)
                                for k in names])
    total = sum(int(np.prod(w[k].shape)) for k in names)
    rows = -(-total // LANES)
    rows = -(-rows // SUBLANES) * SUBLANES
    pad = rows * LANES - total
    def pack(tree):
        return jnp.pad(flat(tree), (0, pad)).reshape(rows, LANES)
    small_op = _gather_op(pack(g))
    _run_comm([small_op], "gather_small_grads")
    res = _adamw(pack(w), small_op["res"][0], pack(m), pack(v), "adamw_small")
    off = 0
    for k in names:
        n = int(np.prod(w[k].shape))
        out[k] = [a.reshape(-1)[off:off + n].reshape(w[k].shape) for a in res]
        off += n
    return (loss, dx[None], *[out[k][0] for k in WEIGHTS], *[out[k][1] for k in WEIGHTS],
            *[out[k][2] for k in WEIGHTS], *[out[k][3] for k in WEIGHTS])


def kernel(x, t5_bias, ab_w_in, ab_w_out, s5_lam_re, s5_lam_im, s5_log_step, s5_b_re, s5_b_im, s5_c_re, s5_c_im, s5_d, s5_w_glu, c_w_qkv, c_w_out, c_rpb, norm_mix, norm_mlp, mlp_w1, mlp_w2, norm_final, loss_target, m_t5_bias, m_ab_w_in, m_ab_w_out, m_s5_lam_re, m_s5_lam_im, m_s5_log_step, m_s5_b_re, m_s5_b_im, m_s5_c_re, m_s5_c_im, m_s5_d, m_s5_w_glu, m_c_w_qkv, m_c_w_out, m_c_rpb, m_norm_mix, m_norm_mlp, m_mlp_w1, m_mlp_w2, m_norm_final, v_t5_bias, v_ab_w_in, v_ab_w_out, v_s5_lam_re, v_s5_lam_im, v_s5_log_step, v_s5_b_re, v_s5_b_im, v_s5_c_re, v_s5_c_im, v_s5_d, v_s5_w_glu, v_c_w_qkv, v_c_w_out, v_c_rpb, v_norm_mix, v_norm_mlp, v_mlp_w1, v_mlp_w2, v_norm_final):
    w = dict(t5_bias=t5_bias, ab_w_in=ab_w_in, ab_w_out=ab_w_out, s5_lam_re=s5_lam_re, s5_lam_im=s5_lam_im,
             s5_log_step=s5_log_step, s5_b_re=s5_b_re, s5_b_im=s5_b_im, s5_c_re=s5_c_re, s5_c_im=s5_c_im, s5_d=s5_d,
             s5_w_glu=s5_w_glu, c_w_qkv=c_w_qkv, c_w_out=c_w_out, c_rpb=c_rpb, norm_mix=norm_mix, norm_mlp=norm_mlp,
             mlp_w1=mlp_w1, mlp_w2=mlp_w2, norm_final=norm_final)
    m = dict(t5_bias=m_t5_bias, ab_w_in=m_ab_w_in, ab_w_out=m_ab_w_out, s5_lam_re=m_s5_lam_re, s5_lam_im=m_s5_lam_im,
             s5_log_step=m_s5_log_step, s5_b_re=m_s5_b_re, s5_b_im=m_s5_b_im, s5_c_re=m_s5_c_re, s5_c_im=m_s5_c_im,
             s5_d=m_s5_d, s5_w_glu=m_s5_w_glu, c_w_qkv=m_c_w_qkv, c_w_out=m_c_w_out, c_rpb=m_c_rpb, norm_mix=m_norm_mix,
             norm_mlp=m_norm_mlp, mlp_w1=m_mlp_w1, mlp_w2=m_mlp_w2, norm_final=m_norm_final)
    v = dict(t5_bias=v_t5_bias, ab_w_in=v_ab_w_in, ab_w_out=v_ab_w_out, s5_lam_re=v_s5_lam_re, s5_lam_im=v_s5_lam_im,
             s5_log_step=v_s5_log_step, s5_b_re=v_s5_b_re, s5_b_im=v_s5_b_im, s5_c_re=v_s5_c_re, s5_c_im=v_s5_c_im,
             s5_d=v_s5_d, s5_w_glu=v_s5_w_glu, c_w_qkv=v_c_w_qkv, c_w_out=v_c_w_out, c_rpb=v_c_rpb, norm_mix=v_norm_mix,
             norm_mlp=v_norm_mlp, mlp_w1=v_mlp_w1, mlp_w2=v_mlp_w2, norm_final=v_norm_final)
    return _step(x, loss_target, w, m, v)
```

```python
import math

import jax
import jax.numpy as jnp
import numpy as np
from jax import lax
from jax.experimental import pallas as pl
from jax.experimental.pallas import tpu as pltpu

F32 = jnp.float32
BF16 = jnp.bfloat16

N_DEV = 8
HEAD_DIM = 128
LANES = 128
QBLOCK = 128
DIL_HALF = 64
DILATED_BRANCHES = ((128, 1), (512, 4), (2048, 16))
S5_GROUP = 16
S5_GROUPS_PER_BLOCK = LANES // S5_GROUP
S5_CHUNK = 512
SUBLANES = 8
GRID_W = 64
NA_ROWS = 8
NA_COLS = 16
NA_GROUP = 4
NA_WIN = NA_GROUP + NA_ROWS - 1
NA_GROUPS_PER_STEP = 4
T5_BUCKETS = 32
T5_MAX_DISTANCE = 1024
RMS_EPS = 1e-6
NEG_INF = -1e30
ADAM_LR = 0.001
ADAM_B1 = 0.9
ADAM_B2 = 0.999
ADAM_EPS = 1e-08
ADAM_WD = 0.01
ADAM_STEP = 10
VMEM_LIMIT_BYTES = 56 * 1024 * 1024
MESH = pl.DeviceIdType.MESH


def _cparams(sem=None):
    return pltpu.CompilerParams(dimension_semantics=sem, vmem_limit_bytes=VMEM_LIMIT_BYTES)


def _tile(dim, pref):
    t = min(dim, pref)
    while dim % t and t > LANES:
        t -= LANES
    assert dim % t == 0, (dim, pref)
    return t


def _dot(a, b, ca, cb):
    return lax.dot_general(a, b, (((ca,), (cb,)), ((), ())), preferred_element_type=F32)


def _nn(a, b):
    return _dot(a, b, 1, 0)


def _nt(a, b):
    return _dot(a, b, 1, 1)


def _tn(a, b):
    return _dot(a, b, 0, 0)


HBM_SPEC = pl.BlockSpec(memory_space=pltpu.HBM)


def _split_comm_refs(comm, in_refs, out_refs, sem_refs):
    parts, i, o, s = [], 0, 0, 0
    for op in comm:
        ni, no, ns = len(op["ins"]), len(op["outs"]), len(op["sems"])
        parts.append((in_refs[i:i + ni], out_refs[o:o + no], sem_refs[s:s + ns]))
        i, o, s = i + ni, o + no, s + ns
    return parts


def _mm_call(name, a, b, a_spec, b_spec, grid, nk, out_shapes, out_specs, acc_shape,
             ta=False, tb=False, a_fn=None, epi=None, extras=(), extra_specs=(), comm=()):
    ne, no = len(extras), len(out_shapes)
    comm_ins = [x for op in comm for x in op["ins"]]
    comm_outs = [x for op in comm for x in op["outs"]]
    comm_sems = [x for op in comm for x in op["sems"]]
    nci, nco = len(comm_ins), len(comm_outs)
    total = grid[0] * grid[1] * grid[2]

    def body(a_ref, b_ref, *rest):
        ex, rest = rest[:ne], rest[ne:]
        cin, rest = rest[:nci], rest[nci:]
        outs, rest = rest[:no], rest[no:]
        cout, rest = rest[:nco], rest[nco:]
        acc, csem = rest[0], rest[1:]
        k = pl.program_id(2)
        step = (pl.program_id(0) * grid[1] + pl.program_id(1)) * grid[2] + k
        parts = _split_comm_refs(comm, cin, cout, csem)

        def phase(which, at):
            if comm:
                @pl.when(step == at)
                def _():
                    for op, refs in zip(comm, parts):
                        op["run"](which, *refs)

        phase("start", 0)
        phase("mid", total - 1 - total // 8)

        @pl.when(k == 0)
        def _():
            acc[...] = jnp.zeros_like(acc)

        av = a_ref[...]
        if a_fn is not None:
            av = a_fn(av)
        acc[...] += _dot(av.astype(BF16), b_ref[...].astype(BF16), 0 if ta else 1, 1 if tb else 0)

        @pl.when(k == nk - 1)
        def _():
            r = acc[...]
            res = epi(r, *[e[...] for e in ex]) if epi is not None else (r,)
            for o, v in zip(outs, res):
                o[...] = v.astype(o.dtype)

        phase("wait", total - 1)

    res = pl.pallas_call(
        body, grid=grid, in_specs=[a_spec, b_spec, *extra_specs] + [HBM_SPEC] * nci,
        out_specs=list(out_specs) + [HBM_SPEC] * nco, out_shape=list(out_shapes) + comm_outs,
        scratch_shapes=[pltpu.VMEM(acc_shape, F32)] + comm_sems,
        compiler_params=_cparams(("arbitrary",) * 3 if comm else ("parallel", "parallel", "arbitrary")), name=name,
    )(a, b, *extras, *comm_ins)
    o = no
    for op in comm:
        op["res"] = res[o:o + len(op["outs"])]
        o += len(op["outs"])
    return res[:no]


def _run_comm(comm, name):
    comm_ins = [x for op in comm for x in op["ins"]]
    comm_outs = [x for op in comm for x in op["outs"]]
    comm_sems = [x for op in comm for x in op["sems"]]
    nci, nco = len(comm_ins), len(comm_outs)

    def body(*refs):
        parts = _split_comm_refs(comm, refs[:nci], refs[nci:nci + nco], refs[nci + nco:])
        for which in ("start", "mid", "wait"):
            for op, r in zip(comm, parts):
                op["run"](which, *r)

    res = pl.pallas_call(
        body, in_specs=[HBM_SPEC] * nci, out_specs=[HBM_SPEC] * nco, out_shape=comm_outs, scratch_shapes=comm_sems,
        compiler_params=pltpu.CompilerParams(has_side_effects=True), name=name)(*comm_ins)
    o = 0
    for op in comm:
        op["res"] = res[o:o + len(op["outs"])]
        o += len(op["outs"])


def _mm(name, a, b, *, ta=False, tb=False, a_fn=None, epi=None, extras=(), extra_cols=None,
        out_dtypes=(F32,), tm=1024, tn=1024, tk=2048, comm=()):
    m, kdim = (a.shape[1], a.shape[0]) if ta else a.shape
    n = b.shape[0] if tb else b.shape[1]
    assert (b.shape[1] if tb else b.shape[0]) == kdim, (a.shape, b.shape)
    tm, tn, tk = _tile(m, tm), _tile(n, tn), _tile(kdim, tk)
    a_spec = pl.BlockSpec((tk, tm), lambda i, j, k: (k, i)) if ta else pl.BlockSpec((tm, tk), lambda i, j, k: (i, k))
    b_spec = pl.BlockSpec((tn, tk), lambda i, j, k: (j, k)) if tb else pl.BlockSpec((tk, tn), lambda i, j, k: (k, j))
    o_spec = pl.BlockSpec((tm, tn), lambda i, j, k: (i, j))
    extra_cols = extra_cols or (0,) * len(extras)
    especs = []
    for c0 in extra_cols:
        assert c0 % tn == 0
        cb = c0 // tn
        especs.append(pl.BlockSpec((tm, tn), lambda i, j, k, cb=cb: (i, cb + j)))
    return _mm_call(name, a, b, a_spec, b_spec, (m // tm, n // tn, kdim // tk), kdim // tk,
                    [jax.ShapeDtypeStruct((m, n), d) for d in out_dtypes], [o_spec] * len(out_dtypes), (tm, tn),
                    ta=ta, tb=tb, a_fn=a_fn, epi=epi, extras=extras, extra_specs=especs, comm=comm)


def _mm_cols(name, a, w, *, comm=()):
    return _mm(name, a, w, out_dtypes=(BF16,), comm=comm)[0]


def _mm_cols_t(name, a, w, *, comm=()):
    return _mm(name, a, w, tb=True, out_dtypes=(BF16,), comm=comm)[0]


def _mm_cols_grad(name, a, dy, *, tm=1024, tk=2048, comm=()):
    s, kout = a.shape
    n = dy.shape[1] // N_DEV
    tm, tk = _tile(kout, tm), _tile(s, tk)
    return _mm_call(name, a, dy, pl.BlockSpec((tk, tm), lambda i, j, k: (k, i)),
                    pl.BlockSpec((tk, n), lambda i, j, k: (k, j)),
                    (kout // tm, N_DEV, s // tk), s // tk,
                    [jax.ShapeDtypeStruct((N_DEV, kout, n), F32)], [pl.BlockSpec((None, tm, n), lambda i, j, k: (j, i, 0))],
                    (tm, n), ta=True, comm=comm)[0]


_GELU_C = math.sqrt(2.0 / math.pi)


def _gelu(x):
    return 0.5 * x * (1.0 + jnp.tanh(_GELU_C * (x + 0.044715 * x * x * x)))


def _gelu_grad(x):
    t = jnp.tanh(_GELU_C * (x + 0.044715 * x * x * x))
    return 0.5 * (1.0 + t) + 0.5 * x * (1.0 - t * t) * _GELU_C * (1.0 + 3.0 * 0.044715 * x * x)


def _relu_sq(x):
    r = jnp.maximum(x.astype(F32), 0.0)
    return r * r


def _rms_fwd(x, g, name):
    s, d = x.shape
    tr = _tile(s, 512)

    def body(x_ref, g_ref, o_ref):
        xv = x_ref[...]
        r = lax.rsqrt(jnp.mean(xv * xv, axis=-1, keepdims=True) + RMS_EPS)
        o_ref[...] = (xv * r * g_ref[...]).astype(BF16)

    return pl.pallas_call(
        body, grid=(s // tr,),
        in_specs=[pl.BlockSpec((tr, d), lambda i: (i, 0)), pl.BlockSpec((1, d), lambda i: (0, 0))],
        out_specs=pl.BlockSpec((tr, d), lambda i: (i, 0)), out_shape=jax.ShapeDtypeStruct((s, d), BF16),
        compiler_params=_cparams(("parallel",)), name=name)(x, g.reshape(1, d))


def _rms_bwd(x, g, dy, dres, name):
    s, d = x.shape
    tr = _tile(s, 512)

    def body(x_ref, g_ref, dy_ref, dres_ref, dx_ref, dg_ref):
        @pl.when(pl.program_id(0) == 0)
        def _():
            dg_ref[...] = jnp.zeros_like(dg_ref)

        xv = x_ref[...]
        dyv = dy_ref[...].astype(F32)
        r = lax.rsqrt(jnp.mean(xv * xv, axis=-1, keepdims=True) + RMS_EPS)
        xh = xv * r
        gdy = dyv * g_ref[...]
        dx_ref[...] = dres_ref[...] + r * (gdy - xh * jnp.mean(gdy * xh, axis=-1, keepdims=True))
        dg_ref[...] += jnp.sum(dyv * xh, axis=0, keepdims=True)

    return pl.pallas_call(
        body, grid=(s // tr,),
        in_specs=[pl.BlockSpec((tr, d), lambda i: (i, 0)), pl.BlockSpec((1, d), lambda i: (0, 0)),
                  pl.BlockSpec((tr, d), lambda i: (i, 0)), pl.BlockSpec((tr, d), lambda i: (i, 0))],
        out_specs=[pl.BlockSpec((tr, d), lambda i: (i, 0)), pl.BlockSpec((1, d), lambda i: (0, 0))],
        out_shape=[jax.ShapeDtypeStruct((s, d), F32), jax.ShapeDtypeStruct((1, d), F32)],
        compiler_params=_cparams(("arbitrary",)), name=name)(x, g.reshape(1, d), dy, dres)


def _final_loss(x, g, target):
    s, d = x.shape
    tr = _tile(s, 512)

    def body(x_ref, g_ref, t_ref, loss_ref, dx_ref, dg_ref):
        @pl.when(pl.program_id(0) == 0)
        def _():
            dg_ref[...] = jnp.zeros_like(dg_ref)
            loss_ref[...] = jnp.zeros_like(loss_ref)

        xv = x_ref[...]
        gv = g_ref[...]
        r = lax.rsqrt(jnp.mean(xv * xv, axis=-1, keepdims=True) + RMS_EPS)
        xh = xv * r
        err = xh * gv - t_ref[...]
        loss_ref[...] += jnp.sum(jnp.sum(err * err, axis=-1, keepdims=True), axis=0, keepdims=True)
        dyv = err * (1.0 / d)
        gdy = dyv * gv
        dx_ref[...] = r * (gdy - xh * jnp.mean(gdy * xh, axis=-1, keepdims=True))
        dg_ref[...] += jnp.sum(dyv * xh, axis=0, keepdims=True)

    return pl.pallas_call(
        body, grid=(s // tr,),
        in_specs=[pl.BlockSpec((tr, d), lambda i: (i, 0)), pl.BlockSpec((1, d), lambda i: (0, 0)),
                  pl.BlockSpec((tr, d), lambda i: (i, 0))],
        out_specs=[pl.BlockSpec((1, 1), lambda i: (0, 0)), pl.BlockSpec((tr, d), lambda i: (i, 0)),
                   pl.BlockSpec((1, d), lambda i: (0, 0))],
        out_shape=[jax.ShapeDtypeStruct((1, 1), F32), jax.ShapeDtypeStruct((s, d), F32),
                   jax.ShapeDtypeStruct((1, d), F32)],
        compiler_params=_cparams(("arbitrary",)), name="final_norm_loss")(x, g.reshape(1, d), target)


def _t5_bucket(rel):
    half = T5_BUCKETS // 2
    max_exact = half // 2
    n = jnp.abs(rel)
    nf = jnp.maximum(n, 1).astype(F32)
    large = max_exact + (jnp.log(nf / max_exact) / math.log(T5_MAX_DISTANCE / max_exact)
                         * (half - max_exact)).astype(jnp.int32)
    large = jnp.minimum(large, half - 1)
    return jnp.where(rel > 0, half, 0) + jnp.where(n < max_exact, n, large)


def _dil_offsets():
    i = jnp.arange(QBLOCK)[:, None]
    kk = jnp.arange(2 * QBLOCK)[None, :]
    return kk - DIL_HALF - i, (jnp.arange(QBLOCK)[None, :] + DIL_HALF) - jnp.arange(2 * QBLOCK)[:, None]


def _expand(table, onehot, name, tn=8192):
    r, n = table.shape[0], onehot.shape[1]
    tn = _tile(n, tn)

    def body(t_ref, oh_ref, o_ref):
        o_ref[...] = lax.dot_general(t_ref[...], oh_ref[...], (((1,), (0,)), ((), ())),
                                     precision=lax.Precision.HIGHEST, preferred_element_type=F32)

    return pl.pallas_call(
        body, grid=(n // tn,),
        in_specs=[pl.BlockSpec((r, LANES), lambda i: (0, 0)), pl.BlockSpec((LANES, tn), lambda i: (0, i))],
        out_specs=pl.BlockSpec((r, tn), lambda i: (0, i)), out_shape=jax.ShapeDtypeStruct((r, n), F32),
        compiler_params=_cparams(("parallel",)), name=name)(table, onehot)


def _pad_rows_lanes(t):
    r, c = t.shape
    return jnp.pad(t.astype(F32), ((0, -r % SUBLANES), (0, LANES - c)))


def _dil_bias(t5, dil):
    ah = t5.shape[1]
    off1, off2 = _dil_offsets()
    bucket = jnp.concatenate([_t5_bucket(off1 * dil).reshape(-1), _t5_bucket(off2 * dil).reshape(-1)])
    onehot = (jnp.arange(LANES)[:, None] == bucket[None, :]).astype(F32)
    b = _expand(_pad_rows_lanes(t5.T), onehot, f"t5_bias_d{dil}")[:ah]
    n1 = QBLOCK * 2 * QBLOCK
    return b[:, :n1].reshape(ah, QBLOCK, 2 * QBLOCK), b[:, n1:].reshape(ah, 2 * QBLOCK, QBLOCK)


def _window(p, c, n, cols=slice(None)):
    return jnp.concatenate([p[pl.ds(DIL_HALF, DIL_HALF), cols], c[:, cols], n[pl.ds(0, DIL_HALF), cols]], axis=0)


def _dil_specs(width, ncol_blocks, col_block, nb):
    def spec(dn):
        return pl.BlockSpec((QBLOCK, width), lambda r, n: (jnp.clip(n + dn, 0, nb - 1), r * ncol_blocks + col_block))
    return [spec(-1), spec(0), spec(1)]


def _dil_fwd(proj, bias1, dil, ah, name):
    s, wtot = proj.shape
    ln = s // dil
    nb = ln // QBLOCK
    assert nb * QBLOCK * dil == s
    aw = ah * HEAD_DIM
    wb = wtot // aw
    scale = 1.0 / math.sqrt(HEAD_DIM)
    pv = proj.reshape(ln, dil * wtot)

    def body(q_ref, kp, kc, kn, vp, vc, vn, b_ref, o_ref, lse_ref):
        n = pl.program_id(1)
        ii = lax.broadcasted_iota(jnp.int32, (QBLOCK, 2 * QBLOCK), 0)
        jj = lax.broadcasted_iota(jnp.int32, (QBLOCK, 2 * QBLOCK), 1)
        kpos = n * QBLOCK + jj - DIL_HALF
        valid = (jnp.abs(jj - DIL_HALF - ii) <= DIL_HALF) & (kpos >= 0) & (kpos < ln)
        for h in range(ah):
            cs = pl.ds(h * HEAD_DIM, HEAD_DIM)
            kw, vw = _window(kp, kc, kn, cs), _window(vp, vc, vn, cs)
            sc = jnp.where(valid, _nt(q_ref[:, cs], kw) * scale + b_ref[h], NEG_INF)
            m = jnp.max(sc, axis=-1, keepdims=True)
            p = jnp.exp(sc - m)
            l = jnp.sum(p, axis=-1, keepdims=True)
            o_ref[:, cs] = (_nn(p.astype(BF16), vw) / l).astype(BF16)
            lse_ref[:, cs] = jnp.broadcast_to(m + jnp.log(l), (QBLOCK, HEAD_DIM))

    ospec = pl.BlockSpec((QBLOCK, aw), lambda r, n: (n, r))
    o, lse = pl.pallas_call(
        body, grid=(dil, nb),
        in_specs=[_dil_specs(aw, wb, 0, nb)[1], *_dil_specs(aw, wb, 1, nb), *_dil_specs(aw, wb, 2, nb),
                  pl.BlockSpec((ah, QBLOCK, 2 * QBLOCK), lambda r, n: (0, 0, 0))],
        out_specs=[ospec, ospec],
        out_shape=[jax.ShapeDtypeStruct((ln, dil * aw), BF16), jax.ShapeDtypeStruct((ln, dil * aw), F32)],
        compiler_params=_cparams(("parallel", "parallel")), name=name,
    )(pv, pv, pv, pv, pv, pv, pv, bias1)
    return o.reshape(s, aw), lse.reshape(s, aw)


def _dil_merge(outs, lses):
    s, aw = outs[0].shape
    tr = _tile(s, 512)

    def body(o1, l1, o2, l2, o3, l3, o_ref, lse_ref):
        a, b, c = l1[...], l2[...], l3[...]
        m = jnp.maximum(jnp.maximum(a, b), c)
        w1, w2, w3 = jnp.exp(a - m), jnp.exp(b - m), jnp.exp(c - m)
        tot = w1 + w2 + w3
        o_ref[...] = ((w1 * o1[...].astype(F32) + w2 * o2[...].astype(F32) + w3 * o3[...].astype(F32)) / tot).astype(BF16)
        lse_ref[...] = m + jnp.log(tot)

    spec = pl.BlockSpec((tr, aw), lambda i: (i, 0))
    return pl.pallas_call(
        body, grid=(s // tr,), in_specs=[spec] * 6, out_specs=[spec, spec],
        out_shape=[jax.ShapeDtypeStruct((s, aw), BF16), jax.ShapeDtypeStruct((s, aw), F32)],
        compiler_params=_cparams(("parallel",)), name="dil_merge",
    )(outs[0], lses[0], outs[1], lses[1], outs[2], lses[2])


def _head_delta(do, do_col0, o, name):
    s, w = o.shape
    tr = _tile(s, 512)
    cb = do_col0 // HEAD_DIM

    def body(do_ref, o_ref, d_ref):
        d = jnp.sum(do_ref[...].astype(F32) * o_ref[...].astype(F32), axis=-1, keepdims=True)
        d_ref[...] = jnp.broadcast_to(d, d_ref.shape)

    return pl.pallas_call(
        body, grid=(s // tr, w // HEAD_DIM),
        in_specs=[pl.BlockSpec((tr, HEAD_DIM), lambda i, h: (i, cb + h)), pl.BlockSpec((tr, HEAD_DIM), lambda i, h: (i, h))],
        out_specs=pl.BlockSpec((tr, HEAD_DIM), lambda i, h: (i, h)), out_shape=jax.ShapeDtypeStruct((s, w), F32),
        compiler_params=_cparams(("parallel", "parallel")), name=name)(do, o)


def _dil_bwd(proj, dmerged, lse, delta, bias1, bias2, dil, ah, name):
    s, wtot = proj.shape
    ln = s // dil
    nb = ln // QBLOCK
    aw = ah * HEAD_DIM
    wb = wtot // aw
    wd = dmerged.shape[1] // aw
    scale = 1.0 / math.sqrt(HEAD_DIM)
    pv = proj.reshape(ln, dil * wtot)
    dov = dmerged.reshape(ln, dil * dmerged.shape[1])
    lv = lse.reshape(ln, dil * aw)
    dlv = delta.reshape(ln, dil * aw)

    def body(qp, qc, qn, kp, kc, kn, vp, vc, vn, dop, doc, don, lp, lc, lnx, dp, dc, dn, b1_ref, b2_ref,
             dq_ref, dk_ref, dv_ref, db_ref):
        n = pl.program_id(1)

        @pl.when((pl.program_id(0) == 0) & (n == 0))
        def _():
            db_ref[...] = jnp.zeros_like(db_ref)

        ii = lax.broadcasted_iota(jnp.int32, (QBLOCK, 2 * QBLOCK), 0)
        jj = lax.broadcasted_iota(jnp.int32, (QBLOCK, 2 * QBLOCK), 1)
        kpos = n * QBLOCK + jj - DIL_HALF
        valid = (jnp.abs(jj - DIL_HALF - ii) <= DIL_HALF) & (kpos >= 0) & (kpos < ln)
        ww = lax.broadcasted_iota(jnp.int32, (2 * QBLOCK, QBLOCK), 0)
        cc = lax.broadcasted_iota(jnp.int32, (2 * QBLOCK, QBLOCK), 1)
        qpos = n * QBLOCK - DIL_HALF + ww
        valid2 = (jnp.abs(cc + DIL_HALF - ww) <= DIL_HALF) & (qpos >= 0) & (qpos < ln)
        for h in range(ah):
            cs = pl.ds(h * HEAD_DIM, HEAD_DIM)
            kw, vw = _window(kp, kc, kn, cs), _window(vp, vc, vn, cs)
            sc = _nt(qc[:, cs], kw) * scale + b1_ref[h]
            lse2 = jnp.concatenate([lc[:, cs], lc[:, cs]], axis=1)
            p = jnp.where(valid, jnp.exp(jnp.where(valid, sc - lse2, 0.0)), 0.0)
            ds = p * (_nt(doc[:, cs], vw) - jnp.concatenate([dc[:, cs], dc[:, cs]], axis=1))
            dq_ref[:, cs] = _nn(ds.astype(BF16), kw) * scale
            db_ref[h] += ds
            qw, dow = _window(qp, qc, qn, cs), _window(dop, doc, don, cs)
            sc2 = _nt(qw, kc[:, cs]) * scale + b2_ref[h]
            p2 = jnp.where(valid2, jnp.exp(jnp.where(valid2, sc2 - _window(lp, lc, lnx, cs), 0.0)), 0.0)
            dv_ref[:, cs] = _tn(p2.astype(BF16), dow)
            ds2 = p2 * (_nt(dow, vc[:, cs]) - _window(dp, dc, dn, cs))
            dk_ref[:, cs] = _tn(ds2.astype(BF16), qw) * scale

    ospec = pl.BlockSpec((QBLOCK, aw), lambda r, n: (n, r))
    dq, dk, dv, db = pl.pallas_call(
        body, grid=(dil, nb),
        in_specs=[*_dil_specs(aw, wb, 0, nb), *_dil_specs(aw, wb, 1, nb), *_dil_specs(aw, wb, 2, nb),
                  *_dil_specs(aw, wd, 0, nb), *_dil_specs(aw, 1, 0, nb), *_dil_specs(aw, 1, 0, nb),
                  pl.BlockSpec((ah, QBLOCK, 2 * QBLOCK), lambda r, n: (0, 0, 0)),
                  pl.BlockSpec((ah, 2 * QBLOCK, QBLOCK), lambda r, n: (0, 0, 0))],
        out_specs=[ospec, ospec, ospec, pl.BlockSpec((ah, QBLOCK, 2 * QBLOCK), lambda r, n: (0, 0, 0))],
        out_shape=[jax.ShapeDtypeStruct((ln, dil * aw), F32)] * 3 + [jax.ShapeDtypeStruct((ah, QBLOCK, 2 * QBLOCK), F32)],
        compiler_params=_cparams(("arbitrary", "arbitrary")), name=name,
    )(pv, pv, pv, pv, pv, pv, pv, pv, pv, dov, dov, dov, lv, lv, lv, dlv, dlv, dlv, bias1, bias2)
    return dq.reshape(s, aw), dk.reshape(s, aw), dv.reshape(s, aw), db


def _strided_specs(rows, col_block, nsb):
    half = rows // 2
    return [pl.BlockSpec((half, HEAD_DIM), lambda h, n: (jnp.clip(2 * n - 1, 0, 2 * nsb - 1), col_block + h)),
            pl.BlockSpec((rows, HEAD_DIM), lambda h, n: (n, col_block + h)),
            pl.BlockSpec((half, HEAD_DIM), lambda h, n: (jnp.clip(2 * n + 2, 0, 2 * nsb - 1), col_block + h))]


def _fill_window(dst, p, c, n):
    half, rows = p.shape[0], c.shape[0]
    dst[pl.ds(0, half), :] = p[...].astype(F32)
    dst[pl.ds(half, rows), :] = c[...].astype(F32)
    dst[pl.ds(half + rows, half), :] = n[...].astype(F32)


def _dil_masks(n, ln):
    ii = lax.broadcasted_iota(jnp.int32, (QBLOCK, 2 * QBLOCK), 0)
    jj = lax.broadcasted_iota(jnp.int32, (QBLOCK, 2 * QBLOCK), 1)
    kpos = n * QBLOCK + jj - DIL_HALF
    valid = (jnp.abs(jj - DIL_HALF - ii) <= DIL_HALF) & (kpos >= 0) & (kpos < ln)
    ww = lax.broadcasted_iota(jnp.int32, (2 * QBLOCK, QBLOCK), 0)
    cc = lax.broadcasted_iota(jnp.int32, (2 * QBLOCK, QBLOCK), 1)
    qpos = n * QBLOCK - DIL_HALF + ww
    valid2 = (jnp.abs(cc + DIL_HALF - ww) <= DIL_HALF) & (qpos >= 0) & (qpos < ln)
    return valid, valid2


def _dil_fwd_strided(proj, bias1, dil, ah, name):
    s, wtot = proj.shape
    ln = s // dil
    nsb = ln // QBLOCK
    assert nsb * QBLOCK * dil == s
    aw = ah * HEAD_DIM
    rows = QBLOCK * dil
    half = rows // 2
    scale = 1.0 / math.sqrt(HEAD_DIM)

    def body(q_ref, kp, kc, kn, vp, vc, vn, b_ref, o_ref, lse_ref, qf, kf, vf, of):
        valid, _ = _dil_masks(pl.program_id(1), ln)
        qf[...] = q_ref[...].astype(F32)
        _fill_window(kf, kp, kc, kn)
        _fill_window(vf, vp, vc, vn)
        bias = b_ref[...]

        def residue(r, carry):
            qs = pl.ds(r, QBLOCK, stride=dil)
            ws = pl.ds(r, 2 * QBLOCK, stride=dil)
            kw, vw = kf[ws, :].astype(BF16), vf[ws, :].astype(BF16)
            sc = jnp.where(valid, _nt(qf[qs, :].astype(BF16), kw) * scale + bias, NEG_INF)
            m = jnp.max(sc, axis=-1, keepdims=True)
            p = jnp.exp(sc - m)
            l = jnp.sum(p, axis=-1, keepdims=True)
            of[qs, :] = _nn(p.astype(BF16), vw) / l
            lse_ref[qs, :] = jnp.broadcast_to(m + jnp.log(l), (QBLOCK, HEAD_DIM))
            return carry

        lax.fori_loop(0, dil, residue, 0, unroll=4)
        o_ref[...] = of[...].astype(BF16)

    ospec = pl.BlockSpec((rows, HEAD_DIM), lambda h, n: (n, h))
    return pl.pallas_call(
        body, grid=(ah, nsb),
        in_specs=[_strided_specs(rows, 0, nsb)[1], *_strided_specs(rows, ah, nsb), *_strided_specs(rows, 2 * ah, nsb),
                  pl.BlockSpec((None, QBLOCK, 2 * QBLOCK), lambda h, n: (h, 0, 0))],
        out_specs=[ospec, ospec],
        out_shape=[jax.ShapeDtypeStruct((s, aw), BF16), jax.ShapeDtypeStruct((s, aw), F32)],
        scratch_shapes=[pltpu.VMEM((rows, HEAD_DIM), F32), pltpu.VMEM((2 * rows, HEAD_DIM), F32),
                        pltpu.VMEM((2 * rows, HEAD_DIM), F32), pltpu.VMEM((rows, HEAD_DIM), F32)],
        compiler_params=_cparams(("parallel", "parallel")), name=name,
    )(proj, proj, proj, proj, proj, proj, proj, bias1)


def _dil_bwd_strided(proj, dmerged, lse, delta, bias1, bias2, dil, ah, name):
    s, wtot = proj.shape
    ln = s // dil
    nsb = ln // QBLOCK
    aw = ah * HEAD_DIM
    rows = QBLOCK * dil
    half = rows // 2
    scale = 1.0 / math.sqrt(HEAD_DIM)

    def body(qp, qc, qn, kp, kc, kn, vp, vc, vn, dop, doc, don, lp, lc, lnx, dp, dc, dn, b1_ref, b2_ref,
             dq_ref, dk_ref, dv_ref, db_ref, qf, kf, vf, dof):
        n = pl.program_id(1)

        @pl.when(n == 0)
        def _():
            db_ref[...] = jnp.zeros_like(db_ref)

        valid, valid2 = _dil_masks(n, ln)
        _fill_window(qf, qp, qc, qn)
        _fill_window(kf, kp, kc, kn)
        _fill_window(vf, vp, vc, vn)
        _fill_window(dof, dop, doc, don)
        b1, b2 = b1_ref[...], b2_ref[...]

        def stat_window(p, c, nx, r):
            return jnp.concatenate([p[pl.ds(r, DIL_HALF, stride=dil), :], c[pl.ds(r, QBLOCK, stride=dil), :],
                                    nx[pl.ds(r, DIL_HALF, stride=dil), :]], axis=0)

        def residue(r, carry):
            cs = pl.ds(half + r, QBLOCK, stride=dil)
            ws = pl.ds(r, 2 * QBLOCK, stride=dil)
            os = pl.ds(r, QBLOCK, stride=dil)
            qw, kw, vw, dow = (t[ws, :].astype(BF16) for t in (qf, kf, vf, dof))
            q, k, v, do = (t[cs, :].astype(BF16) for t in (qf, kf, vf, dof))
            lse_c, delta_c = lc[os, :], dc[os, :]
            sc = _nt(q, kw) * scale + b1
            p = jnp.where(valid, jnp.exp(jnp.where(valid, sc - jnp.concatenate([lse_c, lse_c], axis=1), 0.0)), 0.0)
            ds = p * (_nt(do, vw) - jnp.concatenate([delta_c, delta_c], axis=1))
            dq_ref[os, :] = _nn(ds.astype(BF16), kw) * scale
            db_ref[...] += ds
            sc2 = _nt(qw, k) * scale + b2
            p2 = jnp.where(valid2, jnp.exp(jnp.where(valid2, sc2 - stat_window(lp, lc, lnx, r), 0.0)), 0.0)
            dv_ref[os, :] = _tn(p2.astype(BF16), dow)
            ds2 = p2 * (_nt(dow, v) - stat_window(dp, dc, dn, r))
            dk_ref[os, :] = _tn(ds2.astype(BF16), qw) * scale
            return carry

        lax.fori_loop(0, dil, residue, 0, unroll=4)

    ospec = pl.BlockSpec((rows, HEAD_DIM), lambda h, n: (n, h))
    win = pltpu.VMEM((2 * rows, HEAD_DIM), F32)
    return pl.pallas_call(
        body, grid=(ah, nsb),
        in_specs=[*_strided_specs(rows, 0, nsb), *_strided_specs(rows, ah, nsb), *_strided_specs(rows, 2 * ah, nsb),
                  *_strided_specs(rows, 0, nsb), *_strided_specs(rows, 0, nsb), *_strided_specs(rows, 0, nsb),
                  pl.BlockSpec((None, QBLOCK, 2 * QBLOCK), lambda h, n: (h, 0, 0)),
                  pl.BlockSpec((None, 2 * QBLOCK, QBLOCK), lambda h, n: (h, 0, 0))],
        out_specs=[ospec, ospec, ospec, pl.BlockSpec((None, QBLOCK, 2 * QBLOCK), lambda h, n: (h, 0, 0))],
        out_shape=[jax.ShapeDtypeStruct((s, aw), F32)] * 3 + [jax.ShapeDtypeStruct((ah, QBLOCK, 2 * QBLOCK), F32)],
        scratch_shapes=[win, win, win, win],
        compiler_params=_cparams(("parallel", "arbitrary")), name=name,
    )(proj, proj, proj, proj, proj, proj, proj, proj, proj, dmerged, dmerged, dmerged, lse, lse, lse,
      delta, delta, delta, bias1, bias2)


def _bucket_sum(vals, onehot, name):
    r, n = vals.shape
    b = onehot.shape[1]

    def body(v_ref, oh_ref, o_ref):
        o_ref[...] = lax.dot_general(v_ref[...], oh_ref[...], (((1,), (0,)), ((), ())),
                                     precision=lax.Precision.HIGHEST, preferred_element_type=F32)

    tr = max(t for t in range(SUBLANES, 257, SUBLANES) if r % t == 0)
    return pl.pallas_call(
        body, grid=(r // tr,),
        in_specs=[pl.BlockSpec((tr, n), lambda i: (i, 0)), pl.BlockSpec((n, b), lambda i: (0, 0))],
        out_specs=pl.BlockSpec((tr, b), lambda i: (i, 0)), out_shape=jax.ShapeDtypeStruct((r, b), F32),
        compiler_params=_cparams(("parallel",)), name=name)(vals, onehot)


def _t5_grad(dbias, dil):
    ah = dbias.shape[0]
    off1, _ = _dil_offsets()
    bucket = _t5_bucket(off1 * dil).reshape(-1)
    inside = (jnp.abs(off1) <= DIL_HALF).reshape(-1)
    onehot = ((bucket[:, None] == jnp.arange(LANES)[None, :]) & inside[:, None]).astype(F32)
    vals = jnp.pad(dbias.reshape(ah, -1), ((0, -ah % SUBLANES), (0, 0)))
    return _bucket_sum(vals, onehot, f"t5_grad_d{dil}")[:ah, :T5_BUCKETS].T


def _na_table_rows():
    ro = -np.ones((3, NA_GROUP, NA_WIN), np.int64)
    for i in range(NA_GROUP):
        for j in range(NA_WIN):
            if j < NA_ROWS:
                ro[0, i, j] = j - i + NA_ROWS - 1
            if i <= j < i + NA_ROWS:
                ro[1, i, j] = j - i + NA_ROWS // 2 - 1
            if j >= NA_WIN - NA_ROWS:
                ro[2, i, j] = j - i
    return ro


def _na_bias(rpb):
    ch, nro, nco = rpb.shape
    c = np.arange(GRID_W)
    col_start = np.clip(c - NA_COLS // 2, 0, GRID_W - NA_COLS)
    col_ok = (c[None, :] >= col_start[:, None]) & (c[None, :] < col_start[:, None] + NA_COLS)
    col_idx = np.clip(c[None, :] - c[:, None] + NA_COLS - 1, 0, 2 * NA_COLS - 2).reshape(-1)
    onehot = (np.arange(LANES)[:, None] == col_idx[None, :]).astype(np.float32)
    table = jnp.pad(rpb.astype(F32).reshape(ch * nro, nco), ((0, -(ch * nro) % SUBLANES), (0, LANES - nco)))
    by_row = _expand(table, jnp.asarray(onehot), "rpb_bias", tn=GRID_W * GRID_W)[:ch * nro]
    by_row = jnp.where(jnp.asarray(col_ok.reshape(-1))[None, :], by_row, NEG_INF).reshape(ch, nro, GRID_W, GRID_W)
    neg = jnp.full((ch, GRID_W, GRID_W), NEG_INF, F32)
    tiles = [by_row[:, r] if r >= 0 else neg for r in _na_table_rows().reshape(-1)]
    b = jnp.stack(tiles, axis=1).reshape(ch, 3, NA_GROUP, NA_WIN, GRID_W, GRID_W)
    return jnp.transpose(b, (0, 1, 2, 4, 3, 5)).reshape(ch, 3, NA_GROUP * GRID_W, NA_WIN * GRID_W)


def _na_group(g, rows):
    ngroups = rows // NA_GROUP
    ws = jnp.clip(g * NA_GROUP - NA_ROWS // 2, 0, rows - NA_WIN)
    return pl.multiple_of(ws * GRID_W, GRID_W), jnp.where(g == 0, 0, jnp.where(g == ngroups - 1, 2, 1))


def _na_fwd(qkv, bias, ch, name):
    s = qkv.shape[0]
    rows = s // GRID_W
    assert rows >= NA_WIN and rows % (NA_GROUP * NA_GROUPS_PER_STEP) == 0
    cw = ch * HEAD_DIM
    tg = NA_GROUP * GRID_W
    tq = NA_GROUPS_PER_STEP * tg
    win = NA_WIN * GRID_W
    scale = 1.0 / math.sqrt(HEAD_DIM)

    def body(q_ref, k_ref, v_ref, b_ref, o_ref, lse_ref):
        gb = pl.program_id(1)
        for i in range(NA_GROUPS_PER_STEP):
            st, var = _na_group(gb * NA_GROUPS_PER_STEP + i, rows)
            kw, vw = k_ref[pl.ds(st, win), :], v_ref[pl.ds(st, win), :]
            qs = pl.ds(i * tg, tg)
            sc = _nt(q_ref[qs, :], kw) * scale + b_ref[var]
            m = jnp.max(sc, axis=-1, keepdims=True)
            p = jnp.exp(sc - m)
            l = jnp.sum(p, axis=-1, keepdims=True)
            o_ref[qs, :] = (_nn(p.astype(BF16), vw) / l).astype(BF16)
            lse_ref[qs, :] = jnp.broadcast_to(m + jnp.log(l), (tg, HEAD_DIM))

    ospec = pl.BlockSpec((tq, HEAD_DIM), lambda h, gb: (gb, h))
    return pl.pallas_call(
        body, grid=(ch, s // tq),
        in_specs=[pl.BlockSpec((tq, HEAD_DIM), lambda h, gb: (gb, h)),
                  pl.BlockSpec((s, HEAD_DIM), lambda h, gb: (0, ch + h)),
                  pl.BlockSpec((s, HEAD_DIM), lambda h, gb: (0, 2 * ch + h)),
                  pl.BlockSpec((None, 3, tg, win), lambda h, gb: (h, 0, 0, 0))],
        out_specs=[ospec, ospec],
        out_shape=[jax.ShapeDtypeStruct((s, cw), BF16), jax.ShapeDtypeStruct((s, cw), F32)],
        compiler_params=_cparams(("parallel", "parallel")), name=name)(qkv, qkv, qkv, bias)


def _na_bwd(qkv, o, do, lse, bias, ch, name):
    s = qkv.shape[0]
    rows = s // GRID_W
    cw = ch * HEAD_DIM
    tg = NA_GROUP * GRID_W
    tq = NA_GROUPS_PER_STEP * tg
    win = NA_WIN * GRID_W
    scale = 1.0 / math.sqrt(HEAD_DIM)

    def body(q_ref, k_ref, v_ref, o_ref, do_ref, lse_ref, b_ref, dq_ref, dk_ref, dv_ref, db_ref):
        gb = pl.program_id(1)

        @pl.when(gb == 0)
        def _():
            dk_ref[...] = jnp.zeros_like(dk_ref)
            dv_ref[...] = jnp.zeros_like(dv_ref)
            db_ref[...] = jnp.zeros_like(db_ref)

        for i in range(NA_GROUPS_PER_STEP):
            st, var = _na_group(gb * NA_GROUPS_PER_STEP + i, rows)
            ws = pl.ds(st, win)
            kw, vw = k_ref[ws, :], v_ref[ws, :]
            qs = pl.ds(i * tg, tg)
            q, dov = q_ref[qs, :], do_ref[qs, :]
            sc = _nt(q, kw) * scale + b_ref[var]
            p = jnp.exp(sc - lse_ref[qs, :][:, :1])
            delta = jnp.sum(dov.astype(F32) * o_ref[qs, :].astype(F32), axis=-1, keepdims=True)
            ds = p * (_nt(dov, vw) - delta)
            dsb = ds.astype(BF16)
            dq_ref[qs, :] = (_nn(dsb, kw) * scale).astype(BF16)
            dk_ref[ws, :] += _tn(dsb, q) * scale
            dv_ref[ws, :] += _tn(p.astype(BF16), dov)
            db_ref[var] += ds

    qspec = pl.BlockSpec((tq, HEAD_DIM), lambda h, gb: (gb, h))
    kvspec = pl.BlockSpec((s, HEAD_DIM), lambda h, gb: (0, h))
    bspec = pl.BlockSpec((None, 3, tg, win), lambda h, gb: (h, 0, 0, 0))
    return pl.pallas_call(
        body, grid=(ch, s // tq),
        in_specs=[qspec, pl.BlockSpec((s, HEAD_DIM), lambda h, gb: (0, ch + h)),
                  pl.BlockSpec((s, HEAD_DIM), lambda h, gb: (0, 2 * ch + h)), qspec, qspec, qspec, bspec],
        out_specs=[qspec, kvspec, kvspec, bspec],
        out_shape=[jax.ShapeDtypeStruct((s, cw), BF16), jax.ShapeDtypeStruct((s, cw), F32),
                   jax.ShapeDtypeStruct((s, cw), F32), jax.ShapeDtypeStruct((ch, 3, tg, win), F32)],
        compiler_params=_cparams(("parallel", "arbitrary")), name=name)(qkv, qkv, qkv, o, do, lse, bias)


def _rpb_grad(dbias):
    ch = dbias.shape[0]
    ntile = 3 * NA_GROUP * NA_WIN
    c = np.arange(GRID_W)
    col_idx = (c[None, :] - c[:, None] + NA_COLS - 1).reshape(-1)
    oh_col = (col_idx[:, None] == np.arange(LANES)[None, :]).astype(np.float32)
    d6 = dbias.reshape(ch, 3, NA_GROUP, GRID_W, NA_WIN, GRID_W)
    vals = jnp.transpose(d6, (0, 1, 2, 4, 3, 5)).reshape(ch * ntile, GRID_W * GRID_W)
    by_col = _bucket_sum(vals, jnp.asarray(oh_col), "rpb_grad_cols")
    npad = 2 * LANES
    oh_row = np.zeros((npad, LANES), np.float32)
    for t, r in enumerate(_na_table_rows().reshape(-1)):
        if r >= 0:
            oh_row[t, r] = 1.0
    by_col = jnp.pad(by_col.reshape(ch, ntile, LANES), ((0, 0), (0, npad - ntile), (0, 0)))
    vals2 = jnp.transpose(by_col, (0, 2, 1)).reshape(ch * LANES, npad)
    by_row = _bucket_sum(vals2, jnp.asarray(oh_row), "rpb_grad_rows")
    return jnp.transpose(by_row.reshape(ch, LANES, LANES), (0, 2, 1))[:, :2 * NA_ROWS - 1, :2 * NA_COLS - 1]


def _s5_discretize(lam_re, lam_im, log_step, b_re, b_im):
    step = jnp.exp(log_step.astype(F32))[:, None]
    lr = jnp.minimum(lam_re.astype(F32), -1e-4)
    li = lam_im.astype(F32)
    mag = jnp.exp(lr * step)
    ab_re = mag * jnp.cos(li * step)
    ab_im = mag * jnp.sin(li * step)
    den = lr * lr + li * li
    zr = ((ab_re - 1.0) * lr + ab_im * li) / den
    zi = (ab_im * lr - (ab_re - 1.0) * li) / den
    br = b_re.astype(F32)
    bi = b_im.astype(F32)
    return ab_re, ab_im, zr[..., None] * br - zi[..., None] * bi, zr[..., None] * bi + zi[..., None] * br


def _scan_tables(a_re, a_im, rev):
    ar, ai = a_re.reshape(-1), a_im.reshape(-1)
    pows = [(ar, ai)]
    for _ in range(SUBLANES - 1):
        pr, pi = pows[-1]
        pows.append((pr * ar - pi * ai, pr * ai + pi * ar))
    row = jnp.arange(SUBLANES)[:, None]
    tabs = []
    for k in (1, 2, 4):
        keep = (row < SUBLANES - k) if rev else (row >= k)
        tabs += [jnp.where(keep, pows[k - 1][0][None, :], 0.0), jnp.where(keep, pows[k - 1][1][None, :], 0.0)]
    order = list(range(SUBLANES - 1, -1, -1)) if rev else list(range(SUBLANES))
    tabs += [jnp.stack([pows[i][0] for i in order]), jnp.stack([pows[i][1] for i in order])]
    t = jnp.stack(tabs)
    nblk = t.shape[-1] // (4 * LANES)
    return jnp.transpose(t.reshape(8, SUBLANES, nblk, 4 * LANES), (2, 0, 1, 3))


def _block_diag(w):
    g, a, b = w.shape
    nblk = g // S5_GROUPS_PER_BLOCK
    eye = jnp.eye(S5_GROUPS_PER_BLOCK, dtype=w.dtype)
    w4 = w.reshape(nblk, S5_GROUPS_PER_BLOCK, a, b)
    return (w4[:, :, :, None, :] * eye[None, :, None, :, None]).reshape(nblk, S5_GROUPS_PER_BLOCK * a, S5_GROUPS_PER_BLOCK * b)


def _block_diag_take(w, a, b):
    nblk = w.shape[0]
    w5 = w.reshape(nblk, S5_GROUPS_PER_BLOCK, a, S5_GROUPS_PER_BLOCK, b)
    eye = jnp.eye(S5_GROUPS_PER_BLOCK, dtype=w.dtype)
    return jnp.sum(w5 * eye[None, :, None, :, None], axis=3).reshape(nblk * S5_GROUPS_PER_BLOCK, a, b)


def _scan_tile(r, i, tab_ref, carry, rev):
    for lvl, k in enumerate((1, 2, 4)):
        mr, mi = tab_ref[2 * lvl], tab_ref[2 * lvl + 1]
        sh = SUBLANES - k if rev else k
        rr, ri = pltpu.roll(r, sh, 0), pltpu.roll(i, sh, 0)
        r, i = r + (mr * rr - mi * ri), i + (mr * ri + mi * rr)
    pr, pi = tab_ref[6], tab_ref[7]
    cr, ci = carry
    return r + (pr * cr - pi * ci), i + (pr * ci + pi * cr)


def _s5_fwd(proj, ucol0, tabs, bre, bim, cre, cim, rev, final, name):
    s = proj.shape[0]
    nblk = tabs.shape[0]
    bw = nblk * LANES
    w = 4 * LANES
    t = _tile(s, S5_CHUNK)
    nc, nt = s // t, t // SUBLANES
    ub = ucol0 // LANES
    cm = (lambda c: nc - 1 - c) if rev else (lambda c: c)
    last = 0 if rev else SUBLANES - 1

    def body(u_ref, tab_ref, bre_ref, bim_ref, cre_ref, cim_ref, *rest):
        if final is not None:
            yo_ref, d_ref, y_ref, xr_ref, xi_ref, xr_s, xi_s, car_r, car_i = rest
        else:
            y_ref, xr_ref, xi_ref, xr_s, xi_s, car_r, car_i = rest

        @pl.when(pl.program_id(1) == 0)
        def _():
            car_r[...] = jnp.zeros_like(car_r)
            car_i[...] = jnp.zeros_like(car_i)

        u = u_ref[...]
        xr_s[...] = _nn(u, bre_ref[...])
        xi_s[...] = _nn(u, bim_ref[...])

        def tile(tt, carry):
            k = nt - 1 - tt if rev else tt
            rows = pl.ds(pl.multiple_of(k * SUBLANES, SUBLANES), SUBLANES)
            r, i = _scan_tile(xr_s[rows, :], xi_s[rows, :], tab_ref, carry, rev)
            xr_s[rows, :] = r
            xi_s[rows, :] = i
            return (jnp.broadcast_to(r[last:last + 1, :], r.shape), jnp.broadcast_to(i[last:last + 1, :], i.shape))

        carry = lax.fori_loop(0, nt, tile, (car_r[...], car_i[...]))
        car_r[...], car_i[...] = carry
        xr, xi = xr_s[...].astype(BF16), xi_s[...].astype(BF16)
        y = _nn(xr, cre_ref[...]) - _nn(xi, cim_ref[...])
        if final is not None:
            y = y + yo_ref[...] + d_ref[...] * u.astype(F32)
        y_ref[...] = y
        xr_ref[...] = xr
        xi_ref[...] = xi

    yspec = pl.BlockSpec((t, LANES), lambda j, c: (cm(c), j))
    xspec = pl.BlockSpec((t, w), lambda j, c: (cm(c), j))
    in_specs = [pl.BlockSpec((t, LANES), lambda j, c: (cm(c), ub + j)),
                pl.BlockSpec((None, 8, SUBLANES, w), lambda j, c: (j, 0, 0, 0)),
                pl.BlockSpec((None, LANES, w), lambda j, c: (j, 0, 0)), pl.BlockSpec((None, LANES, w), lambda j, c: (j, 0, 0)),
                pl.BlockSpec((None, w, LANES), lambda j, c: (j, 0, 0)), pl.BlockSpec((None, w, LANES), lambda j, c: (j, 0, 0))]
    args = [proj, tabs, bre, bim, cre, cim]
    if final is not None:
        in_specs += [yspec, pl.BlockSpec((1, LANES), lambda j, c: (0, j))]
        args += [final[0], final[1].reshape(1, bw)]
    return pl.pallas_call(
        body, grid=(nblk, nc), in_specs=in_specs, out_specs=[yspec, xspec, xspec],
        out_shape=[jax.ShapeDtypeStruct((s, bw), F32), jax.ShapeDtypeStruct((s, nblk * w), BF16),
                   jax.ShapeDtypeStruct((s, nblk * w), BF16)],
        scratch_shapes=[pltpu.VMEM((t, w), F32), pltpu.VMEM((t, w), F32), pltpu.VMEM((SUBLANES, w), F32),
                        pltpu.VMEM((SUBLANES, w), F32)],
        compiler_params=_cparams(("parallel", "arbitrary")), name=name)(*args)


def _s5_bwd(proj, ucol0, dy, xr, xi, gtabs, bre, bim, cre, cim, rev, final, name):
    s = proj.shape[0]
    nblk = gtabs.shape[0]
    bw = nblk * LANES
    w = 4 * LANES
    t = _tile(s, S5_CHUNK)
    nc, nt = s // t, t // SUBLANES
    ub = ucol0 // LANES
    grev = not rev
    cm = (lambda c: nc - 1 - c) if grev else (lambda c: c)
    last = 0 if grev else SUBLANES - 1
    nfin = 2 if final is not None else 0

    def body(dy_ref, u_ref, xr_ref, xi_ref, tab_ref, bre_ref, bim_ref, cre_ref, cim_ref, *rest):
        fin, rest = rest[:nfin], rest[nfin:]
        du_ref, dar_ref, dai_ref, dbr_ref, dbi_ref, dcr_ref, dci_ref = rest[:7]
        rest = rest[7:]
        if final is not None:
            dd_ref, rest = rest[0], rest[1:]
        gr_s, gi_s, xr_s, xi_s, car_r, car_i = rest

        @pl.when(pl.program_id(1) == 0)
        def _():
            for ref in (car_r, car_i, dar_ref, dai_ref, dbr_ref, dbi_ref, dcr_ref, dci_ref):
                ref[...] = jnp.zeros_like(ref)
            if final is not None:
                dd_ref[...] = jnp.zeros_like(dd_ref)

        dyv = dy_ref[...]
        dyb = dyv.astype(BF16)
        u = u_ref[...]
        xrb, xib = xr_ref[...], xi_ref[...]
        gr_s[...] = _nt(dyb, cre_ref[...])
        gi_s[...] = -_nt(dyb, cim_ref[...])
        xr_s[...] = xrb.astype(F32)
        xi_s[...] = xib.astype(F32)
        rowid = lax.broadcasted_iota(jnp.int32, (SUBLANES, w), 0)

        def tile(tt, carry):
            k = nt - 1 - tt if grev else tt
            rows = pl.ds(pl.multiple_of(k * SUBLANES, SUBLANES), SUBLANES)
            r, i = _scan_tile(gr_s[rows, :], gi_s[rows, :], tab_ref, carry, grev)
            gr_s[rows, :] = r
            gi_s[rows, :] = i
            if grev:
                er = jnp.where(rowid == SUBLANES - 1, carry[0], pltpu.roll(r, SUBLANES - 1, 0))
                ei = jnp.where(rowid == SUBLANES - 1, carry[1], pltpu.roll(i, SUBLANES - 1, 0))
            else:
                er = jnp.where(rowid == 0, carry[0], pltpu.roll(r, 1, 0))
                ei = jnp.where(rowid == 0, carry[1], pltpu.roll(i, 1, 0))
            sr, si = xr_s[rows, :], xi_s[rows, :]
            dar_ref[...] += er * sr + ei * si
            dai_ref[...] += ei * sr - er * si
            return (jnp.broadcast_to(r[last:last + 1, :], r.shape), jnp.broadcast_to(i[last:last + 1, :], i.shape))

        carry = lax.fori_loop(0, nt, tile, (car_r[...], car_i[...]))
        car_r[...], car_i[...] = carry
        gr, gi = gr_s[...].astype(BF16), gi_s[...].astype(BF16)
        du = _nt(gr, bre_ref[...]) + _nt(gi, bim_ref[...])
        if final is not None:
            du = du + fin[0][...] + fin[1][...] * dyv.astype(F32)
            dd_ref[...] += jnp.sum(dyv.astype(F32) * u.astype(F32), axis=0, keepdims=True)
        du_ref[...] = du
        dbr_ref[...] += _tn(u, gr)
        dbi_ref[...] += _tn(u, gi)
        dcr_ref[...] += _tn(xrb, dyb)
        dci_ref[...] -= _tn(xib, dyb)

    yspec = pl.BlockSpec((t, LANES), lambda j, c: (cm(c), j))
    xspec = pl.BlockSpec((t, w), lambda j, c: (cm(c), j))
    bspec = pl.BlockSpec((None, LANES, w), lambda j, c: (j, 0, 0))
    cspec = pl.BlockSpec((None, w, LANES), lambda j, c: (j, 0, 0))
    aspec = pl.BlockSpec((None, SUBLANES, w), lambda j, c: (j, 0, 0))
    dspec = pl.BlockSpec((1, LANES), lambda j, c: (0, j))
    in_specs = [yspec, pl.BlockSpec((t, LANES), lambda j, c: (cm(c), ub + j)), xspec, xspec,
                pl.BlockSpec((None, 8, SUBLANES, w), lambda j, c: (j, 0, 0, 0)), bspec, bspec, cspec, cspec]
    args = [dy, proj, xr, xi, gtabs, bre, bim, cre, cim]
    out_specs = [yspec, aspec, aspec, bspec, bspec, cspec, cspec]
    out_shape = [jax.ShapeDtypeStruct((s, bw), F32)] + [jax.ShapeDtypeStruct((nblk, SUBLANES, w), F32)] * 2 \
        + [jax.ShapeDtypeStruct((nblk, LANES, w), F32)] * 2 + [jax.ShapeDtypeStruct((nblk, w, LANES), F32)] * 2
    if final is not None:
        in_specs += [yspec, dspec]
        args += [final[0], final[1].reshape(1, bw)]
        out_specs.append(dspec)
        out_shape.append(jax.ShapeDtypeStruct((1, bw), F32))
    return pl.pallas_call(
        body, grid=(nblk, nc), in_specs=in_specs, out_specs=out_specs, out_shape=out_shape,
        scratch_shapes=[pltpu.VMEM((t, w), F32)] * 4 + [pltpu.VMEM((SUBLANES, w), F32)] * 2,
        compiler_params=_cparams(("parallel", "arbitrary")), name=name)(*args)


N_CHIPS = 4


def _place():
    mx, my, mc = lax.axis_index("x"), lax.axis_index("y"), lax.axis_index("c")
    return (mx, my, mc), (mx, my, 1 - mc), [(1 - mx, my), (mx, 1 - my), (1 - mx, 1 - my)]


def _gather_op(x, cols=False):
    r, c = x.shape

    def run(which, ins, outs, sems):
        (x_ref,), (out_ref,), (send_sems, recv_sems, local_sem) = ins, outs, sems
        me, sibling, chips = _place()
        mc = me[2]

        def slot(px, py, pc):
            d = 4 * px + 2 * py + pc
            return out_ref.at[:, pl.ds(pl.multiple_of(d * c, LANES), c)] if cols else out_ref.at[d]

        def copy(k, block, to, src=None):
            return pltpu.make_async_remote_copy(src_ref=slot(*block) if src is None else src, dst_ref=slot(*block),
                                                send_sem=send_sems.at[k], recv_sem=recv_sems.at[k],
                                                device_id=to, device_id_type=MESH)

        def mine():
            return pltpu.make_async_copy(x_ref, slot(*me), local_sem)

        def first():
            return [copy(0, me, sibling, src=x_ref)] + [copy(1 + j, me, (*chip, mc), src=x_ref) for j, chip in enumerate(chips)]

        def passed():
            return [copy(4 + j, (*chip, mc), sibling) for j, chip in enumerate(chips)]

        if which == "start":
            mine().start()
            for cp in first():
                cp.start()
        elif which == "mid":
            for j, (chip, cp) in enumerate(zip(chips, passed())):
                copy(1 + j, (*chip, mc), me).wait_recv()
                cp.start()
        else:
            copy(0, sibling, me).wait_recv()
            for j, chip in enumerate(chips):
                copy(4 + j, (*chip, 1 - mc), me).wait_recv()
            for cp in first() + passed():
                cp.wait_send()
            mine().wait()

    return dict(ins=[x], outs=[jax.ShapeDtypeStruct((r, N_DEV * c) if cols else (N_DEV, r, c), x.dtype)], run=run,
                sems=[pltpu.SemaphoreType.DMA((N_DEV - 1,)), pltpu.SemaphoreType.DMA((N_DEV - 1,)),
                      pltpu.SemaphoreType.DMA])


def _pair_op(gs):
    nt = len(gs)

    def run(which, g_refs, out_refs, sems):
        if which == "mid":
            return
        send_sems, recv_sems = sems
        me, sibling, _ = _place()
        copies = [pltpu.make_async_remote_copy(src_ref=g_refs[t].at[2 * q + (1 - me[2])], dst_ref=out_refs[t].at[q],
                                               send_sem=send_sems.at[t, q], recv_sem=recv_sems.at[t, q],
                                               device_id=sibling, device_id_type=MESH)
                  for t in range(nt) for q in range(N_CHIPS)]
        if which == "start":
            for cp in copies:
                cp.start()
        elif which == "wait":
            for cp in copies:
                cp.wait_recv()
            for cp in copies:
                cp.wait_send()

    return dict(ins=list(gs), outs=[jax.ShapeDtypeStruct((N_CHIPS,) + g.shape[1:], F32) for g in gs], run=run,
                sems=[pltpu.SemaphoreType.DMA((nt, N_CHIPS)), pltpu.SemaphoreType.DMA((nt, N_CHIPS))])


def _pair_add(g, t, core, name):
    _, r, c = g.shape
    tr = r
    while tr * c > 512 * 1024 and tr % 32 == 0:
        tr //= 2

    def body(core_ref, g_ref, t_ref, o_ref):
        o_ref[...] = (g_ref[...] + t_ref[...]).astype(BF16)

    return pl.pallas_call(
        body,
        grid_spec=pltpu.PrefetchScalarGridSpec(
            num_scalar_prefetch=1, grid=(N_CHIPS, r // tr),
            in_specs=[pl.BlockSpec((None, tr, c), lambda q, i, core_ref: (2 * q + core_ref[0], i, 0)),
                      pl.BlockSpec((None, tr, c), lambda q, i, core_ref: (q, i, 0))],
            out_specs=pl.BlockSpec((None, tr, c), lambda q, i, core_ref: (q, i, 0))),
        out_shape=jax.ShapeDtypeStruct((N_CHIPS, r, c), BF16),
        compiler_params=_cparams(("parallel", "parallel")), name=name)(core, g, t)


def _chip_op(ps):
    nt = len(ps)

    def run(which, p_refs, out_refs, sems):
        if which == "mid":
            return
        send_sems, recv_sems, local_sems = sems
        me, _, chips = _place()
        mychip = 2 * me[0] + me[1]
        owns = [pltpu.make_async_copy(p_refs[t].at[mychip], out_refs[t].at[mychip], local_sems.at[t]) for t in range(nt)]
        sends = [pltpu.make_async_remote_copy(src_ref=p_refs[t].at[2 * px + py], dst_ref=out_refs[t].at[mychip],
                                              send_sem=send_sems.at[t, j], recv_sem=recv_sems.at[t, j],
                                              device_id=(px, py, me[2]), device_id_type=MESH)
                 for t in range(nt) for j, (px, py) in enumerate(chips)]
        if which == "start":
            for cp in owns + sends:
                cp.start()
        elif which == "wait":
            for t in range(nt):
                for j, (px, py) in enumerate(chips):
                    pltpu.make_async_remote_copy(src_ref=p_refs[t].at[2 * px + py], dst_ref=out_refs[t].at[2 * px + py],
                                                 send_sem=send_sems.at[t, j], recv_sem=recv_sems.at[t, j],
                                                 device_id=(px, py, me[2]), device_id_type=MESH).wait_recv()
            for cp in sends:
                cp.wait_send()
            for cp in owns:
                cp.wait()

    return dict(ins=list(ps), outs=[jax.ShapeDtypeStruct(p.shape, p.dtype) for p in ps], run=run,
                sems=[pltpu.SemaphoreType.DMA((nt, N_CHIPS - 1)), pltpu.SemaphoreType.DMA((nt, N_CHIPS - 1)),
                      pltpu.SemaphoreType.DMA((nt,))])


def _adamw(w, gstack, m, v, name, layer=None, prev=None):
    r, c = w.shape[-2:]
    nstack = gstack.shape[0]
    tr = r
    while tr * c > 128 * 1024 and tr % 32 == 0:
        tr //= 2
    c1 = 1.0 - ADAM_B1 ** ADAM_STEP
    c2 = 1.0 - ADAM_B2 ** ADAM_STEP

    def body(w_ref, g_ref, m_ref, v_ref, *rest):
        go_ref, d_ref, mo_ref, vo_ref = rest[-4:]
        g = g_ref[0].astype(F32)
        for p in range(1, nstack):
            g = g + g_ref[p].astype(F32)
        mn = ADAM_B1 * m_ref[...] + (1.0 - ADAM_B1) * g
        vn = ADAM_B2 * v_ref[...] + (1.0 - ADAM_B2) * (g * g)
        go_ref[...] = g
        mo_ref[...] = mn
        vo_ref[...] = vn
        d_ref[...] = -ADAM_LR * ((mn / c1) / (jnp.sqrt(vn / c2) + ADAM_EPS) + ADAM_WD * w_ref[...])

    if layer is None:
        spec = pl.BlockSpec((tr, c), lambda i: (i, 0))
        full = (r, c)
    else:
        spec = pl.BlockSpec((None, tr, c), lambda i: (layer, i, 0))
        full = w.shape
    prev = list(prev) if prev is not None else []
    return pl.pallas_call(
        body, grid=(r // tr,),
        in_specs=[spec, pl.BlockSpec((nstack, tr, c), lambda i: (0, i, 0)), spec, spec] + [pl.BlockSpec(memory_space=pl.ANY)] * len(prev),
        out_specs=[spec] * 4, out_shape=[jax.ShapeDtypeStruct(full, F32)] * 4,
        input_output_aliases={4 + n: n for n in range(len(prev))},
        compiler_params=_cparams(("parallel",)), name=name)(w, gstack, m, v, *prev)


def _s5_tables(lam_re, lam_im, log_step, b_re, b_im, c_re, c_im):
    out = []
    for d in range(2):
        ab_re, ab_im, bb_re, bb_im = _s5_discretize(lam_re[d], lam_im[d], log_step[d], b_re, b_im)
        rev = d == 1
        out.append(dict(
            tabs=_scan_tables(ab_re, ab_im, rev), gtabs=_scan_tables(ab_re, -ab_im, not rev),
            bre=_block_diag(jnp.transpose(bb_re, (0, 2, 1))).astype(BF16), bim=_block_diag(jnp.transpose(bb_im, (0, 2, 1))).astype(BF16),
            cre=_block_diag(jnp.transpose(c_re[d], (0, 2, 1))).astype(BF16), cim=_block_diag(jnp.transpose(c_im[d], (0, 2, 1))).astype(BF16)))
    return out


def _layer_tensors(i):
    j = i // 2
    mixer = [("ab_w_in", j), ("ab_w_out", j), ("s5_w_glu", j)] if i % 2 == 0 else [("c_w_qkv", j), ("c_w_out", j)]
    return mixer + [("mlp_w1", i), ("mlp_w2", i)]


class _WeightGather:
    def __init__(self, shards, depth):
        self.shards, self.depth, self.ops = shards, depth, {}

    def _op(self, key):
        self.ops[key] = _gather_op(self.shards[key[0]][key[1]], cols=key[0] not in ROW_SHARDED)
        return self.ops[key]

    def start(self):
        _run_comm([self._op(key) for key in _layer_tensors(0)[:-2]], "gather_mixer0")

    def carry(self, i, slot):
        t = _layer_tensors(i)
        nxt = _layer_tensors(i + 1)[:-2] if i + 1 < self.depth else []
        plan = {"in": t[-2:-1], "up": t[-1:], "down": nxt}
        return [self._op(key) for key in plan[slot]]

    def get(self, name, l):
        full = self.ops[(name, l)]["res"][0]
        return full.reshape(-1, full.shape[-1]) if name in ROW_SHARDED else full


class _GradExchange:
    def __init__(self, core, depth):
        self.core, self.depth, self.g, self.recv_ops, self.pairs, self.ps = core, depth, {}, [], {}, {}

    def put(self, name, l, g):
        self.g[(name, l)] = g.reshape(N_DEV, -1, g.shape[-1])

    def _pair(self, keys):
        op = _pair_op([self.g[key] for key in keys])
        for n, key in enumerate(keys):
            self.pairs[key] = (op, n)
        return [op]

    def _chip(self, keys):
        for key in keys:
            op, n = self.pairs[key]
            self.ps[key] = _pair_add(self.g[key], op["res"][n], self.core, f"pair_add_{key[0]}{key[1]}")
        op = _chip_op([self.ps[key] for key in keys])
        self.recv_ops.append((keys, op))
        return [op]

    def carry(self, i, slot):
        t = _layer_tensors(i)
        later = _layer_tensors(i + 1)[:-2] if i + 1 < self.depth else []
        if slot == "up_bwd":
            return self._pair(t[-2:])
        if slot == "in_grad":
            return self._chip(t[-2:-1])
        if slot == "in_bwd":
            return self._chip(t[-1:])
        if slot == "down_bwd":
            return self._pair(later) if later else ()
        return self._chip(later) if later else ()

    def finish(self):
        keys = _layer_tensors(0)[:-2]
        _run_comm(self._pair(keys), "pair_exchange_mixer0")
        _run_comm(self._chip(keys), "chip_exchange_mixer0")
        return {key: op["res"][n] for keys, op in self.recv_ops for n, key in enumerate(keys)}


def _forward_backward(x, target, p, wsrc, gsink):
    s, d = x.shape
    depth = p["norm_mix"].shape[0]
    ah = p["t5_bias"].shape[1]
    aw = ah * HEAD_DIM
    ch = p["c_rpb"].shape[1]
    groups, pstate = p["s5_lam_re"].shape[2:]
    bw = groups * S5_GROUP
    assert aw + bw == d and ch * HEAD_DIM == d

    dil_bias = [_dil_bias(p["t5_bias"], dil) for _, dil in DILATED_BRANCHES]
    saved = []
    for i in range(depth):
        j = i // 2
        sv = dict(x=x)
        hn = _rms_fwd(x, p["norm_mix"][i], f"norm_mix_fwd{i}")
        sv["hn"] = hn
        if i % 2 == 0:
            proj = _mm_cols(f"ab_in_fwd{i}", hn, wsrc.get("ab_w_in", j), comm=wsrc.carry(i, "in"))
            outs = [(_dil_fwd if dil == 1 else _dil_fwd_strided)(proj, dil_bias[b][0], dil, ah, f"dil_fwd_d{dil}_{i}")
                    for b, (_, dil) in enumerate(DILATED_BRANCHES)]
            o_a, lse = _dil_merge([o for o, _ in outs], [l for _, l in outs])
            tb = _s5_tables(p["s5_lam_re"][j], p["s5_lam_im"][j], p["s5_log_step"][j], p["s5_b_re"][j], p["s5_b_im"][j],
                            p["s5_c_re"][j], p["s5_c_im"][j])
            y0, x0r, x0i = _s5_fwd(proj, 3 * aw, tb[0]["tabs"], tb[0]["bre"], tb[0]["bim"], tb[0]["cre"], tb[0]["cim"],
                                   False, None, f"s5_fwd_a{i}")
            y_pre, x1r, x1i = _s5_fwd(proj, 3 * aw, tb[1]["tabs"], tb[1]["bre"], tb[1]["bim"], tb[1]["cre"], tb[1]["cim"],
                                      True, (y0, p["s5_d"][j]), f"s5_fwd_b{i}")
            o_b = _mm(f"glu_fwd{i}", y_pre, wsrc.get("s5_w_glu", j), a_fn=_gelu, extras=(y_pre,), out_dtypes=(BF16,),
                      epi=lambda acc, yp: (_gelu(yp) * jax.nn.sigmoid(acc),))[0]
            merged = jnp.concatenate([o_a, o_b], axis=1)
            x = _mm(f"ab_out_fwd{i}", merged, wsrc.get("ab_w_out", j), extras=(x,), epi=lambda acc, xr: (acc + xr,))[0]
            sv.update(proj=proj, o_a=o_a, lse=lse, tb=tb, states=((x0r, x0i), (x1r, x1i)), y_pre=y_pre, merged=merged)
        else:
            qkv = _mm_cols(f"c_qkv_fwd{i}", hn, wsrc.get("c_w_qkv", j), comm=wsrc.carry(i, "in"))
            nbias = _na_bias(p["c_rpb"][j])
            o, lse = _na_fwd(qkv, nbias, ch, f"na_fwd{i}")
            x = _mm(f"c_out_fwd{i}", o, wsrc.get("c_w_out", j), extras=(x,), epi=lambda acc, xr: (acc + xr,))[0]
            sv.update(qkv=qkv, o=o, lse=lse, nbias=nbias)
        sv["x_mid"] = x
        hn2 = _rms_fwd(x, p["norm_mlp"][i], f"norm_mlp_fwd{i}")
        h_pre = _mm_cols(f"mlp_up_fwd{i}", hn2, wsrc.get("mlp_w1", i), comm=wsrc.carry(i, "up"))
        x = _mm(f"mlp_down_fwd{i}", h_pre, wsrc.get("mlp_w2", i), a_fn=_relu_sq, extras=(x,), epi=lambda acc, xr: (acc + xr,),
                comm=wsrc.carry(i, "down"))[0]
        sv.update(hn2=hn2, h_pre=h_pre)
        saved.append(sv)

    loss_sum, dx, g_final = _final_loss(x, p["norm_final"], target)

    g = ({k: [None] * p[k].shape[0] for k in ("norm_mix", "norm_mlp", "s5_lam_re", "s5_lam_im", "s5_log_step", "s5_b_re",
                                                  "s5_b_im", "s5_c_re", "s5_c_im", "s5_d", "c_rpb")})
    g_t5 = jnp.zeros_like(p["t5_bias"], dtype=F32)
    for i in reversed(range(depth)):
        j = i // 2
        sv = saved[i]
        dh = _mm(f"mlp_down_bwd{i}", dx, wsrc.get("mlp_w2", i), tb=True, extras=(sv["h_pre"],), out_dtypes=(BF16,),
                 epi=lambda acc, hp: (acc * (2.0 * jnp.maximum(hp.astype(F32), 0.0)),), comm=gsink.carry(i, "down_bwd"))[0]
        gsink.put("mlp_w2", i, _mm(f"mlp_w2_grad{i}", sv["h_pre"], dx, ta=True, a_fn=_relu_sq)[0])
        gsink.put("mlp_w1", i, _mm_cols_grad(f"mlp_w1_grad{i}", sv["hn2"], dh, comm=gsink.carry(i, "w1_grad")))
        dhn2 = _mm_cols_t(f"mlp_up_bwd{i}", dh, wsrc.get("mlp_w1", i), comm=gsink.carry(i, "up_bwd"))
        dx, gn = _rms_bwd(sv["x_mid"], p["norm_mlp"][i], dhn2, dx, f"norm_mlp_bwd{i}")
        g["norm_mlp"][i] = gn[0]
        if i % 2 == 0:
            tb = sv["tb"]
            dmerged = _mm(f"ab_out_bwd{i}", dx, wsrc.get("ab_w_out", j), tb=True, out_dtypes=(BF16,))[0]
            gsink.put("ab_w_out", j, _mm(f"ab_w_out_grad{i}", sv["merged"], dx, ta=True)[0])
            def glu_epi(acc, yp, dob):
                sg = jax.nn.sigmoid(acc)
                dob = dob.astype(F32)
                return dob * _gelu(yp) * sg * (1.0 - sg), dob * sg
            dz, t1 = _mm(f"glu_bwd_z{i}", sv["y_pre"], wsrc.get("s5_w_glu", j), a_fn=_gelu, extras=(sv["y_pre"], dmerged),
                         extra_cols=(0, aw), epi=glu_epi, out_dtypes=(BF16, F32))
            dy_pre = _mm(f"glu_bwd_y{i}", dz, wsrc.get("s5_w_glu", j), tb=True, extras=(t1, sv["y_pre"]),
                         epi=lambda acc, t, yp: ((acc + t) * _gelu_grad(yp),), out_dtypes=(BF16,))[0]
            gsink.put("s5_w_glu", j, _mm(f"glu_w_grad{i}", sv["y_pre"], dz, ta=True, a_fn=_gelu)[0])
            r0 = _s5_bwd(sv["proj"], 3 * aw, dy_pre, *sv["states"][0], tb[0]["gtabs"], tb[0]["bre"], tb[0]["bim"],
                         tb[0]["cre"], tb[0]["cim"], False, None, f"s5_bwd_a{i}")
            r1 = _s5_bwd(sv["proj"], 3 * aw, dy_pre, *sv["states"][1], tb[1]["gtabs"], tb[1]["bre"], tb[1]["bim"],
                         tb[1]["cre"], tb[1]["cim"], True, (r0[0], p["s5_d"][j]), f"s5_bwd_b{i}")
            du = r1[0]
            g["s5_d"][j] = r1[7][0]
            gl_re, gl_im, gls, gb_re, gb_im, gc_re, gc_im = [], [], [], 0.0, 0.0, [], []
            for dnum, rr in enumerate((r0, r1)):
                da_re = jnp.sum(rr[1], axis=1).reshape(groups, pstate)
                da_im = jnp.sum(rr[2], axis=1).reshape(groups, pstate)
                dbb_re = jnp.transpose(_block_diag_take(rr[3], S5_GROUP, pstate), (0, 2, 1))
                dbb_im = jnp.transpose(_block_diag_take(rr[4], S5_GROUP, pstate), (0, 2, 1))
                _, vjp = jax.vjp(_s5_discretize, p["s5_lam_re"][j][dnum], p["s5_lam_im"][j][dnum], p["s5_log_step"][j][dnum],
                                 p["s5_b_re"][j], p["s5_b_im"][j])
                a, b, c, e, f = vjp((da_re, da_im, dbb_re, dbb_im))
                gl_re.append(a)
                gl_im.append(b)
                gls.append(c)
                gb_re, gb_im = gb_re + e, gb_im + f
                gc_re.append(jnp.transpose(_block_diag_take(rr[5], pstate, S5_GROUP), (0, 2, 1)))
                gc_im.append(jnp.transpose(_block_diag_take(rr[6], pstate, S5_GROUP), (0, 2, 1)))
            g["s5_lam_re"][j], g["s5_lam_im"][j], g["s5_log_step"][j] = jnp.stack(gl_re), jnp.stack(gl_im), jnp.stack(gls)
            g["s5_b_re"][j], g["s5_b_im"][j] = gb_re, gb_im
            g["s5_c_re"][j], g["s5_c_im"][j] = jnp.stack(gc_re), jnp.stack(gc_im)
            delta = _head_delta(dmerged, 0, sv["o_a"], f"dil_delta{i}")
            dq = dk = dv = 0.0
            for b, (_, dil) in enumerate(DILATED_BRANCHES):
                q1, k1, v1, db = (_dil_bwd if dil == 1 else _dil_bwd_strided)(
                    sv["proj"], dmerged, sv["lse"], delta, dil_bias[b][0], dil_bias[b][1], dil, ah,
                                          f"dil_bwd_d{dil}_{i}")
                dq, dk, dv = dq + q1, dk + k1, dv + v1
                g_t5 = g_t5 + _t5_grad(db, dil)
            dproj = jnp.concatenate([dq, dk, dv, du], axis=1).astype(BF16)
            gsink.put("ab_w_in", j, _mm_cols_grad(f"ab_w_in_grad{i}", sv["hn"], dproj, comm=gsink.carry(i, "in_grad")))
            dhn = _mm_cols_t(f"ab_in_bwd{i}", dproj, wsrc.get("ab_w_in", j), comm=gsink.carry(i, "in_bwd"))
        else:
            do = _mm(f"c_out_bwd{i}", dx, wsrc.get("c_w_out", j), tb=True, out_dtypes=(BF16,))[0]
            gsink.put("c_w_out", j, _mm(f"c_w_out_grad{i}", sv["o"], dx, ta=True)[0])
            dq, dk, dv, db = _na_bwd(sv["qkv"], sv["o"], do, sv["lse"], sv["nbias"], ch, f"na_bwd{i}")
            g["c_rpb"][j] = _rpb_grad(db)
            dqkv = jnp.concatenate([dq, dk.astype(BF16), dv.astype(BF16)], axis=1)
            gsink.put("c_w_qkv", j, _mm_cols_grad(f"c_w_qkv_grad{i}", sv["hn"], dqkv, comm=gsink.carry(i, "in_grad")))
            dhn = _mm_cols_t(f"c_qkv_bwd{i}", dqkv, wsrc.get("c_w_qkv", j), comm=gsink.carry(i, "in_bwd"))
        dx, gn = _rms_bwd(sv["x"], p["norm_mix"][i], dhn, dx, f"norm_mix_bwd{i}")
        g["norm_mix"][i] = gn[0]
    g["t5_bias"] = g_t5
    g["norm_final"] = g_final[0]
    return loss_sum[0, 0], dx, g


BIG = ("ab_w_in", "ab_w_out", "s5_w_glu", "c_w_qkv", "c_w_out", "mlp_w1", "mlp_w2")
ROW_SHARDED = ("ab_w_out", "s5_w_glu", "c_w_out", "mlp_w2")
WEIGHTS = ("t5_bias", "ab_w_in", "ab_w_out", "s5_lam_re", "s5_lam_im", "s5_log_step", "s5_b_re", "s5_b_im", "s5_c_re",
           "s5_c_im", "s5_d", "s5_w_glu", "c_w_qkv", "c_w_out", "c_rpb", "norm_mix", "norm_mlp", "mlp_w1", "mlp_w2",
           "norm_final")


def _step(x, target, w, m, v):
    d = x.shape[-1]
    depth = w["norm_mix"].shape[0]
    wsrc = _WeightGather({k: w[k].astype(BF16) for k in BIG}, depth)
    wsrc.start()
    gsink = _GradExchange(lax.axis_index("c").astype(jnp.int32).reshape(1), depth)
    small = {k: w[k] for k in WEIGHTS if k not in BIG}
    loss_sum, dx, g = _forward_backward(x[0], target[0], small, wsrc, gsink)
    loss = lax.psum(0.5 * loss_sum / d, ("x", "y", "c"))

    out = {}
    recv = gsink.finish()
    for k in BIG:
        res = None
        for l in range(w[k].shape[0]):
            res = _adamw(w[k], recv[(k, l)], m[k], v[k], f"adamw_{k}{l}", layer=l, prev=res)
        out[k] = res
    names = [k for k in WEIGHTS if k not in BIG]
    def flat(tree):
        return jnp.concatenate([jnp.asarray(jnp.stack(tree[k]) if isinstance(tree[k], list) else tree[k], F32).reshape(-1)
                                for k in names])
    total = sum(int(np.prod(w[k].shape)) for k in names)
    rows = -(-total // LANES)
    rows = -(-rows // SUBLANES) * SUBLANES
    pad = rows * LANES - total
    def pack(tree):
        return jnp.pad(flat(tree), (0, pad)).reshape(rows, LANES)
    small_op = _gather_op(pack(g))
    _run_comm([small_op], "gather_small_grads")
    res = _adamw(pack(w), small_op["res"][0], pack(m), pack(v), "adamw_small")
    off = 0
    for k in names:
        n = int(np.prod(w[k].shape))
        out[k] = [a.reshape(-1)[off:off + n].reshape(w[k].shape) for a in res]
        off += n
    return (loss, dx[None], *[out[k][0] for k in WEIGHTS], *[out[k][1] for k in WEIGHTS],
            *[out[k][2] for k in WEIGHTS], *[out[k][3] for k in WEIGHTS])


def kernel(x, t5_bias, ab_w_in, ab_w_out, s5_lam_re, s5_lam_im, s5_log_step, s5_b_re, s5_b_im, s5_c_re, s5_c_im, s5_d, s5_w_glu, c_w_qkv, c_w_out, c_rpb, norm_mix, norm_mlp, mlp_w1, mlp_w2, norm_final, loss_target, m_t5_bias, m_ab_w_in, m_ab_w_out, m_s5_lam_re, m_s5_lam_im, m_s5_log_step, m_s5_b_re, m_s5_b_im, m_s5_c_re, m_s5_c_im, m_s5_d, m_s5_w_glu, m_c_w_qkv, m_c_w_out, m_c_rpb, m_norm_mix, m_norm_mlp, m_mlp_w1, m_mlp_w2, m_norm_final, v_t5_bias, v_ab_w_in, v_ab_w_out, v_s5_lam_re, v_s5_lam_im, v_s5_log_step, v_s5_b_re, v_s5_b_im, v_s5_c_re, v_s5_c_im, v_s5_d, v_s5_w_glu, v_c_w_qkv, v_c_w_out, v_c_rpb, v_norm_mix, v_norm_mlp, v_mlp_w1, v_mlp_w2, v_norm_final):
    w = dict(t5_bias=t5_bias, ab_w_in=ab_w_in, ab_w_out=ab_w_out, s5_lam_re=s5_lam_re, s5_lam_im=s5_lam_im,
             s5_log_step=s5_log_step, s5_b_re=s5_b_re, s5_b_im=s5_b_im, s5_c_re=s5_c_re, s5_c_im=s5_c_im, s5_d=s5_d,
             s5_w_glu=s5_w_glu, c_w_qkv=c_w_qkv, c_w_out=c_w_out, c_rpb=c_rpb, norm_mix=norm_mix, norm_mlp=norm_mlp,
             mlp_w1=mlp_w1, mlp_w2=mlp_w2, norm_final=norm_final)
    m = dict(t5_bias=m_t5_bias, ab_w_in=m_ab_w_in, ab_w_out=m_ab_w_out, s5_lam_re=m_s5_lam_re, s5_lam_im=m_s5_lam_im,
             s5_log_step=m_s5_log_step, s5_b_re=m_s5_b_re, s5_b_im=m_s5_b_im, s5_c_re=m_s5_c_re, s5_c_im=m_s5_c_im,
             s5_d=m_s5_d, s5_w_glu=m_s5_w_glu, c_w_qkv=m_c_w_qkv, c_w_out=m_c_w_out, c_rpb=m_c_rpb, norm_mix=m_norm_mix,
             norm_mlp=m_norm_mlp, mlp_w1=m_mlp_w1, mlp_w2=m_mlp_w2, norm_final=m_norm_final)
    v = dict(t5_bias=v_t5_bias, ab_w_in=v_ab_w_in, ab_w_out=v_ab_w_out, s5_lam_re=v_s5_lam_re, s5_lam_im=v_s5_lam_im,
             s5_log_step=v_s5_log_step, s5_b_re=v_s5_b_re, s5_b_im=v_s5_b_im, s5_c_re=v_s5_c_re, s5_c_im=v_s5_c_im,
             s5_d=v_s5_d, s5_w_glu=v_s5_w_glu, c_w_qkv=v_c_w_qkv, c_w_out=v_c_w_out, c_rpb=v_c_rpb, norm_mix=v_norm_mix,
             norm_mlp=v_norm_mlp, mlp_w1=v_mlp_w1, mlp_w2=v_mlp_w2, norm_final=v_norm_final)
    return _step(x, loss_target, w, m, v)
```

```python
import math

import jax
import jax.numpy as jnp
import numpy as np
from jax import lax
from jax.experimental import pallas as pl
from jax.experimental.pallas import tpu as pltpu

F32 = jnp.float32
BF16 = jnp.bfloat16

N_DEV = 8
HEAD_DIM = 128
LANES = 128
QBLOCK = 128
DIL_HALF = 64
DILATED_BRANCHES = ((128, 1), (512, 4), (2048, 16))
S5_GROUP = 16
S5_GROUPS_PER_BLOCK = LANES // S5_GROUP
S5_CHUNK = 512
SUBLANES = 8
GRID_W = 64
NA_ROWS = 8
NA_COLS = 16
NA_GROUP = 4
NA_WIN = NA_GROUP + NA_ROWS - 1
NA_GROUPS_PER_STEP = 4
T5_BUCKETS = 32
T5_MAX_DISTANCE = 1024
RMS_EPS = 1e-6
NEG_INF = -1e30
ADAM_LR = 0.001
ADAM_B1 = 0.9
ADAM_B2 = 0.999
ADAM_EPS = 1e-08
ADAM_WD = 0.01
ADAM_STEP = 10
VMEM_LIMIT_BYTES = 56 * 1024 * 1024
MESH = pl.DeviceIdType.MESH


def _cparams(sem=None):
    return pltpu.CompilerParams(dimension_semantics=sem, vmem_limit_bytes=VMEM_LIMIT_BYTES)


def _tile(dim, pref):
    t = min(dim, pref)
    while dim % t and t > LANES:
        t -= LANES
    assert dim % t == 0, (dim, pref)
    return t


def _dot(a, b, ca, cb):
    return lax.dot_general(a, b, (((ca,), (cb,)), ((), ())), preferred_element_type=F32)


def _nn(a, b):
    return _dot(a, b, 1, 0)


def _nt(a, b):
    return _dot(a, b, 1, 1)


def _tn(a, b):
    return _dot(a, b, 0, 0)


HBM_SPEC = pl.BlockSpec(memory_space=pltpu.HBM)


def _split_comm_refs(comm, in_refs, out_refs, sem_refs):
    parts, i, o, s = [], 0, 0, 0
    for op in comm:
        ni, no, ns = len(op["ins"]), len(op["outs"]), len(op["sems"])
        parts.append((in_refs[i:i + ni], out_refs[o:o + no], sem_refs[s:s + ns]))
        i, o, s = i + ni, o + no, s + ns
    return parts


def _mm_call(name, a, b, a_spec, b_spec, grid, nk, out_shapes, out_specs, acc_shape,
             ta=False, tb=False, a_fn=None, epi=None, extras=(), extra_specs=(), comm=()):
    ne, no = len(extras), len(out_shapes)
    comm_ins = [x for op in comm for x in op["ins"]]
    comm_outs = [x for op in comm for x in op["outs"]]
    comm_sems = [x for op in comm for x in op["sems"]]
    nci, nco = len(comm_ins), len(comm_outs)
    total = grid[0] * grid[1] * grid[2]

    def body(a_ref, b_ref, *rest):
        ex, rest = rest[:ne], rest[ne:]
        cin, rest = rest[:nci], rest[nci:]
        outs, rest = rest[:no], rest[no:]
        cout, rest = rest[:nco], rest[nco:]
        acc, csem = rest[0], rest[1:]
        k = pl.program_id(2)
        step = (pl.program_id(0) * grid[1] + pl.program_id(1)) * grid[2] + k
        parts = _split_comm_refs(comm, cin, cout, csem)

        def phase(which, at):
            if comm:
                @pl.when(step == at)
                def _():
                    for op, refs in zip(comm, parts):
                        op["run"](which, *refs)

        phase("start", 0)
        phase("mid", total - 1 - total // 8)

        @pl.when(k == 0)
        def _():
            acc[...] = jnp.zeros_like(acc)

        av = a_ref[...]
        if a_fn is not None:
            av = a_fn(av)
        acc[...] += _dot(av.astype(BF16), b_ref[...].astype(BF16), 0 if ta else 1, 1 if tb else 0)

        @pl.when(k == nk - 1)
        def _():
            r = acc[...]
            res = epi(r, *[e[...] for e in ex]) if epi is not None else (r,)
            for o, v in zip(outs, res):
                o[...] = v.astype(o.dtype)

        phase("wait", total - 1)

    res = pl.pallas_call(
        body, grid=grid, in_specs=[a_spec, b_spec, *extra_specs] + [HBM_SPEC] * nci,
        out_specs=list(out_specs) + [HBM_SPEC] * nco, out_shape=list(out_shapes) + comm_outs,
        scratch_shapes=[pltpu.VMEM(acc_shape, F32)] + comm_sems,
        compiler_params=_cparams(("arbitrary",) * 3 if comm else ("parallel", "parallel", "arbitrary")), name=name,
    )(a, b, *extras, *comm_ins)
    o = no
    for op in comm:
        op["res"] = res[o:o + len(op["outs"])]
        o += len(op["outs"])
    return res[:no]


def _run_comm(comm, name):
    comm_ins = [x for op in comm for x in op["ins"]]
    comm_outs = [x for op in comm for x in op["outs"]]
    comm_sems = [x for op in comm for x in op["sems"]]
    nci, nco = len(comm_ins), len(comm_outs)

    def body(*refs):
        parts = _split_comm_refs(comm, refs[:nci], refs[nci:nci + nco], refs[nci + nco:])
        for which in ("start", "mid", "wait"):
            for op, r in zip(comm, parts):
                op["run"](which, *r)

    res = pl.pallas_call(
        body, in_specs=[HBM_SPEC] * nci, out_specs=[HBM_SPEC] * nco, out_shape=comm_outs, scratch_shapes=comm_sems,
        compiler_params=pltpu.CompilerParams(has_side_effects=True), name=name)(*comm_ins)
    o = 0
    for op in comm:
        op["res"] = res[o:o + len(op["outs"])]
        o += len(op["outs"])


def _mm(name, a, b, *, ta=False, tb=False, a_fn=None, epi=None, extras=(), extra_cols=None,
        out_dtypes=(F32,), tm=1024, tn=1024, tk=2048, comm=()):
    m, kdim = (a.shape[1], a.shape[0]) if ta else a.shape
    n = b.shape[0] if tb else b.shape[1]
    assert (b.shape[1] if tb else b.shape[0]) == kdim, (a.shape, b.shape)
    tm, tn, tk = _tile(m, tm), _tile(n, tn), _tile(kdim, tk)
    a_spec = pl.BlockSpec((tk, tm), lambda i, j, k: (k, i)) if ta else pl.BlockSpec((tm, tk), lambda i, j, k: (i, k))
    b_spec = pl.BlockSpec((tn, tk), lambda i, j, k: (j, k)) if tb else pl.BlockSpec((tk, tn), lambda i, j, k: (k, j))
    o_spec = pl.BlockSpec((tm, tn), lambda i, j, k: (i, j))
    extra_cols = extra_cols or (0,) * len(extras)
    especs = []
    for c0 in extra_cols:
        assert c0 % tn == 0
        cb = c0 // tn
        especs.append(pl.BlockSpec((tm, tn), lambda i, j, k, cb=cb: (i, cb + j)))
    return _mm_call(name, a, b, a_spec, b_spec, (m // tm, n // tn, kdim // tk), kdim // tk,
                    [jax.ShapeDtypeStruct((m, n), d) for d in out_dtypes], [o_spec] * len(out_dtypes), (tm, tn),
                    ta=ta, tb=tb, a_fn=a_fn, epi=epi, extras=extras, extra_specs=especs, comm=comm)


def _mm_cols(name, a, w, *, comm=()):
    return _mm(name, a, w, out_dtypes=(BF16,), comm=comm)[0]


def _mm_cols_t(name, a, w, *, comm=()):
    return _mm(name, a, w, tb=True, out_dtypes=(BF16,), comm=comm)[0]


def _mm_cols_grad(name, a, dy, *, tm=1024, tk=2048, comm=()):
    s, kout = a.shape
    n = dy.shape[1] // N_DEV
    tm, tk = _tile(kout, tm), _tile(s, tk)
    return _mm_call(name, a, dy, pl.BlockSpec((tk, tm), lambda i, j, k: (k, i)),
                    pl.BlockSpec((tk, n), lambda i, j, k: (k, j)),
                    (kout // tm, N_DEV, s // tk), s // tk,
                    [jax.ShapeDtypeStruct((N_DEV, kout, n), F32)], [pl.BlockSpec((None, tm, n), lambda i, j, k: (j, i, 0))],
                    (tm, n), ta=True, comm=comm)[0]


_GELU_C = math.sqrt(2.0 / math.pi)


def _gelu(x):
    return 0.5 * x * (1.0 + jnp.tanh(_GELU_C * (x + 0.044715 * x * x * x)))


def _gelu_grad(x):
    t = jnp.tanh(_GELU_C * (x + 0.044715 * x * x * x))
    return 0.5 * (1.0 + t) + 0.5 * x * (1.0 - t * t) * _GELU_C * (1.0 + 3.0 * 0.044715 * x * x)


def _relu_sq(x):
    r = jnp.maximum(x.astype(F32), 0.0)
    return r * r


def _rms_fwd(x, g, name):
    s, d = x.shape
    tr = _tile(s, 512)

    def body(x_ref, g_ref, o_ref):
        xv = x_ref[...]
        r = lax.rsqrt(jnp.mean(xv * xv, axis=-1, keepdims=True) + RMS_EPS)
        o_ref[...] = (xv * r * g_ref[...]).astype(BF16)

    return pl.pallas_call(
        body, grid=(s // tr,),
        in_specs=[pl.BlockSpec((tr, d), lambda i: (i, 0)), pl.BlockSpec((1, d), lambda i: (0, 0))],
        out_specs=pl.BlockSpec((tr, d), lambda i: (i, 0)), out_shape=jax.ShapeDtypeStruct((s, d), BF16),
        compiler_params=_cparams(("parallel",)), name=name)(x, g.reshape(1, d))


def _rms_bwd(x, g, dy, dres, name):
    s, d = x.shape
    tr = _tile(s, 512)

    def body(x_ref, g_ref, dy_ref, dres_ref, dx_ref, dg_ref):
        @pl.when(pl.program_id(0) == 0)
        def _():
            dg_ref[...] = jnp.zeros_like(dg_ref)

        xv = x_ref[...]
        dyv = dy_ref[...].astype(F32)
        r = lax.rsqrt(jnp.mean(xv * xv, axis=-1, keepdims=True) + RMS_EPS)
        xh = xv * r
        gdy = dyv * g_ref[...]
        dx_ref[...] = dres_ref[...] + r * (gdy - xh * jnp.mean(gdy * xh, axis=-1, keepdims=True))
        dg_ref[...] += jnp.sum(dyv * xh, axis=0, keepdims=True)

    return pl.pallas_call(
        body, grid=(s // tr,),
        in_specs=[pl.BlockSpec((tr, d), lambda i: (i, 0)), pl.BlockSpec((1, d), lambda i: (0, 0)),
                  pl.BlockSpec((tr, d), lambda i: (i, 0)), pl.BlockSpec((tr, d), lambda i: (i, 0))],
        out_specs=[pl.BlockSpec((tr, d), lambda i: (i, 0)), pl.BlockSpec((1, d), lambda i: (0, 0))],
        out_shape=[jax.ShapeDtypeStruct((s, d), F32), jax.ShapeDtypeStruct((1, d), F32)],
        compiler_params=_cparams(("arbitrary",)), name=name)(x, g.reshape(1, d), dy, dres)


def _final_loss(x, g, target):
    s, d = x.shape
    tr = _tile(s, 512)

    def body(x_ref, g_ref, t_ref, loss_ref, dx_ref, dg_ref):
        @pl.when(pl.program_id(0) == 0)
        def _():
            dg_ref[...] = jnp.zeros_like(dg_ref)
            loss_ref[...] = jnp.zeros_like(loss_ref)

        xv = x_ref[...]
        gv = g_ref[...]
        r = lax.rsqrt(jnp.mean(xv * xv, axis=-1, keepdims=True) + RMS_EPS)
        xh = xv * r
        err = xh * gv - t_ref[...]
        loss_ref[...] += jnp.sum(jnp.sum(err * err, axis=-1, keepdims=True), axis=0, keepdims=True)
        dyv = err * (1.0 / d)
        gdy = dyv * gv
        dx_ref[...] = r * (gdy - xh * jnp.mean(gdy * xh, axis=-1, keepdims=True))
        dg_ref[...] += jnp.sum(dyv * xh, axis=0, keepdims=True)

    return pl.pallas_call(
        body, grid=(s // tr,),
        in_specs=[pl.BlockSpec((tr, d), lambda i: (i, 0)), pl.BlockSpec((1, d), lambda i: (0, 0)),
                  pl.BlockSpec((tr, d), lambda i: (i, 0))],
        out_specs=[pl.BlockSpec((1, 1), lambda i: (0, 0)), pl.BlockSpec((tr, d), lambda i: (i, 0)),
                   pl.BlockSpec((1, d), lambda i: (0, 0))],
        out_shape=[jax.ShapeDtypeStruct((1, 1), F32), jax.ShapeDtypeStruct((s, d), F32),
                   jax.ShapeDtypeStruct((1, d), F32)],
        compiler_params=_cparams(("arbitrary",)), name="final_norm_loss")(x, g.reshape(1, d), target)


def _t5_bucket(rel):
    half = T5_BUCKETS // 2
    max_exact = half // 2
    n = jnp.abs(rel)
    nf = jnp.maximum(n, 1).astype(F32)
    large = max_exact + (jnp.log(nf / max_exact) / math.log(T5_MAX_DISTANCE / max_exact)
                         * (half - max_exact)).astype(jnp.int32)
    large = jnp.minimum(large, half - 1)
    return jnp.where(rel > 0, half, 0) + jnp.where(n < max_exact, n, large)


def _dil_offsets():
    i = jnp.arange(QBLOCK)[:, None]
    kk = jnp.arange(2 * QBLOCK)[None, :]
    return kk - DIL_HALF - i, (jnp.arange(QBLOCK)[None, :] + DIL_HALF) - jnp.arange(2 * QBLOCK)[:, None]


def _expand(table, onehot, name, tn=8192):
    r, n = table.shape[0], onehot.shape[1]
    tn = _tile(n, tn)

    def body(t_ref, oh_ref, o_ref):
        o_ref[...] = lax.dot_general(t_ref[...], oh_ref[...], (((1,), (0,)), ((), ())),
                                     precision=lax.Precision.HIGHEST, preferred_element_type=F32)

    return pl.pallas_call(
        body, grid=(n // tn,),
        in_specs=[pl.BlockSpec((r, LANES), lambda i: (0, 0)), pl.BlockSpec((LANES, tn), lambda i: (0, i))],
        out_specs=pl.BlockSpec((r, tn), lambda i: (0, i)), out_shape=jax.ShapeDtypeStruct((r, n), F32),
        compiler_params=_cparams(("parallel",)), name=name)(table, onehot)


def _pad_rows_lanes(t):
    r, c = t.shape
    return jnp.pad(t.astype(F32), ((0, -r % SUBLANES), (0, LANES - c)))


def _dil_bias(t5, dil):
    ah = t5.shape[1]
    off1, off2 = _dil_offsets()
    bucket = jnp.concatenate([_t5_bucket(off1 * dil).reshape(-1), _t5_bucket(off2 * dil).reshape(-1)])
    onehot = (jnp.arange(LANES)[:, None] == bucket[None, :]).astype(F32)
    b = _expand(_pad_rows_lanes(t5.T), onehot, f"t5_bias_d{dil}")[:ah]
    n1 = QBLOCK * 2 * QBLOCK
    return b[:, :n1].reshape(ah, QBLOCK, 2 * QBLOCK), b[:, n1:].reshape(ah, 2 * QBLOCK, QBLOCK)


def _window(p, c, n, cols=slice(None)):
    return jnp.concatenate([p[pl.ds(DIL_HALF, DIL_HALF), cols], c[:, cols], n[pl.ds(0, DIL_HALF), cols]], axis=0)


def _dil_specs(width, ncol_blocks, col_block, nb):
    def spec(dn):
        return pl.BlockSpec((QBLOCK, width), lambda r, n: (jnp.clip(n + dn, 0, nb - 1), r * ncol_blocks + col_block))
    return [spec(-1), spec(0), spec(1)]


def _dil_fwd(proj, bias1, dil, ah, name):
    s, wtot = proj.shape
    ln = s // dil
    nb = ln // QBLOCK
    assert nb * QBLOCK * dil == s
    aw = ah * HEAD_DIM
    wb = wtot // aw
    scale = 1.0 / math.sqrt(HEAD_DIM)
    pv = proj.reshape(ln, dil * wtot)

    def body(q_ref, kp, kc, kn, vp, vc, vn, b_ref, o_ref, lse_ref):
        n = pl.program_id(1)
        ii = lax.broadcasted_iota(jnp.int32, (QBLOCK, 2 * QBLOCK), 0)
        jj = lax.broadcasted_iota(jnp.int32, (QBLOCK, 2 * QBLOCK), 1)
        kpos = n * QBLOCK + jj - DIL_HALF
        valid = (jnp.abs(jj - DIL_HALF - ii) <= DIL_HALF) & (kpos >= 0) & (kpos < ln)
        for h in range(ah):
            cs = pl.ds(h * HEAD_DIM, HEAD_DIM)
            kw, vw = _window(kp, kc, kn, cs), _window(vp, vc, vn, cs)
            sc = jnp.where(valid, _nt(q_ref[:, cs], kw) * scale + b_ref[h], NEG_INF)
            m = jnp.max(sc, axis=-1, keepdims=True)
            p = jnp.exp(sc - m)
            l = jnp.sum(p, axis=-1, keepdims=True)
            o_ref[:, cs] = (_nn(p.astype(BF16), vw) / l).astype(BF16)
            lse_ref[:, cs] = jnp.broadcast_to(m + jnp.log(l), (QBLOCK, HEAD_DIM))

    ospec = pl.BlockSpec((QBLOCK, aw), lambda r, n: (n, r))
    o, lse = pl.pallas_call(
        body, grid=(dil, nb),
        in_specs=[_dil_specs(aw, wb, 0, nb)[1], *_dil_specs(aw, wb, 1, nb), *_dil_specs(aw, wb, 2, nb),
                  pl.BlockSpec((ah, QBLOCK, 2 * QBLOCK), lambda r, n: (0, 0, 0))],
        out_specs=[ospec, ospec],
        out_shape=[jax.ShapeDtypeStruct((ln, dil * aw), BF16), jax.ShapeDtypeStruct((ln, dil * aw), F32)],
        compiler_params=_cparams(("parallel", "parallel")), name=name,
    )(pv, pv, pv, pv, pv, pv, pv, bias1)
    return o.reshape(s, aw), lse.reshape(s, aw)


def _dil_merge(outs, lses):
    s, aw = outs[0].shape
    tr = _tile(s, 512)

    def body(o1, l1, o2, l2, o3, l3, o_ref, lse_ref):
        a, b, c = l1[...], l2[...], l3[...]
        m = jnp.maximum(jnp.maximum(a, b), c)
        w1, w2, w3 = jnp.exp(a - m), jnp.exp(b - m), jnp.exp(c - m)
        tot = w1 + w2 + w3
        o_ref[...] = ((w1 * o1[...].astype(F32) + w2 * o2[...].astype(F32) + w3 * o3[...].astype(F32)) / tot).astype(BF16)
        lse_ref[...] = m + jnp.log(tot)

    spec = pl.BlockSpec((tr, aw), lambda i: (i, 0))
    return pl.pallas_call(
        body, grid=(s // tr,), in_specs=[spec] * 6, out_specs=[spec, spec],
        out_shape=[jax.ShapeDtypeStruct((s, aw), BF16), jax.ShapeDtypeStruct((s, aw), F32)],
        compiler_params=_cparams(("parallel",)), name="dil_merge",
    )(outs[0], lses[0], outs[1], lses[1], outs[2], lses[2])


def _head_delta(do, do_col0, o, name):
    s, w = o.shape
    tr = _tile(s, 512)
    cb = do_col0 // HEAD_DIM

    def body(do_ref, o_ref, d_ref):
        d = jnp.sum(do_ref[...].astype(F32) * o_ref[...].astype(F32), axis=-1, keepdims=True)
        d_ref[...] = jnp.broadcast_to(d, d_ref.shape)

    return pl.pallas_call(
        body, grid=(s // tr, w // HEAD_DIM),
        in_specs=[pl.BlockSpec((tr, HEAD_DIM), lambda i, h: (i, cb + h)), pl.BlockSpec((tr, HEAD_DIM), lambda i, h: (i, h))],
        out_specs=pl.BlockSpec((tr, HEAD_DIM), lambda i, h: (i, h)), out_shape=jax.ShapeDtypeStruct((s, w), F32),
        compiler_params=_cparams(("parallel", "parallel")), name=name)(do, o)


def _dil_bwd(proj, dmerged, lse, delta, bias1, bias2, dil, ah, name):
    s, wtot = proj.shape
    ln = s // dil
    nb = ln // QBLOCK
    aw = ah * HEAD_DIM
    wb = wtot // aw
    wd = dmerged.shape[1] // aw
    scale = 1.0 / math.sqrt(HEAD_DIM)
    pv = proj.reshape(ln, dil * wtot)
    dov = dmerged.reshape(ln, dil * dmerged.shape[1])
    lv = lse.reshape(ln, dil * aw)
    dlv = delta.reshape(ln, dil * aw)

    def body(qp, qc, qn, kp, kc, kn, vp, vc, vn, dop, doc, don, lp, lc, lnx, dp, dc, dn, b1_ref, b2_ref,
             dq_ref, dk_ref, dv_ref, db_ref):
        n = pl.program_id(1)

        @pl.when((pl.program_id(0) == 0) & (n == 0))
        def _():
            db_ref[...] = jnp.zeros_like(db_ref)

        ii = lax.broadcasted_iota(jnp.int32, (QBLOCK, 2 * QBLOCK), 0)
        jj = lax.broadcasted_iota(jnp.int32, (QBLOCK, 2 * QBLOCK), 1)
        kpos = n * QBLOCK + jj - DIL_HALF
        valid = (jnp.abs(jj - DIL_HALF - ii) <= DIL_HALF) & (kpos >= 0) & (kpos < ln)
        ww = lax.broadcasted_iota(jnp.int32, (2 * QBLOCK, QBLOCK), 0)
        cc = lax.broadcasted_iota(jnp.int32, (2 * QBLOCK, QBLOCK), 1)
        qpos = n * QBLOCK - DIL_HALF + ww
        valid2 = (jnp.abs(cc + DIL_HALF - ww) <= DIL_HALF) & (qpos >= 0) & (qpos < ln)
        for h in range(ah):
            cs = pl.ds(h * HEAD_DIM, HEAD_DIM)
            kw, vw = _window(kp, kc, kn, cs), _window(vp, vc, vn, cs)
            sc = _nt(qc[:, cs], kw) * scale + b1_ref[h]
            lse2 = jnp.concatenate([lc[:, cs], lc[:, cs]], axis=1)
            p = jnp.where(valid, jnp.exp(jnp.where(valid, sc - lse2, 0.0)), 0.0)
            ds = p * (_nt(doc[:, cs], vw) - jnp.concatenate([dc[:, cs], dc[:, cs]], axis=1))
            dq_ref[:, cs] = _nn(ds.astype(BF16), kw) * scale
            db_ref[h] += ds
            qw, dow = _window(qp, qc, qn, cs), _window(dop, doc, don, cs)
            sc2 = _nt(qw, kc[:, cs]) * scale + b2_ref[h]
            p2 = jnp.where(valid2, jnp.exp(jnp.where(valid2, sc2 - _window(lp, lc, lnx, cs), 0.0)), 0.0)
            dv_ref[:, cs] = _tn(p2.astype(BF16), dow)
            ds2 = p2 * (_nt(dow, vc[:, cs]) - _window(dp, dc, dn, cs))
            dk_ref[:, cs] = _tn(ds2.astype(BF16), qw) * scale

    ospec = pl.BlockSpec((QBLOCK, aw), lambda r, n: (n, r))
    dq, dk, dv, db = pl.pallas_call(
        body, grid=(dil, nb),
        in_specs=[*_dil_specs(aw, wb, 0, nb), *_dil_specs(aw, wb, 1, nb), *_dil_specs(aw, wb, 2, nb),
                  *_dil_specs(aw, wd, 0, nb), *_dil_specs(aw, 1, 0, nb), *_dil_specs(aw, 1, 0, nb),
                  pl.BlockSpec((ah, QBLOCK, 2 * QBLOCK), lambda r, n: (0, 0, 0)),
                  pl.BlockSpec((ah, 2 * QBLOCK, QBLOCK), lambda r, n: (0, 0, 0))],
        out_specs=[ospec, ospec, ospec, pl.BlockSpec((ah, QBLOCK, 2 * QBLOCK), lambda r, n: (0, 0, 0))],
        out_shape=[jax.ShapeDtypeStruct((ln, dil * aw), F32)] * 3 + [jax.ShapeDtypeStruct((ah, QBLOCK, 2 * QBLOCK), F32)],
        compiler_params=_cparams(("arbitrary", "arbitrary")), name=name,
    )(pv, pv, pv, pv, pv, pv, pv, pv, pv, dov, dov, dov, lv, lv, lv, dlv, dlv, dlv, bias1, bias2)
    return dq.reshape(s, aw), dk.reshape(s, aw), dv.reshape(s, aw), db


def _strided_specs(rows, col_block, nsb):
    half = rows // 2
    return [pl.BlockSpec((half, HEAD_DIM), lambda h, n: (jnp.clip(2 * n - 1, 0, 2 * nsb - 1), col_block + h)),
            pl.BlockSpec((rows, HEAD_DIM), lambda h, n: (n, col_block + h)),
            pl.BlockSpec((half, HEAD_DIM), lambda h, n: (jnp.clip(2 * n + 2, 0, 2 * nsb - 1), col_block + h))]


def _fill_window(dst, p, c, n):
    half, rows = p.shape[0], c.shape[0]
    dst[pl.ds(0, half), :] = p[...].astype(F32)
    dst[pl.ds(half, rows), :] = c[...].astype(F32)
    dst[pl.ds(half + rows, half), :] = n[...].astype(F32)


def _dil_masks(n, ln):
    ii = lax.broadcasted_iota(jnp.int32, (QBLOCK, 2 * QBLOCK), 0)
    jj = lax.broadcasted_iota(jnp.int32, (QBLOCK, 2 * QBLOCK), 1)
    kpos = n * QBLOCK + jj - DIL_HALF
    valid = (jnp.abs(jj - DIL_HALF - ii) <= DIL_HALF) & (kpos >= 0) & (kpos < ln)
    ww = lax.broadcasted_iota(jnp.int32, (2 * QBLOCK, QBLOCK), 0)
    cc = lax.broadcasted_iota(jnp.int32, (2 * QBLOCK, QBLOCK), 1)
    qpos = n * QBLOCK - DIL_HALF + ww
    valid2 = (jnp.abs(cc + DIL_HALF - ww) <= DIL_HALF) & (qpos >= 0) & (qpos < ln)
    return valid, valid2


def _dil_fwd_strided(proj, bias1, dil, ah, name):
    s, wtot = proj.shape
    ln = s // dil
    nsb = ln // QBLOCK
    assert nsb * QBLOCK * dil == s
    aw = ah * HEAD_DIM
    rows = QBLOCK * dil
    half = rows // 2
    scale = 1.0 / math.sqrt(HEAD_DIM)

    def body(q_ref, kp, kc, kn, vp, vc, vn, b_ref, o_ref, lse_ref, qf, kf, vf, of):
        valid, _ = _dil_masks(pl.program_id(1), ln)
        qf[...] = q_ref[...].astype(F32)
        _fill_window(kf, kp, kc, kn)
        _fill_window(vf, vp, vc, vn)
        bias = b_ref[...]

        def residue(r, carry):
            qs = pl.ds(r, QBLOCK, stride=dil)
            ws = pl.ds(r, 2 * QBLOCK, stride=dil)
            kw, vw = kf[ws, :].astype(BF16), vf[ws, :].astype(BF16)
            sc = jnp.where(valid, _nt(qf[qs, :].astype(BF16), kw) * scale + bias, NEG_INF)
            m = jnp.max(sc, axis=-1, keepdims=True)
            p = jnp.exp(sc - m)
            l = jnp.sum(p, axis=-1, keepdims=True)
            of[qs, :] = _nn(p.astype(BF16), vw) / l
            lse_ref[qs, :] = jnp.broadcast_to(m + jnp.log(l), (QBLOCK, HEAD_DIM))
            return carry

        lax.fori_loop(0, dil, residue, 0, unroll=4)
        o_ref[...] = of[...].astype(BF16)

    ospec = pl.BlockSpec((rows, HEAD_DIM), lambda h, n: (n, h))
    return pl.pallas_call(
        body, grid=(ah, nsb),
        in_specs=[_strided_specs(rows, 0, nsb)[1], *_strided_specs(rows, ah, nsb), *_strided_specs(rows, 2 * ah, nsb),
                  pl.BlockSpec((None, QBLOCK, 2 * QBLOCK), lambda h, n: (h, 0, 0))],
        out_specs=[ospec, ospec],
        out_shape=[jax.ShapeDtypeStruct((s, aw), BF16), jax.ShapeDtypeStruct((s, aw), F32)],
        scratch_shapes=[pltpu.VMEM((rows, HEAD_DIM), F32), pltpu.VMEM((2 * rows, HEAD_DIM), F32),
                        pltpu.VMEM((2 * rows, HEAD_DIM), F32), pltpu.VMEM((rows, HEAD_DIM), F32)],
        compiler_params=_cparams(("parallel", "parallel")), name=name,
    )(proj, proj, proj, proj, proj, proj, proj, bias1)


def _dil_bwd_strided(proj, dmerged, lse, delta, bias1, bias2, dil, ah, name, acc, out_dtype):
    s, wtot = proj.shape
    ln = s // dil
    nsb = ln // QBLOCK
    aw = ah * HEAD_DIM
    rows = QBLOCK * dil
    half = rows // 2
    scale = 1.0 / math.sqrt(HEAD_DIM)
    center = slice(DIL_HALF, DIL_HALF + QBLOCK)

    def body(qp, qc, qn, kp, kc, kn, vp, vc, vn, dop, doc, don, lp, lc, lnx, dp, dc, dn, b1_ref, b2_ref, aq, ak, av,
             dq_ref, dk_ref, dv_ref, db_ref, qf, kf, vf, dof, dq_s, dk_s, dv_s):
        n = pl.program_id(1)

        @pl.when(n == 0)
        def _():
            db_ref[...] = jnp.zeros_like(db_ref)

        valid, valid2 = _dil_masks(n, ln)
        _fill_window(qf, qp, qc, qn)
        _fill_window(kf, kp, kc, kn)
        _fill_window(vf, vp, vc, vn)
        _fill_window(dof, dop, doc, don)
        b1, b2 = b1_ref[...], b2_ref[...]

        def stat_window(p, c, nx, r):
            return jnp.concatenate([p[pl.ds(r, DIL_HALF, stride=dil), :], c[pl.ds(r, QBLOCK, stride=dil), :],
                                    nx[pl.ds(r, DIL_HALF, stride=dil), :]], axis=0)

        def residue(r, carry):
            ws = pl.ds(r, 2 * QBLOCK, stride=dil)
            os = pl.ds(r, QBLOCK, stride=dil)
            qw, kw, vw, dow = (t[ws, :].astype(BF16) for t in (qf, kf, vf, dof))
            q, k, v, do = qw[center], kw[center], vw[center], dow[center]
            lse_w, delta_w = stat_window(lp, lc, lnx, r), stat_window(dp, dc, dn, r)
            lse_c, delta_c = lse_w[center], delta_w[center]
            sc = _nt(q, kw) * scale + b1
            p = jnp.where(valid, jnp.exp(jnp.where(valid, sc - jnp.concatenate([lse_c, lse_c], axis=1), 0.0)), 0.0)
            ds = p * (_nt(do, vw) - jnp.concatenate([delta_c, delta_c], axis=1))
            dq_s[os, :] = _nn(ds.astype(BF16), kw) * scale
            db_ref[...] += ds
            sc2 = _nt(qw, k) * scale + b2
            p2 = jnp.where(valid2, jnp.exp(jnp.where(valid2, sc2 - lse_w, 0.0)), 0.0)
            dv_s[os, :] = _tn(p2.astype(BF16), dow)
            ds2 = p2 * (_nt(dow, v) - delta_w)
            dk_s[os, :] = _tn(ds2.astype(BF16), qw) * scale
            return carry

        lax.fori_loop(0, dil, residue, 0, unroll=4)
        dq_ref[...] = (dq_s[...] + aq[...]).astype(dq_ref.dtype)
        dk_ref[...] = (dk_s[...] + ak[...]).astype(dk_ref.dtype)
        dv_ref[...] = (dv_s[...] + av[...]).astype(dv_ref.dtype)

    ospec = pl.BlockSpec((rows, HEAD_DIM), lambda h, n: (n, h))
    win = pltpu.VMEM((2 * rows, HEAD_DIM), F32)
    blk = pltpu.VMEM((rows, HEAD_DIM), F32)
    return pl.pallas_call(
        body, grid=(ah, nsb),
        in_specs=[*_strided_specs(rows, 0, nsb), *_strided_specs(rows, ah, nsb), *_strided_specs(rows, 2 * ah, nsb),
                  *_strided_specs(rows, 0, nsb), *_strided_specs(rows, 0, nsb), *_strided_specs(rows, 0, nsb),
                  pl.BlockSpec((None, QBLOCK, 2 * QBLOCK), lambda h, n: (h, 0, 0)),
                  pl.BlockSpec((None, 2 * QBLOCK, QBLOCK), lambda h, n: (h, 0, 0)), ospec, ospec, ospec],
        out_specs=[ospec, ospec, ospec, pl.BlockSpec((None, QBLOCK, 2 * QBLOCK), lambda h, n: (h, 0, 0))],
        out_shape=[jax.ShapeDtypeStruct((s, aw), out_dtype)] * 3 + [jax.ShapeDtypeStruct((ah, QBLOCK, 2 * QBLOCK), F32)],
        scratch_shapes=[win, win, win, win, blk, blk, blk],
        compiler_params=_cparams(("parallel", "arbitrary")), name=name,
    )(proj, proj, proj, proj, proj, proj, proj, proj, proj, dmerged, dmerged, dmerged, lse, lse, lse,
      delta, delta, delta, bias1, bias2, *acc)


def _bucket_sum(vals, onehot, name):
    r, n = vals.shape
    b = onehot.shape[1]

    def body(v_ref, oh_ref, o_ref):
        o_ref[...] = lax.dot_general(v_ref[...], oh_ref[...], (((1,), (0,)), ((), ())),
                                     precision=lax.Precision.HIGHEST, preferred_element_type=F32)

    tr = max(t for t in range(SUBLANES, 257, SUBLANES) if r % t == 0)
    return pl.pallas_call(
        body, grid=(r // tr,),
        in_specs=[pl.BlockSpec((tr, n), lambda i: (i, 0)), pl.BlockSpec((n, b), lambda i: (0, 0))],
        out_specs=pl.BlockSpec((tr, b), lambda i: (i, 0)), out_shape=jax.ShapeDtypeStruct((r, b), F32),
        compiler_params=_cparams(("parallel",)), name=name)(vals, onehot)


def _t5_grad(dbias, dil):
    ah = dbias.shape[0]
    off1, _ = _dil_offsets()
    bucket = _t5_bucket(off1 * dil).reshape(-1)
    inside = (jnp.abs(off1) <= DIL_HALF).reshape(-1)
    onehot = ((bucket[:, None] == jnp.arange(LANES)[None, :]) & inside[:, None]).astype(F32)
    vals = jnp.pad(dbias.reshape(ah, -1), ((0, -ah % SUBLANES), (0, 0)))
    return _bucket_sum(vals, onehot, f"t5_grad_d{dil}")[:ah, :T5_BUCKETS].T


def _na_table_rows():
    ro = -np.ones((3, NA_GROUP, NA_WIN), np.int64)
    for i in range(NA_GROUP):
        for j in range(NA_WIN):
            if j < NA_ROWS:
                ro[0, i, j] = j - i + NA_ROWS - 1
            if i <= j < i + NA_ROWS:
                ro[1, i, j] = j - i + NA_ROWS // 2 - 1
            if j >= NA_WIN - NA_ROWS:
                ro[2, i, j] = j - i
    return ro


def _na_bias(rpb):
    ch, nro, nco = rpb.shape
    c = np.arange(GRID_W)
    col_start = np.clip(c - NA_COLS // 2, 0, GRID_W - NA_COLS)
    col_ok = (c[None, :] >= col_start[:, None]) & (c[None, :] < col_start[:, None] + NA_COLS)
    col_idx = np.clip(c[None, :] - c[:, None] + NA_COLS - 1, 0, 2 * NA_COLS - 2).reshape(-1)
    onehot = (np.arange(LANES)[:, None] == col_idx[None, :]).astype(np.float32)
    table = jnp.pad(rpb.astype(F32).reshape(ch * nro, nco), ((0, -(ch * nro) % SUBLANES), (0, LANES - nco)))
    by_row = _expand(table, jnp.asarray(onehot), "rpb_bias", tn=GRID_W * GRID_W)[:ch * nro]
    by_row = jnp.where(jnp.asarray(col_ok.reshape(-1))[None, :], by_row, NEG_INF).reshape(ch, nro, GRID_W, GRID_W)
    neg = jnp.full((ch, GRID_W, GRID_W), NEG_INF, F32)
    tiles = [by_row[:, r] if r >= 0 else neg for r in _na_table_rows().reshape(-1)]
    b = jnp.stack(tiles, axis=1).reshape(ch, 3, NA_GROUP, NA_WIN, GRID_W, GRID_W)
    return jnp.transpose(b, (0, 1, 2, 4, 3, 5)).reshape(ch, 3, NA_GROUP * GRID_W, NA_WIN * GRID_W)


def _na_group(g, rows):
    ngroups = rows // NA_GROUP
    ws = jnp.clip(g * NA_GROUP - NA_ROWS // 2, 0, rows - NA_WIN)
    return pl.multiple_of(ws * GRID_W, GRID_W), jnp.where(g == 0, 0, jnp.where(g == ngroups - 1, 2, 1))


def _na_fwd(qkv, bias, ch, name):
    s = qkv.shape[0]
    rows = s // GRID_W
    assert rows >= NA_WIN and rows % (NA_GROUP * NA_GROUPS_PER_STEP) == 0
    cw = ch * HEAD_DIM
    tg = NA_GROUP * GRID_W
    tq = NA_GROUPS_PER_STEP * tg
    win = NA_WIN * GRID_W
    scale = 1.0 / math.sqrt(HEAD_DIM)

    def body(q_ref, k_ref, v_ref, b_ref, o_ref, lse_ref):
        gb = pl.program_id(1)
        for i in range(NA_GROUPS_PER_STEP):
            st, var = _na_group(gb * NA_GROUPS_PER_STEP + i, rows)
            kw, vw = k_ref[pl.ds(st, win), :], v_ref[pl.ds(st, win), :]
            qs = pl.ds(i * tg, tg)
            sc = _nt(q_ref[qs, :], kw) * scale + b_ref[var]
            m = jnp.max(sc, axis=-1, keepdims=True)
            p = jnp.exp(sc - m)
            l = jnp.sum(p, axis=-1, keepdims=True)
            o_ref[qs, :] = (_nn(p.astype(BF16), vw) / l).astype(BF16)
            lse_ref[qs, :] = jnp.broadcast_to(m + jnp.log(l), (tg, HEAD_DIM))

    ospec = pl.BlockSpec((tq, HEAD_DIM), lambda h, gb: (gb, h))
    return pl.pallas_call(
        body, grid=(ch, s // tq),
        in_specs=[pl.BlockSpec((tq, HEAD_DIM), lambda h, gb: (gb, h)),
                  pl.BlockSpec((s, HEAD_DIM), lambda h, gb: (0, ch + h)),
                  pl.BlockSpec((s, HEAD_DIM), lambda h, gb: (0, 2 * ch + h)),
                  pl.BlockSpec((None, 3, tg, win), lambda h, gb: (h, 0, 0, 0))],
        out_specs=[ospec, ospec],
        out_shape=[jax.ShapeDtypeStruct((s, cw), BF16), jax.ShapeDtypeStruct((s, cw), F32)],
        compiler_params=_cparams(("parallel", "parallel")), name=name)(qkv, qkv, qkv, bias)


def _na_bwd(qkv, o, do, lse, bias, ch, name):
    s = qkv.shape[0]
    rows = s // GRID_W
    cw = ch * HEAD_DIM
    tg = NA_GROUP * GRID_W
    tq = NA_GROUPS_PER_STEP * tg
    win = NA_WIN * GRID_W
    scale = 1.0 / math.sqrt(HEAD_DIM)

    def body(q_ref, k_ref, v_ref, o_ref, do_ref, lse_ref, b_ref, dq_ref, dk_ref, dv_ref, db_ref):
        gb = pl.program_id(1)

        @pl.when(gb == 0)
        def _():
            dk_ref[...] = jnp.zeros_like(dk_ref)
            dv_ref[...] = jnp.zeros_like(dv_ref)
            db_ref[...] = jnp.zeros_like(db_ref)

        for i in range(NA_GROUPS_PER_STEP):
            st, var = _na_group(gb * NA_GROUPS_PER_STEP + i, rows)
            ws = pl.ds(st, win)
            kw, vw = k_ref[ws, :], v_ref[ws, :]
            qs = pl.ds(i * tg, tg)
            q, dov = q_ref[qs, :], do_ref[qs, :]
            sc = _nt(q, kw) * scale + b_ref[var]
            p = jnp.exp(sc - lse_ref[qs, :][:, :1])
            delta = jnp.sum(dov.astype(F32) * o_ref[qs, :].astype(F32), axis=-1, keepdims=True)
            ds = p * (_nt(dov, vw) - delta)
            dsb = ds.astype(BF16)
            dq_ref[qs, :] = (_nn(dsb, kw) * scale).astype(BF16)
            dk_ref[ws, :] += _tn(dsb, q) * scale
            dv_ref[ws, :] += _tn(p.astype(BF16), dov)
            db_ref[var] += ds

    qspec = pl.BlockSpec((tq, HEAD_DIM), lambda h, gb: (gb, h))
    kvspec = pl.BlockSpec((s, HEAD_DIM), lambda h, gb: (0, h))
    bspec = pl.BlockSpec((None, 3, tg, win), lambda h, gb: (h, 0, 0, 0))
    return pl.pallas_call(
        body, grid=(ch, s // tq),
        in_specs=[qspec, pl.BlockSpec((s, HEAD_DIM), lambda h, gb: (0, ch + h)),
                  pl.BlockSpec((s, HEAD_DIM), lambda h, gb: (0, 2 * ch + h)), qspec, qspec, qspec, bspec],
        out_specs=[qspec, kvspec, kvspec, bspec],
        out_shape=[jax.ShapeDtypeStruct((s, cw), BF16), jax.ShapeDtypeStruct((s, cw), F32),
                   jax.ShapeDtypeStruct((s, cw), F32), jax.ShapeDtypeStruct((ch, 3, tg, win), F32)],
        compiler_params=_cparams(("parallel", "arbitrary")), name=name)(qkv, qkv, qkv, o, do, lse, bias)


def _rpb_grad(dbias):
    ch = dbias.shape[0]
    ntile = 3 * NA_GROUP * NA_WIN
    c = np.arange(GRID_W)
    col_idx = (c[None, :] - c[:, None] + NA_COLS - 1).reshape(-1)
    oh_col = (col_idx[:, None] == np.arange(LANES)[None, :]).astype(np.float32)
    d6 = dbias.reshape(ch, 3, NA_GROUP, GRID_W, NA_WIN, GRID_W)
    vals = jnp.transpose(d6, (0, 1, 2, 4, 3, 5)).reshape(ch * ntile, GRID_W * GRID_W)
    by_col = _bucket_sum(vals, jnp.asarray(oh_col), "rpb_grad_cols")
    npad = 2 * LANES
    oh_row = np.zeros((npad, LANES), np.float32)
    for t, r in enumerate(_na_table_rows().reshape(-1)):
        if r >= 0:
            oh_row[t, r] = 1.0
    by_col = jnp.pad(by_col.reshape(ch, ntile, LANES), ((0, 0), (0, npad - ntile), (0, 0)))
    vals2 = jnp.transpose(by_col, (0, 2, 1)).reshape(ch * LANES, npad)
    by_row = _bucket_sum(vals2, jnp.asarray(oh_row), "rpb_grad_rows")
    return jnp.transpose(by_row.reshape(ch, LANES, LANES), (0, 2, 1))[:, :2 * NA_ROWS - 1, :2 * NA_COLS - 1]


def _s5_discretize(lam_re, lam_im, log_step, b_re, b_im):
    step = jnp.exp(log_step.astype(F32))[:, None]
    lr = jnp.minimum(lam_re.astype(F32), -1e-4)
    li = lam_im.astype(F32)
    mag = jnp.exp(lr * step)
    ab_re = mag * jnp.cos(li * step)
    ab_im = mag * jnp.sin(li * step)
    den = lr * lr + li * li
    zr = ((ab_re - 1.0) * lr + ab_im * li) / den
    zi = (ab_im * lr - (ab_re - 1.0) * li) / den
    br = b_re.astype(F32)
    bi = b_im.astype(F32)
    return ab_re, ab_im, zr[..., None] * br - zi[..., None] * bi, zr[..., None] * bi + zi[..., None] * br


def _scan_tables(a_re, a_im, rev):
    ar, ai = a_re.reshape(-1), a_im.reshape(-1)
    pows = [(ar, ai)]
    for _ in range(SUBLANES - 1):
        pr, pi = pows[-1]
        pows.append((pr * ar - pi * ai, pr * ai + pi * ar))
    row = jnp.arange(SUBLANES)[:, None]
    tabs = []
    for k in (1, 2, 4):
        keep = (row < SUBLANES - k) if rev else (row >= k)
        tabs += [jnp.where(keep, pows[k - 1][0][None, :], 0.0), jnp.where(keep, pows[k - 1][1][None, :], 0.0)]
    order = list(range(SUBLANES - 1, -1, -1)) if rev else list(range(SUBLANES))
    tabs += [jnp.stack([pows[i][0] for i in order]), jnp.stack([pows[i][1] for i in order])]
    t = jnp.stack(tabs)
    nblk = t.shape[-1] // (4 * LANES)
    return jnp.transpose(t.reshape(8, SUBLANES, nblk, 4 * LANES), (2, 0, 1, 3))


def _block_diag(w):
    g, a, b = w.shape
    nblk = g // S5_GROUPS_PER_BLOCK
    eye = jnp.eye(S5_GROUPS_PER_BLOCK, dtype=w.dtype)
    w4 = w.reshape(nblk, S5_GROUPS_PER_BLOCK, a, b)
    return (w4[:, :, :, None, :] * eye[None, :, None, :, None]).reshape(nblk, S5_GROUPS_PER_BLOCK * a, S5_GROUPS_PER_BLOCK * b)


def _block_diag_take(w, a, b):
    nblk = w.shape[0]
    w5 = w.reshape(nblk, S5_GROUPS_PER_BLOCK, a, S5_GROUPS_PER_BLOCK, b)
    eye = jnp.eye(S5_GROUPS_PER_BLOCK, dtype=w.dtype)
    return jnp.sum(w5 * eye[None, :, None, :, None], axis=3).reshape(nblk * S5_GROUPS_PER_BLOCK, a, b)


def _scan_tile(r, i, tab_ref, carry, rev):
    for lvl, k in enumerate((1, 2, 4)):
        mr, mi = tab_ref[2 * lvl], tab_ref[2 * lvl + 1]
        sh = SUBLANES - k if rev else k
        rr, ri = pltpu.roll(r, sh, 0), pltpu.roll(i, sh, 0)
        r, i = r + (mr * rr - mi * ri), i + (mr * ri + mi * rr)
    pr, pi = tab_ref[6], tab_ref[7]
    cr, ci = carry
    return r + (pr * cr - pi * ci), i + (pr * ci + pi * cr)


def _s5_fwd(proj, ucol0, tabs, bre, bim, cre, cim, rev, final, name):
    s = proj.shape[0]
    nblk = tabs.shape[0]
    bw = nblk * LANES
    w = 4 * LANES
    t = _tile(s, S5_CHUNK)
    nc, nt = s // t, t // SUBLANES
    ub = ucol0 // LANES
    cm = (lambda c: nc - 1 - c) if rev else (lambda c: c)
    last = 0 if rev else SUBLANES - 1

    def body(u_ref, tab_ref, bre_ref, bim_ref, cre_ref, cim_ref, *rest):
        if final is not None:
            yo_ref, d_ref, y_ref, xr_ref, xi_ref, xr_s, xi_s, car_r, car_i = rest
        else:
            y_ref, xr_ref, xi_ref, xr_s, xi_s, car_r, car_i = rest

        @pl.when(pl.program_id(1) == 0)
        def _():
            car_r[...] = jnp.zeros_like(car_r)
            car_i[...] = jnp.zeros_like(car_i)

        u = u_ref[...]
        xr_s[...] = _nn(u, bre_ref[...])
        xi_s[...] = _nn(u, bim_ref[...])

        def tile(tt, carry):
            k = nt - 1 - tt if rev else tt
            rows = pl.ds(pl.multiple_of(k * SUBLANES, SUBLANES), SUBLANES)
            r, i = _scan_tile(xr_s[rows, :], xi_s[rows, :], tab_ref, carry, rev)
            xr_s[rows, :] = r
            xi_s[rows, :] = i
            return (jnp.broadcast_to(r[last:last + 1, :], r.shape), jnp.broadcast_to(i[last:last + 1, :], i.shape))

        carry = lax.fori_loop(0, nt, tile, (car_r[...], car_i[...]))
        car_r[...], car_i[...] = carry
        xr, xi = xr_s[...].astype(BF16), xi_s[...].astype(BF16)
        y = _nn(xr, cre_ref[...]) - _nn(xi, cim_ref[...])
        if final is not None:
            y = y + yo_ref[...] + d_ref[...] * u.astype(F32)
        y_ref[...] = y
        xr_ref[...] = xr
        xi_ref[...] = xi

    yspec = pl.BlockSpec((t, LANES), lambda j, c: (cm(c), j))
    xspec = pl.BlockSpec((t, w), lambda j, c: (cm(c), j))
    in_specs = [pl.BlockSpec((t, LANES), lambda j, c: (cm(c), ub + j)),
                pl.BlockSpec((None, 8, SUBLANES, w), lambda j, c: (j, 0, 0, 0)),
                pl.BlockSpec((None, LANES, w), lambda j, c: (j, 0, 0)), pl.BlockSpec((None, LANES, w), lambda j, c: (j, 0, 0)),
                pl.BlockSpec((None, w, LANES), lambda j, c: (j, 0, 0)), pl.BlockSpec((None, w, LANES), lambda j, c: (j, 0, 0))]
    args = [proj, tabs, bre, bim, cre, cim]
    if final is not None:
        in_specs += [yspec, pl.BlockSpec((1, LANES), lambda j, c: (0, j))]
        args += [final[0], final[1].reshape(1, bw)]
    return pl.pallas_call(
        body, grid=(nblk, nc), in_specs=in_specs, out_specs=[yspec, xspec, xspec],
        out_shape=[jax.ShapeDtypeStruct((s, bw), F32), jax.ShapeDtypeStruct((s, nblk * w), BF16),
                   jax.ShapeDtypeStruct((s, nblk * w), BF16)],
        scratch_shapes=[pltpu.VMEM((t, w), F32), pltpu.VMEM((t, w), F32), pltpu.VMEM((SUBLANES, w), F32),
                        pltpu.VMEM((SUBLANES, w), F32)],
        compiler_params=_cparams(("parallel", "arbitrary")), name=name)(*args)


def _s5_bwd(proj, ucol0, dy, xr, xi, gtabs, bre, bim, cre, cim, rev, final, name):
    s = proj.shape[0]
    nblk = gtabs.shape[0]
    bw = nblk * LANES
    w = 4 * LANES
    t = _tile(s, S5_CHUNK)
    nc, nt = s // t, t // SUBLANES
    ub = ucol0 // LANES
    grev = not rev
    cm = (lambda c: nc - 1 - c) if grev else (lambda c: c)
    last = 0 if grev else SUBLANES - 1
    nfin = 2 if final is not None else 0

    def body(dy_ref, u_ref, xr_ref, xi_ref, tab_ref, bre_ref, bim_ref, cre_ref, cim_ref, *rest):
        fin, rest = rest[:nfin], rest[nfin:]
        du_ref, dar_ref, dai_ref, dbr_ref, dbi_ref, dcr_ref, dci_ref = rest[:7]
        rest = rest[7:]
        if final is not None:
            dd_ref, rest = rest[0], rest[1:]
        gr_s, gi_s, xr_s, xi_s, car_r, car_i = rest

        @pl.when(pl.program_id(1) == 0)
        def _():
            for ref in (car_r, car_i, dar_ref, dai_ref, dbr_ref, dbi_ref, dcr_ref, dci_ref):
                ref[...] = jnp.zeros_like(ref)
            if final is not None:
                dd_ref[...] = jnp.zeros_like(dd_ref)

        dyv = dy_ref[...]
        dyb = dyv.astype(BF16)
        u = u_ref[...]
        xrb, xib = xr_ref[...], xi_ref[...]
        gr_s[...] = _nt(dyb, cre_ref[...])
        gi_s[...] = -_nt(dyb, cim_ref[...])
        xr_s[...] = xrb.astype(F32)
        xi_s[...] = xib.astype(F32)
        rowid = lax.broadcasted_iota(jnp.int32, (SUBLANES, w), 0)

        def tile(tt, carry):
            k = nt - 1 - tt if grev else tt
            rows = pl.ds(pl.multiple_of(k * SUBLANES, SUBLANES), SUBLANES)
            r, i = _scan_tile(gr_s[rows, :], gi_s[rows, :], tab_ref, carry, grev)
            gr_s[rows, :] = r
            gi_s[rows, :] = i
            if grev:
                er = jnp.where(rowid == SUBLANES - 1, carry[0], pltpu.roll(r, SUBLANES - 1, 0))
                ei = jnp.where(rowid == SUBLANES - 1, carry[1], pltpu.roll(i, SUBLANES - 1, 0))
            else:
                er = jnp.where(rowid == 0, carry[0], pltpu.roll(r, 1, 0))
                ei = jnp.where(rowid == 0, carry[1], pltpu.roll(i, 1, 0))
            sr, si = xr_s[rows, :], xi_s[rows, :]
            dar_ref[...] += er * sr + ei * si
            dai_ref[...] += ei * sr - er * si
            return (jnp.broadcast_to(r[last:last + 1, :], r.shape), jnp.broadcast_to(i[last:last + 1, :], i.shape))

        carry = lax.fori_loop(0, nt, tile, (car_r[...], car_i[...]))
        car_r[...], car_i[...] = carry
        gr, gi = gr_s[...].astype(BF16), gi_s[...].astype(BF16)
        du = _nt(gr, bre_ref[...]) + _nt(gi, bim_ref[...])
        if final is not None:
            du = du + fin[0][...] + fin[1][...] * dyv.astype(F32)
            dd_ref[...] += jnp.sum(dyv.astype(F32) * u.astype(F32), axis=0, keepdims=True)
        du_ref[...] = du.astype(du_ref.dtype)
        dbr_ref[...] += _tn(u, gr)
        dbi_ref[...] += _tn(u, gi)
        dcr_ref[...] += _tn(xrb, dyb)
        dci_ref[...] -= _tn(xib, dyb)

    yspec = pl.BlockSpec((t, LANES), lambda j, c: (cm(c), j))
    xspec = pl.BlockSpec((t, w), lambda j, c: (cm(c), j))
    bspec = pl.BlockSpec((None, LANES, w), lambda j, c: (j, 0, 0))
    cspec = pl.BlockSpec((None, w, LANES), lambda j, c: (j, 0, 0))
    aspec = pl.BlockSpec((None, SUBLANES, w), lambda j, c: (j, 0, 0))
    dspec = pl.BlockSpec((1, LANES), lambda j, c: (0, j))
    in_specs = [yspec, pl.BlockSpec((t, LANES), lambda j, c: (cm(c), ub + j)), xspec, xspec,
                pl.BlockSpec((None, 8, SUBLANES, w), lambda j, c: (j, 0, 0, 0)), bspec, bspec, cspec, cspec]
    args = [dy, proj, xr, xi, gtabs, bre, bim, cre, cim]
    out_specs = [yspec, aspec, aspec, bspec, bspec, cspec, cspec]
    du_dtype = BF16 if final is not None else F32
    out_shape = [jax.ShapeDtypeStruct((s, bw), du_dtype)] + [jax.ShapeDtypeStruct((nblk, SUBLANES, w), F32)] * 2 \
        + [jax.ShapeDtypeStruct((nblk, LANES, w), F32)] * 2 + [jax.ShapeDtypeStruct((nblk, w, LANES), F32)] * 2
    if final is not None:
        in_specs += [yspec, dspec]
        args += [final[0], final[1].reshape(1, bw)]
        out_specs.append(dspec)
        out_shape.append(jax.ShapeDtypeStruct((1, bw), F32))
    return pl.pallas_call(
        body, grid=(nblk, nc), in_specs=in_specs, out_specs=out_specs, out_shape=out_shape,
        scratch_shapes=[pltpu.VMEM((t, w), F32)] * 4 + [pltpu.VMEM((SUBLANES, w), F32)] * 2,
        compiler_params=_cparams(("parallel", "arbitrary")), name=name)(*args)


N_CHIPS = 4


def _place():
    mx, my, mc = lax.axis_index("x"), lax.axis_index("y"), lax.axis_index("c")
    return (mx, my, mc), (mx, my, 1 - mc), [(1 - mx, my), (mx, 1 - my), (1 - mx, 1 - my)]


def _gather_op(x, cols=False):
    r, c = x.shape

    def run(which, ins, outs, sems):
        (x_ref,), (out_ref,), (send_sems, recv_sems, local_sem) = ins, outs, sems
        me, sibling, chips = _place()
        mc = me[2]

        def slot(px, py, pc):
            d = 4 * px + 2 * py + pc
            return out_ref.at[:, pl.ds(pl.multiple_of(d * c, LANES), c)] if cols else out_ref.at[d]

        def copy(k, block, to, src=None):
            return pltpu.make_async_remote_copy(src_ref=slot(*block) if src is None else src, dst_ref=slot(*block),
                                                send_sem=send_sems.at[k], recv_sem=recv_sems.at[k],
                                                device_id=to, device_id_type=MESH)

        def mine():
            return pltpu.make_async_copy(x_ref, slot(*me), local_sem)

        def first():
            return [copy(0, me, sibling, src=x_ref)] + [copy(1 + j, me, (*chip, mc), src=x_ref) for j, chip in enumerate(chips)]

        def passed():
            return [copy(4 + j, (*chip, mc), sibling) for j, chip in enumerate(chips)]

        if which == "start":
            mine().start()
            for cp in first():
                cp.start()
        elif which == "mid":
            for j, (chip, cp) in enumerate(zip(chips, passed())):
                copy(1 + j, (*chip, mc), me).wait_recv()
                cp.start()
        else:
            copy(0, sibling, me).wait_recv()
            for j, chip in enumerate(chips):
                copy(4 + j, (*chip, 1 - mc), me).wait_recv()
            for cp in first() + passed():
                cp.wait_send()
            mine().wait()

    return dict(ins=[x], outs=[jax.ShapeDtypeStruct((r, N_DEV * c) if cols else (N_DEV, r, c), x.dtype)], run=run,
                sems=[pltpu.SemaphoreType.DMA((N_DEV - 1,)), pltpu.SemaphoreType.DMA((N_DEV - 1,)),
                      pltpu.SemaphoreType.DMA])


def _pair_op(gs):
    nt = len(gs)

    def run(which, g_refs, out_refs, sems):
        if which == "mid":
            return
        send_sems, recv_sems = sems
        me, sibling, _ = _place()
        copies = [pltpu.make_async_remote_copy(src_ref=g_refs[t].at[2 * q + (1 - me[2])], dst_ref=out_refs[t].at[q],
                                               send_sem=send_sems.at[t, q], recv_sem=recv_sems.at[t, q],
                                               device_id=sibling, device_id_type=MESH)
                  for t in range(nt) for q in range(N_CHIPS)]
        if which == "start":
            for cp in copies:
                cp.start()
        elif which == "wait":
            for cp in copies:
                cp.wait_recv()
            for cp in copies:
                cp.wait_send()

    return dict(ins=list(gs), outs=[jax.ShapeDtypeStruct((N_CHIPS,) + g.shape[1:], F32) for g in gs], run=run,
                sems=[pltpu.SemaphoreType.DMA((nt, N_CHIPS)), pltpu.SemaphoreType.DMA((nt, N_CHIPS))])


def _pair_add(g, t, core, name):
    _, r, c = g.shape
    tr = r
    while tr * c > 512 * 1024 and tr % 32 == 0:
        tr //= 2

    def body(core_ref, g_ref, t_ref, o_ref):
        o_ref[...] = (g_ref[...] + t_ref[...]).astype(BF16)

    return pl.pallas_call(
        body,
        grid_spec=pltpu.PrefetchScalarGridSpec(
            num_scalar_prefetch=1, grid=(N_CHIPS, r // tr),
            in_specs=[pl.BlockSpec((None, tr, c), lambda q, i, core_ref: (2 * q + core_ref[0], i, 0)),
                      pl.BlockSpec((None, tr, c), lambda q, i, core_ref: (q, i, 0))],
            out_specs=pl.BlockSpec((None, tr, c), lambda q, i, core_ref: (q, i, 0))),
        out_shape=jax.ShapeDtypeStruct((N_CHIPS, r, c), BF16),
        compiler_params=_cparams(("parallel", "parallel")), name=name)(core, g, t)


def _chip_op(ps):
    nt = len(ps)

    def run(which, p_refs, out_refs, sems):
        if which == "mid":
            return
        send_sems, recv_sems, local_sems = sems
        me, _, chips = _place()
        mychip = 2 * me[0] + me[1]
        owns = [pltpu.make_async_copy(p_refs[t].at[mychip], out_refs[t].at[mychip], local_sems.at[t]) for t in range(nt)]
        sends = [pltpu.make_async_remote_copy(src_ref=p_refs[t].at[2 * px + py], dst_ref=out_refs[t].at[mychip],
                                              send_sem=send_sems.at[t, j], recv_sem=recv_sems.at[t, j],
                                              device_id=(px, py, me[2]), device_id_type=MESH)
                 for t in range(nt) for j, (px, py) in enumerate(chips)]
        if which == "start":
            for cp in owns + sends:
                cp.start()
        elif which == "wait":
            for t in range(nt):
                for j, (px, py) in enumerate(chips):
                    pltpu.make_async_remote_copy(src_ref=p_refs[t].at[2 * px + py], dst_ref=out_refs[t].at[2 * px + py],
                                                 send_sem=send_sems.at[t, j], recv_sem=recv_sems.at[t, j],
                                                 device_id=(px, py, me[2]), device_id_type=MESH).wait_recv()
            for cp in sends:
                cp.wait_send()
            for cp in owns:
                cp.wait()

    return dict(ins=list(ps), outs=[jax.ShapeDtypeStruct(p.shape, p.dtype) for p in ps], run=run,
                sems=[pltpu.SemaphoreType.DMA((nt, N_CHIPS - 1)), pltpu.SemaphoreType.DMA((nt, N_CHIPS - 1)),
                      pltpu.SemaphoreType.DMA((nt,))])


def _adamw(w, gstack, m, v, name, layer=None, prev=None):
    r, c = w.shape[-2:]
    nstack = gstack.shape[0]
    tr = r
    while tr * c > 128 * 1024 and tr % 32 == 0:
        tr //= 2
    c1 = 1.0 - ADAM_B1 ** ADAM_STEP
    c2 = 1.0 - ADAM_B2 ** ADAM_STEP

    def body(w_ref, g_ref, m_ref, v_ref, *rest):
        go_ref, d_ref, mo_ref, vo_ref = rest[-4:]
        g = g_ref[0].astype(F32)
        for p in range(1, nstack):
            g = g + g_ref[p].astype(F32)
        mn = ADAM_B1 * m_ref[...] + (1.0 - ADAM_B1) * g
        vn = ADAM_B2 * v_ref[...] + (1.0 - ADAM_B2) * (g * g)
        go_ref[...] = g
        mo_ref[...] = mn
        vo_ref[...] = vn
        d_ref[...] = -ADAM_LR * ((mn / c1) / (jnp.sqrt(vn / c2) + ADAM_EPS) + ADAM_WD * w_ref[...])

    if layer is None:
        spec = pl.BlockSpec((tr, c), lambda i: (i, 0))
        full = (r, c)
    else:
        spec = pl.BlockSpec((None, tr, c), lambda i: (layer, i, 0))
        full = w.shape
    prev = list(prev) if prev is not None else []
    return pl.pallas_call(
        body, grid=(r // tr,),
        in_specs=[spec, pl.BlockSpec((nstack, tr, c), lambda i: (0, i, 0)), spec, spec] + [pl.BlockSpec(memory_space=pl.ANY)] * len(prev),
        out_specs=[spec] * 4, out_shape=[jax.ShapeDtypeStruct(full, F32)] * 4,
        input_output_aliases={4 + n: n for n in range(len(prev))},
        compiler_params=_cparams(("parallel",)), name=name)(w, gstack, m, v, *prev)


def _s5_tables(lam_re, lam_im, log_step, b_re, b_im, c_re, c_im):
    out = []
    for d in range(2):
        ab_re, ab_im, bb_re, bb_im = _s5_discretize(lam_re[d], lam_im[d], log_step[d], b_re, b_im)
        rev = d == 1
        out.append(dict(
            tabs=_scan_tables(ab_re, ab_im, rev), gtabs=_scan_tables(ab_re, -ab_im, not rev),
            bre=_block_diag(jnp.transpose(bb_re, (0, 2, 1))).astype(BF16), bim=_block_diag(jnp.transpose(bb_im, (0, 2, 1))).astype(BF16),
            cre=_block_diag(jnp.transpose(c_re[d], (0, 2, 1))).astype(BF16), cim=_block_diag(jnp.transpose(c_im[d], (0, 2, 1))).astype(BF16)))
    return out


def _layer_tensors(i):
    j = i // 2
    mixer = [("ab_w_in", j), ("ab_w_out", j), ("s5_w_glu", j)] if i % 2 == 0 else [("c_w_qkv", j), ("c_w_out", j)]
    return mixer + [("mlp_w1", i), ("mlp_w2", i)]


class _WeightGather:
    def __init__(self, shards, depth):
        self.shards, self.depth, self.ops = shards, depth, {}

    def _op(self, key):
        self.ops[key] = _gather_op(self.shards[key[0]][key[1]], cols=key[0] not in ROW_SHARDED)
        return self.ops[key]

    def start(self):
        _run_comm([self._op(key) for key in _layer_tensors(0)[:-2]], "gather_mixer0")

    def carry(self, i, slot):
        t = _layer_tensors(i)
        nxt = _layer_tensors(i + 1)[:-2] if i + 1 < self.depth else []
        plan = {"in": t[-2:-1], "up": t[-1:], "down": nxt}
        return [self._op(key) for key in plan[slot]]

    def get(self, name, l):
        full = self.ops[(name, l)]["res"][0]
        return full.reshape(-1, full.shape[-1]) if name in ROW_SHARDED else full


class _GradExchange:
    def __init__(self, core, depth):
        self.core, self.depth, self.g, self.recv_ops, self.pairs, self.ps = core, depth, {}, [], {}, {}

    def put(self, name, l, g):
        self.g[(name, l)] = g.reshape(N_DEV, -1, g.shape[-1])

    def _pair(self, keys):
        op = _pair_op([self.g[key] for key in keys])
        for n, key in enumerate(keys):
            self.pairs[key] = (op, n)
        return [op]

    def _chip(self, keys):
        for key in keys:
            op, n = self.pairs[key]
            self.ps[key] = _pair_add(self.g[key], op["res"][n], self.core, f"pair_add_{key[0]}{key[1]}")
        op = _chip_op([self.ps[key] for key in keys])
        self.recv_ops.append((keys, op))
        return [op]

    def carry(self, i, slot):
        t = _layer_tensors(i)
        later = _layer_tensors(i + 1)[:-2] if i + 1 < self.depth else []
        if slot == "up_bwd":
            return self._pair(t[-2:])
        if slot == "in_grad":
            return self._chip(t[-2:-1])
        if slot == "in_bwd":
            return self._chip(t[-1:])
        if slot == "down_bwd":
            return self._pair(later) if later else ()
        return self._chip(later) if later else ()

    def finish(self):
        keys = _layer_tensors(0)[:-2]
        _run_comm(self._pair(keys), "pair_exchange_mixer0")
        _run_comm(self._chip(keys), "chip_exchange_mixer0")
        return {key: op["res"][n] for keys, op in self.recv_ops for n, key in enumerate(keys)}


def _forward_backward(x, target, p, wsrc, gsink):
    s, d = x.shape
    depth = p["norm_mix"].shape[0]
    ah = p["t5_bias"].shape[1]
    aw = ah * HEAD_DIM
    ch = p["c_rpb"].shape[1]
    groups, pstate = p["s5_lam_re"].shape[2:]
    bw = groups * S5_GROUP
    assert aw + bw == d and ch * HEAD_DIM == d

    dil_bias = [_dil_bias(p["t5_bias"], dil) for _, dil in DILATED_BRANCHES]
    saved = []
    for i in range(depth):
        j = i // 2
        sv = dict(x=x)
        hn = _rms_fwd(x, p["norm_mix"][i], f"norm_mix_fwd{i}")
        sv["hn"] = hn
        if i % 2 == 0:
            proj = _mm_cols(f"ab_in_fwd{i}", hn, wsrc.get("ab_w_in", j), comm=wsrc.carry(i, "in"))
            outs = [(_dil_fwd if dil == 1 else _dil_fwd_strided)(proj, dil_bias[b][0], dil, ah, f"dil_fwd_d{dil}_{i}")
                    for b, (_, dil) in enumerate(DILATED_BRANCHES)]
            o_a, lse = _dil_merge([o for o, _ in outs], [l for _, l in outs])
            tb = _s5_tables(p["s5_lam_re"][j], p["s5_lam_im"][j], p["s5_log_step"][j], p["s5_b_re"][j], p["s5_b_im"][j],
                            p["s5_c_re"][j], p["s5_c_im"][j])
            y0, x0r, x0i = _s5_fwd(proj, 3 * aw, tb[0]["tabs"], tb[0]["bre"], tb[0]["bim"], tb[0]["cre"], tb[0]["cim"],
                                   False, None, f"s5_fwd_a{i}")
            y_pre, x1r, x1i = _s5_fwd(proj, 3 * aw, tb[1]["tabs"], tb[1]["bre"], tb[1]["bim"], tb[1]["cre"], tb[1]["cim"],
                                      True, (y0, p["s5_d"][j]), f"s5_fwd_b{i}")
            o_b = _mm(f"glu_fwd{i}", y_pre, wsrc.get("s5_w_glu", j), a_fn=_gelu, extras=(y_pre,), out_dtypes=(BF16,),
                      epi=lambda acc, yp: (_gelu(yp) * jax.nn.sigmoid(acc),))[0]
            merged = jnp.concatenate([o_a, o_b], axis=1)
            x = _mm(f"ab_out_fwd{i}", merged, wsrc.get("ab_w_out", j), extras=(x,), epi=lambda acc, xr: (acc + xr,))[0]
            sv.update(proj=proj, o_a=o_a, lse=lse, tb=tb, states=((x0r, x0i), (x1r, x1i)), y_pre=y_pre, merged=merged)
        else:
            qkv = _mm_cols(f"c_qkv_fwd{i}", hn, wsrc.get("c_w_qkv", j), comm=wsrc.carry(i, "in"))
            nbias = _na_bias(p["c_rpb"][j])
            o, lse = _na_fwd(qkv, nbias, ch, f"na_fwd{i}")
            x = _mm(f"c_out_fwd{i}", o, wsrc.get("c_w_out", j), extras=(x,), epi=lambda acc, xr: (acc + xr,))[0]
            sv.update(qkv=qkv, o=o, lse=lse, nbias=nbias)
        sv["x_mid"] = x
        hn2 = _rms_fwd(x, p["norm_mlp"][i], f"norm_mlp_fwd{i}")
        h_pre = _mm_cols(f"mlp_up_fwd{i}", hn2, wsrc.get("mlp_w1", i), comm=wsrc.carry(i, "up"))
        x = _mm(f"mlp_down_fwd{i}", h_pre, wsrc.get("mlp_w2", i), a_fn=_relu_sq, extras=(x,), epi=lambda acc, xr: (acc + xr,),
                comm=wsrc.carry(i, "down"))[0]
        sv.update(hn2=hn2, h_pre=h_pre)
        saved.append(sv)

    loss_sum, dx, g_final = _final_loss(x, p["norm_final"], target)

    g = ({k: [None] * p[k].shape[0] for k in ("norm_mix", "norm_mlp", "s5_lam_re", "s5_lam_im", "s5_log_step", "s5_b_re",
                                                  "s5_b_im", "s5_c_re", "s5_c_im", "s5_d", "c_rpb")})
    g_t5 = jnp.zeros_like(p["t5_bias"], dtype=F32)
    for i in reversed(range(depth)):
        j = i // 2
        sv = saved[i]
        dh = _mm(f"mlp_down_bwd{i}", dx, wsrc.get("mlp_w2", i), tb=True, extras=(sv["h_pre"],), out_dtypes=(BF16,),
                 epi=lambda acc, hp: (acc * (2.0 * jnp.maximum(hp.astype(F32), 0.0)),), comm=gsink.carry(i, "down_bwd"))[0]
        gsink.put("mlp_w2", i, _mm(f"mlp_w2_grad{i}", sv["h_pre"], dx, ta=True, a_fn=_relu_sq)[0])
        gsink.put("mlp_w1", i, _mm_cols_grad(f"mlp_w1_grad{i}", sv["hn2"], dh, comm=gsink.carry(i, "w1_grad")))
        dhn2 = _mm_cols_t(f"mlp_up_bwd{i}", dh, wsrc.get("mlp_w1", i), comm=gsink.carry(i, "up_bwd"))
        dx, gn = _rms_bwd(sv["x_mid"], p["norm_mlp"][i], dhn2, dx, f"norm_mlp_bwd{i}")
        g["norm_mlp"][i] = gn[0]
        if i % 2 == 0:
            tb = sv["tb"]
            dmerged = _mm(f"ab_out_bwd{i}", dx, wsrc.get("ab_w_out", j), tb=True, out_dtypes=(BF16,))[0]
            gsink.put("ab_w_out", j, _mm(f"ab_w_out_grad{i}", sv["merged"], dx, ta=True)[0])
            def glu_epi(acc, yp, dob):
                sg = jax.nn.sigmoid(acc)
                dob = dob.astype(F32)
                return dob * _gelu(yp) * sg * (1.0 - sg), dob * sg
            dz, t1 = _mm(f"glu_bwd_z{i}", sv["y_pre"], wsrc.get("s5_w_glu", j), a_fn=_gelu, extras=(sv["y_pre"], dmerged),
                         extra_cols=(0, aw), epi=glu_epi, out_dtypes=(BF16, F32))
            dy_pre = _mm(f"glu_bwd_y{i}", dz, wsrc.get("s5_w_glu", j), tb=True, extras=(t1, sv["y_pre"]),
                         epi=lambda acc, t, yp: ((acc + t) * _gelu_grad(yp),), out_dtypes=(BF16,))[0]
            gsink.put("s5_w_glu", j, _mm(f"glu_w_grad{i}", sv["y_pre"], dz, ta=True, a_fn=_gelu)[0])
            r0 = _s5_bwd(sv["proj"], 3 * aw, dy_pre, *sv["states"][0], tb[0]["gtabs"], tb[0]["bre"], tb[0]["bim"],
                         tb[0]["cre"], tb[0]["cim"], False, None, f"s5_bwd_a{i}")
            r1 = _s5_bwd(sv["proj"], 3 * aw, dy_pre, *sv["states"][1], tb[1]["gtabs"], tb[1]["bre"], tb[1]["bim"],
                         tb[1]["cre"], tb[1]["cim"], True, (r0[0], p["s5_d"][j]), f"s5_bwd_b{i}")
            du = r1[0]
            g["s5_d"][j] = r1[7][0]
            gl_re, gl_im, gls, gb_re, gb_im, gc_re, gc_im = [], [], [], 0.0, 0.0, [], []
            for dnum, rr in enumerate((r0, r1)):
                da_re = jnp.sum(rr[1], axis=1).reshape(groups, pstate)
                da_im = jnp.sum(rr[2], axis=1).reshape(groups, pstate)
                dbb_re = jnp.transpose(_block_diag_take(rr[3], S5_GROUP, pstate), (0, 2, 1))
                dbb_im = jnp.transpose(_block_diag_take(rr[4], S5_GROUP, pstate), (0, 2, 1))
                _, vjp = jax.vjp(_s5_discretize, p["s5_lam_re"][j][dnum], p["s5_lam_im"][j][dnum], p["s5_log_step"][j][dnum],
                                 p["s5_b_re"][j], p["s5_b_im"][j])
                a, b, c, e, f = vjp((da_re, da_im, dbb_re, dbb_im))
                gl_re.append(a)
                gl_im.append(b)
                gls.append(c)
                gb_re, gb_im = gb_re + e, gb_im + f
                gc_re.append(jnp.transpose(_block_diag_take(rr[5], pstate, S5_GROUP), (0, 2, 1)))
                gc_im.append(jnp.transpose(_block_diag_take(rr[6], pstate, S5_GROUP), (0, 2, 1)))
            g["s5_lam_re"][j], g["s5_lam_im"][j], g["s5_log_step"][j] = jnp.stack(gl_re), jnp.stack(gl_im), jnp.stack(gls)
            g["s5_b_re"][j], g["s5_b_im"][j] = gb_re, gb_im
            g["s5_c_re"][j], g["s5_c_im"][j] = jnp.stack(gc_re), jnp.stack(gc_im)
            delta = _head_delta(dmerged, 0, sv["o_a"], f"dil_delta{i}")
            acc = None
            for b, (_, dil) in enumerate(DILATED_BRANCHES):
                args = (sv["proj"], dmerged, sv["lse"], delta, dil_bias[b][0], dil_bias[b][1], dil, ah, f"dil_bwd_d{dil}_{i}")
                if dil == 1:
                    assert acc is None
                    *acc, db = _dil_bwd(*args)
                else:
                    *acc, db = _dil_bwd_strided(*args, acc, BF16 if b == len(DILATED_BRANCHES) - 1 else F32)
                g_t5 = g_t5 + _t5_grad(db, dil)
            dproj = jnp.concatenate([*acc, du], axis=1)
            gsink.put("ab_w_in", j, _mm_cols_grad(f"ab_w_in_grad{i}", sv["hn"], dproj, comm=gsink.carry(i, "in_grad")))
            dhn = _mm_cols_t(f"ab_in_bwd{i}", dproj, wsrc.get("ab_w_in", j), comm=gsink.carry(i, "in_bwd"))
        else:
            do = _mm(f"c_out_bwd{i}", dx, wsrc.get("c_w_out", j), tb=True, out_dtypes=(BF16,))[0]
            gsink.put("c_w_out", j, _mm(f"c_w_out_grad{i}", sv["o"], dx, ta=True)[0])
            dq, dk, dv, db = _na_bwd(sv["qkv"], sv["o"], do, sv["lse"], sv["nbias"], ch, f"na_bwd{i}")
            g["c_rpb"][j] = _rpb_grad(db)
            dqkv = jnp.concatenate([dq, dk.astype(BF16), dv.astype(BF16)], axis=1)
            gsink.put("c_w_qkv", j, _mm_cols_grad(f"c_w_qkv_grad{i}", sv["hn"], dqkv, comm=gsink.carry(i, "in_grad")))
            dhn = _mm_cols_t(f"c_qkv_bwd{i}", dqkv, wsrc.get("c_w_qkv", j), comm=gsink.carry(i, "in_bwd"))
        dx, gn = _rms_bwd(sv["x"], p["norm_mix"][i], dhn, dx, f"norm_mix_bwd{i}")
        g["norm_mix"][i] = gn[0]
    g["t5_bias"] = g_t5
    g["norm_final"] = g_final[0]
    return loss_sum[0, 0], dx, g


BIG = ("ab_w_in", "ab_w_out", "s5_w_glu", "c_w_qkv", "c_w_out", "mlp_w1", "mlp_w2")
ROW_SHARDED = ("ab_w_out", "s5_w_glu", "c_w_out", "mlp_w2")
WEIGHTS = ("t5_bias", "ab_w_in", "ab_w_out", "s5_lam_re", "s5_lam_im", "s5_log_step", "s5_b_re", "s5_b_im", "s5_c_re",
           "s5_c_im", "s5_d", "s5_w_glu", "c_w_qkv", "c_w_out", "c_rpb", "norm_mix", "norm_mlp", "mlp_w1", "mlp_w2",
           "norm_final")


def _step(x, target, w, m, v):
    d = x.shape[-1]
    depth = w["norm_mix"].shape[0]
    wsrc = _WeightGather({k: w[k].astype(BF16) for k in BIG}, depth)
    wsrc.start()
    gsink = _GradExchange(lax.axis_index("c").astype(jnp.int32).reshape(1), depth)
    small = {k: w[k] for k in WEIGHTS if k not in BIG}
    loss_sum, dx, g = _forward_backward(x[0], target[0], small, wsrc, gsink)
    loss = lax.psum(0.5 * loss_sum / d, ("x", "y", "c"))

    out = {}
    recv = gsink.finish()
    for k in BIG:
        res = None
        for l in range(w[k].shape[0]):
            res = _adamw(w[k], recv[(k, l)], m[k], v[k], f"adamw_{k}{l}", layer=l, prev=res)
        out[k] = res
    names = [k for k in WEIGHTS if k not in BIG]
    def flat(tree):
        return jnp.concatenate([jnp.asarray(jnp.stack(tree[k]) if isinstance(tree[k], list) else tree[k], F32).reshape(-1)
                                for k in names])
    total = sum(int(np.prod(w[k].shape)) for k in names)
    rows = -(-total // LANES)
    rows = -(-rows // SUBLANES) * SUBLANES
    pad = rows * LANES - total
    def pack(tree):
        return jnp.pad(flat(tree), (0, pad)).reshape(rows, LANES)
    small_op = _gather_op(pack(g))
    _run_comm([small_op], "gather_small_grads")
    res = _adamw(pack(w), small_op["res"][0], pack(m), pack(v), "adamw_small")
    off = 0
    for k in names:
        n = int(np.prod(w[k].shape))
        out[k] = [a.reshape(-1)[off:off + n].reshape(w[k].shape) for a in res]
        off += n
    return (loss, dx[None], *[out[k][0] for k in WEIGHTS], *[out[k][1] for k in WEIGHTS],
            *[out[k][2] for k in WEIGHTS], *[out[k][3] for k in WEIGHTS])


def kernel(x, t5_bias, ab_w_in, ab_w_out, s5_lam_re, s5_lam_im, s5_log_step, s5_b_re, s5_b_im, s5_c_re, s5_c_im, s5_d, s5_w_glu, c_w_qkv, c_w_out, c_rpb, norm_mix, norm_mlp, mlp_w1, mlp_w2, norm_final, loss_target, m_t5_bias, m_ab_w_in, m_ab_w_out, m_s5_lam_re, m_s5_lam_im, m_s5_log_step, m_s5_b_re, m_s5_b_im, m_s5_c_re, m_s5_c_im, m_s5_d, m_s5_w_glu, m_c_w_qkv, m_c_w_out, m_c_rpb, m_norm_mix, m_norm_mlp, m_mlp_w1, m_mlp_w2, m_norm_final, v_t5_bias, v_ab_w_in, v_ab_w_out, v_s5_lam_re, v_s5_lam_im, v_s5_log_step, v_s5_b_re, v_s5_b_im, v_s5_c_re, v_s5_c_im, v_s5_d, v_s5_w_glu, v_c_w_qkv, v_c_w_out, v_c_rpb, v_norm_mix, v_norm_mlp, v_mlp_w1, v_mlp_w2, v_norm_final):
    w = dict(t5_bias=t5_bias, ab_w_in=ab_w_in, ab_w_out=ab_w_out, s5_lam_re=s5_lam_re, s5_lam_im=s5_lam_im,
             s5_log_step=s5_log_step, s5_b_re=s5_b_re, s5_b_im=s5_b_im, s5_c_re=s5_c_re, s5_c_im=s5_c_im, s5_d=s5_d,
             s5_w_glu=s5_w_glu, c_w_qkv=c_w_qkv, c_w_out=c_w_out, c_rpb=c_rpb, norm_mix=norm_mix, norm_mlp=norm_mlp,
             mlp_w1=mlp_w1, mlp_w2=mlp_w2, norm_final=norm_final)
    m = dict(t5_bias=m_t5_bias, ab_w_in=m_ab_w_in, ab_w_out=m_ab_w_out, s5_lam_re=m_s5_lam_re, s5_lam_im=m_s5_lam_im,
             s5_log_step=m_s5_log_step, s5_b_re=m_s5_b_re, s5_b_im=m_s5_b_im, s5_c_re=m_s5_c_re, s5_c_im=m_s5_c_im,
             s5_d=m_s5_d, s5_w_glu=m_s5_w_glu, c_w_qkv=m_c_w_qkv, c_w_out=m_c_w_out, c_rpb=m_c_rpb, norm_mix=m_norm_mix,
             norm_mlp=m_norm_mlp, mlp_w1=m_mlp_w1, mlp_w2=m_mlp_w2, norm_final=m_norm_final)
    v = dict(t5_bias=v_t5_bias, ab_w_in=v_ab_w_in, ab_w_out=v_ab_w_out, s5_lam_re=v_s5_lam_re, s5_lam_im=v_s5_lam_im,
             s5_log_step=v_s5_log_step, s5_b_re=v_s5_b_re, s5_b_im=v_s5_b_im, s5_c_re=v_s5_c_re, s5_c_im=v_s5_c_im,
             s5_d=v_s5_d, s5_w_glu=v_s5_w_glu, c_w_qkv=v_c_w_qkv, c_w_out=v_c_w_out, c_rpb=v_c_rpb, norm_mix=v_norm_mix,
             norm_mlp=v_norm_mlp, mlp_w1=v_mlp_w1, mlp_w2=v_mlp_w2, norm_final=v_norm_final)
    return _step(x, loss_target, w, m, v)
```

```python
import math

import jax
import jax.numpy as jnp
import numpy as np
from jax import lax
from jax.experimental import pallas as pl
from jax.experimental.pallas import tpu as pltpu

F32 = jnp.float32
BF16 = jnp.bfloat16

N_DEV = 8
HEAD_DIM = 128
LANES = 128
QBLOCK = 128
DIL_HALF = 64
DILATED_BRANCHES = ((128, 1), (512, 4), (2048, 16))
S5_GROUP = 16
S5_GROUPS_PER_BLOCK = LANES // S5_GROUP
S5_CHUNK = 2048
SUBLANES = 8
GRID_W = 64
NA_ROWS = 8
NA_COLS = 16
NA_GROUP = 4
NA_WIN = NA_GROUP + NA_ROWS - 1
NA_GROUPS_PER_STEP = 8
T5_BUCKETS = 32
T5_MAX_DISTANCE = 1024
RMS_EPS = 1e-6
NEG_INF = -1e30
ADAM_LR = 0.001
ADAM_B1 = 0.9
ADAM_B2 = 0.999
ADAM_EPS = 1e-08
ADAM_WD = 0.01
ADAM_STEP = 10
VMEM_LIMIT_BYTES = 56 * 1024 * 1024
MESH = pl.DeviceIdType.MESH


def _cparams(sem=None):
    return pltpu.CompilerParams(dimension_semantics=sem, vmem_limit_bytes=VMEM_LIMIT_BYTES)


def _tile(dim, pref):
    t = min(dim, pref)
    while dim % t and t > LANES:
        t -= LANES
    assert dim % t == 0, (dim, pref)
    return t


def _dot(a, b, ca, cb):
    return lax.dot_general(a, b, (((ca,), (cb,)), ((), ())), preferred_element_type=F32)


def _nn(a, b):
    return _dot(a, b, 1, 0)


def _nt(a, b):
    return _dot(a, b, 1, 1)


def _tn(a, b):
    return _dot(a, b, 0, 0)


HBM_SPEC = pl.BlockSpec(memory_space=pltpu.HBM)


def _split_comm_refs(comm, in_refs, out_refs, sem_refs):
    parts, i, o, s = [], 0, 0, 0
    for op in comm:
        ni, no, ns = len(op["ins"]), len(op["outs"]), len(op["sems"])
        parts.append((in_refs[i:i + ni], out_refs[o:o + no], sem_refs[s:s + ns]))
        i, o, s = i + ni, o + no, s + ns
    return parts


def _mm_call(name, a, b, a_spec, b_spec, grid, nk, out_shapes, out_specs, acc_shape,
             ta=False, tb=False, a_fn=None, epi=None, extras=(), extra_specs=(), comm=()):
    ne, no = len(extras), len(out_shapes)
    comm_ins = [x for op in comm for x in op["ins"]]
    comm_outs = [x for op in comm for x in op["outs"]]
    comm_sems = [x for op in comm for x in op["sems"]]
    nci, nco = len(comm_ins), len(comm_outs)
    total = grid[0] * grid[1] * grid[2]

    def body(a_ref, b_ref, *rest):
        ex, rest = rest[:ne], rest[ne:]
        cin, rest = rest[:nci], rest[nci:]
        outs, rest = rest[:no], rest[no:]
        cout, rest = rest[:nco], rest[nco:]
        acc, csem = rest[0], rest[1:]
        k = pl.program_id(2)
        step = (pl.program_id(0) * grid[1] + pl.program_id(1)) * grid[2] + k
        parts = _split_comm_refs(comm, cin, cout, csem)

        def phase(which, at):
            if comm:
                @pl.when(step == at)
                def _():
                    for op, refs in zip(comm, parts):
                        op["run"](which, *refs)

        phase("start", 0)
        phase("mid", total - 1 - total // 8)

        @pl.when(k == 0)
        def _():
            acc[...] = jnp.zeros_like(acc)

        av = a_ref[...]
        if a_fn is not None:
            av = a_fn(av)
        acc[...] += _dot(av.astype(BF16), b_ref[...].astype(BF16), 0 if ta else 1, 1 if tb else 0)

        @pl.when(k == nk - 1)
        def _():
            r = acc[...]
            res = epi(r, *[e[...] for e in ex]) if epi is not None else (r,)
            for o, v in zip(outs, res):
                o[...] = v.astype(o.dtype)

        phase("wait", total - 1)

    res = pl.pallas_call(
        body, grid=grid, in_specs=[a_spec, b_spec, *extra_specs] + [HBM_SPEC] * nci,
        out_specs=list(out_specs) + [HBM_SPEC] * nco, out_shape=list(out_shapes) + comm_outs,
        scratch_shapes=[pltpu.VMEM(acc_shape, F32)] + comm_sems,
        compiler_params=_cparams(("arbitrary",) * 3 if comm else ("parallel", "parallel", "arbitrary")), name=name,
    )(a, b, *extras, *comm_ins)
    o = no
    for op in comm:
        op["res"] = res[o:o + len(op["outs"])]
        o += len(op["outs"])
    return res[:no]


def _run_comm(comm, name):
    comm_ins = [x for op in comm for x in op["ins"]]
    comm_outs = [x for op in comm for x in op["outs"]]
    comm_sems = [x for op in comm for x in op["sems"]]
    nci, nco = len(comm_ins), len(comm_outs)

    def body(*refs):
        parts = _split_comm_refs(comm, refs[:nci], refs[nci:nci + nco], refs[nci + nco:])
        for which in ("start", "mid", "wait"):
            for op, r in zip(comm, parts):
                op["run"](which, *r)

    res = pl.pallas_call(
        body, in_specs=[HBM_SPEC] * nci, out_specs=[HBM_SPEC] * nco, out_shape=comm_outs, scratch_shapes=comm_sems,
        compiler_params=pltpu.CompilerParams(has_side_effects=True), name=name)(*comm_ins)
    o = 0
    for op in comm:
        op["res"] = res[o:o + len(op["outs"])]
        o += len(op["outs"])


def _mm(name, a, b, *, ta=False, tb=False, a_fn=None, epi=None, extras=(), extra_cols=None,
        out_dtypes=(F32,), tm=1024, tn=1024, tk=2048, comm=()):
    m, kdim = (a.shape[1], a.shape[0]) if ta else a.shape
    n = b.shape[0] if tb else b.shape[1]
    assert (b.shape[1] if tb else b.shape[0]) == kdim, (a.shape, b.shape)
    tm, tn, tk = _tile(m, tm), _tile(n, tn), _tile(kdim, tk)
    a_spec = pl.BlockSpec((tk, tm), lambda i, j, k: (k, i)) if ta else pl.BlockSpec((tm, tk), lambda i, j, k: (i, k))
    b_spec = pl.BlockSpec((tn, tk), lambda i, j, k: (j, k)) if tb else pl.BlockSpec((tk, tn), lambda i, j, k: (k, j))
    o_spec = pl.BlockSpec((tm, tn), lambda i, j, k: (i, j))
    extra_cols = extra_cols or (0,) * len(extras)
    especs = []
    for c0 in extra_cols:
        assert c0 % tn == 0
        cb = c0 // tn
        especs.append(pl.BlockSpec((tm, tn), lambda i, j, k, cb=cb: (i, cb + j)))
    return _mm_call(name, a, b, a_spec, b_spec, (m // tm, n // tn, kdim // tk), kdim // tk,
                    [jax.ShapeDtypeStruct((m, n), d) for d in out_dtypes], [o_spec] * len(out_dtypes), (tm, tn),
                    ta=ta, tb=tb, a_fn=a_fn, epi=epi, extras=extras, extra_specs=especs, comm=comm)


def _mm_cols(name, a, w, *, comm=()):
    return _mm(name, a, w, out_dtypes=(BF16,), comm=comm)[0]


def _mm_cols_t(name, a, w, *, comm=()):
    return _mm(name, a, w, tb=True, out_dtypes=(BF16,), comm=comm)[0]


def _mm_cols_grad(name, a, dy, *, tm=1024, tk=2048, comm=()):
    s, kout = a.shape
    n = dy.shape[1] // N_DEV
    tm, tk = _tile(kout, tm), _tile(s, tk)
    return _mm_call(name, a, dy, pl.BlockSpec((tk, tm), lambda i, j, k: (k, i)),
                    pl.BlockSpec((tk, n), lambda i, j, k: (k, j)),
                    (kout // tm, N_DEV, s // tk), s // tk,
                    [jax.ShapeDtypeStruct((N_DEV, kout, n), F32)], [pl.BlockSpec((None, tm, n), lambda i, j, k: (j, i, 0))],
                    (tm, n), ta=True, comm=comm)[0]


_GELU_C = math.sqrt(2.0 / math.pi)


def _gelu(x):
    return 0.5 * x * (1.0 + jnp.tanh(_GELU_C * (x + 0.044715 * x * x * x)))


def _gelu_grad(x):
    t = jnp.tanh(_GELU_C * (x + 0.044715 * x * x * x))
    return 0.5 * (1.0 + t) + 0.5 * x * (1.0 - t * t) * _GELU_C * (1.0 + 3.0 * 0.044715 * x * x)


def _relu_sq(x):
    r = jnp.maximum(x.astype(F32), 0.0)
    return r * r


def _rms_fwd(x, g, name):
    s, d = x.shape
    tr = _tile(s, 512)

    def body(x_ref, g_ref, o_ref):
        xv = x_ref[...]
        r = lax.rsqrt(jnp.mean(xv * xv, axis=-1, keepdims=True) + RMS_EPS)
        o_ref[...] = (xv * r * g_ref[...]).astype(BF16)

    return pl.pallas_call(
        body, grid=(s // tr,),
        in_specs=[pl.BlockSpec((tr, d), lambda i: (i, 0)), pl.BlockSpec((1, d), lambda i: (0, 0))],
        out_specs=pl.BlockSpec((tr, d), lambda i: (i, 0)), out_shape=jax.ShapeDtypeStruct((s, d), BF16),
        compiler_params=_cparams(("parallel",)), name=name)(x, g.reshape(1, d))


def _rms_bwd(x, g, dy, dres, name):
    s, d = x.shape
    tr = _tile(s, 512)

    def body(x_ref, g_ref, dy_ref, dres_ref, dx_ref, dg_ref):
        @pl.when(pl.program_id(0) == 0)
        def _():
            dg_ref[...] = jnp.zeros_like(dg_ref)

        xv = x_ref[...]
        dyv = dy_ref[...].astype(F32)
        r = lax.rsqrt(jnp.mean(xv * xv, axis=-1, keepdims=True) + RMS_EPS)
        xh = xv * r
        gdy = dyv * g_ref[...]
        dx_ref[...] = dres_ref[...] + r * (gdy - xh * jnp.mean(gdy * xh, axis=-1, keepdims=True))
        dg_ref[...] += jnp.sum(dyv * xh, axis=0, keepdims=True)

    return pl.pallas_call(
        body, grid=(s // tr,),
        in_specs=[pl.BlockSpec((tr, d), lambda i: (i, 0)), pl.BlockSpec((1, d), lambda i: (0, 0)),
                  pl.BlockSpec((tr, d), lambda i: (i, 0)), pl.BlockSpec((tr, d), lambda i: (i, 0))],
        out_specs=[pl.BlockSpec((tr, d), lambda i: (i, 0)), pl.BlockSpec((1, d), lambda i: (0, 0))],
        out_shape=[jax.ShapeDtypeStruct((s, d), F32), jax.ShapeDtypeStruct((1, d), F32)],
        compiler_params=_cparams(("arbitrary",)), name=name)(x, g.reshape(1, d), dy, dres)


def _final_loss(x, g, target):
    s, d = x.shape
    tr = _tile(s, 512)

    def body(x_ref, g_ref, t_ref, loss_ref, dx_ref, dg_ref):
        @pl.when(pl.program_id(0) == 0)
        def _():
            dg_ref[...] = jnp.zeros_like(dg_ref)
            loss_ref[...] = jnp.zeros_like(loss_ref)

        xv = x_ref[...]
        gv = g_ref[...]
        r = lax.rsqrt(jnp.mean(xv * xv, axis=-1, keepdims=True) + RMS_EPS)
        xh = xv * r
        err = xh * gv - t_ref[...]
        loss_ref[...] += jnp.sum(jnp.sum(err * err, axis=-1, keepdims=True), axis=0, keepdims=True)
        dyv = err * (1.0 / d)
        gdy = dyv * gv
        dx_ref[...] = r * (gdy - xh * jnp.mean(gdy * xh, axis=-1, keepdims=True))
        dg_ref[...] += jnp.sum(dyv * xh, axis=0, keepdims=True)

    return pl.pallas_call(
        body, grid=(s // tr,),
        in_specs=[pl.BlockSpec((tr, d), lambda i: (i, 0)), pl.BlockSpec((1, d), lambda i: (0, 0)),
                  pl.BlockSpec((tr, d), lambda i: (i, 0))],
        out_specs=[pl.BlockSpec((1, 1), lambda i: (0, 0)), pl.BlockSpec((tr, d), lambda i: (i, 0)),
                   pl.BlockSpec((1, d), lambda i: (0, 0))],
        out_shape=[jax.ShapeDtypeStruct((1, 1), F32), jax.ShapeDtypeStruct((s, d), F32),
                   jax.ShapeDtypeStruct((1, d), F32)],
        compiler_params=_cparams(("arbitrary",)), name="final_norm_loss")(x, g.reshape(1, d), target)


def _t5_bucket(rel):
    half = T5_BUCKETS // 2
    max_exact = half // 2
    n = jnp.abs(rel)
    nf = jnp.maximum(n, 1).astype(F32)
    large = max_exact + (jnp.log(nf / max_exact) / math.log(T5_MAX_DISTANCE / max_exact)
                         * (half - max_exact)).astype(jnp.int32)
    large = jnp.minimum(large, half - 1)
    return jnp.where(rel > 0, half, 0) + jnp.where(n < max_exact, n, large)


def _dil_offsets():
    i = jnp.arange(QBLOCK)[:, None]
    kk = jnp.arange(2 * QBLOCK)[None, :]
    return kk - DIL_HALF - i, (jnp.arange(QBLOCK)[None, :] + DIL_HALF) - jnp.arange(2 * QBLOCK)[:, None]


def _expand(table, onehot, name, tn=8192):
    r, n = table.shape[0], onehot.shape[1]
    tn = _tile(n, tn)

    def body(t_ref, oh_ref, o_ref):
        o_ref[...] = lax.dot_general(t_ref[...], oh_ref[...], (((1,), (0,)), ((), ())),
                                     precision=lax.Precision.HIGHEST, preferred_element_type=F32)

    return pl.pallas_call(
        body, grid=(n // tn,),
        in_specs=[pl.BlockSpec((r, LANES), lambda i: (0, 0)), pl.BlockSpec((LANES, tn), lambda i: (0, i))],
        out_specs=pl.BlockSpec((r, tn), lambda i: (0, i)), out_shape=jax.ShapeDtypeStruct((r, n), F32),
        compiler_params=_cparams(("parallel",)), name=name)(table, onehot)


def _pad_rows_lanes(t):
    r, c = t.shape
    return jnp.pad(t.astype(F32), ((0, -r % SUBLANES), (0, LANES - c)))


def _dil_bias(t5, dil):
    ah = t5.shape[1]
    off1, off2 = _dil_offsets()
    bucket = jnp.concatenate([_t5_bucket(off1 * dil).reshape(-1), _t5_bucket(off2 * dil).reshape(-1)])
    onehot = (jnp.arange(LANES)[:, None] == bucket[None, :]).astype(F32)
    b = _expand(_pad_rows_lanes(t5.T), onehot, f"t5_bias_d{dil}")[:ah]
    n1 = QBLOCK * 2 * QBLOCK
    return b[:, :n1].reshape(ah, QBLOCK, 2 * QBLOCK), b[:, n1:].reshape(ah, 2 * QBLOCK, QBLOCK)


def _window(p, c, n, cols=slice(None)):
    return jnp.concatenate([p[pl.ds(DIL_HALF, DIL_HALF), cols], c[:, cols], n[pl.ds(0, DIL_HALF), cols]], axis=0)


def _dil_specs(width, ncol_blocks, col_block, nb):
    def spec(dn):
        return pl.BlockSpec((QBLOCK, width), lambda r, n: (jnp.clip(n + dn, 0, nb - 1), r * ncol_blocks + col_block))
    return [spec(-1), spec(0), spec(1)]


def _dil_fwd(proj, bias1, dil, ah, name):
    s, wtot = proj.shape
    ln = s // dil
    nb = ln // QBLOCK
    assert nb * QBLOCK * dil == s
    aw = ah * HEAD_DIM
    wb = wtot // aw
    scale = 1.0 / math.sqrt(HEAD_DIM)
    pv = proj.reshape(ln, dil * wtot)

    def body(q_ref, kp, kc, kn, vp, vc, vn, b_ref, o_ref, lse_ref):
        n = pl.program_id(1)
        ii = lax.broadcasted_iota(jnp.int32, (QBLOCK, 2 * QBLOCK), 0)
        jj = lax.broadcasted_iota(jnp.int32, (QBLOCK, 2 * QBLOCK), 1)
        kpos = n * QBLOCK + jj - DIL_HALF
        valid = (jnp.abs(jj - DIL_HALF - ii) <= DIL_HALF) & (kpos >= 0) & (kpos < ln)
        for h in range(ah):
            cs = pl.ds(h * HEAD_DIM, HEAD_DIM)
            kw, vw = _window(kp, kc, kn, cs), _window(vp, vc, vn, cs)
            sc = jnp.where(valid, _nt(q_ref[:, cs], kw) * scale + b_ref[h], NEG_INF)
            m = jnp.max(sc, axis=-1, keepdims=True)
            p = jnp.exp(sc - m)
            l = jnp.sum(p, axis=-1, keepdims=True)
            o_ref[:, cs] = (_nn(p.astype(BF16), vw) / l).astype(BF16)
            lse_ref[:, cs] = jnp.broadcast_to(m + jnp.log(l), (QBLOCK, HEAD_DIM))

    ospec = pl.BlockSpec((QBLOCK, aw), lambda r, n: (n, r))
    o, lse = pl.pallas_call(
        body, grid=(dil, nb),
        in_specs=[_dil_specs(aw, wb, 0, nb)[1], *_dil_specs(aw, wb, 1, nb), *_dil_specs(aw, wb, 2, nb),
                  pl.BlockSpec((ah, QBLOCK, 2 * QBLOCK), lambda r, n: (0, 0, 0))],
        out_specs=[ospec, ospec],
        out_shape=[jax.ShapeDtypeStruct((ln, dil * aw), BF16), jax.ShapeDtypeStruct((ln, dil * aw), F32)],
        compiler_params=_cparams(("parallel", "parallel")), name=name,
    )(pv, pv, pv, pv, pv, pv, pv, bias1)
    return o.reshape(s, aw), lse.reshape(s, aw)


def _dil_merge(outs, lses):
    s, aw = outs[0].shape
    tr = _tile(s, 512)

    def body(o1, l1, o2, l2, o3, l3, o_ref, lse_ref):
        a, b, c = l1[...], l2[...], l3[...]
        m = jnp.maximum(jnp.maximum(a, b), c)
        w1, w2, w3 = jnp.exp(a - m), jnp.exp(b - m), jnp.exp(c - m)
        tot = w1 + w2 + w3
        o_ref[...] = ((w1 * o1[...].astype(F32) + w2 * o2[...].astype(F32) + w3 * o3[...].astype(F32)) / tot).astype(BF16)
        lse_ref[...] = m + jnp.log(tot)

    spec = pl.BlockSpec((tr, aw), lambda i: (i, 0))
    return pl.pallas_call(
        body, grid=(s // tr,), in_specs=[spec] * 6, out_specs=[spec, spec],
        out_shape=[jax.ShapeDtypeStruct((s, aw), BF16), jax.ShapeDtypeStruct((s, aw), F32)],
        compiler_params=_cparams(("parallel",)), name="dil_merge",
    )(outs[0], lses[0], outs[1], lses[1], outs[2], lses[2])


def _head_delta(do, do_col0, o, name):
    s, w = o.shape
    tr = _tile(s, 512)
    cb = do_col0 // HEAD_DIM

    def body(do_ref, o_ref, d_ref):
        d = jnp.sum(do_ref[...].astype(F32) * o_ref[...].astype(F32), axis=-1, keepdims=True)
        d_ref[...] = jnp.broadcast_to(d, d_ref.shape)

    return pl.pallas_call(
        body, grid=(s // tr, w // HEAD_DIM),
        in_specs=[pl.BlockSpec((tr, HEAD_DIM), lambda i, h: (i, cb + h)), pl.BlockSpec((tr, HEAD_DIM), lambda i, h: (i, h))],
        out_specs=pl.BlockSpec((tr, HEAD_DIM), lambda i, h: (i, h)), out_shape=jax.ShapeDtypeStruct((s, w), F32),
        compiler_params=_cparams(("parallel", "parallel")), name=name)(do, o)


def _dil_bwd(proj, dmerged, lse, delta, bias1, bias2, dil, ah, name):
    s, wtot = proj.shape
    ln = s // dil
    nb = ln // QBLOCK
    aw = ah * HEAD_DIM
    wb = wtot // aw
    wd = dmerged.shape[1] // aw
    scale = 1.0 / math.sqrt(HEAD_DIM)
    pv = proj.reshape(ln, dil * wtot)
    dov = dmerged.reshape(ln, dil * dmerged.shape[1])
    lv = lse.reshape(ln, dil * aw)
    dlv = delta.reshape(ln, dil * aw)

    def body(qp, qc, qn, kp, kc, kn, vp, vc, vn, dop, doc, don, lp, lc, lnx, dp, dc, dn, b1_ref, b2_ref,
             dq_ref, dk_ref, dv_ref, db_ref):
        n = pl.program_id(1)

        @pl.when((pl.program_id(0) == 0) & (n == 0))
        def _():
            db_ref[...] = jnp.zeros_like(db_ref)

        ii = lax.broadcasted_iota(jnp.int32, (QBLOCK, 2 * QBLOCK), 0)
        jj = lax.broadcasted_iota(jnp.int32, (QBLOCK, 2 * QBLOCK), 1)
        kpos = n * QBLOCK + jj - DIL_HALF
        valid = (jnp.abs(jj - DIL_HALF - ii) <= DIL_HALF) & (kpos >= 0) & (kpos < ln)
        ww = lax.broadcasted_iota(jnp.int32, (2 * QBLOCK, QBLOCK), 0)
        cc = lax.broadcasted_iota(jnp.int32, (2 * QBLOCK, QBLOCK), 1)
        qpos = n * QBLOCK - DIL_HALF + ww
        valid2 = (jnp.abs(cc + DIL_HALF - ww) <= DIL_HALF) & (qpos >= 0) & (qpos < ln)
        for h in range(ah):
            cs = pl.ds(h * HEAD_DIM, HEAD_DIM)
            kw, vw = _window(kp, kc, kn, cs), _window(vp, vc, vn, cs)
            sc = _nt(qc[:, cs], kw) * scale + b1_ref[h]
            lse2 = jnp.concatenate([lc[:, cs], lc[:, cs]], axis=1)
            p = jnp.where(valid, jnp.exp(jnp.where(valid, sc - lse2, 0.0)), 0.0)
            ds = p * (_nt(doc[:, cs], vw) - jnp.concatenate([dc[:, cs], dc[:, cs]], axis=1))
            dq_ref[:, cs] = _nn(ds.astype(BF16), kw) * scale
            db_ref[h] += ds
            qw, dow = _window(qp, qc, qn, cs), _window(dop, doc, don, cs)
            sc2 = _nt(qw, kc[:, cs]) * scale + b2_ref[h]
            p2 = jnp.where(valid2, jnp.exp(jnp.where(valid2, sc2 - _window(lp, lc, lnx, cs), 0.0)), 0.0)
            dv_ref[:, cs] = _tn(p2.astype(BF16), dow)
            ds2 = p2 * (_nt(dow, vc[:, cs]) - _window(dp, dc, dn, cs))
            dk_ref[:, cs] = _tn(ds2.astype(BF16), qw) * scale

    ospec = pl.BlockSpec((QBLOCK, aw), lambda r, n: (n, r))
    dq, dk, dv, db = pl.pallas_call(
        body, grid=(dil, nb),
        in_specs=[*_dil_specs(aw, wb, 0, nb), *_dil_specs(aw, wb, 1, nb), *_dil_specs(aw, wb, 2, nb),
                  *_dil_specs(aw, wd, 0, nb), *_dil_specs(aw, 1, 0, nb), *_dil_specs(aw, 1, 0, nb),
                  pl.BlockSpec((ah, QBLOCK, 2 * QBLOCK), lambda r, n: (0, 0, 0)),
                  pl.BlockSpec((ah, 2 * QBLOCK, QBLOCK), lambda r, n: (0, 0, 0))],
        out_specs=[ospec, ospec, ospec, pl.BlockSpec((ah, QBLOCK, 2 * QBLOCK), lambda r, n: (0, 0, 0))],
        out_shape=[jax.ShapeDtypeStruct((ln, dil * aw), F32)] * 3 + [jax.ShapeDtypeStruct((ah, QBLOCK, 2 * QBLOCK), F32)],
        compiler_params=_cparams(("arbitrary", "arbitrary")), name=name,
    )(pv, pv, pv, pv, pv, pv, pv, pv, pv, dov, dov, dov, lv, lv, lv, dlv, dlv, dlv, bias1, bias2)
    return dq.reshape(s, aw), dk.reshape(s, aw), dv.reshape(s, aw), db


def _strided_specs(rows, col_block, nsb):
    half = rows // 2
    return [pl.BlockSpec((half, HEAD_DIM), lambda h, n: (jnp.clip(2 * n - 1, 0, 2 * nsb - 1), col_block + h)),
            pl.BlockSpec((rows, HEAD_DIM), lambda h, n: (n, col_block + h)),
            pl.BlockSpec((half, HEAD_DIM), lambda h, n: (jnp.clip(2 * n + 2, 0, 2 * nsb - 1), col_block + h))]


def _fill_window(dst, p, c, n):
    half, rows = p.shape[0], c.shape[0]
    dst[pl.ds(0, half), :] = p[...].astype(F32)
    dst[pl.ds(half, rows), :] = c[...].astype(F32)
    dst[pl.ds(half + rows, half), :] = n[...].astype(F32)


def _dil_masks(n, ln):
    ii = lax.broadcasted_iota(jnp.int32, (QBLOCK, 2 * QBLOCK), 0)
    jj = lax.broadcasted_iota(jnp.int32, (QBLOCK, 2 * QBLOCK), 1)
    kpos = n * QBLOCK + jj - DIL_HALF
    valid = (jnp.abs(jj - DIL_HALF - ii) <= DIL_HALF) & (kpos >= 0) & (kpos < ln)
    ww = lax.broadcasted_iota(jnp.int32, (2 * QBLOCK, QBLOCK), 0)
    cc = lax.broadcasted_iota(jnp.int32, (2 * QBLOCK, QBLOCK), 1)
    qpos = n * QBLOCK - DIL_HALF + ww
    valid2 = (jnp.abs(cc + DIL_HALF - ww) <= DIL_HALF) & (qpos >= 0) & (qpos < ln)
    return valid, valid2


def _dil_fwd_strided(proj, bias1, dil, ah, name):
    s, wtot = proj.shape
    ln = s // dil
    nsb = ln // QBLOCK
    assert nsb * QBLOCK * dil == s
    aw = ah * HEAD_DIM
    rows = QBLOCK * dil
    half = rows // 2
    scale = 1.0 / math.sqrt(HEAD_DIM)

    def body(q_ref, kp, kc, kn, vp, vc, vn, b_ref, o_ref, lse_ref, qf, kf, vf, of):
        valid, _ = _dil_masks(pl.program_id(1), ln)
        qf[...] = q_ref[...].astype(F32)
        _fill_window(kf, kp, kc, kn)
        _fill_window(vf, vp, vc, vn)
        bias = b_ref[...]

        def residue(r, carry):
            qs = pl.ds(r, QBLOCK, stride=dil)
            ws = pl.ds(r, 2 * QBLOCK, stride=dil)
            kw, vw = kf[ws, :].astype(BF16), vf[ws, :].astype(BF16)
            sc = jnp.where(valid, _nt(qf[qs, :].astype(BF16), kw) * scale + bias, NEG_INF)
            m = jnp.max(sc, axis=-1, keepdims=True)
            p = jnp.exp(sc - m)
            l = jnp.sum(p, axis=-1, keepdims=True)
            of[qs, :] = _nn(p.astype(BF16), vw) / l
            lse_ref[qs, :] = jnp.broadcast_to(m + jnp.log(l), (QBLOCK, HEAD_DIM))
            return carry

        lax.fori_loop(0, dil, residue, 0, unroll=4)
        o_ref[...] = of[...].astype(BF16)

    ospec = pl.BlockSpec((rows, HEAD_DIM), lambda h, n: (n, h))
    return pl.pallas_call(
        body, grid=(ah, nsb),
        in_specs=[_strided_specs(rows, 0, nsb)[1], *_strided_specs(rows, ah, nsb), *_strided_specs(rows, 2 * ah, nsb),
                  pl.BlockSpec((None, QBLOCK, 2 * QBLOCK), lambda h, n: (h, 0, 0))],
        out_specs=[ospec, ospec],
        out_shape=[jax.ShapeDtypeStruct((s, aw), BF16), jax.ShapeDtypeStruct((s, aw), F32)],
        scratch_shapes=[pltpu.VMEM((rows, HEAD_DIM), F32), pltpu.VMEM((2 * rows, HEAD_DIM), F32),
                        pltpu.VMEM((2 * rows, HEAD_DIM), F32), pltpu.VMEM((rows, HEAD_DIM), F32)],
        compiler_params=_cparams(("parallel", "parallel")), name=name,
    )(proj, proj, proj, proj, proj, proj, proj, bias1)


def _dil_bwd_strided(proj, dmerged, lse, delta, bias1, bias2, dil, ah, name, acc, out_dtype):
    s, wtot = proj.shape
    ln = s // dil
    nsb = ln // QBLOCK
    aw = ah * HEAD_DIM
    rows = QBLOCK * dil
    half = rows // 2
    scale = 1.0 / math.sqrt(HEAD_DIM)
    center = slice(DIL_HALF, DIL_HALF + QBLOCK)

    def body(qp, qc, qn, kp, kc, kn, vp, vc, vn, dop, doc, don, lp, lc, lnx, dp, dc, dn, b1_ref, b2_ref, aq, ak, av,
             dq_ref, dk_ref, dv_ref, db_ref, qf, kf, vf, dof, dq_s, dk_s, dv_s):
        n = pl.program_id(1)

        @pl.when(n == 0)
        def _():
            db_ref[...] = jnp.zeros_like(db_ref)

        valid, valid2 = _dil_masks(n, ln)
        _fill_window(qf, qp, qc, qn)
        _fill_window(kf, kp, kc, kn)
        _fill_window(vf, vp, vc, vn)
        _fill_window(dof, dop, doc, don)
        b1, b2 = b1_ref[...], b2_ref[...]

        def stat_window(p, c, nx, r):
            return jnp.concatenate([p[pl.ds(r, DIL_HALF, stride=dil), :], c[pl.ds(r, QBLOCK, stride=dil), :],
                                    nx[pl.ds(r, DIL_HALF, stride=dil), :]], axis=0)

        def residue(r, carry):
            ws = pl.ds(r, 2 * QBLOCK, stride=dil)
            os = pl.ds(r, QBLOCK, stride=dil)
            qw, kw, vw, dow = (t[ws, :].astype(BF16) for t in (qf, kf, vf, dof))
            q, k, v, do = qw[center], kw[center], vw[center], dow[center]
            lse_w, delta_w = stat_window(lp, lc, lnx, r), stat_window(dp, dc, dn, r)
            lse_c, delta_c = lse_w[center], delta_w[center]
            sc = _nt(q, kw) * scale + b1
            p = jnp.where(valid, jnp.exp(jnp.where(valid, sc - jnp.concatenate([lse_c, lse_c], axis=1), 0.0)), 0.0)
            ds = p * (_nt(do, vw) - jnp.concatenate([delta_c, delta_c], axis=1))
            dq_s[os, :] = _nn(ds.astype(BF16), kw) * scale
            db_ref[...] += ds
            sc2 = _nt(qw, k) * scale + b2
            p2 = jnp.where(valid2, jnp.exp(jnp.where(valid2, sc2 - lse_w, 0.0)), 0.0)
            dv_s[os, :] = _tn(p2.astype(BF16), dow)
            ds2 = p2 * (_nt(dow, v) - delta_w)
            dk_s[os, :] = _tn(ds2.astype(BF16), qw) * scale
            return carry

        lax.fori_loop(0, dil, residue, 0, unroll=4)
        dq_ref[...] = (dq_s[...] + aq[...]).astype(dq_ref.dtype)
        dk_ref[...] = (dk_s[...] + ak[...]).astype(dk_ref.dtype)
        dv_ref[...] = (dv_s[...] + av[...]).astype(dv_ref.dtype)

    ospec = pl.BlockSpec((rows, HEAD_DIM), lambda h, n: (n, h))
    win = pltpu.VMEM((2 * rows, HEAD_DIM), F32)
    blk = pltpu.VMEM((rows, HEAD_DIM), F32)
    return pl.pallas_call(
        body, grid=(ah, nsb),
        in_specs=[*_strided_specs(rows, 0, nsb), *_strided_specs(rows, ah, nsb), *_strided_specs(rows, 2 * ah, nsb),
                  *_strided_specs(rows, 0, nsb), *_strided_specs(rows, 0, nsb), *_strided_specs(rows, 0, nsb),
                  pl.BlockSpec((None, QBLOCK, 2 * QBLOCK), lambda h, n: (h, 0, 0)),
                  pl.BlockSpec((None, 2 * QBLOCK, QBLOCK), lambda h, n: (h, 0, 0)), ospec, ospec, ospec],
        out_specs=[ospec, ospec, ospec, pl.BlockSpec((None, QBLOCK, 2 * QBLOCK), lambda h, n: (h, 0, 0))],
        out_shape=[jax.ShapeDtypeStruct((s, aw), out_dtype)] * 3 + [jax.ShapeDtypeStruct((ah, QBLOCK, 2 * QBLOCK), F32)],
        scratch_shapes=[win, win, win, win, blk, blk, blk],
        compiler_params=_cparams(("parallel", "arbitrary")), name=name,
    )(proj, proj, proj, proj, proj, proj, proj, proj, proj, dmerged, dmerged, dmerged, lse, lse, lse,
      delta, delta, delta, bias1, bias2, *acc)


def _bucket_sum(vals, onehot, name):
    r, n = vals.shape
    b = onehot.shape[1]

    def body(v_ref, oh_ref, o_ref):
        o_ref[...] = lax.dot_general(v_ref[...], oh_ref[...], (((1,), (0,)), ((), ())),
                                     precision=lax.Precision.HIGHEST, preferred_element_type=F32)

    tr = max(t for t in range(SUBLANES, 257, SUBLANES) if r % t == 0)
    return pl.pallas_call(
        body, grid=(r // tr,),
        in_specs=[pl.BlockSpec((tr, n), lambda i: (i, 0)), pl.BlockSpec((n, b), lambda i: (0, 0))],
        out_specs=pl.BlockSpec((tr, b), lambda i: (i, 0)), out_shape=jax.ShapeDtypeStruct((r, b), F32),
        compiler_params=_cparams(("parallel",)), name=name)(vals, onehot)


def _t5_grad(dbias, dil):
    ah = dbias.shape[0]
    off1, _ = _dil_offsets()
    bucket = _t5_bucket(off1 * dil).reshape(-1)
    inside = (jnp.abs(off1) <= DIL_HALF).reshape(-1)
    onehot = ((bucket[:, None] == jnp.arange(LANES)[None, :]) & inside[:, None]).astype(F32)
    vals = jnp.pad(dbias.reshape(ah, -1), ((0, -ah % SUBLANES), (0, 0)))
    return _bucket_sum(vals, onehot, f"t5_grad_d{dil}")[:ah, :T5_BUCKETS].T


def _na_table_rows():
    ro = -np.ones((3, NA_GROUP, NA_WIN), np.int64)
    for i in range(NA_GROUP):
        for j in range(NA_WIN):
            if j < NA_ROWS:
                ro[0, i, j] = j - i + NA_ROWS - 1
            if i <= j < i + NA_ROWS:
                ro[1, i, j] = j - i + NA_ROWS // 2 - 1
            if j >= NA_WIN - NA_ROWS:
                ro[2, i, j] = j - i
    return ro


def _na_bias(rpb):
    ch, nro, nco = rpb.shape
    c = np.arange(GRID_W)
    col_start = np.clip(c - NA_COLS // 2, 0, GRID_W - NA_COLS)
    col_ok = (c[None, :] >= col_start[:, None]) & (c[None, :] < col_start[:, None] + NA_COLS)
    col_idx = np.clip(c[None, :] - c[:, None] + NA_COLS - 1, 0, 2 * NA_COLS - 2).reshape(-1)
    onehot = (np.arange(LANES)[:, None] == col_idx[None, :]).astype(np.float32)
    table = jnp.pad(rpb.astype(F32).reshape(ch * nro, nco), ((0, -(ch * nro) % SUBLANES), (0, LANES - nco)))
    by_row = _expand(table, jnp.asarray(onehot), "rpb_bias", tn=GRID_W * GRID_W)[:ch * nro]
    by_row = jnp.where(jnp.asarray(col_ok.reshape(-1))[None, :], by_row, NEG_INF).reshape(ch, nro, GRID_W, GRID_W)
    neg = jnp.full((ch, GRID_W, GRID_W), NEG_INF, F32)
    tiles = [by_row[:, r] if r >= 0 else neg for r in _na_table_rows().reshape(-1)]
    b = jnp.stack(tiles, axis=1).reshape(ch, 3, NA_GROUP, NA_WIN, GRID_W, GRID_W)
    return jnp.transpose(b, (0, 1, 2, 4, 3, 5)).reshape(ch, 3, NA_GROUP * GRID_W, NA_WIN * GRID_W)


def _na_group(g, rows):
    ngroups = rows // NA_GROUP
    ws = jnp.clip(g * NA_GROUP - NA_ROWS // 2, 0, rows - NA_WIN)
    return pl.multiple_of(ws * GRID_W, GRID_W), jnp.where(g == 0, 0, jnp.where(g == ngroups - 1, 2, 1))


def _na_fwd(qkv, bias, ch, name):
    s = qkv.shape[0]
    rows = s // GRID_W
    assert rows >= NA_WIN and rows % (NA_GROUP * NA_GROUPS_PER_STEP) == 0
    cw = ch * HEAD_DIM
    tg = NA_GROUP * GRID_W
    tq = NA_GROUPS_PER_STEP * tg
    win = NA_WIN * GRID_W
    scale = 1.0 / math.sqrt(HEAD_DIM)

    def body(q_ref, k_ref, v_ref, b_ref, o_ref, lse_ref):
        gb = pl.program_id(1)
        for i in range(NA_GROUPS_PER_STEP):
            st, var = _na_group(gb * NA_GROUPS_PER_STEP + i, rows)
            kw, vw = k_ref[pl.ds(st, win), :], v_ref[pl.ds(st, win), :]
            qs = pl.ds(i * tg, tg)
            sc = _nt(q_ref[qs, :], kw) * scale + b_ref[var]
            m = jnp.max(sc, axis=-1, keepdims=True)
            p = jnp.exp(sc - m)
            l = jnp.sum(p, axis=-1, keepdims=True)
            o_ref[qs, :] = (_nn(p.astype(BF16), vw) / l).astype(BF16)
            lse_ref[qs, :] = jnp.broadcast_to(m + jnp.log(l), (tg, HEAD_DIM))

    ospec = pl.BlockSpec((tq, HEAD_DIM), lambda h, gb: (gb, h))
    return pl.pallas_call(
        body, grid=(ch, s // tq),
        in_specs=[pl.BlockSpec((tq, HEAD_DIM), lambda h, gb: (gb, h)),
                  pl.BlockSpec((s, HEAD_DIM), lambda h, gb: (0, ch + h)),
                  pl.BlockSpec((s, HEAD_DIM), lambda h, gb: (0, 2 * ch + h)),
                  pl.BlockSpec((None, 3, tg, win), lambda h, gb: (h, 0, 0, 0))],
        out_specs=[ospec, ospec],
        out_shape=[jax.ShapeDtypeStruct((s, cw), BF16), jax.ShapeDtypeStruct((s, cw), F32)],
        compiler_params=_cparams(("parallel", "parallel")), name=name)(qkv, qkv, qkv, bias)


def _na_bwd(qkv, o, do, lse, bias, ch, name):
    s = qkv.shape[0]
    rows = s // GRID_W
    cw = ch * HEAD_DIM
    tg = NA_GROUP * GRID_W
    tq = NA_GROUPS_PER_STEP * tg
    win = NA_WIN * GRID_W
    scale = 1.0 / math.sqrt(HEAD_DIM)

    def body(q_ref, k_ref, v_ref, o_ref, do_ref, lse_ref, b_ref, dq_ref, dk_ref, dv_ref, db_ref):
        gb = pl.program_id(1)

        @pl.when(gb == 0)
        def _():
            dk_ref[...] = jnp.zeros_like(dk_ref)
            dv_ref[...] = jnp.zeros_like(dv_ref)
            db_ref[...] = jnp.zeros_like(db_ref)

        for i in range(NA_GROUPS_PER_STEP):
            st, var = _na_group(gb * NA_GROUPS_PER_STEP + i, rows)
            ws = pl.ds(st, win)
            kw, vw = k_ref[ws, :], v_ref[ws, :]
            qs = pl.ds(i * tg, tg)
            q, dov = q_ref[qs, :], do_ref[qs, :]
            sc = _nt(q, kw) * scale + b_ref[var]
            p = jnp.exp(sc - lse_ref[qs, :][:, :1])
            delta = jnp.sum(dov.astype(F32) * o_ref[qs, :].astype(F32), axis=-1, keepdims=True)
            ds = p * (_nt(dov, vw) - delta)
            dsb = ds.astype(BF16)
            dq_ref[qs, :] = (_nn(dsb, kw) * scale).astype(BF16)
            dk_ref[ws, :] += _tn(dsb, q) * scale
            dv_ref[ws, :] += _tn(p.astype(BF16), dov)
            db_ref[var] += ds

    qspec = pl.BlockSpec((tq, HEAD_DIM), lambda h, gb: (gb, h))
    kvspec = pl.BlockSpec((s, HEAD_DIM), lambda h, gb: (0, h))
    bspec = pl.BlockSpec((None, 3, tg, win), lambda h, gb: (h, 0, 0, 0))
    return pl.pallas_call(
        body, grid=(ch, s // tq),
        in_specs=[qspec, pl.BlockSpec((s, HEAD_DIM), lambda h, gb: (0, ch + h)),
                  pl.BlockSpec((s, HEAD_DIM), lambda h, gb: (0, 2 * ch + h)), qspec, qspec, qspec, bspec],
        out_specs=[qspec, kvspec, kvspec, bspec],
        out_shape=[jax.ShapeDtypeStruct((s, cw), BF16), jax.ShapeDtypeStruct((s, cw), F32),
                   jax.ShapeDtypeStruct((s, cw), F32), jax.ShapeDtypeStruct((ch, 3, tg, win), F32)],
        compiler_params=_cparams(("parallel", "arbitrary")), name=name)(qkv, qkv, qkv, o, do, lse, bias)


def _rpb_grad(dbias):
    ch = dbias.shape[0]
    ntile = 3 * NA_GROUP * NA_WIN
    c = np.arange(GRID_W)
    col_idx = (c[None, :] - c[:, None] + NA_COLS - 1).reshape(-1)
    oh_col = (col_idx[:, None] == np.arange(LANES)[None, :]).astype(np.float32)
    d6 = dbias.reshape(ch, 3, NA_GROUP, GRID_W, NA_WIN, GRID_W)
    vals = jnp.transpose(d6, (0, 1, 2, 4, 3, 5)).reshape(ch * ntile, GRID_W * GRID_W)
    by_col = _bucket_sum(vals, jnp.asarray(oh_col), "rpb_grad_cols")
    npad = 2 * LANES
    oh_row = np.zeros((npad, LANES), np.float32)
    for t, r in enumerate(_na_table_rows().reshape(-1)):
        if r >= 0:
            oh_row[t, r] = 1.0
    by_col = jnp.pad(by_col.reshape(ch, ntile, LANES), ((0, 0), (0, npad - ntile), (0, 0)))
    vals2 = jnp.transpose(by_col, (0, 2, 1)).reshape(ch * LANES, npad)
    by_row = _bucket_sum(vals2, jnp.asarray(oh_row), "rpb_grad_rows")
    return jnp.transpose(by_row.reshape(ch, LANES, LANES), (0, 2, 1))[:, :2 * NA_ROWS - 1, :2 * NA_COLS - 1]


def _s5_discretize(lam_re, lam_im, log_step, b_re, b_im):
    step = jnp.exp(log_step.astype(F32))[:, None]
    lr = jnp.minimum(lam_re.astype(F32), -1e-4)
    li = lam_im.astype(F32)
    mag = jnp.exp(lr * step)
    ab_re = mag * jnp.cos(li * step)
    ab_im = mag * jnp.sin(li * step)
    den = lr * lr + li * li
    zr = ((ab_re - 1.0) * lr + ab_im * li) / den
    zi = (ab_im * lr - (ab_re - 1.0) * li) / den
    br = b_re.astype(F32)
    bi = b_im.astype(F32)
    return ab_re, ab_im, zr[..., None] * br - zi[..., None] * bi, zr[..., None] * bi + zi[..., None] * br


def _scan_tables(a_re, a_im, rev):
    ar, ai = a_re.reshape(-1), a_im.reshape(-1)
    pows = [(ar, ai)]
    for _ in range(SUBLANES - 1):
        pr, pi = pows[-1]
        pows.append((pr * ar - pi * ai, pr * ai + pi * ar))
    row = jnp.arange(SUBLANES)[:, None]
    tabs = []
    for k in (1, 2, 4):
        keep = (row < SUBLANES - k) if rev else (row >= k)
        tabs += [jnp.where(keep, pows[k - 1][0][None, :], 0.0), jnp.where(keep, pows[k - 1][1][None, :], 0.0)]
    order = list(range(SUBLANES - 1, -1, -1)) if rev else list(range(SUBLANES))
    tabs += [jnp.stack([pows[i][0] for i in order]), jnp.stack([pows[i][1] for i in order])]
    t = jnp.stack(tabs)
    nblk = t.shape[-1] // (4 * LANES)
    return jnp.transpose(t.reshape(8, SUBLANES, nblk, 4 * LANES), (2, 0, 1, 3))


def _block_diag(w):
    g, a, b = w.shape
    nblk = g // S5_GROUPS_PER_BLOCK
    eye = jnp.eye(S5_GROUPS_PER_BLOCK, dtype=w.dtype)
    w4 = w.reshape(nblk, S5_GROUPS_PER_BLOCK, a, b)
    return (w4[:, :, :, None, :] * eye[None, :, None, :, None]).reshape(nblk, S5_GROUPS_PER_BLOCK * a, S5_GROUPS_PER_BLOCK * b)


def _block_diag_take(w, a, b):
    nblk = w.shape[0]
    w5 = w.reshape(nblk, S5_GROUPS_PER_BLOCK, a, S5_GROUPS_PER_BLOCK, b)
    eye = jnp.eye(S5_GROUPS_PER_BLOCK, dtype=w.dtype)
    return jnp.sum(w5 * eye[None, :, None, :, None], axis=3).reshape(nblk * S5_GROUPS_PER_BLOCK, a, b)


def _scan_tile(r, i, tab_ref, carry, rev):
    for lvl, k in enumerate((1, 2, 4)):
        mr, mi = tab_ref[2 * lvl], tab_ref[2 * lvl + 1]
        sh = SUBLANES - k if rev else k
        rr, ri = pltpu.roll(r, sh, 0), pltpu.roll(i, sh, 0)
        r, i = r + (mr * rr - mi * ri), i + (mr * ri + mi * rr)
    pr, pi = tab_ref[6], tab_ref[7]
    cr, ci = carry
    return r + (pr * cr - pi * ci), i + (pr * ci + pi * cr)


def _s5_fwd(proj, ucol0, tabs, bre, bim, cre, cim, rev, final, name):
    s = proj.shape[0]
    nblk = tabs.shape[0]
    bw = nblk * LANES
    w = 4 * LANES
    t = _tile(s, S5_CHUNK)
    nc, nt = s // t, t // SUBLANES
    ub = ucol0 // LANES
    cm = (lambda c: nc - 1 - c) if rev else (lambda c: c)
    last = 0 if rev else SUBLANES - 1

    def body(u_ref, tab_ref, bre_ref, bim_ref, cre_ref, cim_ref, *rest):
        if final is not None:
            yo_ref, d_ref, y_ref, xr_ref, xi_ref, xr_s, xi_s, car_r, car_i = rest
        else:
            y_ref, xr_ref, xi_ref, xr_s, xi_s, car_r, car_i = rest

        @pl.when(pl.program_id(1) == 0)
        def _():
            car_r[...] = jnp.zeros_like(car_r)
            car_i[...] = jnp.zeros_like(car_i)

        u = u_ref[...]
        xr_s[...] = _nn(u, bre_ref[...])
        xi_s[...] = _nn(u, bim_ref[...])

        def tile(tt, carry):
            k = nt - 1 - tt if rev else tt
            rows = pl.ds(pl.multiple_of(k * SUBLANES, SUBLANES), SUBLANES)
            r, i = _scan_tile(xr_s[rows, :], xi_s[rows, :], tab_ref, carry, rev)
            xr_s[rows, :] = r
            xi_s[rows, :] = i
            return (jnp.broadcast_to(r[last:last + 1, :], r.shape), jnp.broadcast_to(i[last:last + 1, :], i.shape))

        carry = lax.fori_loop(0, nt, tile, (car_r[...], car_i[...]))
        car_r[...], car_i[...] = carry
        xr, xi = xr_s[...].astype(BF16), xi_s[...].astype(BF16)
        y = _nn(xr, cre_ref[...]) - _nn(xi, cim_ref[...])
        if final is not None:
            y = y + yo_ref[...] + d_ref[...] * u.astype(F32)
        y_ref[...] = y
        xr_ref[...] = xr
        xi_ref[...] = xi

    yspec = pl.BlockSpec((t, LANES), lambda j, c: (cm(c), j))
    xspec = pl.BlockSpec((t, w), lambda j, c: (cm(c), j))
    in_specs = [pl.BlockSpec((t, LANES), lambda j, c: (cm(c), ub + j)),
                pl.BlockSpec((None, 8, SUBLANES, w), lambda j, c: (j, 0, 0, 0)),
                pl.BlockSpec((None, LANES, w), lambda j, c: (j, 0, 0)), pl.BlockSpec((None, LANES, w), lambda j, c: (j, 0, 0)),
                pl.BlockSpec((None, w, LANES), lambda j, c: (j, 0, 0)), pl.BlockSpec((None, w, LANES), lambda j, c: (j, 0, 0))]
    args = [proj, tabs, bre, bim, cre, cim]
    if final is not None:
        in_specs += [yspec, pl.BlockSpec((1, LANES), lambda j, c: (0, j))]
        args += [final[0], final[1].reshape(1, bw)]
    return pl.pallas_call(
        body, grid=(nblk, nc), in_specs=in_specs, out_specs=[yspec, xspec, xspec],
        out_shape=[jax.ShapeDtypeStruct((s, bw), F32), jax.ShapeDtypeStruct((s, nblk * w), BF16),
                   jax.ShapeDtypeStruct((s, nblk * w), BF16)],
        scratch_shapes=[pltpu.VMEM((t, w), F32), pltpu.VMEM((t, w), F32), pltpu.VMEM((SUBLANES, w), F32),
                        pltpu.VMEM((SUBLANES, w), F32)],
        compiler_params=_cparams(("parallel", "arbitrary")), name=name)(*args)


def _s5_bwd(proj, ucol0, dy, xr, xi, gtabs, bre, bim, cre, cim, rev, final, name):
    s = proj.shape[0]
    nblk = gtabs.shape[0]
    bw = nblk * LANES
    w = 4 * LANES
    t = _tile(s, S5_CHUNK)
    nc, nt = s // t, t // SUBLANES
    ub = ucol0 // LANES
    grev = not rev
    cm = (lambda c: nc - 1 - c) if grev else (lambda c: c)
    last = 0 if grev else SUBLANES - 1
    nfin = 2 if final is not None else 0

    def body(dy_ref, u_ref, xr_ref, xi_ref, tab_ref, bre_ref, bim_ref, cre_ref, cim_ref, *rest):
        fin, rest = rest[:nfin], rest[nfin:]
        du_ref, dar_ref, dai_ref, dbr_ref, dbi_ref, dcr_ref, dci_ref = rest[:7]
        rest = rest[7:]
        if final is not None:
            dd_ref, rest = rest[0], rest[1:]
        gr_s, gi_s, xr_s, xi_s, car_r, car_i = rest

        @pl.when(pl.program_id(1) == 0)
        def _():
            for ref in (car_r, car_i, dar_ref, dai_ref, dbr_ref, dbi_ref, dcr_ref, dci_ref):
                ref[...] = jnp.zeros_like(ref)
            if final is not None:
                dd_ref[...] = jnp.zeros_like(dd_ref)

        dyv = dy_ref[...]
        dyb = dyv.astype(BF16)
        u = u_ref[...]
        xrb, xib = xr_ref[...], xi_ref[...]
        gr_s[...] = _nt(dyb, cre_ref[...])
        gi_s[...] = -_nt(dyb, cim_ref[...])
        xr_s[...] = xrb.astype(F32)
        xi_s[...] = xib.astype(F32)
        rowid = lax.broadcasted_iota(jnp.int32, (SUBLANES, w), 0)

        def tile(tt, carry):
            k = nt - 1 - tt if grev else tt
            rows = pl.ds(pl.multiple_of(k * SUBLANES, SUBLANES), SUBLANES)
            r, i = _scan_tile(gr_s[rows, :], gi_s[rows, :], tab_ref, carry, grev)
            gr_s[rows, :] = r
            gi_s[rows, :] = i
            if grev:
                er = jnp.where(rowid == SUBLANES - 1, carry[0], pltpu.roll(r, SUBLANES - 1, 0))
                ei = jnp.where(rowid == SUBLANES - 1, carry[1], pltpu.roll(i, SUBLANES - 1, 0))
            else:
                er = jnp.where(rowid == 0, carry[0], pltpu.roll(r, 1, 0))
                ei = jnp.where(rowid == 0, carry[1], pltpu.roll(i, 1, 0))
            sr, si = xr_s[rows, :], xi_s[rows, :]
            dar_ref[...] += er * sr + ei * si
            dai_ref[...] += ei * sr - er * si
            return (jnp.broadcast_to(r[last:last + 1, :], r.shape), jnp.broadcast_to(i[last:last + 1, :], i.shape))

        carry = lax.fori_loop(0, nt, tile, (car_r[...], car_i[...]))
        car_r[...], car_i[...] = carry
        gr, gi = gr_s[...].astype(BF16), gi_s[...].astype(BF16)
        du = _nt(gr, bre_ref[...]) + _nt(gi, bim_ref[...])
        if final is not None:
            du = du + fin[0][...] + fin[1][...] * dyv.astype(F32)
            dd_ref[...] += jnp.sum(dyv.astype(F32) * u.astype(F32), axis=0, keepdims=True)
        du_ref[...] = du.astype(du_ref.dtype)
        dbr_ref[...] += _tn(u, gr)
        dbi_ref[...] += _tn(u, gi)
        dcr_ref[...] += _tn(xrb, dyb)
        dci_ref[...] -= _tn(xib, dyb)

    yspec = pl.BlockSpec((t, LANES), lambda j, c: (cm(c), j))
    xspec = pl.BlockSpec((t, w), lambda j, c: (cm(c), j))
    bspec = pl.BlockSpec((None, LANES, w), lambda j, c: (j, 0, 0))
    cspec = pl.BlockSpec((None, w, LANES), lambda j, c: (j, 0, 0))
    aspec = pl.BlockSpec((None, SUBLANES, w), lambda j, c: (j, 0, 0))
    dspec = pl.BlockSpec((1, LANES), lambda j, c: (0, j))
    in_specs = [yspec, pl.BlockSpec((t, LANES), lambda j, c: (cm(c), ub + j)), xspec, xspec,
                pl.BlockSpec((None, 8, SUBLANES, w), lambda j, c: (j, 0, 0, 0)), bspec, bspec, cspec, cspec]
    args = [dy, proj, xr, xi, gtabs, bre, bim, cre, cim]
    out_specs = [yspec, aspec, aspec, bspec, bspec, cspec, cspec]
    du_dtype = BF16 if final is not None else F32
    out_shape = [jax.ShapeDtypeStruct((s, bw), du_dtype)] + [jax.ShapeDtypeStruct((nblk, SUBLANES, w), F32)] * 2 \
        + [jax.ShapeDtypeStruct((nblk, LANES, w), F32)] * 2 + [jax.ShapeDtypeStruct((nblk, w, LANES), F32)] * 2
    if final is not None:
        in_specs += [yspec, dspec]
        args += [final[0], final[1].reshape(1, bw)]
        out_specs.append(dspec)
        out_shape.append(jax.ShapeDtypeStruct((1, bw), F32))
    return pl.pallas_call(
        body, grid=(nblk, nc), in_specs=in_specs, out_specs=out_specs, out_shape=out_shape,
        scratch_shapes=[pltpu.VMEM((t, w), F32)] * 4 + [pltpu.VMEM((SUBLANES, w), F32)] * 2,
        compiler_params=_cparams(("parallel", "arbitrary")), name=name)(*args)


N_CHIPS = 4


def _place():
    mx, my, mc = lax.axis_index("x"), lax.axis_index("y"), lax.axis_index("c")
    return (mx, my, mc), (mx, my, 1 - mc), [(1 - mx, my), (mx, 1 - my), (1 - mx, 1 - my)]


def _gather_op(x, cols=False):
    r, c = x.shape

    def run(which, ins, outs, sems):
        (x_ref,), (out_ref,), (send_sems, recv_sems, local_sem) = ins, outs, sems
        me, sibling, chips = _place()
        mc = me[2]

        def slot(px, py, pc):
            d = 4 * px + 2 * py + pc
            return out_ref.at[:, pl.ds(pl.multiple_of(d * c, LANES), c)] if cols else out_ref.at[d]

        def copy(k, block, to, src=None):
            return pltpu.make_async_remote_copy(src_ref=slot(*block) if src is None else src, dst_ref=slot(*block),
                                                send_sem=send_sems.at[k], recv_sem=recv_sems.at[k],
                                                device_id=to, device_id_type=MESH)

        def mine():
            return pltpu.make_async_copy(x_ref, slot(*me), local_sem)

        def first():
            return [copy(0, me, sibling, src=x_ref)] + [copy(1 + j, me, (*chip, mc), src=x_ref) for j, chip in enumerate(chips)]

        def passed():
            return [copy(4 + j, (*chip, mc), sibling) for j, chip in enumerate(chips)]

        if which == "start":
            mine().start()
            for cp in first():
                cp.start()
        elif which == "mid":
            for j, (chip, cp) in enumerate(zip(chips, passed())):
                copy(1 + j, (*chip, mc), me).wait_recv()
                cp.start()
        else:
            copy(0, sibling, me).wait_recv()
            for j, chip in enumerate(chips):
                copy(4 + j, (*chip, 1 - mc), me).wait_recv()
            for cp in first() + passed():
                cp.wait_send()
            mine().wait()

    return dict(ins=[x], outs=[jax.ShapeDtypeStruct((r, N_DEV * c) if cols else (N_DEV, r, c), x.dtype)], run=run,
                sems=[pltpu.SemaphoreType.DMA((N_DEV - 1,)), pltpu.SemaphoreType.DMA((N_DEV - 1,)),
                      pltpu.SemaphoreType.DMA])


def _pair_op(gs):
    nt = len(gs)

    def run(which, g_refs, out_refs, sems):
        if which == "mid":
            return
        send_sems, recv_sems = sems
        me, sibling, _ = _place()
        copies = [pltpu.make_async_remote_copy(src_ref=g_refs[t].at[2 * q + (1 - me[2])], dst_ref=out_refs[t].at[q],
                                               send_sem=send_sems.at[t, q], recv_sem=recv_sems.at[t, q],
                                               device_id=sibling, device_id_type=MESH)
                  for t in range(nt) for q in range(N_CHIPS)]
        if which == "start":
            for cp in copies:
                cp.start()
        elif which == "wait":
            for cp in copies:
                cp.wait_recv()
            for cp in copies:
                cp.wait_send()

    return dict(ins=list(gs), outs=[jax.ShapeDtypeStruct((N_CHIPS,) + g.shape[1:], F32) for g in gs], run=run,
                sems=[pltpu.SemaphoreType.DMA((nt, N_CHIPS)), pltpu.SemaphoreType.DMA((nt, N_CHIPS))])


def _pair_add(g, t, core, name):
    _, r, c = g.shape
    tr = r
    while tr * c > 512 * 1024 and tr % 32 == 0:
        tr //= 2

    def body(core_ref, g_ref, t_ref, o_ref):
        o_ref[...] = (g_ref[...] + t_ref[...]).astype(BF16)

    return pl.pallas_call(
        body,
        grid_spec=pltpu.PrefetchScalarGridSpec(
            num_scalar_prefetch=1, grid=(N_CHIPS, r // tr),
            in_specs=[pl.BlockSpec((None, tr, c), lambda q, i, core_ref: (2 * q + core_ref[0], i, 0)),
                      pl.BlockSpec((None, tr, c), lambda q, i, core_ref: (q, i, 0))],
            out_specs=pl.BlockSpec((None, tr, c), lambda q, i, core_ref: (q, i, 0))),
        out_shape=jax.ShapeDtypeStruct((N_CHIPS, r, c), BF16),
        compiler_params=_cparams(("parallel", "parallel")), name=name)(core, g, t)


def _chip_op(ps):
    nt = len(ps)

    def run(which, p_refs, out_refs, sems):
        if which == "mid":
            return
        send_sems, recv_sems, local_sems = sems
        me, _, chips = _place()
        mychip = 2 * me[0] + me[1]
        owns = [pltpu.make_async_copy(p_refs[t].at[mychip], out_refs[t].at[mychip], local_sems.at[t]) for t in range(nt)]
        sends = [pltpu.make_async_remote_copy(src_ref=p_refs[t].at[2 * px + py], dst_ref=out_refs[t].at[mychip],
                                              send_sem=send_sems.at[t, j], recv_sem=recv_sems.at[t, j],
                                              device_id=(px, py, me[2]), device_id_type=MESH)
                 for t in range(nt) for j, (px, py) in enumerate(chips)]
        if which == "start":
            for cp in owns + sends:
                cp.start()
        elif which == "wait":
            for t in range(nt):
                for j, (px, py) in enumerate(chips):
                    pltpu.make_async_remote_copy(src_ref=p_refs[t].at[2 * px + py], dst_ref=out_refs[t].at[2 * px + py],
                                                 send_sem=send_sems.at[t, j], recv_sem=recv_sems.at[t, j],
                                                 device_id=(px, py, me[2]), device_id_type=MESH).wait_recv()
            for cp in sends:
                cp.wait_send()
            for cp in owns:
                cp.wait()

    return dict(ins=list(ps), outs=[jax.ShapeDtypeStruct(p.shape, p.dtype) for p in ps], run=run,
                sems=[pltpu.SemaphoreType.DMA((nt, N_CHIPS - 1)), pltpu.SemaphoreType.DMA((nt, N_CHIPS - 1)),
                      pltpu.SemaphoreType.DMA((nt,))])


def _adamw(w, gstack, m, v, name, layer=None, prev=None):
    r, c = w.shape[-2:]
    nstack = gstack.shape[0]
    tr = r
    while tr * c > 128 * 1024 and tr % 32 == 0:
        tr //= 2
    c1 = 1.0 - ADAM_B1 ** ADAM_STEP
    c2 = 1.0 - ADAM_B2 ** ADAM_STEP

    def body(w_ref, g_ref, m_ref, v_ref, *rest):
        go_ref, d_ref, mo_ref, vo_ref = rest[-4:]
        g = g_ref[0].astype(F32)
        for p in range(1, nstack):
            g = g + g_ref[p].astype(F32)
        mn = ADAM_B1 * m_ref[...] + (1.0 - ADAM_B1) * g
        vn = ADAM_B2 * v_ref[...] + (1.0 - ADAM_B2) * (g * g)
        go_ref[...] = g
        mo_ref[...] = mn
        vo_ref[...] = vn
        d_ref[...] = -ADAM_LR * ((mn / c1) / (jnp.sqrt(vn / c2) + ADAM_EPS) + ADAM_WD * w_ref[...])

    if layer is None:
        spec = pl.BlockSpec((tr, c), lambda i: (i, 0))
        full = (r, c)
    else:
        spec = pl.BlockSpec((None, tr, c), lambda i: (layer, i, 0))
        full = w.shape
    prev = list(prev) if prev is not None else []
    return pl.pallas_call(
        body, grid=(r // tr,),
        in_specs=[spec, pl.BlockSpec((nstack, tr, c), lambda i: (0, i, 0)), spec, spec] + [pl.BlockSpec(memory_space=pl.ANY)] * len(prev),
        out_specs=[spec] * 4, out_shape=[jax.ShapeDtypeStruct(full, F32)] * 4,
        input_output_aliases={4 + n: n for n in range(len(prev))},
        compiler_params=_cparams(("parallel",)), name=name)(w, gstack, m, v, *prev)


def _s5_tables(lam_re, lam_im, log_step, b_re, b_im, c_re, c_im):
    out = []
    for d in range(2):
        ab_re, ab_im, bb_re, bb_im = _s5_discretize(lam_re[d], lam_im[d], log_step[d], b_re, b_im)
        rev = d == 1
        out.append(dict(
            tabs=_scan_tables(ab_re, ab_im, rev), gtabs=_scan_tables(ab_re, -ab_im, not rev),
            bre=_block_diag(jnp.transpose(bb_re, (0, 2, 1))).astype(BF16), bim=_block_diag(jnp.transpose(bb_im, (0, 2, 1))).astype(BF16),
            cre=_block_diag(jnp.transpose(c_re[d], (0, 2, 1))).astype(BF16), cim=_block_diag(jnp.transpose(c_im[d], (0, 2, 1))).astype(BF16)))
    return out


def _layer_tensors(i):
    j = i // 2
    mixer = [("ab_w_in", j), ("ab_w_out", j), ("s5_w_glu", j)] if i % 2 == 0 else [("c_w_qkv", j), ("c_w_out", j)]
    return mixer + [("mlp_w1", i), ("mlp_w2", i)]


class _WeightGather:
    def __init__(self, shards, depth):
        self.shards, self.depth, self.ops = shards, depth, {}

    def _op(self, key):
        self.ops[key] = _gather_op(self.shards[key[0]][key[1]], cols=key[0] not in ROW_SHARDED)
        return self.ops[key]

    def start(self):
        _run_comm([self._op(key) for key in _layer_tensors(0)[:-2]], "gather_mixer0")

    def carry(self, i, slot):
        t = _layer_tensors(i)
        nxt = _layer_tensors(i + 1)[:-2] if i + 1 < self.depth else []
        plan = {"in": t[-2:-1], "up": t[-1:], "down": nxt}
        return [self._op(key) for key in plan[slot]]

    def get(self, name, l):
        full = self.ops[(name, l)]["res"][0]
        return full.reshape(-1, full.shape[-1]) if name in ROW_SHARDED else full


class _GradExchange:
    def __init__(self, core, depth):
        self.core, self.depth, self.g, self.recv_ops, self.pairs, self.ps = core, depth, {}, [], {}, {}

    def put(self, name, l, g):
        self.g[(name, l)] = g.reshape(N_DEV, -1, g.shape[-1])

    def _pair(self, keys):
        op = _pair_op([self.g[key] for key in keys])
        for n, key in enumerate(keys):
            self.pairs[key] = (op, n)
        return [op]

    def _chip(self, keys):
        for key in keys:
            op, n = self.pairs[key]
            self.ps[key] = _pair_add(self.g[key], op["res"][n], self.core, f"pair_add_{key[0]}{key[1]}")
        op = _chip_op([self.ps[key] for key in keys])
        self.recv_ops.append((keys, op))
        return [op]

    def carry(self, i, slot):
        t = _layer_tensors(i)
        later = _layer_tensors(i + 1)[:-2] if i + 1 < self.depth else []
        if slot == "up_bwd":
            return self._pair(t[-2:])
        if slot == "in_grad":
            return self._chip(t[-2:-1])
        if slot == "in_bwd":
            return self._chip(t[-1:])
        if slot == "down_bwd":
            return self._pair(later) if later else ()
        return self._chip(later) if later else ()

    def finish(self):
        keys = _layer_tensors(0)[:-2]
        _run_comm(self._pair(keys), "pair_exchange_mixer0")
        _run_comm(self._chip(keys), "chip_exchange_mixer0")
        return {key: op["res"][n] for keys, op in self.recv_ops for n, key in enumerate(keys)}


def _forward_backward(x, target, p, wsrc, gsink):
    s, d = x.shape
    depth = p["norm_mix"].shape[0]
    ah = p["t5_bias"].shape[1]
    aw = ah * HEAD_DIM
    ch = p["c_rpb"].shape[1]
    groups, pstate = p["s5_lam_re"].shape[2:]
    bw = groups * S5_GROUP
    assert aw + bw == d and ch * HEAD_DIM == d

    dil_bias = [_dil_bias(p["t5_bias"], dil) for _, dil in DILATED_BRANCHES]
    saved = []
    for i in range(depth):
        j = i // 2
        sv = dict(x=x)
        hn = _rms_fwd(x, p["norm_mix"][i], f"norm_mix_fwd{i}")
        sv["hn"] = hn
        if i % 2 == 0:
            proj = _mm_cols(f"ab_in_fwd{i}", hn, wsrc.get("ab_w_in", j), comm=wsrc.carry(i, "in"))
            outs = [(_dil_fwd if dil == 1 else _dil_fwd_strided)(proj, dil_bias[b][0], dil, ah, f"dil_fwd_d{dil}_{i}")
                    for b, (_, dil) in enumerate(DILATED_BRANCHES)]
            o_a, lse = _dil_merge([o for o, _ in outs], [l for _, l in outs])
            tb = _s5_tables(p["s5_lam_re"][j], p["s5_lam_im"][j], p["s5_log_step"][j], p["s5_b_re"][j], p["s5_b_im"][j],
                            p["s5_c_re"][j], p["s5_c_im"][j])
            y0, x0r, x0i = _s5_fwd(proj, 3 * aw, tb[0]["tabs"], tb[0]["bre"], tb[0]["bim"], tb[0]["cre"], tb[0]["cim"],
                                   False, None, f"s5_fwd_a{i}")
            y_pre, x1r, x1i = _s5_fwd(proj, 3 * aw, tb[1]["tabs"], tb[1]["bre"], tb[1]["bim"], tb[1]["cre"], tb[1]["cim"],
                                      True, (y0, p["s5_d"][j]), f"s5_fwd_b{i}")
            o_b = _mm(f"glu_fwd{i}", y_pre, wsrc.get("s5_w_glu", j), a_fn=_gelu, extras=(y_pre,), out_dtypes=(BF16,),
                      epi=lambda acc, yp: (_gelu(yp) * jax.nn.sigmoid(acc),))[0]
            merged = jnp.concatenate([o_a, o_b], axis=1)
            x = _mm(f"ab_out_fwd{i}", merged, wsrc.get("ab_w_out", j), extras=(x,), epi=lambda acc, xr: (acc + xr,))[0]
            sv.update(proj=proj, o_a=o_a, lse=lse, tb=tb, states=((x0r, x0i), (x1r, x1i)), y_pre=y_pre, merged=merged)
        else:
            qkv = _mm_cols(f"c_qkv_fwd{i}", hn, wsrc.get("c_w_qkv", j), comm=wsrc.carry(i, "in"))
            nbias = _na_bias(p["c_rpb"][j])
            o, lse = _na_fwd(qkv, nbias, ch, f"na_fwd{i}")
            x = _mm(f"c_out_fwd{i}", o, wsrc.get("c_w_out", j), extras=(x,), epi=lambda acc, xr: (acc + xr,))[0]
            sv.update(qkv=qkv, o=o, lse=lse, nbias=nbias)
        sv["x_mid"] = x
        hn2 = _rms_fwd(x, p["norm_mlp"][i], f"norm_mlp_fwd{i}")
        h_pre = _mm_cols(f"mlp_up_fwd{i}", hn2, wsrc.get("mlp_w1", i), comm=wsrc.carry(i, "up"))
        x = _mm(f"mlp_down_fwd{i}", h_pre, wsrc.get("mlp_w2", i), a_fn=_relu_sq, extras=(x,), epi=lambda acc, xr: (acc + xr,),
                comm=wsrc.carry(i, "down"))[0]
        sv.update(hn2=hn2, h_pre=h_pre)
        saved.append(sv)

    loss_sum, dx, g_final = _final_loss(x, p["norm_final"], target)

    g = ({k: [None] * p[k].shape[0] for k in ("norm_mix", "norm_mlp", "s5_lam_re", "s5_lam_im", "s5_log_step", "s5_b_re",
                                                  "s5_b_im", "s5_c_re", "s5_c_im", "s5_d", "c_rpb")})
    g_t5 = jnp.zeros_like(p["t5_bias"], dtype=F32)
    for i in reversed(range(depth)):
        j = i // 2
        sv = saved[i]
        dh = _mm(f"mlp_down_bwd{i}", dx, wsrc.get("mlp_w2", i), tb=True, extras=(sv["h_pre"],), out_dtypes=(BF16,),
                 epi=lambda acc, hp: (acc * (2.0 * jnp.maximum(hp.astype(F32), 0.0)),), comm=gsink.carry(i, "down_bwd"))[0]
        gsink.put("mlp_w2", i, _mm(f"mlp_w2_grad{i}", sv["h_pre"], dx, ta=True, a_fn=_relu_sq)[0])
        gsink.put("mlp_w1", i, _mm_cols_grad(f"mlp_w1_grad{i}", sv["hn2"], dh, comm=gsink.carry(i, "w1_grad")))
        dhn2 = _mm_cols_t(f"mlp_up_bwd{i}", dh, wsrc.get("mlp_w1", i), comm=gsink.carry(i, "up_bwd"))
        dx, gn = _rms_bwd(sv["x_mid"], p["norm_mlp"][i], dhn2, dx, f"norm_mlp_bwd{i}")
        g["norm_mlp"][i] = gn[0]
        if i % 2 == 0:
            tb = sv["tb"]
            dmerged = _mm(f"ab_out_bwd{i}", dx, wsrc.get("ab_w_out", j), tb=True, out_dtypes=(BF16,))[0]
            gsink.put("ab_w_out", j, _mm(f"ab_w_out_grad{i}", sv["merged"], dx, ta=True)[0])
            def glu_epi(acc, yp, dob):
                sg = jax.nn.sigmoid(acc)
                dob = dob.astype(F32)
                return dob * _gelu(yp) * sg * (1.0 - sg), dob * sg
            dz, t1 = _mm(f"glu_bwd_z{i}", sv["y_pre"], wsrc.get("s5_w_glu", j), a_fn=_gelu, extras=(sv["y_pre"], dmerged),
                         extra_cols=(0, aw), epi=glu_epi, out_dtypes=(BF16, F32))
            dy_pre = _mm(f"glu_bwd_y{i}", dz, wsrc.get("s5_w_glu", j), tb=True, extras=(t1, sv["y_pre"]),
                         epi=lambda acc, t, yp: ((acc + t) * _gelu_grad(yp),), out_dtypes=(BF16,))[0]
            gsink.put("s5_w_glu", j, _mm(f"glu_w_grad{i}", sv["y_pre"], dz, ta=True, a_fn=_gelu)[0])
            r0 = _s5_bwd(sv["proj"], 3 * aw, dy_pre, *sv["states"][0], tb[0]["gtabs"], tb[0]["bre"], tb[0]["bim"],
                         tb[0]["cre"], tb[0]["cim"], False, None, f"s5_bwd_a{i}")
            r1 = _s5_bwd(sv["proj"], 3 * aw, dy_pre, *sv["states"][1], tb[1]["gtabs"], tb[1]["bre"], tb[1]["bim"],
                         tb[1]["cre"], tb[1]["cim"], True, (r0[0], p["s5_d"][j]), f"s5_bwd_b{i}")
            du = r1[0]
            g["s5_d"][j] = r1[7][0]
            gl_re, gl_im, gls, gb_re, gb_im, gc_re, gc_im = [], [], [], 0.0, 0.0, [], []
            for dnum, rr in enumerate((r0, r1)):
                da_re = jnp.sum(rr[1], axis=1).reshape(groups, pstate)
                da_im = jnp.sum(rr[2], axis=1).reshape(groups, pstate)
                dbb_re = jnp.transpose(_block_diag_take(rr[3], S5_GROUP, pstate), (0, 2, 1))
                dbb_im = jnp.transpose(_block_diag_take(rr[4], S5_GROUP, pstate), (0, 2, 1))
                _, vjp = jax.vjp(_s5_discretize, p["s5_lam_re"][j][dnum], p["s5_lam_im"][j][dnum], p["s5_log_step"][j][dnum],
                                 p["s5_b_re"][j], p["s5_b_im"][j])
                a, b, c, e, f = vjp((da_re, da_im, dbb_re, dbb_im))
                gl_re.append(a)
                gl_im.append(b)
                gls.append(c)
                gb_re, gb_im = gb_re + e, gb_im + f
                gc_re.append(jnp.transpose(_block_diag_take(rr[5], pstate, S5_GROUP), (0, 2, 1)))
                gc_im.append(jnp.transpose(_block_diag_take(rr[6], pstate, S5_GROUP), (0, 2, 1)))
            g["s5_lam_re"][j], g["s5_lam_im"][j], g["s5_log_step"][j] = jnp.stack(gl_re), jnp.stack(gl_im), jnp.stack(gls)
            g["s5_b_re"][j], g["s5_b_im"][j] = gb_re, gb_im
            g["s5_c_re"][j], g["s5_c_im"][j] = jnp.stack(gc_re), jnp.stack(gc_im)
            delta = _head_delta(dmerged, 0, sv["o_a"], f"dil_delta{i}")
            acc = None
            for b, (_, dil) in enumerate(DILATED_BRANCHES):
                args = (sv["proj"], dmerged, sv["lse"], delta, dil_bias[b][0], dil_bias[b][1], dil, ah, f"dil_bwd_d{dil}_{i}")
                if dil == 1:
                    assert acc is None
                    *acc, db = _dil_bwd(*args)
                else:
                    *acc, db = _dil_bwd_strided(*args, acc, BF16 if b == len(DILATED_BRANCHES) - 1 else F32)
                g_t5 = g_t5 + _t5_grad(db, dil)
            dproj = jnp.concatenate([*acc, du], axis=1)
            gsink.put("ab_w_in", j, _mm_cols_grad(f"ab_w_in_grad{i}", sv["hn"], dproj, comm=gsink.carry(i, "in_grad")))
            dhn = _mm_cols_t(f"ab_in_bwd{i}", dproj, wsrc.get("ab_w_in", j), comm=gsink.carry(i, "in_bwd"))
        else:
            do = _mm(f"c_out_bwd{i}", dx, wsrc.get("c_w_out", j), tb=True, out_dtypes=(BF16,))[0]
            gsink.put("c_w_out", j, _mm(f"c_w_out_grad{i}", sv["o"], dx, ta=True)[0])
            dq, dk, dv, db = _na_bwd(sv["qkv"], sv["o"], do, sv["lse"], sv["nbias"], ch, f"na_bwd{i}")
            g["c_rpb"][j] = _rpb_grad(db)
            dqkv = jnp.concatenate([dq, dk.astype(BF16), dv.astype(BF16)], axis=1)
            gsink.put("c_w_qkv", j, _mm_cols_grad(f"c_w_qkv_grad{i}", sv["hn"], dqkv, comm=gsink.carry(i, "in_grad")))
            dhn = _mm_cols_t(f"c_qkv_bwd{i}", dqkv, wsrc.get("c_w_qkv", j), comm=gsink.carry(i, "in_bwd"))
        dx, gn = _rms_bwd(sv["x"], p["norm_mix"][i], dhn, dx, f"norm_mix_bwd{i}")
        g["norm_mix"][i] = gn[0]
    g["t5_bias"] = g_t5
    g["norm_final"] = g_final[0]
    return loss_sum[0, 0], dx, g


BIG = ("ab_w_in", "ab_w_out", "s5_w_glu", "c_w_qkv", "c_w_out", "mlp_w1", "mlp_w2")
ROW_SHARDED = ("ab_w_out", "s5_w_glu", "c_w_out", "mlp_w2")
WEIGHTS = ("t5_bias", "ab_w_in", "ab_w_out", "s5_lam_re", "s5_lam_im", "s5_log_step", "s5_b_re", "s5_b_im", "s5_c_re",
           "s5_c_im", "s5_d", "s5_w_glu", "c_w_qkv", "c_w_out", "c_rpb", "norm_mix", "norm_mlp", "mlp_w1", "mlp_w2",
           "norm_final")


def _step(x, target, w, m, v):
    d = x.shape[-1]
    depth = w["norm_mix"].shape[0]
    wsrc = _WeightGather({k: w[k].astype(BF16) for k in BIG}, depth)
    wsrc.start()
    gsink = _GradExchange(lax.axis_index("c").astype(jnp.int32).reshape(1), depth)
    small = {k: w[k] for k in WEIGHTS if k not in BIG}
    loss_sum, dx, g = _forward_backward(x[0], target[0], small, wsrc, gsink)
    loss = lax.psum(0.5 * loss_sum / d, ("x", "y", "c"))

    out = {}
    recv = gsink.finish()
    for k in BIG:
        res = None
        for l in range(w[k].shape[0]):
            res = _adamw(w[k], recv[(k, l)], m[k], v[k], f"adamw_{k}{l}", layer=l, prev=res)
        out[k] = res
    names = [k for k in WEIGHTS if k not in BIG]
    def flat(tree):
        return jnp.concatenate([jnp.asarray(jnp.stack(tree[k]) if isinstance(tree[k], list) else tree[k], F32).reshape(-1)
                                for k in names])
    total = sum(int(np.prod(w[k].shape)) for k in names)
    rows = -(-total // LANES)
    rows = -(-rows // (2 * SUBLANES)) * (2 * SUBLANES)
    pad = rows * LANES - total
    def pack(tree):
        return jnp.pad(flat(tree), (0, pad)).reshape(rows, LANES)
    small_op = _gather_op(pack(g).astype(BF16))
    _run_comm([small_op], "gather_small_grads")
    res = _adamw(pack(w), small_op["res"][0], pack(m), pack(v), "adamw_small")
    off = 0
    for k in names:
        n = int(np.prod(w[k].shape))
        out[k] = [a.reshape(-1)[off:off + n].reshape(w[k].shape) for a in res]
        off += n
    return (loss, dx[None], *[out[k][0] for k in WEIGHTS], *[out[k][1] for k in WEIGHTS],
            *[out[k][2] for k in WEIGHTS], *[out[k][3] for k in WEIGHTS])


def kernel(x, t5_bias, ab_w_in, ab_w_out, s5_lam_re, s5_lam_im, s5_log_step, s5_b_re, s5_b_im, s5_c_re, s5_c_im, s5_d, s5_w_glu, c_w_qkv, c_w_out, c_rpb, norm_mix, norm_mlp, mlp_w1, mlp_w2, norm_final, loss_target, m_t5_bias, m_ab_w_in, m_ab_w_out, m_s5_lam_re, m_s5_lam_im, m_s5_log_step, m_s5_b_re, m_s5_b_im, m_s5_c_re, m_s5_c_im, m_s5_d, m_s5_w_glu, m_c_w_qkv, m_c_w_out, m_c_rpb, m_norm_mix, m_norm_mlp, m_mlp_w1, m_mlp_w2, m_norm_final, v_t5_bias, v_ab_w_in, v_ab_w_out, v_s5_lam_re, v_s5_lam_im, v_s5_log_step, v_s5_b_re, v_s5_b_im, v_s5_c_re, v_s5_c_im, v_s5_d, v_s5_w_glu, v_c_w_qkv, v_c_w_out, v_c_rpb, v_norm_mix, v_norm_mlp, v_mlp_w1, v_mlp_w2, v_norm_final):
    w = dict(t5_bias=t5_bias, ab_w_in=ab_w_in, ab_w_out=ab_w_out, s5_lam_re=s5_lam_re, s5_lam_im=s5_lam_im,
             s5_log_step=s5_log_step, s5_b_re=s5_b_re, s5_b_im=s5_b_im, s5_c_re=s5_c_re, s5_c_im=s5_c_im, s5_d=s5_d,
             s5_w_glu=s5_w_glu, c_w_qkv=c_w_qkv, c_w_out=c_w_out, c_rpb=c_rpb, norm_mix=norm_mix, norm_mlp=norm_mlp,
             mlp_w1=mlp_w1, mlp_w2=mlp_w2, norm_final=norm_final)
    m = dict(t5_bias=m_t5_bias, ab_w_in=m_ab_w_in, ab_w_out=m_ab_w_out, s5_lam_re=m_s5_lam_re, s5_lam_im=m_s5_lam_im,
             s5_log_step=m_s5_log_step, s5_b_re=m_s5_b_re, s5_b_im=m_s5_b_im, s5_c_re=m_s5_c_re, s5_c_im=m_s5_c_im,
             s5_d=m_s5_d, s5_w_glu=m_s5_w_glu, c_w_qkv=m_c_w_qkv, c_w_out=m_c_w_out, c_rpb=m_c_rpb, norm_mix=m_norm_mix,
             norm_mlp=m_norm_mlp, mlp_w1=m_mlp_w1, mlp_w2=m_mlp_w2, norm_final=m_norm_final)
    v = dict(t5_bias=v_t5_bias, ab_w_in=v_ab_w_in, ab_w_out=v_ab_w_out, s5_lam_re=v_s5_lam_re, s5_lam_im=v_s5_lam_im,
             s5_log_step=v_s5_log_step, s5_b_re=v_s5_b_re, s5_b_im=v_s5_b_im, s5_c_re=v_s5_c_re, s5_c_im=v_s5_c_im,
             s5_d=v_s5_d, s5_w_glu=v_s5_w_glu, c_w_qkv=v_c_w_qkv, c_w_out=v_c_w_out, c_rpb=v_c_rpb, norm_mix=v_norm_mix,
             norm_mlp=v_norm_mlp, mlp_w1=v_mlp_w1, mlp_w2=v_mlp_w2, norm_final=v_norm_final)
    return _step(x, loss_target, w, m, v)
```

```python
import math

import jax
import jax.numpy as jnp
import numpy as np
from jax import lax
from jax.experimental import pallas as pl
from jax.experimental.pallas import tpu as pltpu

F32 = jnp.float32
BF16 = jnp.bfloat16

N_DEV = 8
HEAD_DIM = 128
LANES = 128
QBLOCK = 128
DIL_HALF = 64
DILATED_BRANCHES = ((128, 1), (512, 4), (2048, 16))
S5_GROUP = 16
S5_GROUPS_PER_BLOCK = LANES // S5_GROUP
S5_CHUNK = 2048
SUBLANES = 8
GRID_W = 64
NA_ROWS = 8
NA_COLS = 16
NA_GROUP = 4
NA_WIN = NA_GROUP + NA_ROWS - 1
NA_GROUPS_PER_STEP = 8
T5_BUCKETS = 32
T5_MAX_DISTANCE = 1024
RMS_EPS = 1e-6
NEG_INF = -1e30
ADAM_LR = 0.001
ADAM_B1 = 0.9
ADAM_B2 = 0.999
ADAM_EPS = 1e-08
ADAM_WD = 0.01
ADAM_STEP = 10
VMEM_LIMIT_BYTES = 56 * 1024 * 1024
MESH = pl.DeviceIdType.MESH


def _cparams(sem=None):
    return pltpu.CompilerParams(dimension_semantics=sem, vmem_limit_bytes=VMEM_LIMIT_BYTES)


def _tile(dim, pref):
    t = min(dim, pref)
    while dim % t and t > LANES:
        t -= LANES
    assert dim % t == 0, (dim, pref)
    return t


def _dot(a, b, ca, cb):
    return lax.dot_general(a, b, (((ca,), (cb,)), ((), ())), preferred_element_type=F32)


def _nn(a, b):
    return _dot(a, b, 1, 0)


def _nt(a, b):
    return _dot(a, b, 1, 1)


def _tn(a, b):
    return _dot(a, b, 0, 0)


HBM_SPEC = pl.BlockSpec(memory_space=pltpu.HBM)


def _split_comm_refs(comm, in_refs, out_refs, sem_refs):
    parts, i, o, s = [], 0, 0, 0
    for op in comm:
        ni, no, ns = len(op["ins"]), len(op["outs"]), len(op["sems"])
        parts.append((in_refs[i:i + ni], out_refs[o:o + no], sem_refs[s:s + ns]))
        i, o, s = i + ni, o + no, s + ns
    return parts


def _mm_call(name, a, b, a_spec, b_spec, grid, nk, out_shapes, out_specs, acc_shape,
             ta=False, tb=False, a_fn=None, epi=None, extras=(), extra_specs=(), comm=()):
    ne, no = len(extras), len(out_shapes)
    comm_ins = [x for op in comm for x in op["ins"]]
    comm_outs = [x for op in comm for x in op["outs"]]
    comm_sems = [x for op in comm for x in op["sems"]]
    nci, nco = len(comm_ins), len(comm_outs)
    total = grid[0] * grid[1] * grid[2]

    def body(a_ref, b_ref, *rest):
        ex, rest = rest[:ne], rest[ne:]
        cin, rest = rest[:nci], rest[nci:]
        outs, rest = rest[:no], rest[no:]
        cout, rest = rest[:nco], rest[nco:]
        acc, csem = rest[0], rest[1:]
        k = pl.program_id(2)
        step = (pl.program_id(0) * grid[1] + pl.program_id(1)) * grid[2] + k
        parts = _split_comm_refs(comm, cin, cout, csem)

        def phase(which, at):
            if comm:
                @pl.when(step == at)
                def _():
                    for op, refs in zip(comm, parts):
                        op["run"](which, *refs)

        phase("start", 0)
        phase("mid", total - 1 - total // 8)

        @pl.when(k == 0)
        def _():
            acc[...] = jnp.zeros_like(acc)

        av = a_ref[...]
        if a_fn is not None:
            av = a_fn(av)
        acc[...] += _dot(av.astype(BF16), b_ref[...].astype(BF16), 0 if ta else 1, 1 if tb else 0)

        @pl.when(k == nk - 1)
        def _():
            r = acc[...]
            res = epi(r, *[e[...] for e in ex]) if epi is not None else (r,)
            for o, v in zip(outs, res):
                o[...] = v.astype(o.dtype)

        phase("wait", total - 1)

    res = pl.pallas_call(
        body, grid=grid, in_specs=[a_spec, b_spec, *extra_specs] + [HBM_SPEC] * nci,
        out_specs=list(out_specs) + [HBM_SPEC] * nco, out_shape=list(out_shapes) + comm_outs,
        scratch_shapes=[pltpu.VMEM(acc_shape, F32)] + comm_sems,
        compiler_params=_cparams(("arbitrary",) * 3 if comm else ("parallel", "parallel", "arbitrary")), name=name,
    )(a, b, *extras, *comm_ins)
    o = no
    for op in comm:
        op["res"] = res[o:o + len(op["outs"])]
        o += len(op["outs"])
    return res[:no]


def _run_comm(comm, name):
    comm_ins = [x for op in comm for x in op["ins"]]
    comm_outs = [x for op in comm for x in op["outs"]]
    comm_sems = [x for op in comm for x in op["sems"]]
    nci, nco = len(comm_ins), len(comm_outs)

    def body(*refs):
        parts = _split_comm_refs(comm, refs[:nci], refs[nci:nci + nco], refs[nci + nco:])
        for which in ("start", "mid", "wait"):
            for op, r in zip(comm, parts):
                op["run"](which, *r)

    res = pl.pallas_call(
        body, in_specs=[HBM_SPEC] * nci, out_specs=[HBM_SPEC] * nco, out_shape=comm_outs, scratch_shapes=comm_sems,
        compiler_params=pltpu.CompilerParams(has_side_effects=True), name=name)(*comm_ins)
    o = 0
    for op in comm:
        op["res"] = res[o:o + len(op["outs"])]
        o += len(op["outs"])


def _mm(name, a, b, *, ta=False, tb=False, a_fn=None, epi=None, extras=(), extra_cols=None,
        out_dtypes=(F32,), tm=1024, tn=1024, tk=2048, comm=()):
    m, kdim = (a.shape[1], a.shape[0]) if ta else a.shape
    n = b.shape[0] if tb else b.shape[1]
    assert (b.shape[1] if tb else b.shape[0]) == kdim, (a.shape, b.shape)
    tm, tn, tk = _tile(m, tm), _tile(n, tn), _tile(kdim, tk)
    a_spec = pl.BlockSpec((tk, tm), lambda i, j, k: (k, i)) if ta else pl.BlockSpec((tm, tk), lambda i, j, k: (i, k))
    b_spec = pl.BlockSpec((tn, tk), lambda i, j, k: (j, k)) if tb else pl.BlockSpec((tk, tn), lambda i, j, k: (k, j))
    o_spec = pl.BlockSpec((tm, tn), lambda i, j, k: (i, j))
    extra_cols = extra_cols or (0,) * len(extras)
    especs = []
    for c0 in extra_cols:
        assert c0 % tn == 0
        cb = c0 // tn
        especs.append(pl.BlockSpec((tm, tn), lambda i, j, k, cb=cb: (i, cb + j)))
    return _mm_call(name, a, b, a_spec, b_spec, (m // tm, n // tn, kdim // tk), kdim // tk,
                    [jax.ShapeDtypeStruct((m, n), d) for d in out_dtypes], [o_spec] * len(out_dtypes), (tm, tn),
                    ta=ta, tb=tb, a_fn=a_fn, epi=epi, extras=extras, extra_specs=especs, comm=comm)


def _mm_cols(name, a, w, *, comm=()):
    return _mm(name, a, w, out_dtypes=(BF16,), comm=comm)[0]


def _mm_cols_t(name, a, w, *, comm=()):
    return _mm(name, a, w, tb=True, out_dtypes=(BF16,), comm=comm)[0]


def _mm_cols_grad(name, a, dy, *, tm=1024, tk=2048, comm=()):
    s, kout = a.shape
    n = dy.shape[1] // N_DEV
    tm, tk = _tile(kout, tm), _tile(s, tk)
    return _mm_call(name, a, dy, pl.BlockSpec((tk, tm), lambda i, j, k: (k, i)),
                    pl.BlockSpec((tk, n), lambda i, j, k: (k, j)),
                    (kout // tm, N_DEV, s // tk), s // tk,
                    [jax.ShapeDtypeStruct((N_DEV, kout, n), F32)], [pl.BlockSpec((None, tm, n), lambda i, j, k: (j, i, 0))],
                    (tm, n), ta=True, comm=comm)[0]


_GELU_C = math.sqrt(2.0 / math.pi)


def _gelu(x):
    return 0.5 * x * (1.0 + jnp.tanh(_GELU_C * (x + 0.044715 * x * x * x)))


def _gelu_grad(x):
    t = jnp.tanh(_GELU_C * (x + 0.044715 * x * x * x))
    return 0.5 * (1.0 + t) + 0.5 * x * (1.0 - t * t) * _GELU_C * (1.0 + 3.0 * 0.044715 * x * x)


def _relu_sq(x):
    r = jnp.maximum(x.astype(F32), 0.0)
    return r * r


def _rms_fwd(x, g, name):
    s, d = x.shape
    tr = _tile(s, 512)

    def body(x_ref, g_ref, o_ref):
        xv = x_ref[...]
        r = lax.rsqrt(jnp.mean(xv * xv, axis=-1, keepdims=True) + RMS_EPS)
        o_ref[...] = (xv * r * g_ref[...]).astype(BF16)

    return pl.pallas_call(
        body, grid=(s // tr,),
        in_specs=[pl.BlockSpec((tr, d), lambda i: (i, 0)), pl.BlockSpec((1, d), lambda i: (0, 0))],
        out_specs=pl.BlockSpec((tr, d), lambda i: (i, 0)), out_shape=jax.ShapeDtypeStruct((s, d), BF16),
        compiler_params=_cparams(("parallel",)), name=name)(x, g.reshape(1, d))


def _rms_bwd(x, g, dy, dres, name):
    s, d = x.shape
    tr = _tile(s, 512)

    def body(x_ref, g_ref, dy_ref, dres_ref, dx_ref, dg_ref):
        @pl.when(pl.program_id(0) == 0)
        def _():
            dg_ref[...] = jnp.zeros_like(dg_ref)

        xv = x_ref[...]
        dyv = dy_ref[...].astype(F32)
        r = lax.rsqrt(jnp.mean(xv * xv, axis=-1, keepdims=True) + RMS_EPS)
        xh = xv * r
        gdy = dyv * g_ref[...]
        dx_ref[...] = dres_ref[...] + r * (gdy - xh * jnp.mean(gdy * xh, axis=-1, keepdims=True))
        dg_ref[...] += jnp.sum(dyv * xh, axis=0, keepdims=True)

    return pl.pallas_call(
        body, grid=(s // tr,),
        in_specs=[pl.BlockSpec((tr, d), lambda i: (i, 0)), pl.BlockSpec((1, d), lambda i: (0, 0)),
                  pl.BlockSpec((tr, d), lambda i: (i, 0)), pl.BlockSpec((tr, d), lambda i: (i, 0))],
        out_specs=[pl.BlockSpec((tr, d), lambda i: (i, 0)), pl.BlockSpec((1, d), lambda i: (0, 0))],
        out_shape=[jax.ShapeDtypeStruct((s, d), F32), jax.ShapeDtypeStruct((1, d), F32)],
        compiler_params=_cparams(("arbitrary",)), name=name)(x, g.reshape(1, d), dy, dres)


def _final_loss(x, g, target):
    s, d = x.shape
    tr = _tile(s, 512)

    def body(x_ref, g_ref, t_ref, loss_ref, dx_ref, dg_ref):
        @pl.when(pl.program_id(0) == 0)
        def _():
            dg_ref[...] = jnp.zeros_like(dg_ref)
            loss_ref[...] = jnp.zeros_like(loss_ref)

        xv = x_ref[...]
        gv = g_ref[...]
        r = lax.rsqrt(jnp.mean(xv * xv, axis=-1, keepdims=True) + RMS_EPS)
        xh = xv * r
        err = xh * gv - t_ref[...]
        loss_ref[...] += jnp.sum(jnp.sum(err * err, axis=-1, keepdims=True), axis=0, keepdims=True)
        dyv = err * (1.0 / d)
        gdy = dyv * gv
        dx_ref[...] = r * (gdy - xh * jnp.mean(gdy * xh, axis=-1, keepdims=True))
        dg_ref[...] += jnp.sum(dyv * xh, axis=0, keepdims=True)

    return pl.pallas_call(
        body, grid=(s // tr,),
        in_specs=[pl.BlockSpec((tr, d), lambda i: (i, 0)), pl.BlockSpec((1, d), lambda i: (0, 0)),
                  pl.BlockSpec((tr, d), lambda i: (i, 0))],
        out_specs=[pl.BlockSpec((1, 1), lambda i: (0, 0)), pl.BlockSpec((tr, d), lambda i: (i, 0)),
                   pl.BlockSpec((1, d), lambda i: (0, 0))],
        out_shape=[jax.ShapeDtypeStruct((1, 1), F32), jax.ShapeDtypeStruct((s, d), F32),
                   jax.ShapeDtypeStruct((1, d), F32)],
        compiler_params=_cparams(("arbitrary",)), name="final_norm_loss")(x, g.reshape(1, d), target)


def _t5_bucket(rel):
    half = T5_BUCKETS // 2
    max_exact = half // 2
    n = jnp.abs(rel)
    nf = jnp.maximum(n, 1).astype(F32)
    large = max_exact + (jnp.log(nf / max_exact) / math.log(T5_MAX_DISTANCE / max_exact)
                         * (half - max_exact)).astype(jnp.int32)
    large = jnp.minimum(large, half - 1)
    return jnp.where(rel > 0, half, 0) + jnp.where(n < max_exact, n, large)


def _dil_offsets():
    i = jnp.arange(QBLOCK)[:, None]
    kk = jnp.arange(2 * QBLOCK)[None, :]
    return kk - DIL_HALF - i, (jnp.arange(QBLOCK)[None, :] + DIL_HALF) - jnp.arange(2 * QBLOCK)[:, None]


def _expand(table, onehot, name, tn=8192):
    r, n = table.shape[0], onehot.shape[1]
    tn = _tile(n, tn)

    def body(t_ref, oh_ref, o_ref):
        o_ref[...] = lax.dot_general(t_ref[...], oh_ref[...], (((1,), (0,)), ((), ())),
                                     precision=lax.Precision.HIGHEST, preferred_element_type=F32)

    return pl.pallas_call(
        body, grid=(n // tn,),
        in_specs=[pl.BlockSpec((r, LANES), lambda i: (0, 0)), pl.BlockSpec((LANES, tn), lambda i: (0, i))],
        out_specs=pl.BlockSpec((r, tn), lambda i: (0, i)), out_shape=jax.ShapeDtypeStruct((r, n), F32),
        compiler_params=_cparams(("parallel",)), name=name)(table, onehot)


def _pad_rows_lanes(t):
    r, c = t.shape
    return jnp.pad(t.astype(F32), ((0, -r % SUBLANES), (0, LANES - c)))


def _dil_bias(t5, dil):
    ah = t5.shape[1]
    off1, off2 = _dil_offsets()
    bucket = jnp.concatenate([_t5_bucket(off1 * dil).reshape(-1), _t5_bucket(off2 * dil).reshape(-1)])
    onehot = (jnp.arange(LANES)[:, None] == bucket[None, :]).astype(F32)
    b = _expand(_pad_rows_lanes(t5.T), onehot, f"t5_bias_d{dil}")[:ah]
    n1 = QBLOCK * 2 * QBLOCK
    return b[:, :n1].reshape(ah, QBLOCK, 2 * QBLOCK), b[:, n1:].reshape(ah, 2 * QBLOCK, QBLOCK)


def _window(p, c, n, cols=slice(None)):
    return jnp.concatenate([p[pl.ds(DIL_HALF, DIL_HALF), cols], c[:, cols], n[pl.ds(0, DIL_HALF), cols]], axis=0)


def _dil_specs(width, ncol_blocks, col_block, nb):
    def spec(dn):
        return pl.BlockSpec((QBLOCK, width), lambda r, n: (jnp.clip(n + dn, 0, nb - 1), r * ncol_blocks + col_block))
    return [spec(-1), spec(0), spec(1)]


def _dil_fwd(proj, bias1, dil, ah, name):
    s, wtot = proj.shape
    ln = s // dil
    nb = ln // QBLOCK
    assert nb * QBLOCK * dil == s
    aw = ah * HEAD_DIM
    wb = wtot // aw
    scale = 1.0 / math.sqrt(HEAD_DIM)
    pv = proj.reshape(ln, dil * wtot)

    def body(q_ref, kp, kc, kn, vp, vc, vn, b_ref, o_ref, lse_ref):
        n = pl.program_id(1)
        ii = lax.broadcasted_iota(jnp.int32, (QBLOCK, 2 * QBLOCK), 0)
        jj = lax.broadcasted_iota(jnp.int32, (QBLOCK, 2 * QBLOCK), 1)
        kpos = n * QBLOCK + jj - DIL_HALF
        valid = (jnp.abs(jj - DIL_HALF - ii) <= DIL_HALF) & (kpos >= 0) & (kpos < ln)
        for h in range(ah):
            cs = pl.ds(h * HEAD_DIM, HEAD_DIM)
            kw, vw = _window(kp, kc, kn, cs), _window(vp, vc, vn, cs)
            sc = jnp.where(valid, _nt(q_ref[:, cs], kw) * scale + b_ref[h], NEG_INF)
            m = jnp.max(sc, axis=-1, keepdims=True)
            p = jnp.exp(sc - m)
            l = jnp.sum(p, axis=-1, keepdims=True)
            o_ref[:, cs] = (_nn(p.astype(BF16), vw) / l).astype(BF16)
            lse_ref[:, cs] = jnp.broadcast_to(m + jnp.log(l), (QBLOCK, HEAD_DIM))

    ospec = pl.BlockSpec((QBLOCK, aw), lambda r, n: (n, r))
    o, lse = pl.pallas_call(
        body, grid=(dil, nb),
        in_specs=[_dil_specs(aw, wb, 0, nb)[1], *_dil_specs(aw, wb, 1, nb), *_dil_specs(aw, wb, 2, nb),
                  pl.BlockSpec((ah, QBLOCK, 2 * QBLOCK), lambda r, n: (0, 0, 0))],
        out_specs=[ospec, ospec],
        out_shape=[jax.ShapeDtypeStruct((ln, dil * aw), BF16), jax.ShapeDtypeStruct((ln, dil * aw), F32)],
        compiler_params=_cparams(("parallel", "parallel")), name=name,
    )(pv, pv, pv, pv, pv, pv, pv, bias1)
    return o.reshape(s, aw), lse.reshape(s, aw)


def _dil_merge(outs, lses):
    s, aw = outs[0].shape
    tr = _tile(s, 512)

    def body(o1, l1, o2, l2, o3, l3, o_ref, lse_ref):
        a, b, c = l1[...], l2[...], l3[...]
        m = jnp.maximum(jnp.maximum(a, b), c)
        w1, w2, w3 = jnp.exp(a - m), jnp.exp(b - m), jnp.exp(c - m)
        tot = w1 + w2 + w3
        o_ref[...] = ((w1 * o1[...].astype(F32) + w2 * o2[...].astype(F32) + w3 * o3[...].astype(F32)) / tot).astype(BF16)
        lse_ref[...] = m + jnp.log(tot)

    spec = pl.BlockSpec((tr, aw), lambda i: (i, 0))
    return pl.pallas_call(
        body, grid=(s // tr,), in_specs=[spec] * 6, out_specs=[spec, spec],
        out_shape=[jax.ShapeDtypeStruct((s, aw), BF16), jax.ShapeDtypeStruct((s, aw), F32)],
        compiler_params=_cparams(("parallel",)), name="dil_merge",
    )(outs[0], lses[0], outs[1], lses[1], outs[2], lses[2])


def _head_delta(do, do_col0, o, name):
    s, w = o.shape
    tr = _tile(s, 512)
    cb = do_col0 // HEAD_DIM

    def body(do_ref, o_ref, d_ref):
        d = jnp.sum(do_ref[...].astype(F32) * o_ref[...].astype(F32), axis=-1, keepdims=True)
        d_ref[...] = jnp.broadcast_to(d, d_ref.shape)

    return pl.pallas_call(
        body, grid=(s // tr, w // HEAD_DIM),
        in_specs=[pl.BlockSpec((tr, HEAD_DIM), lambda i, h: (i, cb + h)), pl.BlockSpec((tr, HEAD_DIM), lambda i, h: (i, h))],
        out_specs=pl.BlockSpec((tr, HEAD_DIM), lambda i, h: (i, h)), out_shape=jax.ShapeDtypeStruct((s, w), F32),
        compiler_params=_cparams(("parallel", "parallel")), name=name)(do, o)


def _dil_bwd(proj, dmerged, lse, delta, bias1, bias2, dil, ah, name):
    s, wtot = proj.shape
    ln = s // dil
    nb = ln // QBLOCK
    aw = ah * HEAD_DIM
    wb = wtot // aw
    wd = dmerged.shape[1] // aw
    scale = 1.0 / math.sqrt(HEAD_DIM)
    pv = proj.reshape(ln, dil * wtot)
    dov = dmerged.reshape(ln, dil * dmerged.shape[1])
    lv = lse.reshape(ln, dil * aw)
    dlv = delta.reshape(ln, dil * aw)

    def body(qp, qc, qn, kp, kc, kn, vp, vc, vn, dop, doc, don, lp, lc, lnx, dp, dc, dn, b1_ref, b2_ref,
             dq_ref, dk_ref, dv_ref, db_ref):
        n = pl.program_id(1)

        @pl.when((pl.program_id(0) == 0) & (n == 0))
        def _():
            db_ref[...] = jnp.zeros_like(db_ref)

        ii = lax.broadcasted_iota(jnp.int32, (QBLOCK, 2 * QBLOCK), 0)
        jj = lax.broadcasted_iota(jnp.int32, (QBLOCK, 2 * QBLOCK), 1)
        kpos = n * QBLOCK + jj - DIL_HALF
        valid = (jnp.abs(jj - DIL_HALF - ii) <= DIL_HALF) & (kpos >= 0) & (kpos < ln)
        ww = lax.broadcasted_iota(jnp.int32, (2 * QBLOCK, QBLOCK), 0)
        cc = lax.broadcasted_iota(jnp.int32, (2 * QBLOCK, QBLOCK), 1)
        qpos = n * QBLOCK - DIL_HALF + ww
        valid2 = (jnp.abs(cc + DIL_HALF - ww) <= DIL_HALF) & (qpos >= 0) & (qpos < ln)
        for h in range(ah):
            cs = pl.ds(h * HEAD_DIM, HEAD_DIM)
            kw, vw = _window(kp, kc, kn, cs), _window(vp, vc, vn, cs)
            sc = _nt(qc[:, cs], kw) * scale + b1_ref[h]
            lse2 = jnp.concatenate([lc[:, cs], lc[:, cs]], axis=1)
            p = jnp.where(valid, jnp.exp(jnp.where(valid, sc - lse2, 0.0)), 0.0)
            ds = p * (_nt(doc[:, cs], vw) - jnp.concatenate([dc[:, cs], dc[:, cs]], axis=1))
            dq_ref[:, cs] = _nn(ds.astype(BF16), kw) * scale
            db_ref[h] += ds
            qw, dow = _window(qp, qc, qn, cs), _window(dop, doc, don, cs)
            sc2 = _nt(qw, kc[:, cs]) * scale + b2_ref[h]
            p2 = jnp.where(valid2, jnp.exp(jnp.where(valid2, sc2 - _window(lp, lc, lnx, cs), 0.0)), 0.0)
            dv_ref[:, cs] = _tn(p2.astype(BF16), dow)
            ds2 = p2 * (_nt(dow, vc[:, cs]) - _window(dp, dc, dn, cs))
            dk_ref[:, cs] = _tn(ds2.astype(BF16), qw) * scale

    ospec = pl.BlockSpec((QBLOCK, aw), lambda r, n: (n, r))
    dq, dk, dv, db = pl.pallas_call(
        body, grid=(dil, nb),
        in_specs=[*_dil_specs(aw, wb, 0, nb), *_dil_specs(aw, wb, 1, nb), *_dil_specs(aw, wb, 2, nb),
                  *_dil_specs(aw, wd, 0, nb), *_dil_specs(aw, 1, 0, nb), *_dil_specs(aw, 1, 0, nb),
                  pl.BlockSpec((ah, QBLOCK, 2 * QBLOCK), lambda r, n: (0, 0, 0)),
                  pl.BlockSpec((ah, 2 * QBLOCK, QBLOCK), lambda r, n: (0, 0, 0))],
        out_specs=[ospec, ospec, ospec, pl.BlockSpec((ah, QBLOCK, 2 * QBLOCK), lambda r, n: (0, 0, 0))],
        out_shape=[jax.ShapeDtypeStruct((ln, dil * aw), F32)] * 3 + [jax.ShapeDtypeStruct((ah, QBLOCK, 2 * QBLOCK), F32)],
        compiler_params=_cparams(("arbitrary", "arbitrary")), name=name,
    )(pv, pv, pv, pv, pv, pv, pv, pv, pv, dov, dov, dov, lv, lv, lv, dlv, dlv, dlv, bias1, bias2)
    return dq.reshape(s, aw), dk.reshape(s, aw), dv.reshape(s, aw), db


def _strided_specs(rows, col_block, nsb):
    half = rows // 2
    return [pl.BlockSpec((half, HEAD_DIM), lambda h, n: (jnp.clip(2 * n - 1, 0, 2 * nsb - 1), col_block + h)),
            pl.BlockSpec((rows, HEAD_DIM), lambda h, n: (n, col_block + h)),
            pl.BlockSpec((half, HEAD_DIM), lambda h, n: (jnp.clip(2 * n + 2, 0, 2 * nsb - 1), col_block + h))]


def _fill_window(dst, p, c, n):
    half, rows = p.shape[0], c.shape[0]
    dst[pl.ds(0, half), :] = p[...].astype(F32)
    dst[pl.ds(half, rows), :] = c[...].astype(F32)
    dst[pl.ds(half + rows, half), :] = n[...].astype(F32)


def _dil_masks(n, ln):
    ii = lax.broadcasted_iota(jnp.int32, (QBLOCK, 2 * QBLOCK), 0)
    jj = lax.broadcasted_iota(jnp.int32, (QBLOCK, 2 * QBLOCK), 1)
    kpos = n * QBLOCK + jj - DIL_HALF
    valid = (jnp.abs(jj - DIL_HALF - ii) <= DIL_HALF) & (kpos >= 0) & (kpos < ln)
    ww = lax.broadcasted_iota(jnp.int32, (2 * QBLOCK, QBLOCK), 0)
    cc = lax.broadcasted_iota(jnp.int32, (2 * QBLOCK, QBLOCK), 1)
    qpos = n * QBLOCK - DIL_HALF + ww
    valid2 = (jnp.abs(cc + DIL_HALF - ww) <= DIL_HALF) & (qpos >= 0) & (qpos < ln)
    return valid, valid2


def _dil_fwd_strided(proj, bias1, dil, ah, name):
    s, wtot = proj.shape
    ln = s // dil
    nsb = ln // QBLOCK
    assert nsb * QBLOCK * dil == s
    aw = ah * HEAD_DIM
    rows = QBLOCK * dil
    half = rows // 2
    scale = 1.0 / math.sqrt(HEAD_DIM)

    def body(q_ref, kp, kc, kn, vp, vc, vn, b_ref, o_ref, lse_ref, qf, kf, vf, of):
        valid, _ = _dil_masks(pl.program_id(1), ln)
        qf[...] = q_ref[...].astype(F32)
        _fill_window(kf, kp, kc, kn)
        _fill_window(vf, vp, vc, vn)
        bias = b_ref[...]

        def residue(r, carry):
            qs = pl.ds(r, QBLOCK, stride=dil)
            ws = pl.ds(r, 2 * QBLOCK, stride=dil)
            kw, vw = kf[ws, :].astype(BF16), vf[ws, :].astype(BF16)
            sc = jnp.where(valid, _nt(qf[qs, :].astype(BF16), kw) * scale + bias, NEG_INF)
            m = jnp.max(sc, axis=-1, keepdims=True)
            p = jnp.exp(sc - m)
            l = jnp.sum(p, axis=-1, keepdims=True)
            of[qs, :] = _nn(p.astype(BF16), vw) / l
            lse_ref[qs, :] = jnp.broadcast_to(m + jnp.log(l), (QBLOCK, HEAD_DIM))
            return carry

        lax.fori_loop(0, dil, residue, 0, unroll=4)
        o_ref[...] = of[...].astype(BF16)

    ospec = pl.BlockSpec((rows, HEAD_DIM), lambda h, n: (n, h))
    return pl.pallas_call(
        body, grid=(ah, nsb),
        in_specs=[_strided_specs(rows, 0, nsb)[1], *_strided_specs(rows, ah, nsb), *_strided_specs(rows, 2 * ah, nsb),
                  pl.BlockSpec((None, QBLOCK, 2 * QBLOCK), lambda h, n: (h, 0, 0))],
        out_specs=[ospec, ospec],
        out_shape=[jax.ShapeDtypeStruct((s, aw), BF16), jax.ShapeDtypeStruct((s, aw), F32)],
        scratch_shapes=[pltpu.VMEM((rows, HEAD_DIM), F32), pltpu.VMEM((2 * rows, HEAD_DIM), F32),
                        pltpu.VMEM((2 * rows, HEAD_DIM), F32), pltpu.VMEM((rows, HEAD_DIM), F32)],
        compiler_params=_cparams(("parallel", "parallel")), name=name,
    )(proj, proj, proj, proj, proj, proj, proj, bias1)


def _dil_bwd_strided(proj, dmerged, lse, delta, bias1, bias2, dil, ah, name, acc, out_dtype):
    s, wtot = proj.shape
    ln = s // dil
    nsb = ln // QBLOCK
    aw = ah * HEAD_DIM
    rows = QBLOCK * dil
    half = rows // 2
    scale = 1.0 / math.sqrt(HEAD_DIM)
    center = slice(DIL_HALF, DIL_HALF + QBLOCK)

    def body(qp, qc, qn, kp, kc, kn, vp, vc, vn, dop, doc, don, lp, lc, lnx, dp, dc, dn, b1_ref, b2_ref, aq, ak, av,
             dq_ref, dk_ref, dv_ref, db_ref, qf, kf, vf, dof, dq_s, dk_s, dv_s):
        n = pl.program_id(1)

        @pl.when(n == 0)
        def _():
            db_ref[...] = jnp.zeros_like(db_ref)

        valid, valid2 = _dil_masks(n, ln)
        _fill_window(qf, qp, qc, qn)
        _fill_window(kf, kp, kc, kn)
        _fill_window(vf, vp, vc, vn)
        _fill_window(dof, dop, doc, don)
        b1, b2 = b1_ref[...], b2_ref[...]

        def stat_window(p, c, nx, r):
            return jnp.concatenate([p[pl.ds(r, DIL_HALF, stride=dil), :], c[pl.ds(r, QBLOCK, stride=dil), :],
                                    nx[pl.ds(r, DIL_HALF, stride=dil), :]], axis=0)

        def residue(r, carry):
            ws = pl.ds(r, 2 * QBLOCK, stride=dil)
            os = pl.ds(r, QBLOCK, stride=dil)
            qw, kw, vw, dow = (t[ws, :].astype(BF16) for t in (qf, kf, vf, dof))
            q, k, v, do = qw[center], kw[center], vw[center], dow[center]
            lse_w, delta_w = stat_window(lp, lc, lnx, r), stat_window(dp, dc, dn, r)
            lse_c, delta_c = lse_w[center], delta_w[center]
            sc = _nt(q, kw) * scale + b1
            p = jnp.where(valid, jnp.exp(jnp.where(valid, sc - jnp.concatenate([lse_c, lse_c], axis=1), 0.0)), 0.0)
            ds = p * (_nt(do, vw) - jnp.concatenate([delta_c, delta_c], axis=1))
            dq_s[os, :] = _nn(ds.astype(BF16), kw) * scale
            db_ref[...] += ds
            sc2 = _nt(qw, k) * scale + b2
            p2 = jnp.where(valid2, jnp.exp(jnp.where(valid2, sc2 - lse_w, 0.0)), 0.0)
            dv_s[os, :] = _tn(p2.astype(BF16), dow)
            ds2 = p2 * (_nt(dow, v) - delta_w)
            dk_s[os, :] = _tn(ds2.astype(BF16), qw) * scale
            return carry

        lax.fori_loop(0, dil, residue, 0, unroll=4)
        dq_ref[...] = (dq_s[...] + aq[...]).astype(dq_ref.dtype)
        dk_ref[...] = (dk_s[...] + ak[...]).astype(dk_ref.dtype)
        dv_ref[...] = (dv_s[...] + av[...]).astype(dv_ref.dtype)

    ospec = pl.BlockSpec((rows, HEAD_DIM), lambda h, n: (n, h))
    win = pltpu.VMEM((2 * rows, HEAD_DIM), F32)
    blk = pltpu.VMEM((rows, HEAD_DIM), F32)
    return pl.pallas_call(
        body, grid=(ah, nsb),
        in_specs=[*_strided_specs(rows, 0, nsb), *_strided_specs(rows, ah, nsb), *_strided_specs(rows, 2 * ah, nsb),
                  *_strided_specs(rows, 0, nsb), *_strided_specs(rows, 0, nsb), *_strided_specs(rows, 0, nsb),
                  pl.BlockSpec((None, QBLOCK, 2 * QBLOCK), lambda h, n: (h, 0, 0)),
                  pl.BlockSpec((None, 2 * QBLOCK, QBLOCK), lambda h, n: (h, 0, 0)), ospec, ospec, ospec],
        out_specs=[ospec, ospec, ospec, pl.BlockSpec((None, QBLOCK, 2 * QBLOCK), lambda h, n: (h, 0, 0))],
        out_shape=[jax.ShapeDtypeStruct((s, aw), out_dtype)] * 3 + [jax.ShapeDtypeStruct((ah, QBLOCK, 2 * QBLOCK), F32)],
        scratch_shapes=[win, win, win, win, blk, blk, blk],
        compiler_params=_cparams(("parallel", "arbitrary")), name=name,
    )(proj, proj, proj, proj, proj, proj, proj, proj, proj, dmerged, dmerged, dmerged, lse, lse, lse,
      delta, delta, delta, bias1, bias2, *acc)


def _bucket_sum(vals, onehot, name):
    r, n = vals.shape
    b = onehot.shape[1]

    def body(v_ref, oh_ref, o_ref):
        o_ref[...] = lax.dot_general(v_ref[...], oh_ref[...], (((1,), (0,)), ((), ())),
                                     precision=lax.Precision.HIGHEST, preferred_element_type=F32)

    tr = max(t for t in range(SUBLANES, 257, SUBLANES) if r % t == 0)
    return pl.pallas_call(
        body, grid=(r // tr,),
        in_specs=[pl.BlockSpec((tr, n), lambda i: (i, 0)), pl.BlockSpec((n, b), lambda i: (0, 0))],
        out_specs=pl.BlockSpec((tr, b), lambda i: (i, 0)), out_shape=jax.ShapeDtypeStruct((r, b), F32),
        compiler_params=_cparams(("parallel",)), name=name)(vals, onehot)


def _t5_grad(dbias, dil):
    ah = dbias.shape[0]
    off1, _ = _dil_offsets()
    bucket = _t5_bucket(off1 * dil).reshape(-1)
    inside = (jnp.abs(off1) <= DIL_HALF).reshape(-1)
    onehot = ((bucket[:, None] == jnp.arange(LANES)[None, :]) & inside[:, None]).astype(F32)
    vals = jnp.pad(dbias.reshape(ah, -1), ((0, -ah % SUBLANES), (0, 0)))
    return _bucket_sum(vals, onehot, f"t5_grad_d{dil}")[:ah, :T5_BUCKETS].T


def _na_table_rows():
    ro = -np.ones((3, NA_GROUP, NA_WIN), np.int64)
    for i in range(NA_GROUP):
        for j in range(NA_WIN):
            if j < NA_ROWS:
                ro[0, i, j] = j - i + NA_ROWS - 1
            if i <= j < i + NA_ROWS:
                ro[1, i, j] = j - i + NA_ROWS // 2 - 1
            if j >= NA_WIN - NA_ROWS:
                ro[2, i, j] = j - i
    return ro


def _na_bias(rpb):
    ch, nro, nco = rpb.shape
    c = np.arange(GRID_W)
    col_start = np.clip(c - NA_COLS // 2, 0, GRID_W - NA_COLS)
    col_ok = (c[None, :] >= col_start[:, None]) & (c[None, :] < col_start[:, None] + NA_COLS)
    col_idx = np.clip(c[None, :] - c[:, None] + NA_COLS - 1, 0, 2 * NA_COLS - 2).reshape(-1)
    onehot = (np.arange(LANES)[:, None] == col_idx[None, :]).astype(np.float32)
    table = jnp.pad(rpb.astype(F32).reshape(ch * nro, nco), ((0, -(ch * nro) % SUBLANES), (0, LANES - nco)))
    by_row = _expand(table, jnp.asarray(onehot), "rpb_bias", tn=GRID_W * GRID_W)[:ch * nro]
    by_row = jnp.where(jnp.asarray(col_ok.reshape(-1))[None, :], by_row, NEG_INF).reshape(ch, nro, GRID_W, GRID_W)
    neg = jnp.full((ch, GRID_W, GRID_W), NEG_INF, F32)
    strips = [jnp.concatenate([by_row[:, r] if r >= 0 else neg for r in strip], axis=-1)
              for strip in _na_table_rows().reshape(3 * NA_GROUP, NA_WIN)]
    return jnp.stack(strips, axis=1).reshape(ch, 3, NA_GROUP * GRID_W, NA_WIN * GRID_W)


def _na_group(g, rows):
    ngroups = rows // NA_GROUP
    ws = jnp.clip(g * NA_GROUP - NA_ROWS // 2, 0, rows - NA_WIN)
    return pl.multiple_of(ws * GRID_W, GRID_W), jnp.where(g == 0, 0, jnp.where(g == ngroups - 1, 2, 1))


def _na_fwd(qkv, bias, ch, name):
    s = qkv.shape[0]
    rows = s // GRID_W
    assert rows >= NA_WIN and rows % (NA_GROUP * NA_GROUPS_PER_STEP) == 0
    cw = ch * HEAD_DIM
    tg = NA_GROUP * GRID_W
    tq = NA_GROUPS_PER_STEP * tg
    win = NA_WIN * GRID_W
    scale = 1.0 / math.sqrt(HEAD_DIM)

    def body(q_ref, k_ref, v_ref, b_ref, o_ref, lse_ref):
        gb = pl.program_id(1)
        for i in range(NA_GROUPS_PER_STEP):
            st, var = _na_group(gb * NA_GROUPS_PER_STEP + i, rows)
            kw, vw = k_ref[pl.ds(st, win), :], v_ref[pl.ds(st, win), :]
            qs = pl.ds(i * tg, tg)
            sc = _nt(q_ref[qs, :], kw) * scale + b_ref[var]
            m = jnp.max(sc, axis=-1, keepdims=True)
            p = jnp.exp(sc - m)
            l = jnp.sum(p, axis=-1, keepdims=True)
            o_ref[qs, :] = (_nn(p.astype(BF16), vw) / l).astype(BF16)
            lse_ref[qs, :] = jnp.broadcast_to(m + jnp.log(l), (tg, HEAD_DIM))

    ospec = pl.BlockSpec((tq, HEAD_DIM), lambda h, gb: (gb, h))
    return pl.pallas_call(
        body, grid=(ch, s // tq),
        in_specs=[pl.BlockSpec((tq, HEAD_DIM), lambda h, gb: (gb, h)),
                  pl.BlockSpec((s, HEAD_DIM), lambda h, gb: (0, ch + h)),
                  pl.BlockSpec((s, HEAD_DIM), lambda h, gb: (0, 2 * ch + h)),
                  pl.BlockSpec((None, 3, tg, win), lambda h, gb: (h, 0, 0, 0))],
        out_specs=[ospec, ospec],
        out_shape=[jax.ShapeDtypeStruct((s, cw), BF16), jax.ShapeDtypeStruct((s, cw), F32)],
        compiler_params=_cparams(("parallel", "parallel")), name=name)(qkv, qkv, qkv, bias)


def _na_bwd(qkv, o, do, lse, bias, ch, name):
    s = qkv.shape[0]
    rows = s // GRID_W
    cw = ch * HEAD_DIM
    tg = NA_GROUP * GRID_W
    tq = NA_GROUPS_PER_STEP * tg
    win = NA_WIN * GRID_W
    scale = 1.0 / math.sqrt(HEAD_DIM)

    def body(q_ref, k_ref, v_ref, o_ref, do_ref, lse_ref, b_ref, dq_ref, dk_out, dv_out, db_ref, dk_ref, dv_ref):
        gb = pl.program_id(1)

        @pl.when(gb == 0)
        def _():
            dk_ref[...] = jnp.zeros_like(dk_ref)
            dv_ref[...] = jnp.zeros_like(dv_ref)
            db_ref[...] = jnp.zeros_like(db_ref)

        for i in range(NA_GROUPS_PER_STEP):
            st, var = _na_group(gb * NA_GROUPS_PER_STEP + i, rows)
            ws = pl.ds(st, win)
            kw, vw = k_ref[ws, :], v_ref[ws, :]
            qs = pl.ds(i * tg, tg)
            q, dov = q_ref[qs, :], do_ref[qs, :]
            sc = _nt(q, kw) * scale + b_ref[var]
            p = jnp.exp(sc - lse_ref[qs, :][:, :1])
            delta = jnp.sum(dov.astype(F32) * o_ref[qs, :].astype(F32), axis=-1, keepdims=True)
            ds = p * (_nt(dov, vw) - delta)
            dsb = ds.astype(BF16)
            dq_ref[qs, :] = (_nn(dsb, kw) * scale).astype(BF16)
            dk_ref[ws, :] += _tn(dsb, q) * scale
            dv_ref[ws, :] += _tn(p.astype(BF16), dov)
            db_ref[var] += ds

        @pl.when(gb == s // tq - 1)
        def _():
            dk_out[...] = dk_ref[...].astype(BF16)
            dv_out[...] = dv_ref[...].astype(BF16)

    qspec = pl.BlockSpec((tq, HEAD_DIM), lambda h, gb: (gb, h))
    kvspec = pl.BlockSpec((s, HEAD_DIM), lambda h, gb: (0, h))
    bspec = pl.BlockSpec((None, 3, tg, win), lambda h, gb: (h, 0, 0, 0))
    return pl.pallas_call(
        body, grid=(ch, s // tq),
        in_specs=[qspec, pl.BlockSpec((s, HEAD_DIM), lambda h, gb: (0, ch + h)),
                  pl.BlockSpec((s, HEAD_DIM), lambda h, gb: (0, 2 * ch + h)), qspec, qspec, qspec, bspec],
        out_specs=[qspec, kvspec, kvspec, bspec],
        out_shape=[jax.ShapeDtypeStruct((s, cw), BF16), jax.ShapeDtypeStruct((s, cw), BF16),
                   jax.ShapeDtypeStruct((s, cw), BF16), jax.ShapeDtypeStruct((ch, 3, tg, win), F32)],
        scratch_shapes=[pltpu.VMEM((s, HEAD_DIM), F32), pltpu.VMEM((s, HEAD_DIM), F32)],
        compiler_params=_cparams(("parallel", "arbitrary")), name=name)(qkv, qkv, qkv, o, do, lse, bias)


def _rpb_grad(dbias):
    ch = dbias.shape[0]
    ntile = 3 * NA_GROUP * NA_WIN
    c = np.arange(GRID_W)
    col_idx = (c[None, :] - c[:, None] + NA_COLS - 1).reshape(-1)
    oh_col = (col_idx[:, None] == np.arange(LANES)[None, :]).astype(np.float32)
    d6 = dbias.reshape(ch, 3, NA_GROUP, GRID_W, NA_WIN, GRID_W)
    vals = jnp.transpose(d6, (0, 1, 2, 4, 3, 5)).reshape(ch * ntile, GRID_W * GRID_W)
    by_col = _bucket_sum(vals, jnp.asarray(oh_col), "rpb_grad_cols")
    npad = 2 * LANES
    oh_row = np.zeros((npad, LANES), np.float32)
    for t, r in enumerate(_na_table_rows().reshape(-1)):
        if r >= 0:
            oh_row[t, r] = 1.0
    by_col = jnp.pad(by_col.reshape(ch, ntile, LANES), ((0, 0), (0, npad - ntile), (0, 0)))
    vals2 = jnp.transpose(by_col, (0, 2, 1)).reshape(ch * LANES, npad)
    by_row = _bucket_sum(vals2, jnp.asarray(oh_row), "rpb_grad_rows")
    return jnp.transpose(by_row.reshape(ch, LANES, LANES), (0, 2, 1))[:, :2 * NA_ROWS - 1, :2 * NA_COLS - 1]


def _s5_discretize(lam_re, lam_im, log_step, b_re, b_im):
    step = jnp.exp(log_step.astype(F32))[:, None]
    lr = jnp.minimum(lam_re.astype(F32), -1e-4)
    li = lam_im.astype(F32)
    mag = jnp.exp(lr * step)
    ab_re = mag * jnp.cos(li * step)
    ab_im = mag * jnp.sin(li * step)
    den = lr * lr + li * li
    zr = ((ab_re - 1.0) * lr + ab_im * li) / den
    zi = (ab_im * lr - (ab_re - 1.0) * li) / den
    br = b_re.astype(F32)
    bi = b_im.astype(F32)
    return ab_re, ab_im, zr[..., None] * br - zi[..., None] * bi, zr[..., None] * bi + zi[..., None] * br


def _scan_tables(a_re, a_im, rev):
    ar, ai = a_re.reshape(-1), a_im.reshape(-1)
    pows = [(ar, ai)]
    for _ in range(SUBLANES - 1):
        pr, pi = pows[-1]
        pows.append((pr * ar - pi * ai, pr * ai + pi * ar))
    row = jnp.arange(SUBLANES)[:, None]
    tabs = []
    for k in (1, 2, 4):
        keep = (row < SUBLANES - k) if rev else (row >= k)
        tabs += [jnp.where(keep, pows[k - 1][0][None, :], 0.0), jnp.where(keep, pows[k - 1][1][None, :], 0.0)]
    order = list(range(SUBLANES - 1, -1, -1)) if rev else list(range(SUBLANES))
    tabs += [jnp.stack([pows[i][0] for i in order]), jnp.stack([pows[i][1] for i in order])]
    t = jnp.stack(tabs)
    nblk = t.shape[-1] // (4 * LANES)
    return jnp.transpose(t.reshape(8, SUBLANES, nblk, 4 * LANES), (2, 0, 1, 3))


def _block_diag(w):
    g, a, b = w.shape
    nblk = g // S5_GROUPS_PER_BLOCK
    eye = jnp.eye(S5_GROUPS_PER_BLOCK, dtype=w.dtype)
    w4 = w.reshape(nblk, S5_GROUPS_PER_BLOCK, a, b)
    return (w4[:, :, :, None, :] * eye[None, :, None, :, None]).reshape(nblk, S5_GROUPS_PER_BLOCK * a, S5_GROUPS_PER_BLOCK * b)


def _block_diag_take(w, a, b):
    nblk = w.shape[0]
    w5 = w.reshape(nblk, S5_GROUPS_PER_BLOCK, a, S5_GROUPS_PER_BLOCK, b)
    eye = jnp.eye(S5_GROUPS_PER_BLOCK, dtype=w.dtype)
    return jnp.sum(w5 * eye[None, :, None, :, None], axis=3).reshape(nblk * S5_GROUPS_PER_BLOCK, a, b)


def _scan_tile(r, i, tab_ref, carry, rev):
    for lvl, k in enumerate((1, 2, 4)):
        mr, mi = tab_ref[2 * lvl], tab_ref[2 * lvl + 1]
        sh = SUBLANES - k if rev else k
        rr, ri = pltpu.roll(r, sh, 0), pltpu.roll(i, sh, 0)
        r, i = r + (mr * rr - mi * ri), i + (mr * ri + mi * rr)
    pr, pi = tab_ref[6], tab_ref[7]
    cr, ci = carry
    return r + (pr * cr - pi * ci), i + (pr * ci + pi * cr)


def _s5_fwd(proj, ucol0, tabs, bre, bim, cre, cim, rev, final, name):
    s = proj.shape[0]
    nblk = tabs.shape[0]
    bw = nblk * LANES
    w = 4 * LANES
    t = _tile(s, S5_CHUNK)
    nc, nt = s // t, t // SUBLANES
    ub = ucol0 // LANES
    cm = (lambda c: nc - 1 - c) if rev else (lambda c: c)
    last = 0 if rev else SUBLANES - 1

    def body(u_ref, tab_ref, bre_ref, bim_ref, cre_ref, cim_ref, *rest):
        if final is not None:
            yo_ref, d_ref, y_ref, xr_ref, xi_ref, xr_s, xi_s, car_r, car_i = rest
        else:
            y_ref, xr_ref, xi_ref, xr_s, xi_s, car_r, car_i = rest

        @pl.when(pl.program_id(1) == 0)
        def _():
            car_r[...] = jnp.zeros_like(car_r)
            car_i[...] = jnp.zeros_like(car_i)

        u = u_ref[...]
        xr_s[...] = _nn(u, bre_ref[...])
        xi_s[...] = _nn(u, bim_ref[...])

        def tile(tt, carry):
            k = nt - 1 - tt if rev else tt
            rows = pl.ds(pl.multiple_of(k * SUBLANES, SUBLANES), SUBLANES)
            r, i = _scan_tile(xr_s[rows, :], xi_s[rows, :], tab_ref, carry, rev)
            xr_s[rows, :] = r
            xi_s[rows, :] = i
            return (jnp.broadcast_to(r[last:last + 1, :], r.shape), jnp.broadcast_to(i[last:last + 1, :], i.shape))

        carry = lax.fori_loop(0, nt, tile, (car_r[...], car_i[...]))
        car_r[...], car_i[...] = carry
        xr, xi = xr_s[...].astype(BF16), xi_s[...].astype(BF16)
        y = _nn(xr, cre_ref[...]) - _nn(xi, cim_ref[...])
        if final is not None:
            y = y + yo_ref[...] + d_ref[...] * u.astype(F32)
        y_ref[...] = y
        xr_ref[...] = xr
        xi_ref[...] = xi

    yspec = pl.BlockSpec((t, LANES), lambda j, c: (cm(c), j))
    xspec = pl.BlockSpec((t, w), lambda j, c: (cm(c), j))
    in_specs = [pl.BlockSpec((t, LANES), lambda j, c: (cm(c), ub + j)),
                pl.BlockSpec((None, 8, SUBLANES, w), lambda j, c: (j, 0, 0, 0)),
                pl.BlockSpec((None, LANES, w), lambda j, c: (j, 0, 0)), pl.BlockSpec((None, LANES, w), lambda j, c: (j, 0, 0)),
                pl.BlockSpec((None, w, LANES), lambda j, c: (j, 0, 0)), pl.BlockSpec((None, w, LANES), lambda j, c: (j, 0, 0))]
    args = [proj, tabs, bre, bim, cre, cim]
    if final is not None:
        in_specs += [yspec, pl.BlockSpec((1, LANES), lambda j, c: (0, j))]
        args += [final[0], final[1].reshape(1, bw)]
    return pl.pallas_call(
        body, grid=(nblk, nc), in_specs=in_specs, out_specs=[yspec, xspec, xspec],
        out_shape=[jax.ShapeDtypeStruct((s, bw), F32), jax.ShapeDtypeStruct((s, nblk * w), BF16),
                   jax.ShapeDtypeStruct((s, nblk * w), BF16)],
        scratch_shapes=[pltpu.VMEM((t, w), F32), pltpu.VMEM((t, w), F32), pltpu.VMEM((SUBLANES, w), F32),
                        pltpu.VMEM((SUBLANES, w), F32)],
        compiler_params=_cparams(("parallel", "arbitrary")), name=name)(*args)


def _s5_bwd(proj, ucol0, dy, xr, xi, gtabs, bre, bim, cre, cim, rev, final, name):
    s = proj.shape[0]
    nblk = gtabs.shape[0]
    bw = nblk * LANES
    w = 4 * LANES
    t = _tile(s, S5_CHUNK)
    nc, nt = s // t, t // SUBLANES
    ub = ucol0 // LANES
    grev = not rev
    cm = (lambda c: nc - 1 - c) if grev else (lambda c: c)
    last = 0 if grev else SUBLANES - 1
    nfin = 2 if final is not None else 0

    def body(dy_ref, u_ref, xr_ref, xi_ref, tab_ref, bre_ref, bim_ref, cre_ref, cim_ref, *rest):
        fin, rest = rest[:nfin], rest[nfin:]
        du_ref, dar_ref, dai_ref, dbr_ref, dbi_ref, dcr_ref, dci_ref = rest[:7]
        rest = rest[7:]
        if final is not None:
            dd_ref, rest = rest[0], rest[1:]
        gr_s, gi_s, xr_s, xi_s, car_r, car_i = rest

        @pl.when(pl.program_id(1) == 0)
        def _():
            for ref in (car_r, car_i, dar_ref, dai_ref, dbr_ref, dbi_ref, dcr_ref, dci_ref):
                ref[...] = jnp.zeros_like(ref)
            if final is not None:
                dd_ref[...] = jnp.zeros_like(dd_ref)

        dyv = dy_ref[...]
        dyb = dyv.astype(BF16)
        u = u_ref[...]
        xrb, xib = xr_ref[...], xi_ref[...]
        gr_s[...] = _nt(dyb, cre_ref[...])
        gi_s[...] = -_nt(dyb, cim_ref[...])
        xr_s[...] = xrb.astype(F32)
        xi_s[...] = xib.astype(F32)
        rowid = lax.broadcasted_iota(jnp.int32, (SUBLANES, w), 0)

        def tile(tt, carry):
            k = nt - 1 - tt if grev else tt
            rows = pl.ds(pl.multiple_of(k * SUBLANES, SUBLANES), SUBLANES)
            r, i = _scan_tile(gr_s[rows, :], gi_s[rows, :], tab_ref, carry, grev)
            gr_s[rows, :] = r
            gi_s[rows, :] = i
            if grev:
                er = jnp.where(rowid == SUBLANES - 1, carry[0], pltpu.roll(r, SUBLANES - 1, 0))
                ei = jnp.where(rowid == SUBLANES - 1, carry[1], pltpu.roll(i, SUBLANES - 1, 0))
            else:
                er = jnp.where(rowid == 0, carry[0], pltpu.roll(r, 1, 0))
                ei = jnp.where(rowid == 0, carry[1], pltpu.roll(i, 1, 0))
            sr, si = xr_s[rows, :], xi_s[rows, :]
            dar_ref[...] += er * sr + ei * si
            dai_ref[...] += ei * sr - er * si
            return (jnp.broadcast_to(r[last:last + 1, :], r.shape), jnp.broadcast_to(i[last:last + 1, :], i.shape))

        carry = lax.fori_loop(0, nt, tile, (car_r[...], car_i[...]))
        car_r[...], car_i[...] = carry
        gr, gi = gr_s[...].astype(BF16), gi_s[...].astype(BF16)
        du = _nt(gr, bre_ref[...]) + _nt(gi, bim_ref[...])
        if final is not None:
            du = du + fin[0][...] + fin[1][...] * dyv.astype(F32)
            dd_ref[...] += jnp.sum(dyv.astype(F32) * u.astype(F32), axis=0, keepdims=True)
        du_ref[...] = du.astype(du_ref.dtype)
        dbr_ref[...] += _tn(u, gr)
        dbi_ref[...] += _tn(u, gi)
        dcr_ref[...] += _tn(xrb, dyb)
        dci_ref[...] -= _tn(xib, dyb)

    yspec = pl.BlockSpec((t, LANES), lambda j, c: (cm(c), j))
    xspec = pl.BlockSpec((t, w), lambda j, c: (cm(c), j))
    bspec = pl.BlockSpec((None, LANES, w), lambda j, c: (j, 0, 0))
    cspec = pl.BlockSpec((None, w, LANES), lambda j, c: (j, 0, 0))
    aspec = pl.BlockSpec((None, SUBLANES, w), lambda j, c: (j, 0, 0))
    dspec = pl.BlockSpec((1, LANES), lambda j, c: (0, j))
    in_specs = [yspec, pl.BlockSpec((t, LANES), lambda j, c: (cm(c), ub + j)), xspec, xspec,
                pl.BlockSpec((None, 8, SUBLANES, w), lambda j, c: (j, 0, 0, 0)), bspec, bspec, cspec, cspec]
    args = [dy, proj, xr, xi, gtabs, bre, bim, cre, cim]
    out_specs = [yspec, aspec, aspec, bspec, bspec, cspec, cspec]
    du_dtype = BF16 if final is not None else F32
    out_shape = [jax.ShapeDtypeStruct((s, bw), du_dtype)] + [jax.ShapeDtypeStruct((nblk, SUBLANES, w), F32)] * 2 \
        + [jax.ShapeDtypeStruct((nblk, LANES, w), F32)] * 2 + [jax.ShapeDtypeStruct((nblk, w, LANES), F32)] * 2
    if final is not None:
        in_specs += [yspec, dspec]
        args += [final[0], final[1].reshape(1, bw)]
        out_specs.append(dspec)
        out_shape.append(jax.ShapeDtypeStruct((1, bw), F32))
    return pl.pallas_call(
        body, grid=(nblk, nc), in_specs=in_specs, out_specs=out_specs, out_shape=out_shape,
        scratch_shapes=[pltpu.VMEM((t, w), F32)] * 4 + [pltpu.VMEM((SUBLANES, w), F32)] * 2,
        compiler_params=_cparams(("parallel", "arbitrary")), name=name)(*args)


N_CHIPS = 4


def _place():
    mx, my, mc = lax.axis_index("x"), lax.axis_index("y"), lax.axis_index("c")
    return (mx, my, mc), (mx, my, 1 - mc), [(1 - mx, my), (mx, 1 - my), (1 - mx, 1 - my)]


def _gather_op(x, cols=False):
    r, c = x.shape

    def run(which, ins, outs, sems):
        (x_ref,), (out_ref,), (send_sems, recv_sems, local_sem) = ins, outs, sems
        me, sibling, chips = _place()
        mc = me[2]

        def slot(px, py, pc):
            d = 4 * px + 2 * py + pc
            return out_ref.at[:, pl.ds(pl.multiple_of(d * c, LANES), c)] if cols else out_ref.at[d]

        def copy(k, block, to, src=None):
            return pltpu.make_async_remote_copy(src_ref=slot(*block) if src is None else src, dst_ref=slot(*block),
                                                send_sem=send_sems.at[k], recv_sem=recv_sems.at[k],
                                                device_id=to, device_id_type=MESH)

        def mine():
            return pltpu.make_async_copy(x_ref, slot(*me), local_sem)

        def first():
            return [copy(0, me, sibling, src=x_ref)] + [copy(1 + j, me, (*chip, mc), src=x_ref) for j, chip in enumerate(chips)]

        def passed():
            return [copy(4 + j, (*chip, mc), sibling) for j, chip in enumerate(chips)]

        if which == "start":
            mine().start()
            for cp in first():
                cp.start()
        elif which == "mid":
            for j, (chip, cp) in enumerate(zip(chips, passed())):
                copy(1 + j, (*chip, mc), me).wait_recv()
                cp.start()
        else:
            copy(0, sibling, me).wait_recv()
            for j, chip in enumerate(chips):
                copy(4 + j, (*chip, 1 - mc), me).wait_recv()
            for cp in first() + passed():
                cp.wait_send()
            mine().wait()

    return dict(ins=[x], outs=[jax.ShapeDtypeStruct((r, N_DEV * c) if cols else (N_DEV, r, c), x.dtype)], run=run,
                sems=[pltpu.SemaphoreType.DMA((N_DEV - 1,)), pltpu.SemaphoreType.DMA((N_DEV - 1,)),
                      pltpu.SemaphoreType.DMA])


def _pair_op(gs):
    nt = len(gs)

    def run(which, g_refs, out_refs, sems):
        if which == "mid":
            return
        send_sems, recv_sems = sems
        me, sibling, _ = _place()
        copies = [pltpu.make_async_remote_copy(src_ref=g_refs[t].at[2 * q + (1 - me[2])], dst_ref=out_refs[t].at[q],
                                               send_sem=send_sems.at[t, q], recv_sem=recv_sems.at[t, q],
                                               device_id=sibling, device_id_type=MESH)
                  for t in range(nt) for q in range(N_CHIPS)]
        if which == "start":
            for cp in copies:
                cp.start()
        elif which == "wait":
            for cp in copies:
                cp.wait_recv()
            for cp in copies:
                cp.wait_send()

    return dict(ins=list(gs), outs=[jax.ShapeDtypeStruct((N_CHIPS,) + g.shape[1:], F32) for g in gs], run=run,
                sems=[pltpu.SemaphoreType.DMA((nt, N_CHIPS)), pltpu.SemaphoreType.DMA((nt, N_CHIPS))])


def _pair_add(g, t, core, name):
    _, r, c = g.shape
    tr = r
    while tr * c > 512 * 1024 and tr % 32 == 0:
        tr //= 2

    def body(core_ref, g_ref, t_ref, o_ref):
        o_ref[...] = (g_ref[...] + t_ref[...]).astype(BF16)

    return pl.pallas_call(
        body,
        grid_spec=pltpu.PrefetchScalarGridSpec(
            num_scalar_prefetch=1, grid=(N_CHIPS, r // tr),
            in_specs=[pl.BlockSpec((None, tr, c), lambda q, i, core_ref: (2 * q + core_ref[0], i, 0)),
                      pl.BlockSpec((None, tr, c), lambda q, i, core_ref: (q, i, 0))],
            out_specs=pl.BlockSpec((None, tr, c), lambda q, i, core_ref: (q, i, 0))),
        out_shape=jax.ShapeDtypeStruct((N_CHIPS, r, c), BF16),
        compiler_params=_cparams(("parallel", "parallel")), name=name)(core, g, t)


def _chip_op(ps):
    nt = len(ps)

    def run(which, p_refs, out_refs, sems):
        if which == "mid":
            return
        send_sems, recv_sems, local_sems = sems
        me, _, chips = _place()
        mychip = 2 * me[0] + me[1]
        owns = [pltpu.make_async_copy(p_refs[t].at[mychip], out_refs[t].at[mychip], local_sems.at[t]) for t in range(nt)]
        sends = [pltpu.make_async_remote_copy(src_ref=p_refs[t].at[2 * px + py], dst_ref=out_refs[t].at[mychip],
                                              send_sem=send_sems.at[t, j], recv_sem=recv_sems.at[t, j],
                                              device_id=(px, py, me[2]), device_id_type=MESH)
                 for t in range(nt) for j, (px, py) in enumerate(chips)]
        if which == "start":
            for cp in owns + sends:
                cp.start()
        elif which == "wait":
            for t in range(nt):
                for j, (px, py) in enumerate(chips):
                    pltpu.make_async_remote_copy(src_ref=p_refs[t].at[2 * px + py], dst_ref=out_refs[t].at[2 * px + py],
                                                 send_sem=send_sems.at[t, j], recv_sem=recv_sems.at[t, j],
                                                 device_id=(px, py, me[2]), device_id_type=MESH).wait_recv()
            for cp in sends:
                cp.wait_send()
            for cp in owns:
                cp.wait()

    return dict(ins=list(ps), outs=[jax.ShapeDtypeStruct(p.shape, p.dtype) for p in ps], run=run,
                sems=[pltpu.SemaphoreType.DMA((nt, N_CHIPS - 1)), pltpu.SemaphoreType.DMA((nt, N_CHIPS - 1)),
                      pltpu.SemaphoreType.DMA((nt,))])


def _adamw(w, gstack, m, v, name, layer=None, prev=None):
    r, c = w.shape[-2:]
    nstack = gstack.shape[0]
    tr = r
    while tr * c > 128 * 1024 and tr % 32 == 0:
        tr //= 2
    c1 = 1.0 - ADAM_B1 ** ADAM_STEP
    c2 = 1.0 - ADAM_B2 ** ADAM_STEP

    def body(w_ref, g_ref, m_ref, v_ref, *rest):
        go_ref, d_ref, mo_ref, vo_ref = rest[-4:]
        g = g_ref[0].astype(F32)
        for p in range(1, nstack):
            g = g + g_ref[p].astype(F32)
        mn = ADAM_B1 * m_ref[...] + (1.0 - ADAM_B1) * g
        vn = ADAM_B2 * v_ref[...] + (1.0 - ADAM_B2) * (g * g)
        go_ref[...] = g
        mo_ref[...] = mn
        vo_ref[...] = vn
        d_ref[...] = -ADAM_LR * ((mn / c1) / (jnp.sqrt(vn / c2) + ADAM_EPS) + ADAM_WD * w_ref[...])

    if layer is None:
        spec = pl.BlockSpec((tr, c), lambda i: (i, 0))
        full = (r, c)
    else:
        spec = pl.BlockSpec((None, tr, c), lambda i: (layer, i, 0))
        full = w.shape
    prev = list(prev) if prev is not None else []
    return pl.pallas_call(
        body, grid=(r // tr,),
        in_specs=[spec, pl.BlockSpec((nstack, tr, c), lambda i: (0, i, 0)), spec, spec] + [pl.BlockSpec(memory_space=pl.ANY)] * len(prev),
        out_specs=[spec] * 4, out_shape=[jax.ShapeDtypeStruct(full, F32)] * 4,
        input_output_aliases={4 + n: n for n in range(len(prev))},
        compiler_params=_cparams(("parallel",)), name=name)(w, gstack, m, v, *prev)


def _s5_tables(lam_re, lam_im, log_step, b_re, b_im, c_re, c_im):
    out = []
    for d in range(2):
        ab_re, ab_im, bb_re, bb_im = _s5_discretize(lam_re[d], lam_im[d], log_step[d], b_re, b_im)
        rev = d == 1
        out.append(dict(
            tabs=_scan_tables(ab_re, ab_im, rev), gtabs=_scan_tables(ab_re, -ab_im, not rev),
            bre=_block_diag(jnp.transpose(bb_re, (0, 2, 1))).astype(BF16), bim=_block_diag(jnp.transpose(bb_im, (0, 2, 1))).astype(BF16),
            cre=_block_diag(jnp.transpose(c_re[d], (0, 2, 1))).astype(BF16), cim=_block_diag(jnp.transpose(c_im[d], (0, 2, 1))).astype(BF16)))
    return out


def _layer_tensors(i):
    j = i // 2
    mixer = [("ab_w_in", j), ("ab_w_out", j), ("s5_w_glu", j)] if i % 2 == 0 else [("c_w_qkv", j), ("c_w_out", j)]
    return mixer + [("mlp_w1", i), ("mlp_w2", i)]


class _WeightGather:
    def __init__(self, shards, depth):
        self.shards, self.depth, self.ops = shards, depth, {}

    def _op(self, key):
        self.ops[key] = _gather_op(self.shards[key[0]][key[1]], cols=key[0] not in ROW_SHARDED)
        return self.ops[key]

    def start(self):
        _run_comm([self._op(key) for key in _layer_tensors(0)[:-2]], "gather_mixer0")

    def carry(self, i, slot):
        t = _layer_tensors(i)
        nxt = _layer_tensors(i + 1)[:-2] if i + 1 < self.depth else []
        plan = {"in": t[-2:-1], "up": t[-1:], "down": nxt}
        return [self._op(key) for key in plan[slot]]

    def get(self, name, l):
        full = self.ops[(name, l)]["res"][0]
        return full.reshape(-1, full.shape[-1]) if name in ROW_SHARDED else full


class _GradExchange:
    def __init__(self, core, depth):
        self.core, self.depth, self.g, self.recv_ops, self.pairs, self.ps = core, depth, {}, [], {}, {}

    def put(self, name, l, g):
        self.g[(name, l)] = g.reshape(N_DEV, -1, g.shape[-1])

    def _pair(self, keys):
        op = _pair_op([self.g[key] for key in keys])
        for n, key in enumerate(keys):
            self.pairs[key] = (op, n)
        return [op]

    def _chip(self, keys):
        for key in keys:
            op, n = self.pairs[key]
            self.ps[key] = _pair_add(self.g[key], op["res"][n], self.core, f"pair_add_{key[0]}{key[1]}")
        op = _chip_op([self.ps[key] for key in keys])
        self.recv_ops.append((keys, op))
        return [op]

    def carry(self, i, slot):
        t = _layer_tensors(i)
        later = _layer_tensors(i + 1)[:-2] if i + 1 < self.depth else []
        if slot == "up_bwd":
            return self._pair(t[-2:])
        if slot == "in_grad":
            return self._chip(t[-2:-1])
        if slot == "in_bwd":
            return self._chip(t[-1:])
        if slot == "down_bwd":
            return self._pair(later) if later else ()
        return self._chip(later) if later else ()

    def finish(self):
        keys = _layer_tensors(0)[:-2]
        _run_comm(self._pair(keys), "pair_exchange_mixer0")
        _run_comm(self._chip(keys), "chip_exchange_mixer0")
        return {key: op["res"][n] for keys, op in self.recv_ops for n, key in enumerate(keys)}


def _forward_backward(x, target, p, wsrc, gsink):
    s, d = x.shape
    depth = p["norm_mix"].shape[0]
    ah = p["t5_bias"].shape[1]
    aw = ah * HEAD_DIM
    ch = p["c_rpb"].shape[1]
    groups, pstate = p["s5_lam_re"].shape[2:]
    bw = groups * S5_GROUP
    assert aw + bw == d and ch * HEAD_DIM == d

    dil_bias = [_dil_bias(p["t5_bias"], dil) for _, dil in DILATED_BRANCHES]
    saved = []
    for i in range(depth):
        j = i // 2
        sv = dict(x=x)
        hn = _rms_fwd(x, p["norm_mix"][i], f"norm_mix_fwd{i}")
        sv["hn"] = hn
        if i % 2 == 0:
            proj = _mm_cols(f"ab_in_fwd{i}", hn, wsrc.get("ab_w_in", j), comm=wsrc.carry(i, "in"))
            outs = [(_dil_fwd if dil == 1 else _dil_fwd_strided)(proj, dil_bias[b][0], dil, ah, f"dil_fwd_d{dil}_{i}")
                    for b, (_, dil) in enumerate(DILATED_BRANCHES)]
            o_a, lse = _dil_merge([o for o, _ in outs], [l for _, l in outs])
            tb = _s5_tables(p["s5_lam_re"][j], p["s5_lam_im"][j], p["s5_log_step"][j], p["s5_b_re"][j], p["s5_b_im"][j],
                            p["s5_c_re"][j], p["s5_c_im"][j])
            y0, x0r, x0i = _s5_fwd(proj, 3 * aw, tb[0]["tabs"], tb[0]["bre"], tb[0]["bim"], tb[0]["cre"], tb[0]["cim"],
                                   False, None, f"s5_fwd_a{i}")
            y_pre, x1r, x1i = _s5_fwd(proj, 3 * aw, tb[1]["tabs"], tb[1]["bre"], tb[1]["bim"], tb[1]["cre"], tb[1]["cim"],
                                      True, (y0, p["s5_d"][j]), f"s5_fwd_b{i}")
            o_b = _mm(f"glu_fwd{i}", y_pre, wsrc.get("s5_w_glu", j), a_fn=_gelu, extras=(y_pre,), out_dtypes=(BF16,),
                      epi=lambda acc, yp: (_gelu(yp) * jax.nn.sigmoid(acc),))[0]
            merged = jnp.concatenate([o_a, o_b], axis=1)
            x = _mm(f"ab_out_fwd{i}", merged, wsrc.get("ab_w_out", j), extras=(x,), epi=lambda acc, xr: (acc + xr,))[0]
            sv.update(proj=proj, o_a=o_a, lse=lse, tb=tb, states=((x0r, x0i), (x1r, x1i)), y_pre=y_pre, merged=merged)
        else:
            qkv = _mm_cols(f"c_qkv_fwd{i}", hn, wsrc.get("c_w_qkv", j), comm=wsrc.carry(i, "in"))
            nbias = _na_bias(p["c_rpb"][j])
            o, lse = _na_fwd(qkv, nbias, ch, f"na_fwd{i}")
            x = _mm(f"c_out_fwd{i}", o, wsrc.get("c_w_out", j), extras=(x,), epi=lambda acc, xr: (acc + xr,))[0]
            sv.update(qkv=qkv, o=o, lse=lse, nbias=nbias)
        sv["x_mid"] = x
        hn2 = _rms_fwd(x, p["norm_mlp"][i], f"norm_mlp_fwd{i}")
        h_pre = _mm_cols(f"mlp_up_fwd{i}", hn2, wsrc.get("mlp_w1", i), comm=wsrc.carry(i, "up"))
        x = _mm(f"mlp_down_fwd{i}", h_pre, wsrc.get("mlp_w2", i), a_fn=_relu_sq, extras=(x,), epi=lambda acc, xr: (acc + xr,),
                comm=wsrc.carry(i, "down"))[0]
        sv.update(hn2=hn2, h_pre=h_pre)
        saved.append(sv)

    loss_sum, dx, g_final = _final_loss(x, p["norm_final"], target)

    g = ({k: [None] * p[k].shape[0] for k in ("norm_mix", "norm_mlp", "s5_lam_re", "s5_lam_im", "s5_log_step", "s5_b_re",
                                                  "s5_b_im", "s5_c_re", "s5_c_im", "s5_d", "c_rpb")})
    g_t5 = jnp.zeros_like(p["t5_bias"], dtype=F32)
    for i in reversed(range(depth)):
        j = i // 2
        sv = saved[i]
        dh = _mm(f"mlp_down_bwd{i}", dx, wsrc.get("mlp_w2", i), tb=True, extras=(sv["h_pre"],), out_dtypes=(BF16,),
                 epi=lambda acc, hp: (acc * (2.0 * jnp.maximum(hp.astype(F32), 0.0)),), comm=gsink.carry(i, "down_bwd"))[0]
        gsink.put("mlp_w2", i, _mm(f"mlp_w2_grad{i}", sv["h_pre"], dx, ta=True, a_fn=_relu_sq)[0])
        gsink.put("mlp_w1", i, _mm_cols_grad(f"mlp_w1_grad{i}", sv["hn2"], dh, comm=gsink.carry(i, "w1_grad")))
        dhn2 = _mm_cols_t(f"mlp_up_bwd{i}", dh, wsrc.get("mlp_w1", i), comm=gsink.carry(i, "up_bwd"))
        dx, gn = _rms_bwd(sv["x_mid"], p["norm_mlp"][i], dhn2, dx, f"norm_mlp_bwd{i}")
        g["norm_mlp"][i] = gn[0]
        if i % 2 == 0:
            tb = sv["tb"]
            dmerged = _mm(f"ab_out_bwd{i}", dx, wsrc.get("ab_w_out", j), tb=True, out_dtypes=(BF16,))[0]
            gsink.put("ab_w_out", j, _mm(f"ab_w_out_grad{i}", sv["merged"], dx, ta=True)[0])
            def glu_epi(acc, yp, dob):
                sg = jax.nn.sigmoid(acc)
                dob = dob.astype(F32)
                return dob * _gelu(yp) * sg * (1.0 - sg), dob * sg
            dz, t1 = _mm(f"glu_bwd_z{i}", sv["y_pre"], wsrc.get("s5_w_glu", j), a_fn=_gelu, extras=(sv["y_pre"], dmerged),
                         extra_cols=(0, aw), epi=glu_epi, out_dtypes=(BF16, F32))
            dy_pre = _mm(f"glu_bwd_y{i}", dz, wsrc.get("s5_w_glu", j), tb=True, extras=(t1, sv["y_pre"]),
                         epi=lambda acc, t, yp: ((acc + t) * _gelu_grad(yp),), out_dtypes=(BF16,))[0]
            gsink.put("s5_w_glu", j, _mm(f"glu_w_grad{i}", sv["y_pre"], dz, ta=True, a_fn=_gelu)[0])
            r0 = _s5_bwd(sv["proj"], 3 * aw, dy_pre, *sv["states"][0], tb[0]["gtabs"], tb[0]["bre"], tb[0]["bim"],
                         tb[0]["cre"], tb[0]["cim"], False, None, f"s5_bwd_a{i}")
            r1 = _s5_bwd(sv["proj"], 3 * aw, dy_pre, *sv["states"][1], tb[1]["gtabs"], tb[1]["bre"], tb[1]["bim"],
                         tb[1]["cre"], tb[1]["cim"], True, (r0[0], p["s5_d"][j]), f"s5_bwd_b{i}")
            du = r1[0]
            g["s5_d"][j] = r1[7][0]
            gl_re, gl_im, gls, gb_re, gb_im, gc_re, gc_im = [], [], [], 0.0, 0.0, [], []
            for dnum, rr in enumerate((r0, r1)):
                da_re = jnp.sum(rr[1], axis=1).reshape(groups, pstate)
                da_im = jnp.sum(rr[2], axis=1).reshape(groups, pstate)
                dbb_re = jnp.transpose(_block_diag_take(rr[3], S5_GROUP, pstate), (0, 2, 1))
                dbb_im = jnp.transpose(_block_diag_take(rr[4], S5_GROUP, pstate), (0, 2, 1))
                _, vjp = jax.vjp(_s5_discretize, p["s5_lam_re"][j][dnum], p["s5_lam_im"][j][dnum], p["s5_log_step"][j][dnum],
                                 p["s5_b_re"][j], p["s5_b_im"][j])
                a, b, c, e, f = vjp((da_re, da_im, dbb_re, dbb_im))
                gl_re.append(a)
                gl_im.append(b)
                gls.append(c)
                gb_re, gb_im = gb_re + e, gb_im + f
                gc_re.append(jnp.transpose(_block_diag_take(rr[5], pstate, S5_GROUP), (0, 2, 1)))
                gc_im.append(jnp.transpose(_block_diag_take(rr[6], pstate, S5_GROUP), (0, 2, 1)))
            g["s5_lam_re"][j], g["s5_lam_im"][j], g["s5_log_step"][j] = jnp.stack(gl_re), jnp.stack(gl_im), jnp.stack(gls)
            g["s5_b_re"][j], g["s5_b_im"][j] = gb_re, gb_im
            g["s5_c_re"][j], g["s5_c_im"][j] = jnp.stack(gc_re), jnp.stack(gc_im)
            delta = _head_delta(dmerged, 0, sv["o_a"], f"dil_delta{i}")
            acc = None
            for b, (_, dil) in enumerate(DILATED_BRANCHES):
                args = (sv["proj"], dmerged, sv["lse"], delta, dil_bias[b][0], dil_bias[b][1], dil, ah, f"dil_bwd_d{dil}_{i}")
                if dil == 1:
                    assert acc is None
                    *acc, db = _dil_bwd(*args)
                else:
                    *acc, db = _dil_bwd_strided(*args, acc, BF16 if b == len(DILATED_BRANCHES) - 1 else F32)
                g_t5 = g_t5 + _t5_grad(db, dil)
            dproj = jnp.concatenate([*acc, du], axis=1)
            gsink.put("ab_w_in", j, _mm_cols_grad(f"ab_w_in_grad{i}", sv["hn"], dproj, comm=gsink.carry(i, "in_grad")))
            dhn = _mm_cols_t(f"ab_in_bwd{i}", dproj, wsrc.get("ab_w_in", j), comm=gsink.carry(i, "in_bwd"))
        else:
            do = _mm(f"c_out_bwd{i}", dx, wsrc.get("c_w_out", j), tb=True, out_dtypes=(BF16,))[0]
            gsink.put("c_w_out", j, _mm(f"c_w_out_grad{i}", sv["o"], dx, ta=True)[0])
            dq, dk, dv, db = _na_bwd(sv["qkv"], sv["o"], do, sv["lse"], sv["nbias"], ch, f"na_bwd{i}")
            g["c_rpb"][j] = _rpb_grad(db)
            dqkv = jnp.concatenate([dq, dk, dv], axis=1)
            gsink.put("c_w_qkv", j, _mm_cols_grad(f"c_w_qkv_grad{i}", sv["hn"], dqkv, comm=gsink.carry(i, "in_grad")))
            dhn = _mm_cols_t(f"c_qkv_bwd{i}", dqkv, wsrc.get("c_w_qkv", j), comm=gsink.carry(i, "in_bwd"))
        dx, gn = _rms_bwd(sv["x"], p["norm_mix"][i], dhn, dx, f"norm_mix_bwd{i}")
        g["norm_mix"][i] = gn[0]
    g["t5_bias"] = g_t5
    g["norm_final"] = g_final[0]
    return loss_sum[0, 0], dx, g


BIG = ("ab_w_in", "ab_w_out", "s5_w_glu", "c_w_qkv", "c_w_out", "mlp_w1", "mlp_w2")
ROW_SHARDED = ("ab_w_out", "s5_w_glu", "c_w_out", "mlp_w2")
WEIGHTS = ("t5_bias", "ab_w_in", "ab_w_out", "s5_lam_re", "s5_lam_im", "s5_log_step", "s5_b_re", "s5_b_im", "s5_c_re",
           "s5_c_im", "s5_d", "s5_w_glu", "c_w_qkv", "c_w_out", "c_rpb", "norm_mix", "norm_mlp", "mlp_w1", "mlp_w2",
           "norm_final")


def _step(x, target, w, m, v):
    d = x.shape[-1]
    depth = w["norm_mix"].shape[0]
    wsrc = _WeightGather({k: w[k].astype(BF16) for k in BIG}, depth)
    wsrc.start()
    gsink = _GradExchange(lax.axis_index("c").astype(jnp.int32).reshape(1), depth)
    small = {k: w[k] for k in WEIGHTS if k not in BIG}
    loss_sum, dx, g = _forward_backward(x[0], target[0], small, wsrc, gsink)
    loss = lax.psum(0.5 * loss_sum / d, ("x", "y", "c"))

    out = {}
    recv = gsink.finish()
    for k in BIG:
        res = None
        for l in range(w[k].shape[0]):
            res = _adamw(w[k], recv[(k, l)], m[k], v[k], f"adamw_{k}{l}", layer=l, prev=res)
        out[k] = res
    names = [k for k in WEIGHTS if k not in BIG]
    def flat(tree):
        return jnp.concatenate([jnp.asarray(jnp.stack(tree[k]) if isinstance(tree[k], list) else tree[k], F32).reshape(-1)
                                for k in names])
    total = sum(int(np.prod(w[k].shape)) for k in names)
    rows = -(-total // LANES)
    rows = -(-rows // (2 * SUBLANES)) * (2 * SUBLANES)
    pad = rows * LANES - total
    def pack(tree):
        return jnp.pad(flat(tree), (0, pad)).reshape(rows, LANES)
    small_op = _gather_op(pack(g).astype(BF16))
    _run_comm([small_op], "gather_small_grads")
    res = _adamw(pack(w), small_op["res"][0], pack(m), pack(v), "adamw_small")
    off = 0
    for k in names:
        n = int(np.prod(w[k].shape))
        out[k] = [a.reshape(-1)[off:off + n].reshape(w[k].shape) for a in res]
        off += n
    return (loss, dx[None], *[out[k][0] for k in WEIGHTS], *[out[k][1] for k in WEIGHTS],
            *[out[k][2] for k in WEIGHTS], *[out[k][3] for k in WEIGHTS])


def kernel(x, t5_bias, ab_w_in, ab_w_out, s5_lam_re, s5_lam_im, s5_log_step, s5_b_re, s5_b_im, s5_c_re, s5_c_im, s5_d, s5_w_glu, c_w_qkv, c_w_out, c_rpb, norm_mix, norm_mlp, mlp_w1, mlp_w2, norm_final, loss_target, m_t5_bias, m_ab_w_in, m_ab_w_out, m_s5_lam_re, m_s5_lam_im, m_s5_log_step, m_s5_b_re, m_s5_b_im, m_s5_c_re, m_s5_c_im, m_s5_d, m_s5_w_glu, m_c_w_qkv, m_c_w_out, m_c_rpb, m_norm_mix, m_norm_mlp, m_mlp_w1, m_mlp_w2, m_norm_final, v_t5_bias, v_ab_w_in, v_ab_w_out, v_s5_lam_re, v_s5_lam_im, v_s5_log_step, v_s5_b_re, v_s5_b_im, v_s5_c_re, v_s5_c_im, v_s5_d, v_s5_w_glu, v_c_w_qkv, v_c_w_out, v_c_rpb, v_norm_mix, v_norm_mlp, v_mlp_w1, v_mlp_w2, v_norm_final):
    w = dict(t5_bias=t5_bias, ab_w_in=ab_w_in, ab_w_out=ab_w_out, s5_lam_re=s5_lam_re, s5_lam_im=s5_lam_im,
             s5_log_step=s5_log_step, s5_b_re=s5_b_re, s5_b_im=s5_b_im, s5_c_re=s5_c_re, s5_c_im=s5_c_im, s5_d=s5_d,
             s5_w_glu=s5_w_glu, c_w_qkv=c_w_qkv, c_w_out=c_w_out, c_rpb=c_rpb, norm_mix=norm_mix, norm_mlp=norm_mlp,
             mlp_w1=mlp_w1, mlp_w2=mlp_w2, norm_final=norm_final)
    m = dict(t5_bias=m_t5_bias, ab_w_in=m_ab_w_in, ab_w_out=m_ab_w_out, s5_lam_re=m_s5_lam_re, s5_lam_im=m_s5_lam_im,
             s5_log_step=m_s5_log_step, s5_b_re=m_s5_b_re, s5_b_im=m_s5_b_im, s5_c_re=m_s5_c_re, s5_c_im=m_s5_c_im,
             s5_d=m_s5_d, s5_w_glu=m_s5_w_glu, c_w_qkv=m_c_w_qkv, c_w_out=m_c_w_out, c_rpb=m_c_rpb, norm_mix=m_norm_mix,
             norm_mlp=m_norm_mlp, mlp_w1=m_mlp_w1, mlp_w2=m_mlp_w2, norm_final=m_norm_final)
    v = dict(t5_bias=v_t5_bias, ab_w_in=v_ab_w_in, ab_w_out=v_ab_w_out, s5_lam_re=v_s5_lam_re, s5_lam_im=v_s5_lam_im,
             s5_log_step=v_s5_log_step, s5_b_re=v_s5_b_re, s5_b_im=v_s5_b_im, s5_c_re=v_s5_c_re, s5_c_im=v_s5_c_im,
             s5_d=v_s5_d, s5_w_glu=v_s5_w_glu, c_w_qkv=v_c_w_qkv, c_w_out=v_c_w_out, c_rpb=v_c_rpb, norm_mix=v_norm_mix,
             norm_mlp=v_norm_mlp, mlp_w1=v_mlp_w1, mlp_w2=v_mlp_w2, norm_final=v_norm_final)
    return _step(x, loss_target, w, m, v)
```

```python
import math

import jax
import jax.numpy as jnp
import numpy as np
from jax import lax
from jax.experimental import pallas as pl
from jax.experimental.pallas import tpu as pltpu

F32 = jnp.float32
BF16 = jnp.bfloat16

N_DEV = 8
HEAD_DIM = 128
LANES = 128
QBLOCK = 128
DIL_HALF = 64
DILATED_BRANCHES = ((128, 1), (512, 4), (2048, 16))
S5_GROUP = 16
S5_GROUPS_PER_BLOCK = LANES // S5_GROUP
S5_CHUNK = 2048
SUBLANES = 8
GRID_W = 64
NA_ROWS = 8
NA_COLS = 16
NA_GROUP = 4
NA_WIN = NA_GROUP + NA_ROWS - 1
NA_GROUPS_PER_STEP = 16
T5_BUCKETS = 32
T5_MAX_DISTANCE = 1024
RMS_EPS = 1e-6
NEG_INF = -1e30
ADAM_LR = 0.001
ADAM_B1 = 0.9
ADAM_B2 = 0.999
ADAM_EPS = 1e-08
ADAM_WD = 0.01
ADAM_STEP = 10
VMEM_LIMIT_BYTES = 56 * 1024 * 1024
MESH = pl.DeviceIdType.MESH


def _cparams(sem=None):
    return pltpu.CompilerParams(dimension_semantics=sem, vmem_limit_bytes=VMEM_LIMIT_BYTES)


def _tile(dim, pref):
    t = min(dim, pref)
    while dim % t and t > LANES:
        t -= LANES
    assert dim % t == 0, (dim, pref)
    return t


def _dot(a, b, ca, cb):
    return lax.dot_general(a, b, (((ca,), (cb,)), ((), ())), preferred_element_type=F32)


def _nn(a, b):
    return _dot(a, b, 1, 0)


def _nt(a, b):
    return _dot(a, b, 1, 1)


def _tn(a, b):
    return _dot(a, b, 0, 0)


HBM_SPEC = pl.BlockSpec(memory_space=pltpu.HBM)


def _split_comm_refs(comm, in_refs, out_refs, sem_refs):
    parts, i, o, s = [], 0, 0, 0
    for op in comm:
        ni, no, ns = len(op["ins"]), len(op["outs"]), len(op["sems"])
        parts.append((in_refs[i:i + ni], out_refs[o:o + no], sem_refs[s:s + ns]))
        i, o, s = i + ni, o + no, s + ns
    return parts


def _mm_call(name, a, b, a_spec, b_spec, grid, nk, out_shapes, out_specs, acc_shape,
             ta=False, tb=False, a_fn=None, epi=None, extras=(), extra_specs=(), comm=()):
    ne, no = len(extras), len(out_shapes)
    comm_ins = [x for op in comm for x in op["ins"]]
    comm_outs = [x for op in comm for x in op["outs"]]
    comm_sems = [x for op in comm for x in op["sems"]]
    nci, nco = len(comm_ins), len(comm_outs)
    total = grid[0] * grid[1] * grid[2]

    def body(a_ref, b_ref, *rest):
        ex, rest = rest[:ne], rest[ne:]
        cin, rest = rest[:nci], rest[nci:]
        outs, rest = rest[:no], rest[no:]
        cout, rest = rest[:nco], rest[nco:]
        acc, csem = rest[0], rest[1:]
        k = pl.program_id(2)
        step = (pl.program_id(0) * grid[1] + pl.program_id(1)) * grid[2] + k
        parts = _split_comm_refs(comm, cin, cout, csem)

        def phase(which, at):
            if comm:
                @pl.when(step == at)
                def _():
                    for op, refs in zip(comm, parts):
                        op["run"](which, *refs)

        phase("start", 0)
        phase("mid", total - 1 - total // 8)

        @pl.when(k == 0)
        def _():
            acc[...] = jnp.zeros_like(acc)

        av = a_ref[...]
        if a_fn is not None:
            av = a_fn(av)
        acc[...] += _dot(av.astype(BF16), b_ref[...].astype(BF16), 0 if ta else 1, 1 if tb else 0)

        @pl.when(k == nk - 1)
        def _():
            r = acc[...]
            res = epi(r, *[e[...] for e in ex]) if epi is not None else (r,)
            for o, v in zip(outs, res):
                o[...] = v.astype(o.dtype)

        phase("wait", total - 1)

    res = pl.pallas_call(
        body, grid=grid, in_specs=[a_spec, b_spec, *extra_specs] + [HBM_SPEC] * nci,
        out_specs=list(out_specs) + [HBM_SPEC] * nco, out_shape=list(out_shapes) + comm_outs,
        scratch_shapes=[pltpu.VMEM(acc_shape, F32)] + comm_sems,
        compiler_params=_cparams(("arbitrary",) * 3 if comm else ("parallel", "parallel", "arbitrary")), name=name,
    )(a, b, *extras, *comm_ins)
    o = no
    for op in comm:
        op["res"] = res[o:o + len(op["outs"])]
        o += len(op["outs"])
    return res[:no]


def _run_comm(comm, name):
    comm_ins = [x for op in comm for x in op["ins"]]
    comm_outs = [x for op in comm for x in op["outs"]]
    comm_sems = [x for op in comm for x in op["sems"]]
    nci, nco = len(comm_ins), len(comm_outs)

    def body(*refs):
        parts = _split_comm_refs(comm, refs[:nci], refs[nci:nci + nco], refs[nci + nco:])
        for which in ("start", "mid", "wait"):
            for op, r in zip(comm, parts):
                op["run"](which, *r)

    res = pl.pallas_call(
        body, in_specs=[HBM_SPEC] * nci, out_specs=[HBM_SPEC] * nco, out_shape=comm_outs, scratch_shapes=comm_sems,
        compiler_params=pltpu.CompilerParams(has_side_effects=True), name=name)(*comm_ins)
    o = 0
    for op in comm:
        op["res"] = res[o:o + len(op["outs"])]
        o += len(op["outs"])


def _mm(name, a, b, *, ta=False, tb=False, a_fn=None, epi=None, extras=(), extra_cols=None,
        out_dtypes=(F32,), tm=1024, tn=1024, tk=2048, comm=()):
    m, kdim = (a.shape[1], a.shape[0]) if ta else a.shape
    n = b.shape[0] if tb else b.shape[1]
    assert (b.shape[1] if tb else b.shape[0]) == kdim, (a.shape, b.shape)
    tm, tn, tk = _tile(m, tm), _tile(n, tn), _tile(kdim, tk)
    a_spec = pl.BlockSpec((tk, tm), lambda i, j, k: (k, i)) if ta else pl.BlockSpec((tm, tk), lambda i, j, k: (i, k))
    b_spec = pl.BlockSpec((tn, tk), lambda i, j, k: (j, k)) if tb else pl.BlockSpec((tk, tn), lambda i, j, k: (k, j))
    o_spec = pl.BlockSpec((tm, tn), lambda i, j, k: (i, j))
    extra_cols = extra_cols or (0,) * len(extras)
    especs = []
    for c0 in extra_cols:
        assert c0 % tn == 0
        cb = c0 // tn
        especs.append(pl.BlockSpec((tm, tn), lambda i, j, k, cb=cb: (i, cb + j)))
    return _mm_call(name, a, b, a_spec, b_spec, (m // tm, n // tn, kdim // tk), kdim // tk,
                    [jax.ShapeDtypeStruct((m, n), d) for d in out_dtypes], [o_spec] * len(out_dtypes), (tm, tn),
                    ta=ta, tb=tb, a_fn=a_fn, epi=epi, extras=extras, extra_specs=especs, comm=comm)


def _mm_cols(name, a, w, *, comm=()):
    return _mm(name, a, w, out_dtypes=(BF16,), comm=comm)[0]


def _mm_cols_t(name, a, w, *, comm=()):
    return _mm(name, a, w, tb=True, out_dtypes=(BF16,), comm=comm)[0]


def _mm_cols_grad(name, a, dy, *, tm=1024, tk=2048, comm=()):
    s, kout = a.shape
    n = dy.shape[1] // N_DEV
    tm, tk = _tile(kout, tm), _tile(s, tk)
    return _mm_call(name, a, dy, pl.BlockSpec((tk, tm), lambda i, j, k: (k, i)),
                    pl.BlockSpec((tk, n), lambda i, j, k: (k, j)),
                    (kout // tm, N_DEV, s // tk), s // tk,
                    [jax.ShapeDtypeStruct((N_DEV, kout, n), F32)], [pl.BlockSpec((None, tm, n), lambda i, j, k: (j, i, 0))],
                    (tm, n), ta=True, comm=comm)[0]


_GELU_C = math.sqrt(2.0 / math.pi)


def _gelu(x):
    return 0.5 * x * (1.0 + jnp.tanh(_GELU_C * (x + 0.044715 * x * x * x)))


def _gelu_grad(x):
    t = jnp.tanh(_GELU_C * (x + 0.044715 * x * x * x))
    return 0.5 * (1.0 + t) + 0.5 * x * (1.0 - t * t) * _GELU_C * (1.0 + 3.0 * 0.044715 * x * x)


def _relu_sq(x):
    r = jnp.maximum(x.astype(F32), 0.0)
    return r * r


def _rms_fwd(x, g, name):
    s, d = x.shape
    tr = _tile(s, 512)

    def body(x_ref, g_ref, o_ref):
        xv = x_ref[...]
        r = lax.rsqrt(jnp.mean(xv * xv, axis=-1, keepdims=True) + RMS_EPS)
        o_ref[...] = (xv * r * g_ref[...]).astype(BF16)

    return pl.pallas_call(
        body, grid=(s // tr,),
        in_specs=[pl.BlockSpec((tr, d), lambda i: (i, 0)), pl.BlockSpec((1, d), lambda i: (0, 0))],
        out_specs=pl.BlockSpec((tr, d), lambda i: (i, 0)), out_shape=jax.ShapeDtypeStruct((s, d), BF16),
        compiler_params=_cparams(("parallel",)), name=name)(x, g.reshape(1, d))


def _rms_bwd(x, g, dy, dres, name):
    s, d = x.shape
    tr = _tile(s, 512)

    def body(x_ref, g_ref, dy_ref, dres_ref, dx_ref, dg_ref):
        @pl.when(pl.program_id(0) == 0)
        def _():
            dg_ref[...] = jnp.zeros_like(dg_ref)

        xv = x_ref[...]
        dyv = dy_ref[...].astype(F32)
        r = lax.rsqrt(jnp.mean(xv * xv, axis=-1, keepdims=True) + RMS_EPS)
        xh = xv * r
        gdy = dyv * g_ref[...]
        dx_ref[...] = dres_ref[...] + r * (gdy - xh * jnp.mean(gdy * xh, axis=-1, keepdims=True))
        dg_ref[...] += jnp.sum(dyv * xh, axis=0, keepdims=True)

    return pl.pallas_call(
        body, grid=(s // tr,),
        in_specs=[pl.BlockSpec((tr, d), lambda i: (i, 0)), pl.BlockSpec((1, d), lambda i: (0, 0)),
                  pl.BlockSpec((tr, d), lambda i: (i, 0)), pl.BlockSpec((tr, d), lambda i: (i, 0))],
        out_specs=[pl.BlockSpec((tr, d), lambda i: (i, 0)), pl.BlockSpec((1, d), lambda i: (0, 0))],
        out_shape=[jax.ShapeDtypeStruct((s, d), F32), jax.ShapeDtypeStruct((1, d), F32)],
        compiler_params=_cparams(("arbitrary",)), name=name)(x, g.reshape(1, d), dy, dres)


def _final_loss(x, g, target):
    s, d = x.shape
    tr = _tile(s, 512)

    def body(x_ref, g_ref, t_ref, loss_ref, dx_ref, dg_ref):
        @pl.when(pl.program_id(0) == 0)
        def _():
            dg_ref[...] = jnp.zeros_like(dg_ref)
            loss_ref[...] = jnp.zeros_like(loss_ref)

        xv = x_ref[...]
        gv = g_ref[...]
        r = lax.rsqrt(jnp.mean(xv * xv, axis=-1, keepdims=True) + RMS_EPS)
        xh = xv * r
        err = xh * gv - t_ref[...]
        loss_ref[...] += jnp.sum(jnp.sum(err * err, axis=-1, keepdims=True), axis=0, keepdims=True)
        dyv = err * (1.0 / d)
        gdy = dyv * gv
        dx_ref[...] = r * (gdy - xh * jnp.mean(gdy * xh, axis=-1, keepdims=True))
        dg_ref[...] += jnp.sum(dyv * xh, axis=0, keepdims=True)

    return pl.pallas_call(
        body, grid=(s // tr,),
        in_specs=[pl.BlockSpec((tr, d), lambda i: (i, 0)), pl.BlockSpec((1, d), lambda i: (0, 0)),
                  pl.BlockSpec((tr, d), lambda i: (i, 0))],
        out_specs=[pl.BlockSpec((1, 1), lambda i: (0, 0)), pl.BlockSpec((tr, d), lambda i: (i, 0)),
                   pl.BlockSpec((1, d), lambda i: (0, 0))],
        out_shape=[jax.ShapeDtypeStruct((1, 1), F32), jax.ShapeDtypeStruct((s, d), F32),
                   jax.ShapeDtypeStruct((1, d), F32)],
        compiler_params=_cparams(("arbitrary",)), name="final_norm_loss")(x, g.reshape(1, d), target)


def _t5_bucket(rel):
    half = T5_BUCKETS // 2
    max_exact = half // 2
    n = jnp.abs(rel)
    nf = jnp.maximum(n, 1).astype(F32)
    large = max_exact + (jnp.log(nf / max_exact) / math.log(T5_MAX_DISTANCE / max_exact)
                         * (half - max_exact)).astype(jnp.int32)
    large = jnp.minimum(large, half - 1)
    return jnp.where(rel > 0, half, 0) + jnp.where(n < max_exact, n, large)


def _dil_offsets():
    i = jnp.arange(QBLOCK)[:, None]
    kk = jnp.arange(2 * QBLOCK)[None, :]
    return kk - DIL_HALF - i, (jnp.arange(QBLOCK)[None, :] + DIL_HALF) - jnp.arange(2 * QBLOCK)[:, None]


def _expand(table, onehot, name, tn=8192):
    r, n = table.shape[0], onehot.shape[1]
    tn = _tile(n, tn)

    def body(t_ref, oh_ref, o_ref):
        o_ref[...] = lax.dot_general(t_ref[...], oh_ref[...], (((1,), (0,)), ((), ())),
                                     precision=lax.Precision.HIGHEST, preferred_element_type=F32)

    return pl.pallas_call(
        body, grid=(n // tn,),
        in_specs=[pl.BlockSpec((r, LANES), lambda i: (0, 0)), pl.BlockSpec((LANES, tn), lambda i: (0, i))],
        out_specs=pl.BlockSpec((r, tn), lambda i: (0, i)), out_shape=jax.ShapeDtypeStruct((r, n), F32),
        compiler_params=_cparams(("parallel",)), name=name)(table, onehot)


def _pad_rows_lanes(t):
    r, c = t.shape
    return jnp.pad(t.astype(F32), ((0, -r % SUBLANES), (0, LANES - c)))


def _dil_bias(t5, dil):
    ah = t5.shape[1]
    off1, off2 = _dil_offsets()
    bucket = jnp.concatenate([_t5_bucket(off1 * dil).reshape(-1), _t5_bucket(off2 * dil).reshape(-1)])
    onehot = (jnp.arange(LANES)[:, None] == bucket[None, :]).astype(F32)
    b = _expand(_pad_rows_lanes(t5.T), onehot, f"t5_bias_d{dil}")[:ah]
    n1 = QBLOCK * 2 * QBLOCK
    return b[:, :n1].reshape(ah, QBLOCK, 2 * QBLOCK), b[:, n1:].reshape(ah, 2 * QBLOCK, QBLOCK)


def _window(p, c, n, cols=slice(None)):
    return jnp.concatenate([p[pl.ds(DIL_HALF, DIL_HALF), cols], c[:, cols], n[pl.ds(0, DIL_HALF), cols]], axis=0)


def _dil_specs(width, ncol_blocks, col_block, nb):
    def spec(dn):
        return pl.BlockSpec((QBLOCK, width), lambda r, n: (jnp.clip(n + dn, 0, nb - 1), r * ncol_blocks + col_block))
    return [spec(-1), spec(0), spec(1)]


def _dil_fwd(proj, bias1, dil, ah, name):
    s, wtot = proj.shape
    ln = s // dil
    nb = ln // QBLOCK
    assert nb * QBLOCK * dil == s
    aw = ah * HEAD_DIM
    wb = wtot // aw
    scale = 1.0 / math.sqrt(HEAD_DIM)
    pv = proj.reshape(ln, dil * wtot)

    def body(q_ref, kp, kc, kn, vp, vc, vn, b_ref, o_ref, lse_ref):
        n = pl.program_id(1)
        ii = lax.broadcasted_iota(jnp.int32, (QBLOCK, 2 * QBLOCK), 0)
        jj = lax.broadcasted_iota(jnp.int32, (QBLOCK, 2 * QBLOCK), 1)
        kpos = n * QBLOCK + jj - DIL_HALF
        valid = (jnp.abs(jj - DIL_HALF - ii) <= DIL_HALF) & (kpos >= 0) & (kpos < ln)
        for h in range(ah):
            cs = pl.ds(h * HEAD_DIM, HEAD_DIM)
            kw, vw = _window(kp, kc, kn, cs), _window(vp, vc, vn, cs)
            sc = jnp.where(valid, _nt(q_ref[:, cs], kw) * scale + b_ref[h], NEG_INF)
            m = jnp.max(sc, axis=-1, keepdims=True)
            p = jnp.exp(sc - m)
            l = jnp.sum(p, axis=-1, keepdims=True)
            o_ref[:, cs] = (_nn(p.astype(BF16), vw) / l).astype(BF16)
            lse_ref[:, cs] = jnp.broadcast_to(m + jnp.log(l), (QBLOCK, HEAD_DIM))

    ospec = pl.BlockSpec((QBLOCK, aw), lambda r, n: (n, r))
    o, lse = pl.pallas_call(
        body, grid=(dil, nb),
        in_specs=[_dil_specs(aw, wb, 0, nb)[1], *_dil_specs(aw, wb, 1, nb), *_dil_specs(aw, wb, 2, nb),
                  pl.BlockSpec((ah, QBLOCK, 2 * QBLOCK), lambda r, n: (0, 0, 0))],
        out_specs=[ospec, ospec],
        out_shape=[jax.ShapeDtypeStruct((ln, dil * aw), BF16), jax.ShapeDtypeStruct((ln, dil * aw), F32)],
        compiler_params=_cparams(("parallel", "parallel")), name=name,
    )(pv, pv, pv, pv, pv, pv, pv, bias1)
    return o.reshape(s, aw), lse.reshape(s, aw)


def _dil_merge(outs, lses):
    s, aw = outs[0].shape
    tr = _tile(s, 512)

    def body(o1, l1, o2, l2, o3, l3, o_ref, lse_ref):
        a, b, c = l1[...], l2[...], l3[...]
        m = jnp.maximum(jnp.maximum(a, b), c)
        w1, w2, w3 = jnp.exp(a - m), jnp.exp(b - m), jnp.exp(c - m)
        tot = w1 + w2 + w3
        o_ref[...] = ((w1 * o1[...].astype(F32) + w2 * o2[...].astype(F32) + w3 * o3[...].astype(F32)) / tot).astype(BF16)
        lse_ref[...] = m + jnp.log(tot)

    spec = pl.BlockSpec((tr, aw), lambda i: (i, 0))
    return pl.pallas_call(
        body, grid=(s // tr,), in_specs=[spec] * 6, out_specs=[spec, spec],
        out_shape=[jax.ShapeDtypeStruct((s, aw), BF16), jax.ShapeDtypeStruct((s, aw), F32)],
        compiler_params=_cparams(("parallel",)), name="dil_merge",
    )(outs[0], lses[0], outs[1], lses[1], outs[2], lses[2])


def _head_delta(do, do_col0, o, name):
    s, w = o.shape
    tr = _tile(s, 512)
    cb = do_col0 // HEAD_DIM

    def body(do_ref, o_ref, d_ref):
        d = jnp.sum(do_ref[...].astype(F32) * o_ref[...].astype(F32), axis=-1, keepdims=True)
        d_ref[...] = jnp.broadcast_to(d, d_ref.shape)

    return pl.pallas_call(
        body, grid=(s // tr, w // HEAD_DIM),
        in_specs=[pl.BlockSpec((tr, HEAD_DIM), lambda i, h: (i, cb + h)), pl.BlockSpec((tr, HEAD_DIM), lambda i, h: (i, h))],
        out_specs=pl.BlockSpec((tr, HEAD_DIM), lambda i, h: (i, h)), out_shape=jax.ShapeDtypeStruct((s, w), F32),
        compiler_params=_cparams(("parallel", "parallel")), name=name)(do, o)


def _dil_bwd(proj, dmerged, lse, delta, bias1, bias2, dil, ah, name):
    s, wtot = proj.shape
    ln = s // dil
    nb = ln // QBLOCK
    aw = ah * HEAD_DIM
    wb = wtot // aw
    wd = dmerged.shape[1] // aw
    scale = 1.0 / math.sqrt(HEAD_DIM)
    pv = proj.reshape(ln, dil * wtot)
    dov = dmerged.reshape(ln, dil * dmerged.shape[1])
    lv = lse.reshape(ln, dil * aw)
    dlv = delta.reshape(ln, dil * aw)

    def body(qp, qc, qn, kp, kc, kn, vp, vc, vn, dop, doc, don, lp, lc, lnx, dp, dc, dn, b1_ref, b2_ref,
             dq_ref, dk_ref, dv_ref, db_ref):
        n = pl.program_id(1)

        @pl.when((pl.program_id(0) == 0) & (n == 0))
        def _():
            db_ref[...] = jnp.zeros_like(db_ref)

        ii = lax.broadcasted_iota(jnp.int32, (QBLOCK, 2 * QBLOCK), 0)
        jj = lax.broadcasted_iota(jnp.int32, (QBLOCK, 2 * QBLOCK), 1)
        kpos = n * QBLOCK + jj - DIL_HALF
        valid = (jnp.abs(jj - DIL_HALF - ii) <= DIL_HALF) & (kpos >= 0) & (kpos < ln)
        ww = lax.broadcasted_iota(jnp.int32, (2 * QBLOCK, QBLOCK), 0)
        cc = lax.broadcasted_iota(jnp.int32, (2 * QBLOCK, QBLOCK), 1)
        qpos = n * QBLOCK - DIL_HALF + ww
        valid2 = (jnp.abs(cc + DIL_HALF - ww) <= DIL_HALF) & (qpos >= 0) & (qpos < ln)
        for h in range(ah):
            cs = pl.ds(h * HEAD_DIM, HEAD_DIM)
            kw, vw = _window(kp, kc, kn, cs), _window(vp, vc, vn, cs)
            sc = _nt(qc[:, cs], kw) * scale + b1_ref[h]
            lse2 = jnp.concatenate([lc[:, cs], lc[:, cs]], axis=1)
            p = jnp.where(valid, jnp.exp(jnp.where(valid, sc - lse2, 0.0)), 0.0)
            ds = p * (_nt(doc[:, cs], vw) - jnp.concatenate([dc[:, cs], dc[:, cs]], axis=1))
            dq_ref[:, cs] = _nn(ds.astype(BF16), kw) * scale
            db_ref[h] += ds
            qw, dow = _window(qp, qc, qn, cs), _window(dop, doc, don, cs)
            sc2 = _nt(qw, kc[:, cs]) * scale + b2_ref[h]
            p2 = jnp.where(valid2, jnp.exp(jnp.where(valid2, sc2 - _window(lp, lc, lnx, cs), 0.0)), 0.0)
            dv_ref[:, cs] = _tn(p2.astype(BF16), dow)
            ds2 = p2 * (_nt(dow, vc[:, cs]) - _window(dp, dc, dn, cs))
            dk_ref[:, cs] = _tn(ds2.astype(BF16), qw) * scale

    ospec = pl.BlockSpec((QBLOCK, aw), lambda r, n: (n, r))
    dq, dk, dv, db = pl.pallas_call(
        body, grid=(dil, nb),
        in_specs=[*_dil_specs(aw, wb, 0, nb), *_dil_specs(aw, wb, 1, nb), *_dil_specs(aw, wb, 2, nb),
                  *_dil_specs(aw, wd, 0, nb), *_dil_specs(aw, 1, 0, nb), *_dil_specs(aw, 1, 0, nb),
                  pl.BlockSpec((ah, QBLOCK, 2 * QBLOCK), lambda r, n: (0, 0, 0)),
                  pl.BlockSpec((ah, 2 * QBLOCK, QBLOCK), lambda r, n: (0, 0, 0))],
        out_specs=[ospec, ospec, ospec, pl.BlockSpec((ah, QBLOCK, 2 * QBLOCK), lambda r, n: (0, 0, 0))],
        out_shape=[jax.ShapeDtypeStruct((ln, dil * aw), F32)] * 3 + [jax.ShapeDtypeStruct((ah, QBLOCK, 2 * QBLOCK), F32)],
        compiler_params=_cparams(("arbitrary", "arbitrary")), name=name,
    )(pv, pv, pv, pv, pv, pv, pv, pv, pv, dov, dov, dov, lv, lv, lv, dlv, dlv, dlv, bias1, bias2)
    return dq.reshape(s, aw), dk.reshape(s, aw), dv.reshape(s, aw), db


def _strided_specs(rows, col_block, nsb):
    half = rows // 2
    return [pl.BlockSpec((half, HEAD_DIM), lambda h, n: (jnp.clip(2 * n - 1, 0, 2 * nsb - 1), col_block + h)),
            pl.BlockSpec((rows, HEAD_DIM), lambda h, n: (n, col_block + h)),
            pl.BlockSpec((half, HEAD_DIM), lambda h, n: (jnp.clip(2 * n + 2, 0, 2 * nsb - 1), col_block + h))]


def _fill_window(dst, p, c, n):
    half, rows = p.shape[0], c.shape[0]
    dst[pl.ds(0, half), :] = p[...].astype(F32)
    dst[pl.ds(half, rows), :] = c[...].astype(F32)
    dst[pl.ds(half + rows, half), :] = n[...].astype(F32)


def _dil_masks(n, ln):
    ii = lax.broadcasted_iota(jnp.int32, (QBLOCK, 2 * QBLOCK), 0)
    jj = lax.broadcasted_iota(jnp.int32, (QBLOCK, 2 * QBLOCK), 1)
    kpos = n * QBLOCK + jj - DIL_HALF
    valid = (jnp.abs(jj - DIL_HALF - ii) <= DIL_HALF) & (kpos >= 0) & (kpos < ln)
    ww = lax.broadcasted_iota(jnp.int32, (2 * QBLOCK, QBLOCK), 0)
    cc = lax.broadcasted_iota(jnp.int32, (2 * QBLOCK, QBLOCK), 1)
    qpos = n * QBLOCK - DIL_HALF + ww
    valid2 = (jnp.abs(cc + DIL_HALF - ww) <= DIL_HALF) & (qpos >= 0) & (qpos < ln)
    return valid, valid2


def _dil_fwd_strided(proj, bias1, dil, ah, name):
    s, wtot = proj.shape
    ln = s // dil
    nsb = ln // QBLOCK
    assert nsb * QBLOCK * dil == s
    aw = ah * HEAD_DIM
    rows = QBLOCK * dil
    half = rows // 2
    scale = 1.0 / math.sqrt(HEAD_DIM)

    def body(q_ref, kp, kc, kn, vp, vc, vn, b_ref, o_ref, lse_ref, qf, kf, vf, of):
        valid, _ = _dil_masks(pl.program_id(1), ln)
        qf[...] = q_ref[...].astype(F32)
        _fill_window(kf, kp, kc, kn)
        _fill_window(vf, vp, vc, vn)
        bias = b_ref[...]

        def residue(r, carry):
            qs = pl.ds(r, QBLOCK, stride=dil)
            ws = pl.ds(r, 2 * QBLOCK, stride=dil)
            kw, vw = kf[ws, :].astype(BF16), vf[ws, :].astype(BF16)
            sc = jnp.where(valid, _nt(qf[qs, :].astype(BF16), kw) * scale + bias, NEG_INF)
            m = jnp.max(sc, axis=-1, keepdims=True)
            p = jnp.exp(sc - m)
            l = jnp.sum(p, axis=-1, keepdims=True)
            of[qs, :] = _nn(p.astype(BF16), vw) / l
            lse_ref[qs, :] = jnp.broadcast_to(m + jnp.log(l), (QBLOCK, HEAD_DIM))
            return carry

        lax.fori_loop(0, dil, residue, 0, unroll=4)
        o_ref[...] = of[...].astype(BF16)

    ospec = pl.BlockSpec((rows, HEAD_DIM), lambda h, n: (n, h))
    return pl.pallas_call(
        body, grid=(ah, nsb),
        in_specs=[_strided_specs(rows, 0, nsb)[1], *_strided_specs(rows, ah, nsb), *_strided_specs(rows, 2 * ah, nsb),
                  pl.BlockSpec((None, QBLOCK, 2 * QBLOCK), lambda h, n: (h, 0, 0))],
        out_specs=[ospec, ospec],
        out_shape=[jax.ShapeDtypeStruct((s, aw), BF16), jax.ShapeDtypeStruct((s, aw), F32)],
        scratch_shapes=[pltpu.VMEM((rows, HEAD_DIM), F32), pltpu.VMEM((2 * rows, HEAD_DIM), F32),
                        pltpu.VMEM((2 * rows, HEAD_DIM), F32), pltpu.VMEM((rows, HEAD_DIM), F32)],
        compiler_params=_cparams(("parallel", "parallel")), name=name,
    )(proj, proj, proj, proj, proj, proj, proj, bias1)


def _dil_bwd_strided(proj, dmerged, lse, delta, bias1, bias2, dil, ah, name, acc, out_dtype):
    s, wtot = proj.shape
    ln = s // dil
    nsb = ln // QBLOCK
    aw = ah * HEAD_DIM
    rows = QBLOCK * dil
    half = rows // 2
    scale = 1.0 / math.sqrt(HEAD_DIM)
    center = slice(DIL_HALF, DIL_HALF + QBLOCK)

    def body(qp, qc, qn, kp, kc, kn, vp, vc, vn, dop, doc, don, lp, lc, lnx, dp, dc, dn, b1_ref, b2_ref, aq, ak, av,
             dq_ref, dk_ref, dv_ref, db_ref, qf, kf, vf, dof, dq_s, dk_s, dv_s):
        n = pl.program_id(1)

        @pl.when(n == 0)
        def _():
            db_ref[...] = jnp.zeros_like(db_ref)

        valid, valid2 = _dil_masks(n, ln)
        _fill_window(qf, qp, qc, qn)
        _fill_window(kf, kp, kc, kn)
        _fill_window(vf, vp, vc, vn)
        _fill_window(dof, dop, doc, don)
        b1, b2 = b1_ref[...], b2_ref[...]

        def stat_window(p, c, nx, r):
            return jnp.concatenate([p[pl.ds(r, DIL_HALF, stride=dil), :], c[pl.ds(r, QBLOCK, stride=dil), :],
                                    nx[pl.ds(r, DIL_HALF, stride=dil), :]], axis=0)

        def residue(r, carry):
            ws = pl.ds(r, 2 * QBLOCK, stride=dil)
            os = pl.ds(r, QBLOCK, stride=dil)
            qw, kw, vw, dow = (t[ws, :].astype(BF16) for t in (qf, kf, vf, dof))
            q, k, v, do = qw[center], kw[center], vw[center], dow[center]
            lse_w, delta_w = stat_window(lp, lc, lnx, r), stat_window(dp, dc, dn, r)
            lse_c, delta_c = lse_w[center], delta_w[center]
            sc = _nt(q, kw) * scale + b1
            p = jnp.where(valid, jnp.exp(jnp.where(valid, sc - jnp.concatenate([lse_c, lse_c], axis=1), 0.0)), 0.0)
            ds = p * (_nt(do, vw) - jnp.concatenate([delta_c, delta_c], axis=1))
            dq_s[os, :] = _nn(ds.astype(BF16), kw) * scale
            db_ref[...] += ds
            sc2 = _nt(qw, k) * scale + b2
            p2 = jnp.where(valid2, jnp.exp(jnp.where(valid2, sc2 - lse_w, 0.0)), 0.0)
            dv_s[os, :] = _tn(p2.astype(BF16), dow)
            ds2 = p2 * (_nt(dow, v) - delta_w)
            dk_s[os, :] = _tn(ds2.astype(BF16), qw) * scale
            return carry

        lax.fori_loop(0, dil, residue, 0, unroll=4)
        dq_ref[...] = (dq_s[...] + aq[...]).astype(dq_ref.dtype)
        dk_ref[...] = (dk_s[...] + ak[...]).astype(dk_ref.dtype)
        dv_ref[...] = (dv_s[...] + av[...]).astype(dv_ref.dtype)

    ospec = pl.BlockSpec((rows, HEAD_DIM), lambda h, n: (n, h))
    win = pltpu.VMEM((2 * rows, HEAD_DIM), F32)
    blk = pltpu.VMEM((rows, HEAD_DIM), F32)
    return pl.pallas_call(
        body, grid=(ah, nsb),
        in_specs=[*_strided_specs(rows, 0, nsb), *_strided_specs(rows, ah, nsb), *_strided_specs(rows, 2 * ah, nsb),
                  *_strided_specs(rows, 0, nsb), *_strided_specs(rows, 0, nsb), *_strided_specs(rows, 0, nsb),
                  pl.BlockSpec((None, QBLOCK, 2 * QBLOCK), lambda h, n: (h, 0, 0)),
                  pl.BlockSpec((None, 2 * QBLOCK, QBLOCK), lambda h, n: (h, 0, 0)), ospec, ospec, ospec],
        out_specs=[ospec, ospec, ospec, pl.BlockSpec((None, QBLOCK, 2 * QBLOCK), lambda h, n: (h, 0, 0))],
        out_shape=[jax.ShapeDtypeStruct((s, aw), out_dtype)] * 3 + [jax.ShapeDtypeStruct((ah, QBLOCK, 2 * QBLOCK), F32)],
        scratch_shapes=[win, win, win, win, blk, blk, blk],
        compiler_params=_cparams(("parallel", "arbitrary")), name=name,
    )(proj, proj, proj, proj, proj, proj, proj, proj, proj, dmerged, dmerged, dmerged, lse, lse, lse,
      delta, delta, delta, bias1, bias2, *acc)


def _bucket_sum(vals, onehot, name):
    r, n = vals.shape
    b = onehot.shape[1]

    def body(v_ref, oh_ref, o_ref):
        o_ref[...] = lax.dot_general(v_ref[...], oh_ref[...], (((1,), (0,)), ((), ())),
                                     precision=lax.Precision.HIGHEST, preferred_element_type=F32)

    tr = max(t for t in range(SUBLANES, 257, SUBLANES) if r % t == 0)
    return pl.pallas_call(
        body, grid=(r // tr,),
        in_specs=[pl.BlockSpec((tr, n), lambda i: (i, 0)), pl.BlockSpec((n, b), lambda i: (0, 0))],
        out_specs=pl.BlockSpec((tr, b), lambda i: (i, 0)), out_shape=jax.ShapeDtypeStruct((r, b), F32),
        compiler_params=_cparams(("parallel",)), name=name)(vals, onehot)


def _t5_grad(dbias, dil):
    ah = dbias.shape[0]
    off1, _ = _dil_offsets()
    bucket = _t5_bucket(off1 * dil).reshape(-1)
    inside = (jnp.abs(off1) <= DIL_HALF).reshape(-1)
    onehot = ((bucket[:, None] == jnp.arange(LANES)[None, :]) & inside[:, None]).astype(F32)
    vals = jnp.pad(dbias.reshape(ah, -1), ((0, -ah % SUBLANES), (0, 0)))
    return _bucket_sum(vals, onehot, f"t5_grad_d{dil}")[:ah, :T5_BUCKETS].T


def _na_table_rows():
    ro = -np.ones((3, NA_GROUP, NA_WIN), np.int64)
    for i in range(NA_GROUP):
        for j in range(NA_WIN):
            if j < NA_ROWS:
                ro[0, i, j] = j - i + NA_ROWS - 1
            if i <= j < i + NA_ROWS:
                ro[1, i, j] = j - i + NA_ROWS // 2 - 1
            if j >= NA_WIN - NA_ROWS:
                ro[2, i, j] = j - i
    return ro


def _na_bias(rpb):
    ch, nro, nco = rpb.shape
    c = np.arange(GRID_W)
    col_start = np.clip(c - NA_COLS // 2, 0, GRID_W - NA_COLS)
    col_ok = (c[None, :] >= col_start[:, None]) & (c[None, :] < col_start[:, None] + NA_COLS)
    col_idx = np.clip(c[None, :] - c[:, None] + NA_COLS - 1, 0, 2 * NA_COLS - 2).reshape(-1)
    onehot = (np.arange(LANES)[:, None] == col_idx[None, :]).astype(np.float32)
    table = jnp.pad(rpb.astype(F32).reshape(ch * nro, nco), ((0, -(ch * nro) % SUBLANES), (0, LANES - nco)))
    by_row = _expand(table, jnp.asarray(onehot), "rpb_bias", tn=GRID_W * GRID_W)[:ch * nro]
    by_row = jnp.where(jnp.asarray(col_ok.reshape(-1))[None, :], by_row, NEG_INF).reshape(ch, nro, GRID_W, GRID_W)
    neg = jnp.full((ch, GRID_W, GRID_W), NEG_INF, F32)
    strips = [jnp.concatenate([by_row[:, r] if r >= 0 else neg for r in strip], axis=-1)
              for strip in _na_table_rows().reshape(3 * NA_GROUP, NA_WIN)]
    return jnp.stack(strips, axis=1).reshape(ch, 3, NA_GROUP * GRID_W, NA_WIN * GRID_W)


def _na_group(g, rows):
    ngroups = rows // NA_GROUP
    ws = jnp.clip(g * NA_GROUP - NA_ROWS // 2, 0, rows - NA_WIN)
    return pl.multiple_of(ws * GRID_W, GRID_W), jnp.where(g == 0, 0, jnp.where(g == ngroups - 1, 2, 1))


def _na_fwd(qkv, bias, ch, name):
    s = qkv.shape[0]
    rows = s // GRID_W
    assert rows >= NA_WIN and rows % (NA_GROUP * NA_GROUPS_PER_STEP) == 0
    cw = ch * HEAD_DIM
    tg = NA_GROUP * GRID_W
    tq = NA_GROUPS_PER_STEP * tg
    win = NA_WIN * GRID_W
    scale = 1.0 / math.sqrt(HEAD_DIM)

    def body(q_ref, k_ref, v_ref, b_ref, o_ref, lse_ref):
        gb = pl.program_id(1)
        for i in range(NA_GROUPS_PER_STEP):
            st, var = _na_group(gb * NA_GROUPS_PER_STEP + i, rows)
            kw, vw = k_ref[pl.ds(st, win), :], v_ref[pl.ds(st, win), :]
            qs = pl.ds(i * tg, tg)
            sc = _nt(q_ref[qs, :], kw) * scale + b_ref[var]
            m = jnp.max(sc, axis=-1, keepdims=True)
            p = jnp.exp(sc - m)
            l = jnp.sum(p, axis=-1, keepdims=True)
            o_ref[qs, :] = (_nn(p.astype(BF16), vw) / l).astype(BF16)
            lse_ref[qs, :] = jnp.broadcast_to(m + jnp.log(l), (tg, HEAD_DIM))

    ospec = pl.BlockSpec((tq, HEAD_DIM), lambda h, gb: (gb, h))
    return pl.pallas_call(
        body, grid=(ch, s // tq),
        in_specs=[pl.BlockSpec((tq, HEAD_DIM), lambda h, gb: (gb, h)),
                  pl.BlockSpec((s, HEAD_DIM), lambda h, gb: (0, ch + h)),
                  pl.BlockSpec((s, HEAD_DIM), lambda h, gb: (0, 2 * ch + h)),
                  pl.BlockSpec((None, 3, tg, win), lambda h, gb: (h, 0, 0, 0))],
        out_specs=[ospec, ospec],
        out_shape=[jax.ShapeDtypeStruct((s, cw), BF16), jax.ShapeDtypeStruct((s, cw), F32)],
        compiler_params=_cparams(("parallel", "parallel")), name=name)(qkv, qkv, qkv, bias)


def _na_bwd(qkv, o, do, lse, bias, ch, name):
    s = qkv.shape[0]
    rows = s // GRID_W
    cw = ch * HEAD_DIM
    tg = NA_GROUP * GRID_W
    tq = NA_GROUPS_PER_STEP * tg
    win = NA_WIN * GRID_W
    scale = 1.0 / math.sqrt(HEAD_DIM)

    def body(q_ref, k_ref, v_ref, o_ref, do_ref, lse_ref, b_ref, dq_ref, dk_out, dv_out, db_ref, dk_ref, dv_ref):
        gb = pl.program_id(1)

        @pl.when(gb == 0)
        def _():
            dk_ref[...] = jnp.zeros_like(dk_ref)
            dv_ref[...] = jnp.zeros_like(dv_ref)
            db_ref[...] = jnp.zeros_like(db_ref)

        for i in range(NA_GROUPS_PER_STEP):
            st, var = _na_group(gb * NA_GROUPS_PER_STEP + i, rows)
            ws = pl.ds(st, win)
            kw, vw = k_ref[ws, :], v_ref[ws, :]
            qs = pl.ds(i * tg, tg)
            q, dov = q_ref[qs, :], do_ref[qs, :]
            sc = _nt(q, kw) * scale + b_ref[var]
            p = jnp.exp(sc - lse_ref[qs, :][:, :1])
            delta = jnp.sum(dov.astype(F32) * o_ref[qs, :].astype(F32), axis=-1, keepdims=True)
            ds = p * (_nt(dov, vw) - delta)
            dsb = ds.astype(BF16)
            dq_ref[qs, :] = (_nn(dsb, kw) * scale).astype(BF16)
            dk_ref[ws, :] += _tn(dsb, q) * scale
            dv_ref[ws, :] += _tn(p.astype(BF16), dov)
            db_ref[var] += ds

        @pl.when(gb == s // tq - 1)
        def _():
            dk_out[...] = dk_ref[...].astype(BF16)
            dv_out[...] = dv_ref[...].astype(BF16)

    qspec = pl.BlockSpec((tq, HEAD_DIM), lambda h, gb: (gb, h))
    kvspec = pl.BlockSpec((s, HEAD_DIM), lambda h, gb: (0, h))
    bspec = pl.BlockSpec((None, 3, tg, win), lambda h, gb: (h, 0, 0, 0))
    return pl.pallas_call(
        body, grid=(ch, s // tq),
        in_specs=[qspec, pl.BlockSpec((s, HEAD_DIM), lambda h, gb: (0, ch + h)),
                  pl.BlockSpec((s, HEAD_DIM), lambda h, gb: (0, 2 * ch + h)), qspec, qspec, qspec, bspec],
        out_specs=[qspec, kvspec, kvspec, bspec],
        out_shape=[jax.ShapeDtypeStruct((s, cw), BF16), jax.ShapeDtypeStruct((s, cw), BF16),
                   jax.ShapeDtypeStruct((s, cw), BF16), jax.ShapeDtypeStruct((ch, 3, tg, win), F32)],
        scratch_shapes=[pltpu.VMEM((s, HEAD_DIM), F32), pltpu.VMEM((s, HEAD_DIM), F32)],
        compiler_params=_cparams(("parallel", "arbitrary")), name=name)(qkv, qkv, qkv, o, do, lse, bias)


def _rpb_grad(dbias):
    ch = dbias.shape[0]
    ntile = 3 * NA_GROUP * NA_WIN
    c = np.arange(GRID_W)
    col_idx = (c[None, :] - c[:, None] + NA_COLS - 1).reshape(-1)
    oh_col = (col_idx[:, None] == np.arange(LANES)[None, :]).astype(np.float32)
    d6 = dbias.reshape(ch, 3, NA_GROUP, GRID_W, NA_WIN, GRID_W)
    vals = jnp.transpose(d6, (0, 1, 2, 4, 3, 5)).reshape(ch * ntile, GRID_W * GRID_W)
    by_col = _bucket_sum(vals, jnp.asarray(oh_col), "rpb_grad_cols")
    npad = 2 * LANES
    oh_row = np.zeros((npad, LANES), np.float32)
    for t, r in enumerate(_na_table_rows().reshape(-1)):
        if r >= 0:
            oh_row[t, r] = 1.0
    by_col = jnp.pad(by_col.reshape(ch, ntile, LANES), ((0, 0), (0, npad - ntile), (0, 0)))
    vals2 = jnp.transpose(by_col, (0, 2, 1)).reshape(ch * LANES, npad)
    by_row = _bucket_sum(vals2, jnp.asarray(oh_row), "rpb_grad_rows")
    return jnp.transpose(by_row.reshape(ch, LANES, LANES), (0, 2, 1))[:, :2 * NA_ROWS - 1, :2 * NA_COLS - 1]


def _s5_discretize(lam_re, lam_im, log_step, b_re, b_im):
    step = jnp.exp(log_step.astype(F32))[:, None]
    lr = jnp.minimum(lam_re.astype(F32), -1e-4)
    li = lam_im.astype(F32)
    mag = jnp.exp(lr * step)
    ab_re = mag * jnp.cos(li * step)
    ab_im = mag * jnp.sin(li * step)
    den = lr * lr + li * li
    zr = ((ab_re - 1.0) * lr + ab_im * li) / den
    zi = (ab_im * lr - (ab_re - 1.0) * li) / den
    br = b_re.astype(F32)
    bi = b_im.astype(F32)
    return ab_re, ab_im, zr[..., None] * br - zi[..., None] * bi, zr[..., None] * bi + zi[..., None] * br


def _scan_tables(a_re, a_im, rev):
    ar, ai = a_re.reshape(-1), a_im.reshape(-1)
    pows = [(ar, ai)]
    for _ in range(SUBLANES - 1):
        pr, pi = pows[-1]
        pows.append((pr * ar - pi * ai, pr * ai + pi * ar))
    row = jnp.arange(SUBLANES)[:, None]
    tabs = []
    for k in (1, 2, 4):
        keep = (row < SUBLANES - k) if rev else (row >= k)
        tabs += [jnp.where(keep, pows[k - 1][0][None, :], 0.0), jnp.where(keep, pows[k - 1][1][None, :], 0.0)]
    order = list(range(SUBLANES - 1, -1, -1)) if rev else list(range(SUBLANES))
    tabs += [jnp.stack([pows[i][0] for i in order]), jnp.stack([pows[i][1] for i in order])]
    t = jnp.stack(tabs)
    nblk = t.shape[-1] // (4 * LANES)
    return jnp.transpose(t.reshape(8, SUBLANES, nblk, 4 * LANES), (2, 0, 1, 3))


def _block_diag(w):
    g, a, b = w.shape
    nblk = g // S5_GROUPS_PER_BLOCK
    eye = jnp.eye(S5_GROUPS_PER_BLOCK, dtype=w.dtype)
    w4 = w.reshape(nblk, S5_GROUPS_PER_BLOCK, a, b)
    return (w4[:, :, :, None, :] * eye[None, :, None, :, None]).reshape(nblk, S5_GROUPS_PER_BLOCK * a, S5_GROUPS_PER_BLOCK * b)


def _block_diag_take(w, a, b):
    nblk = w.shape[0]
    w5 = w.reshape(nblk, S5_GROUPS_PER_BLOCK, a, S5_GROUPS_PER_BLOCK, b)
    eye = jnp.eye(S5_GROUPS_PER_BLOCK, dtype=w.dtype)
    return jnp.sum(w5 * eye[None, :, None, :, None], axis=3).reshape(nblk * S5_GROUPS_PER_BLOCK, a, b)


def _scan_tile(r, i, tab_ref, carry, rev):
    for lvl, k in enumerate((1, 2, 4)):
        mr, mi = tab_ref[2 * lvl], tab_ref[2 * lvl + 1]
        sh = SUBLANES - k if rev else k
        rr, ri = pltpu.roll(r, sh, 0), pltpu.roll(i, sh, 0)
        r, i = r + (mr * rr - mi * ri), i + (mr * ri + mi * rr)
    pr, pi = tab_ref[6], tab_ref[7]
    cr, ci = carry
    return r + (pr * cr - pi * ci), i + (pr * ci + pi * cr)


def _s5_fwd(proj, ucol0, tabs, bre, bim, cre, cim, rev, final, name):
    s = proj.shape[0]
    nblk = tabs.shape[0]
    bw = nblk * LANES
    w = 4 * LANES
    t = _tile(s, S5_CHUNK)
    nc, nt = s // t, t // SUBLANES
    ub = ucol0 // LANES
    cm = (lambda c: nc - 1 - c) if rev else (lambda c: c)
    last = 0 if rev else SUBLANES - 1

    def body(u_ref, tab_ref, bre_ref, bim_ref, cre_ref, cim_ref, *rest):
        if final is not None:
            yo_ref, d_ref, y_ref, xr_ref, xi_ref, xr_s, xi_s, car_r, car_i = rest
        else:
            y_ref, xr_ref, xi_ref, xr_s, xi_s, car_r, car_i = rest

        @pl.when(pl.program_id(1) == 0)
        def _():
            car_r[...] = jnp.zeros_like(car_r)
            car_i[...] = jnp.zeros_like(car_i)

        u = u_ref[...]
        xr_s[...] = _nn(u, bre_ref[...])
        xi_s[...] = _nn(u, bim_ref[...])

        def tile(tt, carry):
            k = nt - 1 - tt if rev else tt
            rows = pl.ds(pl.multiple_of(k * SUBLANES, SUBLANES), SUBLANES)
            r, i = _scan_tile(xr_s[rows, :], xi_s[rows, :], tab_ref, carry, rev)
            xr_s[rows, :] = r
            xi_s[rows, :] = i
            return (jnp.broadcast_to(r[last:last + 1, :], r.shape), jnp.broadcast_to(i[last:last + 1, :], i.shape))

        carry = lax.fori_loop(0, nt, tile, (car_r[...], car_i[...]))
        car_r[...], car_i[...] = carry
        xr, xi = xr_s[...].astype(BF16), xi_s[...].astype(BF16)
        y = _nn(xr, cre_ref[...]) - _nn(xi, cim_ref[...])
        if final is not None:
            y = y + yo_ref[...] + d_ref[...] * u.astype(F32)
        y_ref[...] = y
        xr_ref[...] = xr
        xi_ref[...] = xi

    yspec = pl.BlockSpec((t, LANES), lambda j, c: (cm(c), j))
    xspec = pl.BlockSpec((t, w), lambda j, c: (cm(c), j))
    in_specs = [pl.BlockSpec((t, LANES), lambda j, c: (cm(c), ub + j)),
                pl.BlockSpec((None, 8, SUBLANES, w), lambda j, c: (j, 0, 0, 0)),
                pl.BlockSpec((None, LANES, w), lambda j, c: (j, 0, 0)), pl.BlockSpec((None, LANES, w), lambda j, c: (j, 0, 0)),
                pl.BlockSpec((None, w, LANES), lambda j, c: (j, 0, 0)), pl.BlockSpec((None, w, LANES), lambda j, c: (j, 0, 0))]
    args = [proj, tabs, bre, bim, cre, cim]
    if final is not None:
        in_specs += [yspec, pl.BlockSpec((1, LANES), lambda j, c: (0, j))]
        args += [final[0], final[1].reshape(1, bw)]
    return pl.pallas_call(
        body, grid=(nblk, nc), in_specs=in_specs, out_specs=[yspec, xspec, xspec],
        out_shape=[jax.ShapeDtypeStruct((s, bw), F32), jax.ShapeDtypeStruct((s, nblk * w), BF16),
                   jax.ShapeDtypeStruct((s, nblk * w), BF16)],
        scratch_shapes=[pltpu.VMEM((t, w), F32), pltpu.VMEM((t, w), F32), pltpu.VMEM((SUBLANES, w), F32),
                        pltpu.VMEM((SUBLANES, w), F32)],
        compiler_params=_cparams(("parallel", "arbitrary")), name=name)(*args)


def _s5_bwd(proj, ucol0, dy, xr, xi, gtabs, bre, bim, cre, cim, rev, final, name):
    s = proj.shape[0]
    nblk = gtabs.shape[0]
    bw = nblk * LANES
    w = 4 * LANES
    t = _tile(s, S5_CHUNK)
    nc, nt = s // t, t // SUBLANES
    ub = ucol0 // LANES
    grev = not rev
    cm = (lambda c: nc - 1 - c) if grev else (lambda c: c)
    last = 0 if grev else SUBLANES - 1
    nfin = 2 if final is not None else 0

    def body(dy_ref, u_ref, xr_ref, xi_ref, tab_ref, bre_ref, bim_ref, cre_ref, cim_ref, *rest):
        fin, rest = rest[:nfin], rest[nfin:]
        du_ref, dar_ref, dai_ref, dbr_ref, dbi_ref, dcr_ref, dci_ref = rest[:7]
        rest = rest[7:]
        if final is not None:
            dd_ref, rest = rest[0], rest[1:]
        gr_s, gi_s, xr_s, xi_s, car_r, car_i = rest

        @pl.when(pl.program_id(1) == 0)
        def _():
            for ref in (car_r, car_i, dar_ref, dai_ref, dbr_ref, dbi_ref, dcr_ref, dci_ref):
                ref[...] = jnp.zeros_like(ref)
            if final is not None:
                dd_ref[...] = jnp.zeros_like(dd_ref)

        dyv = dy_ref[...]
        dyb = dyv.astype(BF16)
        u = u_ref[...]
        xrb, xib = xr_ref[...], xi_ref[...]
        gr_s[...] = _nt(dyb, cre_ref[...])
        gi_s[...] = -_nt(dyb, cim_ref[...])
        xr_s[...] = xrb.astype(F32)
        xi_s[...] = xib.astype(F32)
        rowid = lax.broadcasted_iota(jnp.int32, (SUBLANES, w), 0)

        def tile(tt, carry):
            k = nt - 1 - tt if grev else tt
            rows = pl.ds(pl.multiple_of(k * SUBLANES, SUBLANES), SUBLANES)
            r, i = _scan_tile(gr_s[rows, :], gi_s[rows, :], tab_ref, carry, grev)
            gr_s[rows, :] = r
            gi_s[rows, :] = i
            if grev:
                er = jnp.where(rowid == SUBLANES - 1, carry[0], pltpu.roll(r, SUBLANES - 1, 0))
                ei = jnp.where(rowid == SUBLANES - 1, carry[1], pltpu.roll(i, SUBLANES - 1, 0))
            else:
                er = jnp.where(rowid == 0, carry[0], pltpu.roll(r, 1, 0))
                ei = jnp.where(rowid == 0, carry[1], pltpu.roll(i, 1, 0))
            sr, si = xr_s[rows, :], xi_s[rows, :]
            dar_ref[...] += er * sr + ei * si
            dai_ref[...] += ei * sr - er * si
            return (jnp.broadcast_to(r[last:last + 1, :], r.shape), jnp.broadcast_to(i[last:last + 1, :], i.shape))

        carry = lax.fori_loop(0, nt, tile, (car_r[...], car_i[...]))
        car_r[...], car_i[...] = carry
        gr, gi = gr_s[...].astype(BF16), gi_s[...].astype(BF16)
        du = _nt(gr, bre_ref[...]) + _nt(gi, bim_ref[...])
        if final is not None:
            du = du + fin[0][...] + fin[1][...] * dyv.astype(F32)
            dd_ref[...] += jnp.sum(dyv.astype(F32) * u.astype(F32), axis=0, keepdims=True)
        du_ref[...] = du.astype(du_ref.dtype)
        dbr_ref[...] += _tn(u, gr)
        dbi_ref[...] += _tn(u, gi)
        dcr_ref[...] += _tn(xrb, dyb)
        dci_ref[...] -= _tn(xib, dyb)

    yspec = pl.BlockSpec((t, LANES), lambda j, c: (cm(c), j))
    xspec = pl.BlockSpec((t, w), lambda j, c: (cm(c), j))
    bspec = pl.BlockSpec((None, LANES, w), lambda j, c: (j, 0, 0))
    cspec = pl.BlockSpec((None, w, LANES), lambda j, c: (j, 0, 0))
    aspec = pl.BlockSpec((None, SUBLANES, w), lambda j, c: (j, 0, 0))
    dspec = pl.BlockSpec((1, LANES), lambda j, c: (0, j))
    in_specs = [yspec, pl.BlockSpec((t, LANES), lambda j, c: (cm(c), ub + j)), xspec, xspec,
                pl.BlockSpec((None, 8, SUBLANES, w), lambda j, c: (j, 0, 0, 0)), bspec, bspec, cspec, cspec]
    args = [dy, proj, xr, xi, gtabs, bre, bim, cre, cim]
    out_specs = [yspec, aspec, aspec, bspec, bspec, cspec, cspec]
    du_dtype = BF16 if final is not None else F32
    out_shape = [jax.ShapeDtypeStruct((s, bw), du_dtype)] + [jax.ShapeDtypeStruct((nblk, SUBLANES, w), F32)] * 2 \
        + [jax.ShapeDtypeStruct((nblk, LANES, w), F32)] * 2 + [jax.ShapeDtypeStruct((nblk, w, LANES), F32)] * 2
    if final is not None:
        in_specs += [yspec, dspec]
        args += [final[0], final[1].reshape(1, bw)]
        out_specs.append(dspec)
        out_shape.append(jax.ShapeDtypeStruct((1, bw), F32))
    return pl.pallas_call(
        body, grid=(nblk, nc), in_specs=in_specs, out_specs=out_specs, out_shape=out_shape,
        scratch_shapes=[pltpu.VMEM((t, w), F32)] * 4 + [pltpu.VMEM((SUBLANES, w), F32)] * 2,
        compiler_params=_cparams(("parallel", "arbitrary")), name=name)(*args)


N_CHIPS = 4


def _place():
    mx, my, mc = lax.axis_index("x"), lax.axis_index("y"), lax.axis_index("c")
    return (mx, my, mc), (mx, my, 1 - mc), [(1 - mx, my), (mx, 1 - my), (1 - mx, 1 - my)]


def _gather_op(x, cols=False):
    r, c = x.shape

    def run(which, ins, outs, sems):
        (x_ref,), (out_ref,), (send_sems, recv_sems, local_sem) = ins, outs, sems
        me, sibling, chips = _place()
        mc = me[2]

        def slot(px, py, pc):
            d = 4 * px + 2 * py + pc
            return out_ref.at[:, pl.ds(pl.multiple_of(d * c, LANES), c)] if cols else out_ref.at[d]

        def copy(k, block, to, src=None):
            return pltpu.make_async_remote_copy(src_ref=slot(*block) if src is None else src, dst_ref=slot(*block),
                                                send_sem=send_sems.at[k], recv_sem=recv_sems.at[k],
                                                device_id=to, device_id_type=MESH)

        def mine():
            return pltpu.make_async_copy(x_ref, slot(*me), local_sem)

        def first():
            return [copy(0, me, sibling, src=x_ref)] + [copy(1 + j, me, (*chip, mc), src=x_ref) for j, chip in enumerate(chips)]

        def passed():
            return [copy(4 + j, (*chip, mc), sibling) for j, chip in enumerate(chips)]

        if which == "start":
            mine().start()
            for cp in first():
                cp.start()
        elif which == "mid":
            for j, (chip, cp) in enumerate(zip(chips, passed())):
                copy(1 + j, (*chip, mc), me).wait_recv()
                cp.start()
        else:
            copy(0, sibling, me).wait_recv()
            for j, chip in enumerate(chips):
                copy(4 + j, (*chip, 1 - mc), me).wait_recv()
            for cp in first() + passed():
                cp.wait_send()
            mine().wait()

    return dict(ins=[x], outs=[jax.ShapeDtypeStruct((r, N_DEV * c) if cols else (N_DEV, r, c), x.dtype)], run=run,
                sems=[pltpu.SemaphoreType.DMA((N_DEV - 1,)), pltpu.SemaphoreType.DMA((N_DEV - 1,)),
                      pltpu.SemaphoreType.DMA])


def _pair_op(gs):
    nt = len(gs)

    def run(which, g_refs, out_refs, sems):
        if which == "mid":
            return
        send_sems, recv_sems = sems
        me, sibling, _ = _place()
        copies = [pltpu.make_async_remote_copy(src_ref=g_refs[t].at[2 * q + (1 - me[2])], dst_ref=out_refs[t].at[q],
                                               send_sem=send_sems.at[t, q], recv_sem=recv_sems.at[t, q],
                                               device_id=sibling, device_id_type=MESH)
                  for t in range(nt) for q in range(N_CHIPS)]
        if which == "start":
            for cp in copies:
                cp.start()
        elif which == "wait":
            for cp in copies:
                cp.wait_recv()
            for cp in copies:
                cp.wait_send()

    return dict(ins=list(gs), outs=[jax.ShapeDtypeStruct((N_CHIPS,) + g.shape[1:], F32) for g in gs], run=run,
                sems=[pltpu.SemaphoreType.DMA((nt, N_CHIPS)), pltpu.SemaphoreType.DMA((nt, N_CHIPS))])


def _pair_add(g, t, core, name):
    _, r, c = g.shape
    tr = r
    while tr * c > 512 * 1024 and tr % 32 == 0:
        tr //= 2

    def body(core_ref, g_ref, t_ref, o_ref):
        o_ref[...] = (g_ref[...] + t_ref[...]).astype(BF16)

    return pl.pallas_call(
        body,
        grid_spec=pltpu.PrefetchScalarGridSpec(
            num_scalar_prefetch=1, grid=(N_CHIPS, r // tr),
            in_specs=[pl.BlockSpec((None, tr, c), lambda q, i, core_ref: (2 * q + core_ref[0], i, 0)),
                      pl.BlockSpec((None, tr, c), lambda q, i, core_ref: (q, i, 0))],
            out_specs=pl.BlockSpec((None, tr, c), lambda q, i, core_ref: (q, i, 0))),
        out_shape=jax.ShapeDtypeStruct((N_CHIPS, r, c), BF16),
        compiler_params=_cparams(("parallel", "parallel")), name=name)(core, g, t)


def _chip_op(ps):
    nt = len(ps)

    def run(which, p_refs, out_refs, sems):
        if which == "mid":
            return
        send_sems, recv_sems, local_sems = sems
        me, _, chips = _place()
        mychip = 2 * me[0] + me[1]
        owns = [pltpu.make_async_copy(p_refs[t].at[mychip], out_refs[t].at[mychip], local_sems.at[t]) for t in range(nt)]
        sends = [pltpu.make_async_remote_copy(src_ref=p_refs[t].at[2 * px + py], dst_ref=out_refs[t].at[mychip],
                                              send_sem=send_sems.at[t, j], recv_sem=recv_sems.at[t, j],
                                              device_id=(px, py, me[2]), device_id_type=MESH)
                 for t in range(nt) for j, (px, py) in enumerate(chips)]
        if which == "start":
            for cp in owns + sends:
                cp.start()
        elif which == "wait":
            for t in range(nt):
                for j, (px, py) in enumerate(chips):
                    pltpu.make_async_remote_copy(src_ref=p_refs[t].at[2 * px + py], dst_ref=out_refs[t].at[2 * px + py],
                                                 send_sem=send_sems.at[t, j], recv_sem=recv_sems.at[t, j],
                                                 device_id=(px, py, me[2]), device_id_type=MESH).wait_recv()
            for cp in sends:
                cp.wait_send()
            for cp in owns:
                cp.wait()

    return dict(ins=list(ps), outs=[jax.ShapeDtypeStruct(p.shape, p.dtype) for p in ps], run=run,
                sems=[pltpu.SemaphoreType.DMA((nt, N_CHIPS - 1)), pltpu.SemaphoreType.DMA((nt, N_CHIPS - 1)),
                      pltpu.SemaphoreType.DMA((nt,))])


def _adamw(w, gstack, m, v, name, layer=None, prev=None):
    r, c = w.shape[-2:]
    nstack = gstack.shape[0]
    tr = r
    while tr * c > 128 * 1024 and tr % 32 == 0:
        tr //= 2
    c1 = 1.0 - ADAM_B1 ** ADAM_STEP
    c2 = 1.0 - ADAM_B2 ** ADAM_STEP

    def body(w_ref, g_ref, m_ref, v_ref, *rest):
        go_ref, d_ref, mo_ref, vo_ref = rest[-4:]
        g = g_ref[0].astype(F32)
        for p in range(1, nstack):
            g = g + g_ref[p].astype(F32)
        mn = ADAM_B1 * m_ref[...] + (1.0 - ADAM_B1) * g
        vn = ADAM_B2 * v_ref[...] + (1.0 - ADAM_B2) * (g * g)
        go_ref[...] = g
        mo_ref[...] = mn
        vo_ref[...] = vn
        d_ref[...] = -ADAM_LR * ((mn / c1) / (jnp.sqrt(vn / c2) + ADAM_EPS) + ADAM_WD * w_ref[...])

    if layer is None:
        spec = pl.BlockSpec((tr, c), lambda i: (i, 0))
        full = (r, c)
    else:
        spec = pl.BlockSpec((None, tr, c), lambda i: (layer, i, 0))
        full = w.shape
    prev = list(prev) if prev is not None else []
    return pl.pallas_call(
        body, grid=(r // tr,),
        in_specs=[spec, pl.BlockSpec((nstack, tr, c), lambda i: (0, i, 0)), spec, spec] + [pl.BlockSpec(memory_space=pl.ANY)] * len(prev),
        out_specs=[spec] * 4, out_shape=[jax.ShapeDtypeStruct(full, F32)] * 4,
        input_output_aliases={4 + n: n for n in range(len(prev))},
        compiler_params=_cparams(("parallel",)), name=name)(w, gstack, m, v, *prev)


def _s5_tables(lam_re, lam_im, log_step, b_re, b_im, c_re, c_im):
    out = []
    for d in range(2):
        ab_re, ab_im, bb_re, bb_im = _s5_discretize(lam_re[d], lam_im[d], log_step[d], b_re, b_im)
        rev = d == 1
        out.append(dict(
            tabs=_scan_tables(ab_re, ab_im, rev), gtabs=_scan_tables(ab_re, -ab_im, not rev),
            bre=_block_diag(jnp.transpose(bb_re, (0, 2, 1))).astype(BF16), bim=_block_diag(jnp.transpose(bb_im, (0, 2, 1))).astype(BF16),
            cre=_block_diag(jnp.transpose(c_re[d], (0, 2, 1))).astype(BF16), cim=_block_diag(jnp.transpose(c_im[d], (0, 2, 1))).astype(BF16)))
    return out


def _layer_tensors(i):
    j = i // 2
    mixer = [("ab_w_in", j), ("ab_w_out", j), ("s5_w_glu", j)] if i % 2 == 0 else [("c_w_qkv", j), ("c_w_out", j)]
    return mixer + [("mlp_w1", i), ("mlp_w2", i)]


class _WeightGather:
    def __init__(self, shards, depth):
        self.shards, self.depth, self.ops = shards, depth, {}

    def _op(self, key):
        self.ops[key] = _gather_op(self.shards[key[0]][key[1]], cols=key[0] not in ROW_SHARDED)
        return self.ops[key]

    def start(self):
        _run_comm([self._op(key) for key in _layer_tensors(0)[:-2]], "gather_mixer0")

    def carry(self, i, slot):
        t = _layer_tensors(i)
        nxt = _layer_tensors(i + 1)[:-2] if i + 1 < self.depth else []
        plan = {"in": t[-2:-1], "up": t[-1:], "down": nxt}
        return [self._op(key) for key in plan[slot]]

    def get(self, name, l):
        full = self.ops[(name, l)]["res"][0]
        return full.reshape(-1, full.shape[-1]) if name in ROW_SHARDED else full


class _GradExchange:
    def __init__(self, core, depth):
        self.core, self.depth, self.g, self.recv_ops, self.pairs, self.ps = core, depth, {}, [], {}, {}

    def put(self, name, l, g):
        self.g[(name, l)] = g.reshape(N_DEV, -1, g.shape[-1])

    def _pair(self, keys):
        op = _pair_op([self.g[key] for key in keys])
        for n, key in enumerate(keys):
            self.pairs[key] = (op, n)
        return [op]

    def _chip(self, keys):
        for key in keys:
            op, n = self.pairs[key]
            self.ps[key] = _pair_add(self.g[key], op["res"][n], self.core, f"pair_add_{key[0]}{key[1]}")
        op = _chip_op([self.ps[key] for key in keys])
        self.recv_ops.append((keys, op))
        return [op]

    def carry(self, i, slot):
        t = _layer_tensors(i)
        later = _layer_tensors(i + 1)[:-2] if i + 1 < self.depth else []
        if slot == "up_bwd":
            return self._pair(t[-2:])
        if slot == "in_grad":
            return self._chip(t[-2:-1])
        if slot == "in_bwd":
            return self._chip(t[-1:])
        if slot == "down_bwd":
            return self._pair(later) if later else ()
        return self._chip(later) if later else ()

    def finish(self):
        keys = _layer_tensors(0)[:-2]
        _run_comm(self._pair(keys), "pair_exchange_mixer0")
        _run_comm(self._chip(keys), "chip_exchange_mixer0")
        return {key: op["res"][n] for keys, op in self.recv_ops for n, key in enumerate(keys)}


def _forward_backward(x, target, p, wsrc, gsink):
    s, d = x.shape
    depth = p["norm_mix"].shape[0]
    ah = p["t5_bias"].shape[1]
    aw = ah * HEAD_DIM
    ch = p["c_rpb"].shape[1]
    groups, pstate = p["s5_lam_re"].shape[2:]
    bw = groups * S5_GROUP
    assert aw + bw == d and ch * HEAD_DIM == d

    dil_bias = [_dil_bias(p["t5_bias"], dil) for _, dil in DILATED_BRANCHES]
    saved = []
    for i in range(depth):
        j = i // 2
        sv = dict(x=x)
        hn = _rms_fwd(x, p["norm_mix"][i], f"norm_mix_fwd{i}")
        sv["hn"] = hn
        if i % 2 == 0:
            proj = _mm_cols(f"ab_in_fwd{i}", hn, wsrc.get("ab_w_in", j), comm=wsrc.carry(i, "in"))
            outs = [(_dil_fwd if dil == 1 else _dil_fwd_strided)(proj, dil_bias[b][0], dil, ah, f"dil_fwd_d{dil}_{i}")
                    for b, (_, dil) in enumerate(DILATED_BRANCHES)]
            o_a, lse = _dil_merge([o for o, _ in outs], [l for _, l in outs])
            tb = _s5_tables(p["s5_lam_re"][j], p["s5_lam_im"][j], p["s5_log_step"][j], p["s5_b_re"][j], p["s5_b_im"][j],
                            p["s5_c_re"][j], p["s5_c_im"][j])
            y0, x0r, x0i = _s5_fwd(proj, 3 * aw, tb[0]["tabs"], tb[0]["bre"], tb[0]["bim"], tb[0]["cre"], tb[0]["cim"],
                                   False, None, f"s5_fwd_a{i}")
            y_pre, x1r, x1i = _s5_fwd(proj, 3 * aw, tb[1]["tabs"], tb[1]["bre"], tb[1]["bim"], tb[1]["cre"], tb[1]["cim"],
                                      True, (y0, p["s5_d"][j]), f"s5_fwd_b{i}")
            o_b = _mm(f"glu_fwd{i}", y_pre, wsrc.get("s5_w_glu", j), a_fn=_gelu, extras=(y_pre,), out_dtypes=(BF16,),
                      epi=lambda acc, yp: (_gelu(yp) * jax.nn.sigmoid(acc),))[0]
            merged = jnp.concatenate([o_a, o_b], axis=1)
            x = _mm(f"ab_out_fwd{i}", merged, wsrc.get("ab_w_out", j), extras=(x,), epi=lambda acc, xr: (acc + xr,))[0]
            sv.update(proj=proj, o_a=o_a, lse=lse, tb=tb, states=((x0r, x0i), (x1r, x1i)), y_pre=y_pre, merged=merged)
        else:
            qkv = _mm_cols(f"c_qkv_fwd{i}", hn, wsrc.get("c_w_qkv", j), comm=wsrc.carry(i, "in"))
            nbias = _na_bias(p["c_rpb"][j])
            o, lse = _na_fwd(qkv, nbias, ch, f"na_fwd{i}")
            x = _mm(f"c_out_fwd{i}", o, wsrc.get("c_w_out", j), extras=(x,), epi=lambda acc, xr: (acc + xr,))[0]
            sv.update(qkv=qkv, o=o, lse=lse, nbias=nbias)
        sv["x_mid"] = x
        hn2 = _rms_fwd(x, p["norm_mlp"][i], f"norm_mlp_fwd{i}")
        h_pre = _mm_cols(f"mlp_up_fwd{i}", hn2, wsrc.get("mlp_w1", i), comm=wsrc.carry(i, "up"))
        x = _mm(f"mlp_down_fwd{i}", h_pre, wsrc.get("mlp_w2", i), a_fn=_relu_sq, extras=(x,), epi=lambda acc, xr: (acc + xr,),
                comm=wsrc.carry(i, "down"))[0]
        sv.update(hn2=hn2, h_pre=h_pre)
        saved.append(sv)

    loss_sum, dx, g_final = _final_loss(x, p["norm_final"], target)

    g = ({k: [None] * p[k].shape[0] for k in ("norm_mix", "norm_mlp", "s5_lam_re", "s5_lam_im", "s5_log_step", "s5_b_re",
                                                  "s5_b_im", "s5_c_re", "s5_c_im", "s5_d", "c_rpb")})
    g_t5 = jnp.zeros_like(p["t5_bias"], dtype=F32)
    for i in reversed(range(depth)):
        j = i // 2
        sv = saved[i]
        dh = _mm(f"mlp_down_bwd{i}", dx, wsrc.get("mlp_w2", i), tb=True, extras=(sv["h_pre"],), out_dtypes=(BF16,),
                 epi=lambda acc, hp: (acc * (2.0 * jnp.maximum(hp.astype(F32), 0.0)),), comm=gsink.carry(i, "down_bwd"))[0]
        gsink.put("mlp_w2", i, _mm(f"mlp_w2_grad{i}", sv["h_pre"], dx, ta=True, a_fn=_relu_sq)[0])
        gsink.put("mlp_w1", i, _mm_cols_grad(f"mlp_w1_grad{i}", sv["hn2"], dh, comm=gsink.carry(i, "w1_grad")))
        dhn2 = _mm_cols_t(f"mlp_up_bwd{i}", dh, wsrc.get("mlp_w1", i), comm=gsink.carry(i, "up_bwd"))
        dx, gn = _rms_bwd(sv["x_mid"], p["norm_mlp"][i], dhn2, dx, f"norm_mlp_bwd{i}")
        g["norm_mlp"][i] = gn[0]
        if i % 2 == 0:
            tb = sv["tb"]
            dmerged = _mm(f"ab_out_bwd{i}", dx, wsrc.get("ab_w_out", j), tb=True, out_dtypes=(BF16,))[0]
            gsink.put("ab_w_out", j, _mm(f"ab_w_out_grad{i}", sv["merged"], dx, ta=True)[0])
            def glu_epi(acc, yp, dob):
                sg = jax.nn.sigmoid(acc)
                dob = dob.astype(F32)
                return dob * _gelu(yp) * sg * (1.0 - sg), dob * sg
            dz, t1 = _mm(f"glu_bwd_z{i}", sv["y_pre"], wsrc.get("s5_w_glu", j), a_fn=_gelu, extras=(sv["y_pre"], dmerged),
                         extra_cols=(0, aw), epi=glu_epi, out_dtypes=(BF16, F32))
            dy_pre = _mm(f"glu_bwd_y{i}", dz, wsrc.get("s5_w_glu", j), tb=True, extras=(t1, sv["y_pre"]),
                         epi=lambda acc, t, yp: ((acc + t) * _gelu_grad(yp),), out_dtypes=(BF16,))[0]
            gsink.put("s5_w_glu", j, _mm(f"glu_w_grad{i}", sv["y_pre"], dz, ta=True, a_fn=_gelu)[0])
            r0 = _s5_bwd(sv["proj"], 3 * aw, dy_pre, *sv["states"][0], tb[0]["gtabs"], tb[0]["bre"], tb[0]["bim"],
                         tb[0]["cre"], tb[0]["cim"], False, None, f"s5_bwd_a{i}")
            r1 = _s5_bwd(sv["proj"], 3 * aw, dy_pre, *sv["states"][1], tb[1]["gtabs"], tb[1]["bre"], tb[1]["bim"],
                         tb[1]["cre"], tb[1]["cim"], True, (r0[0], p["s5_d"][j]), f"s5_bwd_b{i}")
            du = r1[0]
            g["s5_d"][j] = r1[7][0]
            gl_re, gl_im, gls, gb_re, gb_im, gc_re, gc_im = [], [], [], 0.0, 0.0, [], []
            for dnum, rr in enumerate((r0, r1)):
                da_re = jnp.sum(rr[1], axis=1).reshape(groups, pstate)
                da_im = jnp.sum(rr[2], axis=1).reshape(groups, pstate)
                dbb_re = jnp.transpose(_block_diag_take(rr[3], S5_GROUP, pstate), (0, 2, 1))
                dbb_im = jnp.transpose(_block_diag_take(rr[4], S5_GROUP, pstate), (0, 2, 1))
                _, vjp = jax.vjp(_s5_discretize, p["s5_lam_re"][j][dnum], p["s5_lam_im"][j][dnum], p["s5_log_step"][j][dnum],
                                 p["s5_b_re"][j], p["s5_b_im"][j])
                a, b, c, e, f = vjp((da_re, da_im, dbb_re, dbb_im))
                gl_re.append(a)
                gl_im.append(b)
                gls.append(c)
                gb_re, gb_im = gb_re + e, gb_im + f
                gc_re.append(jnp.transpose(_block_diag_take(rr[5], pstate, S5_GROUP), (0, 2, 1)))
                gc_im.append(jnp.transpose(_block_diag_take(rr[6], pstate, S5_GROUP), (0, 2, 1)))
            g["s5_lam_re"][j], g["s5_lam_im"][j], g["s5_log_step"][j] = jnp.stack(gl_re), jnp.stack(gl_im), jnp.stack(gls)
            g["s5_b_re"][j], g["s5_b_im"][j] = gb_re, gb_im
            g["s5_c_re"][j], g["s5_c_im"][j] = jnp.stack(gc_re), jnp.stack(gc_im)
            delta = _head_delta(dmerged, 0, sv["o_a"], f"dil_delta{i}")
            acc = None
            for b, (_, dil) in enumerate(DILATED_BRANCHES):
                args = (sv["proj"], dmerged, sv["lse"], delta, dil_bias[b][0], dil_bias[b][1], dil, ah, f"dil_bwd_d{dil}_{i}")
                if dil == 1:
                    assert acc is None
                    *acc, db = _dil_bwd(*args)
                else:
                    *acc, db = _dil_bwd_strided(*args, acc, BF16 if b == len(DILATED_BRANCHES) - 1 else F32)
                g_t5 = g_t5 + _t5_grad(db, dil)
            dproj = jnp.concatenate([*acc, du], axis=1)
            gsink.put("ab_w_in", j, _mm_cols_grad(f"ab_w_in_grad{i}", sv["hn"], dproj, comm=gsink.carry(i, "in_grad")))
            dhn = _mm_cols_t(f"ab_in_bwd{i}", dproj, wsrc.get("ab_w_in", j), comm=gsink.carry(i, "in_bwd"))
        else:
            do = _mm(f"c_out_bwd{i}", dx, wsrc.get("c_w_out", j), tb=True, out_dtypes=(BF16,))[0]
            gsink.put("c_w_out", j, _mm(f"c_w_out_grad{i}", sv["o"], dx, ta=True)[0])
            dq, dk, dv, db = _na_bwd(sv["qkv"], sv["o"], do, sv["lse"], sv["nbias"], ch, f"na_bwd{i}")
            g["c_rpb"][j] = _rpb_grad(db)
            dqkv = jnp.concatenate([dq, dk, dv], axis=1)
            gsink.put("c_w_qkv", j, _mm_cols_grad(f"c_w_qkv_grad{i}", sv["hn"], dqkv, comm=gsink.carry(i, "in_grad")))
            dhn = _mm_cols_t(f"c_qkv_bwd{i}", dqkv, wsrc.get("c_w_qkv", j), comm=gsink.carry(i, "in_bwd"))
        dx, gn = _rms_bwd(sv["x"], p["norm_mix"][i], dhn, dx, f"norm_mix_bwd{i}")
        g["norm_mix"][i] = gn[0]
    g["t5_bias"] = g_t5
    g["norm_final"] = g_final[0]
    return loss_sum[0, 0], dx, g


BIG = ("ab_w_in", "ab_w_out", "s5_w_glu", "c_w_qkv", "c_w_out", "mlp_w1", "mlp_w2")
ROW_SHARDED = ("ab_w_out", "s5_w_glu", "c_w_out", "mlp_w2")
WEIGHTS = ("t5_bias", "ab_w_in", "ab_w_out", "s5_lam_re", "s5_lam_im", "s5_log_step", "s5_b_re", "s5_b_im", "s5_c_re",
           "s5_c_im", "s5_d", "s5_w_glu", "c_w_qkv", "c_w_out", "c_rpb", "norm_mix", "norm_mlp", "mlp_w1", "mlp_w2",
           "norm_final")


def _step(x, target, w, m, v):
    d = x.shape[-1]
    depth = w["norm_mix"].shape[0]
    wsrc = _WeightGather({k: w[k].astype(BF16) for k in BIG}, depth)
    wsrc.start()
    gsink = _GradExchange(lax.axis_index("c").astype(jnp.int32).reshape(1), depth)
    small = {k: w[k] for k in WEIGHTS if k not in BIG}
    loss_sum, dx, g = _forward_backward(x[0], target[0], small, wsrc, gsink)
    loss = lax.psum(0.5 * loss_sum / d, ("x", "y", "c"))

    out = {}
    recv = gsink.finish()
    for k in BIG:
        res = None
        for l in range(w[k].shape[0]):
            res = _adamw(w[k], recv[(k, l)], m[k], v[k], f"adamw_{k}{l}", layer=l, prev=res)
        out[k] = res
    names = [k for k in WEIGHTS if k not in BIG]
    def flat(tree):
        return jnp.concatenate([jnp.asarray(jnp.stack(tree[k]) if isinstance(tree[k], list) else tree[k], F32).reshape(-1)
                                for k in names])
    total = sum(int(np.prod(w[k].shape)) for k in names)
    rows = -(-total // LANES)
    rows = -(-rows // (2 * SUBLANES)) * (2 * SUBLANES)
    pad = rows * LANES - total
    def pack(tree):
        return jnp.pad(flat(tree), (0, pad)).reshape(rows, LANES)
    small_op = _gather_op(pack(g).astype(BF16))
    _run_comm([small_op], "gather_small_grads")
    res = _adamw(pack(w), small_op["res"][0], pack(m), pack(v), "adamw_small")
    off = 0
    for k in names:
        n = int(np.prod(w[k].shape))
        out[k] = [a.reshape(-1)[off:off + n].reshape(w[k].shape) for a in res]
        off += n
    return (loss, dx[None], *[out[k][0] for k in WEIGHTS], *[out[k][1] for k in WEIGHTS],
            *[out[k][2] for k in WEIGHTS], *[out[k][3] for k in WEIGHTS])


def kernel(x, t5_bias, ab_w_in, ab_w_out, s5_lam_re, s5_lam_im, s5_log_step, s5_b_re, s5_b_im, s5_c_re, s5_c_im, s5_d, s5_w_glu, c_w_qkv, c_w_out, c_rpb, norm_mix, norm_mlp, mlp_w1, mlp_w2, norm_final, loss_target, m_t5_bias, m_ab_w_in, m_ab_w_out, m_s5_lam_re, m_s5_lam_im, m_s5_log_step, m_s5_b_re, m_s5_b_im, m_s5_c_re, m_s5_c_im, m_s5_d, m_s5_w_glu, m_c_w_qkv, m_c_w_out, m_c_rpb, m_norm_mix, m_norm_mlp, m_mlp_w1, m_mlp_w2, m_norm_final, v_t5_bias, v_ab_w_in, v_ab_w_out, v_s5_lam_re, v_s5_lam_im, v_s5_log_step, v_s5_b_re, v_s5_b_im, v_s5_c_re, v_s5_c_im, v_s5_d, v_s5_w_glu, v_c_w_qkv, v_c_w_out, v_c_rpb, v_norm_mix, v_norm_mlp, v_mlp_w1, v_mlp_w2, v_norm_final):
    w = dict(t5_bias=t5_bias, ab_w_in=ab_w_in, ab_w_out=ab_w_out, s5_lam_re=s5_lam_re, s5_lam_im=s5_lam_im,
             s5_log_step=s5_log_step, s5_b_re=s5_b_re, s5_b_im=s5_b_im, s5_c_re=s5_c_re, s5_c_im=s5_c_im, s5_d=s5_d,
             s5_w_glu=s5_w_glu, c_w_qkv=c_w_qkv, c_w_out=c_w_out, c_rpb=c_rpb, norm_mix=norm_mix, norm_mlp=norm_mlp,
             mlp_w1=mlp_w1, mlp_w2=mlp_w2, norm_final=norm_final)
    m = dict(t5_bias=m_t5_bias, ab_w_in=m_ab_w_in, ab_w_out=m_ab_w_out, s5_lam_re=m_s5_lam_re, s5_lam_im=m_s5_lam_im,
             s5_log_step=m_s5_log_step, s5_b_re=m_s5_b_re, s5_b_im=m_s5_b_im, s5_c_re=m_s5_c_re, s5_c_im=m_s5_c_im,
             s5_d=m_s5_d, s5_w_glu=m_s5_w_glu, c_w_qkv=m_c_w_qkv, c_w_out=m_c_w_out, c_rpb=m_c_rpb, norm_mix=m_norm_mix,
             norm_mlp=m_norm_mlp, mlp_w1=m_mlp_w1, mlp_w2=m_mlp_w2, norm_final=m_norm_final)
    v = dict(t5_bias=v_t5_bias, ab_w_in=v_ab_w_in, ab_w_out=v_ab_w_out, s5_lam_re=v_s5_lam_re, s5_lam_im=v_s5_lam_im,
             s5_log_step=v_s5_log_step, s5_b_re=v_s5_b_re, s5_b_im=v_s5_b_im, s5_c_re=v_s5_c_re, s5_c_im=v_s5_c_im,
             s5_d=v_s5_d, s5_w_glu=v_s5_w_glu, c_w_qkv=v_c_w_qkv, c_w_out=v_c_w_out, c_rpb=v_c_rpb, norm_mix=v_norm_mix,
             norm_mlp=v_norm_mlp, mlp_w1=v_mlp_w1, mlp_w2=v_mlp_w2, norm_final=v_norm_final)
    return _step(x, loss_target, w, m, v)
```
